```python
import math
import jax, jax.numpy as jnp
from jax import lax
import numpy as np

D_MODEL = 1024
BATCH = 8
SEQ = 4096
DEPTH = 1
DEC_BATCH = 2
DEC_SEQ = 16384
PAST_LEN = 128

N_META = 16
GRID_W = 64
ATTN_WIDTH = D_MODEL // 2
SSM_WIDTH = D_MODEL - ATTN_WIDTH
HEAD_DIM = 64
N_Q_HEADS = ATTN_WIDTH // HEAD_DIM
N_KV_HEADS = 2
Q_PER_KV = N_Q_HEADS // N_KV_HEADS
KV_WIDTH = N_KV_HEADS * HEAD_DIM
ROPE_AXIS_DIM = HEAD_DIM // 2
ROPE_THETA = 10000.0
Q_BLOCK = 128
ATTN_SCALE = HEAD_DIM ** -0.5
RMS_EPS = 1e-6
SSM_GROUP = 16
N_SSM_GROUPS = SSM_WIDTH // SSM_GROUP
SSM_STATE = 64
N_DIR = 2
DT_MIN = 1e-3
DT_MAX = 1e-1
Q_END = ATTN_WIDTH
K_END = Q_END + KV_WIDTH
V_END = K_END + KV_WIDTH
IN_WIDTH = V_END + SSM_WIDTH
N_EXPERTS = 16
EXPERT_FF = 2 * D_MODEL
EC_CAPACITY_FACTOR = 2
LN_EPS = 1e-5
DEEPNORM_ALPHA = (2 * DEPTH) ** 0.25
DEEPNORM_BETA = (8 * DEPTH) ** -0.25

kernel_name = 'hymba_s5_gqa_axialrope_ec_moe_deepnorm_encoder'


def layer_norm(x, g, b):
    xf = x.astype(jnp.float32)
    xc = xf - jnp.mean(xf, axis=-1, keepdims=True)
    var = jnp.mean(xc * xc, axis=-1, keepdims=True)
    return (xc * lax.rsqrt(var + LN_EPS) * g.astype(jnp.float32) + b.astype(jnp.float32)).astype(x.dtype)


def rms_norm_f32(x, g):
    xf = x.astype(jnp.float32)
    return xf * lax.rsqrt(jnp.mean(xf * xf, axis=-1, keepdims=True) + RMS_EPS) * g.astype(jnp.float32)


def axial_rope_tables(n_tokens):
    rows = n_tokens // GRID_W
    t = jnp.arange(n_tokens, dtype=jnp.int32)
    m = jnp.arange(N_META, dtype=jnp.int32)
    row = jnp.concatenate([jnp.full((N_META,), -(rows // 2) - 1, jnp.int32), t // GRID_W - rows // 2])
    col = jnp.concatenate([m - GRID_W // 2, t % GRID_W - GRID_W // 2])
    inv_freq = ROPE_THETA ** (-jnp.arange(0, ROPE_AXIS_DIM, 2, dtype=jnp.float32) / ROPE_AXIS_DIM)
    ang_r = row.astype(jnp.float32)[:, None] * inv_freq
    ang_c = col.astype(jnp.float32)[:, None] * inv_freq
    return (jnp.cos(ang_r), jnp.sin(ang_r), jnp.cos(ang_c), jnp.sin(ang_c))


def rotate_half(x, cos, sin):
    x1, x2 = jnp.split(x, 2, axis=-1)
    c = cos[None, :, None, :]
    s = sin[None, :, None, :]
    return jnp.concatenate([x1 * c - x2 * s, x2 * c + x1 * s], axis=-1)


def axial_rope(x, tables):
    cos_r, sin_r, cos_c, sin_c = tables
    x_row, x_col = jnp.split(x, 2, axis=-1)
    return jnp.concatenate([rotate_half(x_row, cos_r, sin_r), rotate_half(x_col, cos_c, sin_c)], axis=-1)


def block_attention(q, k, v):
    bsz, seq_len = q.shape[0], q.shape[1]
    q = q.reshape(bsz, seq_len, N_KV_HEADS, Q_PER_KV, HEAD_DIM)

    def attend(qb):
        s = jnp.einsum('bqgrd,bkgd->bgrqk', qb, k, preferred_element_type=jnp.float32) * ATTN_SCALE
        p = jax.nn.softmax(s, axis=-1).astype(v.dtype)
        return jnp.einsum('bgrqk,bkgd->bqgrd', p, v)

    o_meta = attend(q[:, :N_META])
    n_blk = (seq_len - N_META) // Q_BLOCK
    q_blk = jnp.moveaxis(q[:, N_META:].reshape(bsz, n_blk, Q_BLOCK, N_KV_HEADS, Q_PER_KV, HEAD_DIM), 1, 0)
    o_blk = lax.map(attend, q_blk)
    o_real = jnp.moveaxis(o_blk, 0, 1).reshape(bsz, n_blk * Q_BLOCK, N_KV_HEADS, Q_PER_KV, HEAD_DIM)
    return jnp.concatenate([o_meta, o_real], axis=1).reshape(bsz, seq_len, ATTN_WIDTH)


def complex_linear_combine(e1, e2):
    a1r, a1i, b1r, b1i = e1
    a2r, a2i, b2r, b2i = e2
    return (a2r * a1r - a2i * a1i,
            a2r * a1i + a2i * a1r,
            a2r * b1r - a2i * b1i + b2r,
            a2r * b1i + a2i * b1r + b2i)


def s5_direction(ug, lam_re, lam_im, log_dt, b_re, b_im, c_re, c_im, reverse):
    lr = lam_re.astype(jnp.float32)
    li = lam_im.astype(jnp.float32)
    dt = jnp.exp(log_dt.astype(jnp.float32))[:, None]
    mag = jnp.exp(lr * dt)
    ab_r = mag * jnp.cos(li * dt)
    ab_i = mag * jnp.sin(li * dt)
    nr = ab_r - 1.0
    den = lr * lr + li * li
    f_r = (nr * lr + ab_i * li) / den
    f_i = (ab_i * lr - nr * li) / den
    br = b_re.astype(jnp.float32)
    bi = b_im.astype(jnp.float32)
    bb_r = f_r[..., None] * br - f_i[..., None] * bi
    bb_i = f_r[..., None] * bi + f_i[..., None] * br
    bu_r = jnp.einsum('blgc,gpc->blgp', ug, bb_r)
    bu_i = jnp.einsum('blgc,gpc->blgp', ug, bb_i)
    a_r = jnp.broadcast_to(ab_r, bu_r.shape)
    a_i = jnp.broadcast_to(ab_i, bu_i.shape)
    _, _, h_r, h_i = lax.associative_scan(complex_linear_combine, (a_r, a_i, bu_r, bu_i), reverse=reverse, axis=1)
    return (jnp.einsum('blgp,gcp->blgc', h_r, c_re.astype(jnp.float32))
            - jnp.einsum('blgp,gcp->blgc', h_i, c_im.astype(jnp.float32)))


def s5_glu(u, lam_re, lam_im, log_dt, b_re, b_im, c_re, c_im, d_skip, w_glu, b_glu):
    bsz, seq_len = u.shape[0], u.shape[1]
    ug = u.astype(jnp.float32).reshape(bsz, seq_len, N_SSM_GROUPS, SSM_GROUP)
    y = ug * d_skip.astype(jnp.float32).reshape(N_SSM_GROUPS, SSM_GROUP)
    for direction in range(N_DIR):
        y = y + s5_direction(ug, lam_re[direction], lam_im[direction], log_dt[direction],
                             b_re[direction], b_im[direction], c_re[direction], c_im[direction],
                             reverse=(direction == 1))
    z = jax.nn.gelu(y.reshape(bsz, seq_len, SSM_WIDTH)).astype(u.dtype)
    return z * jax.nn.sigmoid(z @ w_glu + b_glu)


def expert_choice_ffn(h, w_router, w_gate, w_up, w_down):
    bsz, seq_len, dm = h.shape
    tokens = h.reshape(bsz * seq_len, dm)
    n_tok = bsz * seq_len
    capacity = EC_CAPACITY_FACTOR * n_tok // N_EXPERTS
    logits = jnp.einsum('nd,de->ne', tokens, w_router, preferred_element_type=jnp.float32)
    affinity = jax.nn.softmax(logits, axis=-1)
    gate, idx = lax.top_k(affinity.T, capacity)
    xe = tokens[idx]
    hid = jax.nn.silu(jnp.einsum('ecd,edf->ecf', xe, w_gate)) * jnp.einsum('ecd,edf->ecf', xe, w_up)
    ye = jnp.einsum('ecf,efd->ecd', hid, w_down) * gate[..., None].astype(h.dtype)
    out = jnp.zeros_like(tokens).at[idx.reshape(-1)].add(ye.reshape(-1, dm))
    return out.reshape(bsz, seq_len, dm)


def encoder_layer(h, tables, w_in, q_norm_g, k_norm_g, lam_re, lam_im, log_dt, b_re, b_im, c_re, c_im,
                  d_skip, w_glu, b_glu, w_out, ln1_g, ln1_b, w_router, w_gate, w_up, w_down, ln2_g, ln2_b):
    bsz, seq_len = h.shape[0], h.shape[1]
    proj = h @ w_in
    q = proj[..., :Q_END].reshape(bsz, seq_len, N_Q_HEADS, HEAD_DIM)
    k = proj[..., Q_END:K_END].reshape(bsz, seq_len, N_KV_HEADS, HEAD_DIM)
    v = proj[..., K_END:V_END].reshape(bsz, seq_len, N_KV_HEADS, HEAD_DIM)
    u = proj[..., V_END:]
    q = axial_rope(rms_norm_f32(q, q_norm_g), tables).astype(h.dtype)
    k = axial_rope(rms_norm_f32(k, k_norm_g), tables).astype(h.dtype)
    attn_out = block_attention(q, k, v)
    ssm_out = s5_glu(u, lam_re, lam_im, log_dt, b_re, b_im, c_re, c_im, d_skip, w_glu, b_glu)
    mix = jnp.concatenate([attn_out, ssm_out.astype(h.dtype)], axis=-1) @ w_out
    h = layer_norm(DEEPNORM_ALPHA * h + mix, ln1_g, ln1_b)
    h = layer_norm(DEEPNORM_ALPHA * h + expert_choice_ffn(h, w_router, w_gate, w_up, w_down), ln2_g, ln2_b)
    return h


def run_trunk(x, meta_tokens, ln_emb_g, ln_emb_b, layer_weights):
    bsz, n_tokens, dm = x.shape
    meta = jnp.broadcast_to(meta_tokens[None].astype(x.dtype), (bsz, N_META, dm))
    h = layer_norm(jnp.concatenate([meta, x], axis=1), ln_emb_g, ln_emb_b)
    tables = axial_rope_tables(n_tokens)
    for layer in range(DEPTH):
        h = encoder_layer(h, tables, *[w[layer] for w in layer_weights])
    return h[:, N_META:]


def setup_inputs(seed: int = 0) -> dict:
    key = jax.random.key(seed)
    ks = jax.random.split(key, 32)
    f32 = jnp.float32
    nrm = lambda k, shape, scale: jax.random.normal(k, shape, f32) * scale
    x_prompt = nrm(ks[0], (BATCH, SEQ, D_MODEL), 1.0)
    x_sample = nrm(ks[1], (DEC_BATCH, DEC_SEQ, D_MODEL), 1.0)
    meta_tokens = nrm(ks[2], (N_META, D_MODEL), 1.0)
    ln_emb_g = 1.0 + nrm(ks[3], (D_MODEL,), 0.02)
    ln_emb_b = nrm(ks[4], (D_MODEL,), 0.02)
    w_in = nrm(ks[5], (DEPTH, D_MODEL, IN_WIDTH), D_MODEL ** -0.5)
    w_in = w_in.at[:, :, K_END:V_END].multiply(DEEPNORM_BETA)
    q_norm_g = 1.0 + nrm(ks[6], (DEPTH, HEAD_DIM), 0.02)
    k_norm_g = 1.0 + nrm(ks[7], (DEPTH, HEAD_DIM), 0.02)
    n_idx = jnp.arange(SSM_STATE, dtype=f32)
    ssm_lambda_re = -0.5 * (1.0 + nrm(ks[8], (DEPTH, N_DIR, N_SSM_GROUPS, SSM_STATE), 0.01))
    ssm_lambda_im = math.pi * n_idx + nrm(ks[9], (DEPTH, N_DIR, N_SSM_GROUPS, SSM_STATE), 0.01)
    ssm_log_dt = jax.random.uniform(ks[10], (DEPTH, N_DIR, N_SSM_GROUPS), f32,
                                    math.log(DT_MIN), math.log(DT_MAX))
    b_scale = (2 * SSM_GROUP) ** -0.5
    ssm_b_re = nrm(ks[11], (DEPTH, N_DIR, N_SSM_GROUPS, SSM_STATE, SSM_GROUP), b_scale)
    ssm_b_im = nrm(ks[12], (DEPTH, N_DIR, N_SSM_GROUPS, SSM_STATE, SSM_GROUP), b_scale)
    c_scale = (2 * SSM_STATE) ** -0.5
    ssm_c_re = nrm(ks[13], (DEPTH, N_DIR, N_SSM_GROUPS, SSM_GROUP, SSM_STATE), c_scale)
    ssm_c_im = nrm(ks[14], (DEPTH, N_DIR, N_SSM_GROUPS, SSM_GROUP, SSM_STATE), c_scale)
    ssm_d = nrm(ks[15], (DEPTH, SSM_WIDTH), 1.0)
    w_glu = nrm(ks[16], (DEPTH, SSM_WIDTH, SSM_WIDTH), SSM_WIDTH ** -0.5)
    b_glu = nrm(ks[17], (DEPTH, SSM_WIDTH), 0.01)
    w_out = nrm(ks[18], (DEPTH, D_MODEL, D_MODEL), D_MODEL ** -0.5 * DEEPNORM_BETA)
    ln1_g = 1.0 + nrm(ks[19], (DEPTH, D_MODEL), 0.02)
    ln1_b = nrm(ks[20], (DEPTH, D_MODEL), 0.02)
    w_router = nrm(ks[21], (DEPTH, D_MODEL, N_EXPERTS), D_MODEL ** -0.5)
    w_gate = nrm(ks[22], (DEPTH, N_EXPERTS, D_MODEL, EXPERT_FF), D_MODEL ** -0.5)
    w_up = nrm(ks[23], (DEPTH, N_EXPERTS, D_MODEL, EXPERT_FF), D_MODEL ** -0.5)
    w_down = nrm(ks[24], (DEPTH, N_EXPERTS, EXPERT_FF, D_MODEL), EXPERT_FF ** -0.5 * DEEPNORM_BETA)
    ln2_g = 1.0 + nrm(ks[25], (DEPTH, D_MODEL), 0.02)
    ln2_b = nrm(ks[26], (DEPTH, D_MODEL), 0.02)
    return {'x_prompt': x_prompt, 'x_sample': x_sample, 'meta_tokens': meta_tokens,
            'ln_emb_g': ln_emb_g, 'ln_emb_b': ln_emb_b, 'w_in': w_in,
            'q_norm_g': q_norm_g, 'k_norm_g': k_norm_g,
            'ssm_lambda_re': ssm_lambda_re, 'ssm_lambda_im': ssm_lambda_im, 'ssm_log_dt': ssm_log_dt,
            'ssm_b_re': ssm_b_re, 'ssm_b_im': ssm_b_im, 'ssm_c_re': ssm_c_re, 'ssm_c_im': ssm_c_im,
            'ssm_d': ssm_d, 'w_glu': w_glu, 'b_glu': b_glu, 'w_out': w_out,
            'ln1_g': ln1_g, 'ln1_b': ln1_b, 'w_router': w_router,
            'w_gate': w_gate, 'w_up': w_up, 'w_down': w_down, 'ln2_g': ln2_g, 'ln2_b': ln2_b}


def reference(x_prompt, x_sample, meta_tokens, ln_emb_g, ln_emb_b, w_in, q_norm_g, k_norm_g,
              ssm_lambda_re, ssm_lambda_im, ssm_log_dt, ssm_b_re, ssm_b_im, ssm_c_re, ssm_c_im,
              ssm_d, w_glu, b_glu, w_out, ln1_g, ln1_b, w_router, w_gate, w_up, w_down, ln2_g, ln2_b):
    layer_weights = (w_in, q_norm_g, k_norm_g, ssm_lambda_re, ssm_lambda_im, ssm_log_dt,
                     ssm_b_re, ssm_b_im, ssm_c_re, ssm_c_im, ssm_d, w_glu, b_glu, w_out,
                     ln1_g, ln1_b, w_router, w_gate, w_up, w_down, ln2_g, ln2_b)
    y_prompt = run_trunk(x_prompt, meta_tokens, ln_emb_g, ln_emb_b, layer_weights)
    y_sample = run_trunk(x_sample, meta_tokens, ln_emb_g, ln_emb_b, layer_weights)
    return (y_prompt, y_sample)
```

```python
import functools
import math

import jax
import jax.numpy as jnp
from jax import lax
from jax.experimental import pallas as pl
from jax.experimental.pallas import tpu as pltpu

F32 = jnp.float32
BF16 = jnp.bfloat16

D_MODEL = 1024
N_META = 16
GRID_W = 64
ATTN_WIDTH = 512
SSM_WIDTH = 512
HEAD_DIM = 64
N_Q_HEADS = 8
N_KV_HEADS = 2
Q_PER_KV = 4
KV_WIDTH = 128
ROPE_AXIS_DIM = 32
ROPE_THETA = 10000.0
ATTN_SCALE = HEAD_DIM ** -0.5
RMS_EPS = 1e-6
SSM_GROUP = 16
N_SSM_GROUPS = 32
SSM_STATE = 64
Q_END = ATTN_WIDTH
K_END = Q_END + KV_WIDTH
V_END = K_END + KV_WIDTH
IN_WIDTH = V_END + SSM_WIDTH
N_EXPERTS = 16
EXPERT_FF = 2048
EC_CAPACITY_FACTOR = 2
LN_EPS = 1e-5
DEPTH = 1
DEEPNORM_ALPHA = (2 * DEPTH) ** 0.25

LANES = 128
SUBLANES = 8
TAIL = LANES
CHUNK = 16
N_PAIRS = N_SSM_GROUPS // 2
PAIR_W = 2 * CHUNK * SSM_GROUP
PAIR_STATE = 2 * SSM_STATE
EMBED_TILE = 512
ROW_TILE = 384
Q_TILE = 128
KV_TILE = EMBED_TILE
SLOT_TILE = 384
FF_TILE = 512
SSM_ROW_TILE = 256
DEC_ROWS = 48
NEG_BIG = -1e30
VMEM_LIMIT = 56 * 1024 * 1024


def _cparams(sem):
    return pltpu.CompilerParams(dimension_semantics=sem, vmem_limit_bytes=VMEM_LIMIT)


def _const_spec(*shape):
    return pl.BlockSpec(shape, lambda *idx: (0,) * len(shape))


def _layer_norm_rows(x, g, b):
    mu = jnp.mean(x, axis=-1, keepdims=True)
    xc = x - mu
    var = jnp.mean(xc * xc, axis=-1, keepdims=True)
    return xc * lax.rsqrt(var + LN_EPS) * g + b


def _head_rms(t, gain, bones):
    sq = t * t
    hi = sq.astype(BF16)
    lo = (sq - hi.astype(F32)).astype(BF16)
    ss = (jnp.dot(hi, bones, preferred_element_type=F32)
          + jnp.dot(lo, bones, preferred_element_type=F32))
    return t * lax.rsqrt(ss * (1.0 / HEAD_DIM) + RMS_EPS) * gain


def _rope_slab(t, cos, sin_signed):
    lane = lax.broadcasted_iota(jnp.int32, t.shape, 1)
    first = (lane % ROPE_AXIS_DIM) < (ROPE_AXIS_DIM // 2)
    partner = jnp.where(first, pltpu.roll(t, LANES - 16, 1), pltpu.roll(t, 16, 1))
    return t * cos + partner * sin_signed


def _embed_body(x_ref, g_ref, b_ref, w_ref, qg_ref, kg_ref, bones_ref, cos_ref, sin_ref,
                h_ref, q_ref, kt_ref, v_ref, u_ref):
    h = _layer_norm_rows(x_ref[0], g_ref[...], b_ref[...])
    h_ref[0] = h
    proj = jnp.dot(h.astype(BF16), w_ref[...], preferred_element_type=F32)
    cos = cos_ref[...]
    sin = sin_ref[...]
    qn = _head_rms(proj[:, :Q_END], qg_ref[...], bones_ref[...])
    for s in range(ATTN_WIDTH // LANES):
        sl = slice(s * LANES, (s + 1) * LANES)
        q_ref[0, :, sl] = (_rope_slab(qn[:, sl], cos, sin) * ATTN_SCALE).astype(BF16)
    kn = _head_rms(proj[:, Q_END:K_END], kg_ref[...], bones_ref[:KV_WIDTH, :KV_WIDTH])
    kt_ref[0, 0] = _rope_slab(kn, cos, sin).T.astype(BF16)
    v_ref[0] = proj[:, K_END:V_END].astype(BF16)
    u_ref[0] = proj[:, V_END:].astype(BF16)


def _embed_tail_body(x_ref, g_ref, b_ref, w_ref, qg_ref, kg_ref, bones_ref, cos_ref, sin_ref,
                     h_in, q_in, v_in, u_in, h_ref, q_ref, kt_ref, v_ref, u_ref):
    del h_in, q_in, v_in, u_in
    _embed_body(x_ref, g_ref, b_ref, w_ref, qg_ref, kg_ref, bones_ref, cos_ref, sin_ref,
                h_ref, q_ref, kt_ref, v_ref, u_ref)


def _embed(x, meta_pad, ln_g, ln_b, w_in, qg, kg, bones, cos, sin, L, Lp):
    B = x.shape[0]
    TM = EMBED_TILE
    nj = L // TM
    w_specs = [_const_spec(1, D_MODEL), _const_spec(1, D_MODEL), _const_spec(D_MODEL, IN_WIDTH),
               _const_spec(1, Q_END), _const_spec(1, KV_WIDTH), _const_spec(Q_END, Q_END)]

    def out_shapes(kt_tiles, kt_width):
        return [
            jax.ShapeDtypeStruct((B, Lp, D_MODEL), F32),
            jax.ShapeDtypeStruct((B, Lp, Q_END), BF16),
            jax.ShapeDtypeStruct((B, kt_tiles, KV_WIDTH, kt_width), BF16),
            jax.ShapeDtypeStruct((B, Lp, KV_WIDTH), BF16),
            jax.ShapeDtypeStruct((B, Lp, SSM_WIDTH), BF16),
        ]

    def out_specs(tm, row_block):
        return [
            pl.BlockSpec((1, tm, D_MODEL), lambda b, j: (b, row_block(j), 0)),
            pl.BlockSpec((1, tm, Q_END), lambda b, j: (b, row_block(j), 0)),
            pl.BlockSpec((1, 1, KV_WIDTH, tm), lambda b, j: (b, j, 0, 0)),
            pl.BlockSpec((1, tm, KV_WIDTH), lambda b, j: (b, row_block(j), 0)),
            pl.BlockSpec((1, tm, SSM_WIDTH), lambda b, j: (b, row_block(j), 0)),
        ]

    main = pl.pallas_call(
        _embed_body,
        grid=(B, nj),
        in_specs=[pl.BlockSpec((1, TM, D_MODEL), lambda b, j: (b, j, 0))] + w_specs
        + [pl.BlockSpec((TM, LANES), lambda b, j: (j, 0))] * 2,
        out_specs=out_specs(TM, lambda j: j),
        out_shape=out_shapes(nj, TM),
        compiler_params=_cparams(("parallel", "parallel")),
        name="embed_main",
    )
    h0, q, kt, v, u = main(x, ln_g, ln_b, w_in, qg, kg, bones, cos, sin)

    jt = Lp // TAIL - 1
    tail = pl.pallas_call(
        _embed_tail_body,
        grid=(B, 1),
        in_specs=[pl.BlockSpec((1, TAIL, D_MODEL), lambda b, j: (0, 0, 0))] + w_specs
        + [pl.BlockSpec((TAIL, LANES), lambda b, j: (jt, 0))] * 2
        + [pl.BlockSpec(memory_space=pl.ANY)] * 4,
        out_specs=out_specs(TAIL, lambda j: jt),
        out_shape=out_shapes(1, TAIL),
        input_output_aliases={9: 0, 10: 1, 11: 3, 12: 4},
        compiler_params=_cparams(("parallel", "arbitrary")),
        name="embed_tail",
    )
    h0, q, kt_tail, v, u = tail(meta_pad, ln_g, ln_b, w_in, qg, kg, bones, cos, sin, h0, q, v, u)
    return h0, q, kt, kt_tail, v, u


def _attn_body(q_ref, kt_ref, ktt_ref, v_ref, o_ref, qs_ref, m_ref, l_ref, acc_ref, o0_ref,
               *, L, n_kv_tiles):
    tq = Q_TILE
    rows = Q_PER_KV * tq
    lane = lax.broadcasted_iota(jnp.int32, (tq, LANES), 1)

    def online_step(s, v_tile):
        m_old = m_ref[...]
        m_new = jnp.maximum(m_old, jnp.max(s, axis=-1, keepdims=True))
        alpha = jnp.exp(m_old - m_new)
        p = jnp.exp(s - m_new)
        l_ref[...] = alpha * l_ref[...] + jnp.sum(p, axis=-1, keepdims=True)
        acc_ref[...] = alpha * acc_ref[...] + jnp.dot(p.astype(BF16), v_tile,
                                                      preferred_element_type=F32)
        m_ref[...] = m_new

    for g in range(N_KV_HEADS):
        in_group = (lane >= g * HEAD_DIM) & (lane < (g + 1) * HEAD_DIM)
        for r in range(Q_PER_KV):
            slab = q_ref[0, :, r * LANES:(r + 1) * LANES]
            qs_ref[r * tq:(r + 1) * tq, :] = jnp.where(in_group, slab, jnp.zeros_like(slab))
        m_ref[...] = jnp.full((rows, 1), NEG_BIG, F32)
        l_ref[...] = jnp.zeros((rows, 1), F32)
        acc_ref[...] = jnp.zeros((rows, LANES), F32)

        def kv_step(c, carry):
            s = jnp.dot(qs_ref[...], kt_ref[0, c], preferred_element_type=F32)
            start = pl.multiple_of(c * KV_TILE, KV_TILE)
            online_step(s, v_ref[0, pl.ds(start, KV_TILE), :])
            return carry

        lax.fori_loop(0, n_kv_tiles, kv_step, 0)

        s = jnp.dot(qs_ref[...], ktt_ref[0, 0], preferred_element_type=F32)
        key = lax.broadcasted_iota(jnp.int32, (rows, TAIL), 1)
        s = jnp.where(key < N_META, s, NEG_BIG)
        online_step(s, v_ref[0, L:L + TAIL, :])

        out = acc_ref[...] / l_ref[...]
        if g == 0:
            o0_ref[...] = out
        else:
            for r in range(Q_PER_KV):
                rs = slice(r * tq, (r + 1) * tq)
                o_ref[0, :, r * LANES:(r + 1) * LANES] = jnp.where(
                    lane < HEAD_DIM, o0_ref[rs, :], out[rs, :]).astype(BF16)


def _attention(q, kt, kt_tail, v, L, Lp):
    B = q.shape[0]
    nkt = L // KV_TILE
    rows = Q_PER_KV * Q_TILE
    return pl.pallas_call(
        functools.partial(_attn_body, L=L, n_kv_tiles=nkt),
        grid=(B, Lp // Q_TILE),
        in_specs=[
            pl.BlockSpec((1, Q_TILE, Q_END), lambda b, j: (b, j, 0)),
            pl.BlockSpec((1, nkt, KV_WIDTH, KV_TILE), lambda b, j: (b, 0, 0, 0)),
            pl.BlockSpec((1, 1, KV_WIDTH, TAIL), lambda b, j: (b, 0, 0, 0)),
            pl.BlockSpec((1, Lp, KV_WIDTH), lambda b, j: (b, 0, 0)),
        ],
        out_specs=pl.BlockSpec((1, Q_TILE, Q_END), lambda b, j: (b, j, 0)),
        out_shape=jax.ShapeDtypeStruct((B, Lp, Q_END), BF16),
        scratch_shapes=[
            pltpu.VMEM((rows, LANES), BF16),
            pltpu.VMEM((rows, 1), F32),
            pltpu.VMEM((rows, 1), F32),
            pltpu.VMEM((rows, LANES), F32),
            pltpu.VMEM((rows, LANES), F32),
        ],
        compiler_params=_cparams(("parallel", "arbitrary")),
        name="attention",
    )(q, kt, kt_tail, v)


def _ssm_weights(lam_re, lam_im, log_dt, b_re, b_im, c_re, c_im, d_skip):
    hp = lax.Precision.HIGHEST
    dt = jnp.exp(log_dt.astype(F32))[..., None]
    lr = lam_re.astype(F32)
    li = lam_im.astype(F32)

    def apow(n):
        n = jnp.asarray(n, F32)
        mag = jnp.exp(lr[..., None] * dt[..., None] * n)
        ang = li[..., None] * dt[..., None] * n
        return mag * jnp.cos(ang), mag * jnp.sin(ang)

    a1r, a1i = apow(jnp.ones((1,), F32))
    a1r, a1i = a1r[..., 0], a1i[..., 0]
    nr = a1r - 1.0
    den = lr * lr + li * li
    f_r = (nr * lr + a1i * li) / den
    f_i = (a1i * lr - nr * li) / den
    br = b_re.astype(F32)
    bi = b_im.astype(F32)
    bb_r = f_r[..., None] * br - f_i[..., None] * bi
    bb_i = f_r[..., None] * bi + f_i[..., None] * br
    cr = c_re.astype(F32)
    ci = c_im.astype(F32)

    tau = jnp.arange(CHUNK + 1, dtype=F32)
    pr, pi = apow(tau)
    ab_r = pr[..., None] * bb_r[:, :, :, None, :] - pi[..., None] * bb_i[:, :, :, None, :]
    ab_i = pr[..., None] * bb_i[:, :, :, None, :] + pi[..., None] * bb_r[:, :, :, None, :]
    kern = (jnp.einsum("dgop,dgptc->dgtco", cr, ab_r, precision=hp)
            - jnp.einsum("dgop,dgptc->dgtco", ci, ab_i, precision=hp))
    s_idx = jnp.arange(CHUNK)[:, None]
    t_idx = jnp.arange(CHUNK)[None, :]
    lag_f = jnp.clip(t_idx - s_idx, 0, CHUNK)
    lag_b = jnp.clip(s_idx - t_idx, 0, CHUNK)
    m_f = jnp.where((t_idx >= s_idx)[..., None, None], kern[0][:, lag_f], 0.0)
    m_b = jnp.where((s_idx >= t_idx)[..., None, None], kern[1][:, lag_b], 0.0)
    eye_t = jnp.eye(CHUNK, dtype=F32)[None, :, :, None, None]
    eye_c = jnp.eye(SSM_GROUP, dtype=F32)[None, None, None]
    dsk = d_skip.astype(F32).reshape(N_SSM_GROUPS, 1, 1, SSM_GROUP, 1)
    m_all = m_f + m_b + eye_t * eye_c * dsk
    m_all = m_all.transpose(0, 1, 3, 2, 4).reshape(N_SSM_GROUPS, 256, 256)

    def state_in(d, expo):
        er, ei = pr[d][..., expo], pi[d][..., expo]
        wr = er[..., None] * bb_r[d][:, :, None, :] - ei[..., None] * bb_i[d][:, :, None, :]
        wi = er[..., None] * bb_i[d][:, :, None, :] + ei[..., None] * bb_r[d][:, :, None, :]
        to_rows = lambda w: w.transpose(0, 2, 3, 1).reshape(N_SSM_GROUPS, 256, SSM_STATE)
        return to_rows(wr), to_rows(wi)

    sf_r, sf_i = state_in(0, jnp.arange(CHUNK - 1, -1, -1))
    sb_r, sb_i = state_in(1, jnp.arange(CHUNK))

    def state_out(d, expo):
        er, ei = pr[d][..., expo], pi[d][..., expo]
        wr = cr[d].transpose(0, 2, 1)[:, :, None, :] * er[..., None] \
            - ci[d].transpose(0, 2, 1)[:, :, None, :] * ei[..., None]
        wi = cr[d].transpose(0, 2, 1)[:, :, None, :] * ei[..., None] \
            + ci[d].transpose(0, 2, 1)[:, :, None, :] * er[..., None]
        flat = lambda w: w.reshape(N_SSM_GROUPS, SSM_STATE, 256)
        return flat(wr), flat(-wi)

    of_r, of_i = state_out(0, jnp.arange(1, CHUNK + 1))
    ob_r, ob_i = state_out(1, jnp.arange(CHUNK, 0, -1))

    def pair_rows(w):
        return w.reshape(N_PAIRS, 2, *w.shape[1:])

    z256 = jnp.zeros((N_PAIRS, 256, 256), F32)
    mp = pair_rows(m_all)
    w1_y = jnp.concatenate([jnp.concatenate([mp[:, 0], z256], axis=2),
                            jnp.concatenate([z256, mp[:, 1]], axis=2)], axis=1)

    def pair_cols_in(w):
        wp = pair_rows(w)
        z = jnp.zeros_like(wp[:, 0])
        return jnp.concatenate([jnp.concatenate([wp[:, 0], z], axis=2),
                                jnp.concatenate([z, wp[:, 1]], axis=2)], axis=1)

    w1 = jnp.concatenate([w1_y, pair_cols_in(sf_r), pair_cols_in(sf_i),
                          pair_cols_in(sb_r), pair_cols_in(sb_i)], axis=2)

    def pair_rows_out(w):
        wp = pair_rows(w)
        z = jnp.zeros_like(wp[:, 0])
        return jnp.concatenate([jnp.concatenate([wp[:, 0], z], axis=2),
                                jnp.concatenate([z, wp[:, 1]], axis=2)], axis=1)

    w2 = jnp.concatenate([pair_rows_out(of_r), pair_rows_out(of_i),
                          pair_rows_out(ob_r), pair_rows_out(ob_i)], axis=1)

    qr, qi = apow(CHUNK * jnp.arange(SUBLANES + 1, dtype=F32))
    zero = jnp.zeros_like(qr[..., 0])

    def dec_rows(d, q_expo):
        rows = [qr[d][..., 1], qi[d][..., 1], qr[d][..., 2], qi[d][..., 2],
                qr[d][..., 4], qi[d][..., 4], zero[d], zero[d]]
        rows += [qr[d][..., n] for n in q_expo] + [qi[d][..., n] for n in q_expo]
        return jnp.stack(rows, axis=1)

    dec = jnp.concatenate([dec_rows(0, range(1, SUBLANES + 1)),
                           dec_rows(1, range(SUBLANES, 0, -1))], axis=1)
    dec = dec.reshape(N_PAIRS, 2, DEC_ROWS, SSM_STATE).transpose(0, 2, 1, 3)
    dec = dec.reshape(N_PAIRS, DEC_ROWS, PAIR_STATE)
    return w1.astype(BF16), w2.astype(BF16), dec


def _ssm_body(u_ref, w1_ref, w2_ref, dec_ref, z_ref, sh_ref, *, B, n_chunks):
    R = B * n_chunks
    tiles = [(r0, min(SSM_ROW_TILE, R - r0)) for r0 in range(0, R, SSM_ROW_TILE)]
    PS = PAIR_STATE
    n_blocks = n_chunks // SUBLANES

    for r0, n in tiles:
        t = jnp.dot(u_ref[0, r0:r0 + n, :], w1_ref[0], preferred_element_type=F32)
        z_ref[0, r0:r0 + n, :] = t[:, :PAIR_W]
        sh_ref[r0:r0 + n, :] = t[:, PAIR_W:]

    sub = lax.broadcasted_iota(jnp.int32, (SUBLANES, PS), 0)

    def cmul(ar, ai, xr, xi):
        return ar * xr - ai * xi, ar * xi + ai * xr

    def block_scan(sr, si, cr, ci, base, reverse):
        xr, xi = sr, si
        for k, d in enumerate((1, 2, 4)):
            pr_ = dec_ref[0, base + 2 * k:base + 2 * k + 1, :]
            pi_ = dec_ref[0, base + 2 * k + 1:base + 2 * k + 2, :]
            keep = (sub < SUBLANES - d) if reverse else (sub >= d)
            shift = SUBLANES - d if reverse else d
            yr = jnp.where(keep, pltpu.roll(xr, shift, 0), 0.0)
            yi = jnp.where(keep, pltpu.roll(xi, shift, 0), 0.0)
            mr, mi = cmul(pr_, pi_, yr, yi)
            xr, xi = xr + mr, xi + mi
        qr_ = dec_ref[0, base + 8:base + 16, :]
        qi_ = dec_ref[0, base + 16:base + 24, :]
        mr, mi = cmul(qr_, qi_, cr, ci)
        xr, xi = xr + mr, xi + mi
        edge = SUBLANES - 1 if reverse else 0
        shift = SUBLANES - 1 if reverse else 1
        er = jnp.where(sub == edge, cr, pltpu.roll(xr, shift, 0))
        ei = jnp.where(sub == edge, ci, pltpu.roll(xi, shift, 0))
        last = 0 if reverse else SUBLANES - 1
        nr = jnp.broadcast_to(xr[last:last + 1, :], (SUBLANES, PS))
        ni = jnp.broadcast_to(xi[last:last + 1, :], (SUBLANES, PS))
        return er, ei, nr, ni

    def scan_step(i, carry):
        new = []
        for b in range(B):
            cfr, cfi, cbr, cbi = carry[4 * b:4 * b + 4]
            rf = pl.multiple_of(b * n_chunks + i * SUBLANES, SUBLANES)
            rb = pl.multiple_of(b * n_chunks + (n_blocks - 1 - i) * SUBLANES, SUBLANES)
            f_rows, b_rows = pl.ds(rf, SUBLANES), pl.ds(rb, SUBLANES)
            er, ei, cfr, cfi = block_scan(sh_ref[f_rows, 0:PS], sh_ref[f_rows, PS:2 * PS],
                                          cfr, cfi, 0, False)
            sh_ref[f_rows, 0:PS] = er
            sh_ref[f_rows, PS:2 * PS] = ei
            er, ei, cbr, cbi = block_scan(sh_ref[b_rows, 2 * PS:3 * PS], sh_ref[b_rows, 3 * PS:4 * PS],
                                          cbr, cbi, DEC_ROWS // 2, True)
            sh_ref[b_rows, 2 * PS:3 * PS] = er
            sh_ref[b_rows, 3 * PS:4 * PS] = ei
            new += [cfr, cfi, cbr, cbi]
        return tuple(new)

    zero = jnp.zeros((SUBLANES, PS), F32)
    lax.fori_loop(0, n_blocks, scan_step, (zero,) * (4 * B))

    for r0, n in tiles:
        y = z_ref[0, r0:r0 + n, :] + jnp.dot(sh_ref[r0:r0 + n, :].astype(BF16), w2_ref[0],
                                             preferred_element_type=F32)
        z_ref[0, r0:r0 + n, :] = jax.nn.gelu(y)


def _ssm(u_pairs, w1, w2, dec, B, n_chunks):
    assert n_chunks % SUBLANES == 0
    R = B * n_chunks
    return pl.pallas_call(
        functools.partial(_ssm_body, B=B, n_chunks=n_chunks),
        grid=(N_PAIRS,),
        in_specs=[
            pl.BlockSpec((1, R, PAIR_W), lambda p: (p, 0, 0)),
            pl.BlockSpec((1, PAIR_W, 2 * PAIR_W), lambda p: (p, 0, 0)),
            pl.BlockSpec((1, PAIR_W, PAIR_W), lambda p: (p, 0, 0)),
            pl.BlockSpec((1, DEC_ROWS, PAIR_STATE), lambda p: (p, 0, 0)),
        ],
        out_specs=pl.BlockSpec((1, R, PAIR_W), lambda p: (p, 0, 0)),
        out_shape=jax.ShapeDtypeStruct((N_PAIRS, R, PAIR_W), F32),
        scratch_shapes=[pltpu.VMEM((R, 4 * PAIR_STATE), F32)],
        compiler_params=_cparams(("parallel",)),
        name="ssm",
    )(u_pairs, w1, w2, dec)


def _to_chunk_pairs(u, L):
    B, Lp, _ = u.shape
    nc, ncp = L // CHUNK, Lp // CHUNK
    uc = u.reshape(B, ncp, CHUNK, N_PAIRS, 2, SSM_GROUP)
    seq = jnp.concatenate([uc[:, nc:nc + 1], uc[:, :nc],
                           jnp.zeros((B, ncp - nc - 1) + uc.shape[2:], u.dtype)], axis=1)
    return seq.transpose(3, 0, 1, 4, 2, 5).reshape(N_PAIRS, B * ncp, PAIR_W)


def _from_chunk_pairs(z, B, L, Lp):
    nc, ncp = L // CHUNK, Lp // CHUNK
    zc = z.reshape(N_PAIRS, B, ncp, 2, CHUNK, SSM_GROUP)
    seq = zc.transpose(1, 2, 4, 0, 3, 5)
    back = jnp.concatenate([seq[:, 1:nc + 1], seq[:, 0:1], seq[:, nc + 1:]], axis=1)
    return back.reshape(B, Lp, SSM_WIDTH)


def _mix_body(o_ref, z_ref, h0_ref, wglu_ref, bglu_ref, woa_ref, wos_ref, g_ref, b_ref, wr_ref,
              h1_ref, acc_ref, aff_ref, *, L, Lp):
    z = z_ref[...]
    gate = jax.nn.sigmoid(jnp.dot(z.astype(BF16), wglu_ref[...], preferred_element_type=F32)
                          + bglu_ref[...])
    ssm_out = (z * gate).astype(BF16)
    mix = (jnp.dot(o_ref[...], woa_ref[...], preferred_element_type=F32)
           + jnp.dot(ssm_out, wos_ref[...], preferred_element_type=F32))
    h1 = _layer_norm_rows(DEEPNORM_ALPHA * h0_ref[...] + mix, g_ref[...], b_ref[...])
    h1_ref[...] = h1
    acc_ref[...] = DEEPNORM_ALPHA * h1
    logits = jnp.dot(h1.astype(BF16), wr_ref[...], preferred_element_type=F32)
    lane = lax.broadcasted_iota(jnp.int32, logits.shape, 1)
    logits = jnp.where(lane < N_EXPERTS, logits, NEG_BIG)
    e = jnp.exp(logits - jnp.max(logits, axis=-1, keepdims=True))
    aff = (e / jnp.sum(e, axis=-1, keepdims=True)).T[:N_EXPERTS, :]
    tiles_per_seq = Lp // ROW_TILE
    pos = (pl.program_id(0) % tiles_per_seq) * ROW_TILE + lax.broadcasted_iota(
        jnp.int32, aff.shape, 1)
    aff_ref[...] = jnp.where(pos < L + N_META, aff, -1.0)


def _mix(o, z, h0, wglu, bglu, woa, wos, g, b, wr, L, Lp):
    N = o.shape[0]
    TM = ROW_TILE
    row = lambda w: pl.BlockSpec((TM, w), lambda i: (i, 0))
    return pl.pallas_call(
        functools.partial(_mix_body, L=L, Lp=Lp),
        grid=(N // TM,),
        in_specs=[row(ATTN_WIDTH), row(SSM_WIDTH), row(D_MODEL),
                  _const_spec(SSM_WIDTH, SSM_WIDTH), _const_spec(1, SSM_WIDTH),
                  _const_spec(ATTN_WIDTH, D_MODEL), _const_spec(SSM_WIDTH, D_MODEL),
                  _const_spec(1, D_MODEL), _const_spec(1, D_MODEL), _const_spec(D_MODEL, LANES)],
        out_specs=[row(D_MODEL), row(D_MODEL), pl.BlockSpec((N_EXPERTS, TM), lambda i: (0, i))],
        out_shape=[jax.ShapeDtypeStruct((N, D_MODEL), F32),
                   jax.ShapeDtypeStruct((N, D_MODEL), F32),
                   jax.ShapeDtypeStruct((N_EXPERTS, N), F32)],
        compiler_params=_cparams(("parallel",)),
        name="mix",
    )(o, z, h0, wglu, bglu, woa, wos, g, b, wr)


def _select_body(aff_ref, ord_ref, idx_ref, *, capacity, n_slot_tiles):
    nbp = aff_ref.shape[1]
    ri = lax.broadcasted_iota(jnp.int32, (LANES, LANES), 0)
    ci = lax.broadcasted_iota(jnp.int32, (LANES, LANES), 1)
    strict_upper = (ri < ci).astype(BF16)
    incl_upper = (ri <= ci).astype(BF16)
    ones = jnp.ones((LANES, LANES), BF16)
    order = ord_ref[...]
    blk = lax.broadcasted_iota(jnp.int32, (nbp, LANES), 0).astype(F32)
    blk_hi = jnp.floor(blk * (1.0 / 16.0))
    blk_lo = blk - 16.0 * blk_hi
    dot = functools.partial(jnp.dot, preferred_element_type=F32)
    dot_nt = functools.partial(lax.dot_general, dimension_numbers=(((1,), (1,)), ((), ())),
                               preferred_element_type=F32)

    def per_expert(e, carry):
        bits = pltpu.bitcast(aff_ref[e], jnp.int32)
        t = jnp.zeros((1, 1), jnp.int32)
        for bit in range(30, -1, -1):
            cand = t | (1 << bit)
            cnt = jnp.sum((bits >= cand).astype(jnp.int32), keepdims=True)
            t = jnp.where(cnt >= capacity, cand, t)
        gt = bits > t
        eq = bits == t
        need = (capacity - jnp.sum(gt.astype(jnp.int32), keepdims=True)).astype(F32)
        eqb = eq.astype(BF16)
        tie_rank = dot(order, dot(eqb, ones).astype(BF16)) + dot(eqb, strict_upper)
        sel = (gt | (eq & (tie_rank < need))).astype(BF16)
        csum_in_blk = dot(sel, incl_upper)
        blk_tot = dot(sel, ones)
        blk_off = dot(order, blk_tot.astype(BF16))
        off_hi = jnp.floor(blk_off * (1.0 / 64.0))
        off_lo = blk_off - 64.0 * off_hi
        table = jnp.concatenate([csum_in_blk, off_hi, off_lo, blk_hi, blk_lo],
                                axis=1).astype(BF16)
        tot_l = dot_nt(ones[:SUBLANES], sel)
        off_l = dot_nt(tot_l.astype(BF16), order)[0:1, :]
        tot_l = tot_l[0:1, :]

        def per_tile(st, carry2):
            j = (st * SLOT_TILE + lax.broadcasted_iota(jnp.int32, (SLOT_TILE, 1), 0)).astype(F32)
            onehot = ((off_l <= j) & (j < off_l + tot_l)).astype(BF16)
            got = dot(onehot, table)
            local = j - (64.0 * got[:, LANES:2 * LANES] + got[:, 2 * LANES:3 * LANES])
            lane_idx = dot((got[:, :LANES] <= local).astype(BF16), ones)
            block = 16.0 * got[:, 3 * LANES:4 * LANES] + got[:, 4 * LANES:5 * LANES]
            tok = jnp.where(j < capacity, block * LANES + lane_idx, 0.0)
            idx_ref[e, st] = tok.T[0:1, :].astype(jnp.int32)
            return carry2

        lax.fori_loop(0, n_slot_tiles, per_tile, 0)
        return carry

    lax.fori_loop(0, N_EXPERTS, per_expert, 0)


def _select(aff_blocks, order, capacity, n_slot_tiles):
    nbp = aff_blocks.shape[1]
    return pl.pallas_call(
        functools.partial(_select_body, capacity=capacity, n_slot_tiles=n_slot_tiles),
        grid=(1,),
        in_specs=[_const_spec(N_EXPERTS, nbp, LANES), _const_spec(nbp, nbp)],
        out_specs=_const_spec(N_EXPERTS, n_slot_tiles, 1, SLOT_TILE),
        out_shape=jax.ShapeDtypeStruct((N_EXPERTS, n_slot_tiles, 1, SLOT_TILE), jnp.int32),
        compiler_params=_cparams(("arbitrary",)),
        name="select",
    )(aff_blocks, order)


def _row_copy(src_hbm, dst_vmem, sem, row, slot):
    return pltpu.make_async_copy(src_hbm.at[pl.ds(row, 1)], dst_vmem.at[pl.ds(slot, 1)], sem)


def _ffn_body(idx_ref, h1_hbm, wg_ref, wu_ref, wd_ref, wr_ref, ye_ref, xbuf, sem):
    T = SLOT_TILE

    def start(j, c):
        _row_copy(h1_hbm, xbuf, sem, idx_ref[0, 0, 0, j], j).start()
        return c

    def wait(j, c):
        _row_copy(h1_hbm, xbuf, sem, 0, j).wait()
        return c

    lax.fori_loop(0, T, start, 0)
    lax.fori_loop(0, T, wait, 0)
    x = xbuf[...].astype(BF16)
    logits = jnp.dot(x, wr_ref[...], preferred_element_type=F32)
    lane = lax.broadcasted_iota(jnp.int32, logits.shape, 1)
    logits = jnp.where(lane < N_EXPERTS, logits, NEG_BIG)
    p = jnp.exp(logits - jnp.max(logits, axis=-1, keepdims=True))
    gate = (jnp.sum(jnp.where(lane == pl.program_id(0), p, 0.0), axis=-1, keepdims=True)
            / jnp.sum(p, axis=-1, keepdims=True))
    y = jnp.zeros((T, D_MODEL), F32)
    for f in range(EXPERT_FF // FF_TILE):
        fs = slice(f * FF_TILE, (f + 1) * FF_TILE)
        hg = jnp.dot(x, wg_ref[0, :, fs], preferred_element_type=F32)
        hu = jnp.dot(x, wu_ref[0, :, fs], preferred_element_type=F32)
        hid = (jax.nn.silu(hg) * hu).astype(BF16)
        y = y + jnp.dot(hid, wd_ref[0, fs, :], preferred_element_type=F32)
    ye_ref[0] = y * gate


def _ffn(idx, h1, wg, wu, wd, wr, n_slot_tiles):
    T = SLOT_TILE
    return pl.pallas_call(
        _ffn_body,
        grid=(N_EXPERTS, n_slot_tiles),
        in_specs=[
            pl.BlockSpec((1, 1, 1, T), lambda e, c: (e, c, 0, 0), memory_space=pltpu.SMEM),
            pl.BlockSpec(memory_space=pl.ANY),
            pl.BlockSpec((1, D_MODEL, EXPERT_FF), lambda e, c: (e, 0, 0)),
            pl.BlockSpec((1, D_MODEL, EXPERT_FF), lambda e, c: (e, 0, 0)),
            pl.BlockSpec((1, EXPERT_FF, D_MODEL), lambda e, c: (e, 0, 0)),
            _const_spec(D_MODEL, LANES),
        ],
        out_specs=pl.BlockSpec((1, T, D_MODEL), lambda e, c: (e, c, 0)),
        out_shape=jax.ShapeDtypeStruct((N_EXPERTS, n_slot_tiles * T, D_MODEL), F32),
        scratch_shapes=[pltpu.VMEM((T, D_MODEL), F32), pltpu.SemaphoreType.DMA(())],
        compiler_params=_cparams(("arbitrary", "arbitrary")),
        name="expert_ffn",
    )(idx, h1, wg, wu, wd, wr)


def _scatter_body(idx_ref, ye_ref, acc_in, acc_hbm, abuf, sem, *, capacity):
    del acc_in
    T = SLOT_TILE
    n_valid = jnp.clip(capacity - pl.program_id(1) * T, 0, T)

    def gather_start(j, c):
        _row_copy(acc_hbm, abuf, sem, idx_ref[0, 0, 0, j], j).start()
        return c

    def gather_wait(j, c):
        _row_copy(acc_hbm, abuf, sem, 0, j).wait()
        return c

    def scatter_copy(j, row):
        return pltpu.make_async_copy(abuf.at[pl.ds(j, 1)], acc_hbm.at[pl.ds(row, 1)], sem)

    def scatter_start(j, c):
        scatter_copy(j, idx_ref[0, 0, 0, j]).start()
        return c

    def scatter_wait(j, c):
        scatter_copy(j, 0).wait()
        return c

    lax.fori_loop(0, n_valid, gather_start, 0)
    lax.fori_loop(0, n_valid, gather_wait, 0)
    abuf[...] = abuf[...] + ye_ref[0]
    lax.fori_loop(0, n_valid, scatter_start, 0)
    lax.fori_loop(0, n_valid, scatter_wait, 0)


def _scatter(idx, ye, acc, capacity, n_slot_tiles):
    T = SLOT_TILE
    return pl.pallas_call(
        functools.partial(_scatter_body, capacity=capacity),
        grid=(N_EXPERTS, n_slot_tiles),
        in_specs=[
            pl.BlockSpec((1, 1, 1, T), lambda e, c: (e, c, 0, 0), memory_space=pltpu.SMEM),
            pl.BlockSpec((1, T, D_MODEL), lambda e, c: (e, c, 0)),
            pl.BlockSpec(memory_space=pl.ANY),
        ],
        out_specs=pl.BlockSpec(memory_space=pl.ANY),
        out_shape=jax.ShapeDtypeStruct(acc.shape, F32),
        scratch_shapes=[pltpu.VMEM((T, D_MODEL), F32), pltpu.SemaphoreType.DMA(())],
        input_output_aliases={2: 0},
        compiler_params=_cparams(("arbitrary", "arbitrary")),
        name="expert_scatter",
    )(idx, ye, acc)


def _final_body(a_ref, g_ref, b_ref, o_ref):
    o_ref[0] = _layer_norm_rows(a_ref[0], g_ref[...], b_ref[...])


def _final_norm(acc, g, b, L):
    B = acc.shape[0]
    TM = EMBED_TILE
    return pl.pallas_call(
        _final_body,
        grid=(B, L // TM),
        in_specs=[pl.BlockSpec((1, TM, D_MODEL), lambda b_, j: (b_, j, 0)),
                  _const_spec(1, D_MODEL), _const_spec(1, D_MODEL)],
        out_specs=pl.BlockSpec((1, TM, D_MODEL), lambda b_, j: (b_, j, 0)),
        out_shape=jax.ShapeDtypeStruct((B, L, D_MODEL), F32),
        compiler_params=_cparams(("parallel", "parallel")),
        name="final_norm",
    )(acc, g, b)


def _rope_tables(L, Lp):
    rows = L // GRID_W
    t = jnp.arange(L, dtype=jnp.int32)
    m = jnp.arange(N_META, dtype=jnp.int32)
    pad = jnp.zeros((Lp - L - N_META,), jnp.int32)
    row = jnp.concatenate([t // GRID_W - rows // 2, jnp.full((N_META,), -(rows // 2) - 1, jnp.int32),
                           pad])
    col = jnp.concatenate([t % GRID_W - GRID_W // 2, m - GRID_W // 2, pad])
    inv_freq = ROPE_THETA ** (-jnp.arange(0, ROPE_AXIS_DIM, 2, dtype=F32) / ROPE_AXIS_DIM)
    ang_r = row.astype(F32)[:, None] * inv_freq
    ang_c = col.astype(F32)[:, None] * inv_freq
    cos = jnp.concatenate([jnp.cos(ang_r)] * 2 + [jnp.cos(ang_c)] * 2, axis=1)
    sin = jnp.concatenate([-jnp.sin(ang_r), jnp.sin(ang_r), -jnp.sin(ang_c), jnp.sin(ang_c)], axis=1)
    return jnp.tile(cos, (1, 2)), jnp.tile(sin, (1, 2))


def _block_order(B, Lp, nbp):
    nbb = Lp // LANES
    r = jnp.arange(nbp)
    b, jb = r // nbb, r % nbb
    rank = jnp.where(r < B * nbb, b * nbb + jnp.where(jb == nbb - 1, 0, jb + 1), r)
    return (rank[None, :] < rank[:, None]).astype(BF16)


def _run_trunk(x, meta_pad, shared):
    B, L, _ = x.shape
    Lp = L + TAIL
    N = B * Lp
    cos, sin = _rope_tables(L, Lp)
    h0, q, kt, kt_tail, v, u = _embed(x, meta_pad, shared["ln_emb_g"], shared["ln_emb_b"],
                                      shared["w_in"], shared["qg"], shared["kg"], shared["bones"],
                                      cos, sin, L, Lp)
    o = _attention(q, kt, kt_tail, v, L, Lp)
    z = _ssm(_to_chunk_pairs(u, L), shared["ssm_w1"], shared["ssm_w2"], shared["ssm_dec"],
             B, Lp // CHUNK)
    z = _from_chunk_pairs(z, B, L, Lp)
    h1, acc, aff = _mix(o.reshape(N, ATTN_WIDTH), z.reshape(N, SSM_WIDTH), h0.reshape(N, D_MODEL),
                        shared["w_glu"], shared["b_glu"], shared["w_out_attn"], shared["w_out_ssm"],
                        shared["ln1_g"], shared["ln1_b"], shared["w_router"], L, Lp)

    capacity = EC_CAPACITY_FACTOR * B * (L + N_META) // N_EXPERTS
    n_slot_tiles = -(-capacity // SLOT_TILE)
    nb = N // LANES
    nbp = -(-nb // LANES) * LANES
    aff_blocks = jnp.pad(aff.reshape(N_EXPERTS, nb, LANES), ((0, 0), (0, nbp - nb), (0, 0)),
                         constant_values=-1.0)
    idx = _select(aff_blocks, _block_order(B, Lp, nbp), capacity, n_slot_tiles)
    ye = _ffn(idx, h1, shared["w_gate"], shared["w_up"], shared["w_down"], shared["w_router"],
              n_slot_tiles)
    acc = _scatter(idx, ye, acc, capacity, n_slot_tiles)
    return _final_norm(acc.reshape(B, Lp, D_MODEL), shared["ln2_g"], shared["ln2_b"], L)


def kernel(x_prompt, x_sample, meta_tokens, ln_emb_g, ln_emb_b, w_in, q_norm_g, k_norm_g, ssm_lambda_re, ssm_lambda_im, ssm_log_dt, ssm_b_re, ssm_b_im, ssm_c_re, ssm_c_im, ssm_d, w_glu, b_glu, w_out, ln1_g, ln1_b, w_router, w_gate, w_up, w_down, ln2_g, ln2_b):
    row = lambda a: a.reshape(1, -1).astype(F32)
    w_q = w_in[0][:, :Q_END].reshape(D_MODEL, N_KV_HEADS, Q_PER_KV, HEAD_DIM)
    w_q = w_q.transpose(0, 2, 1, 3).reshape(D_MODEL, Q_END)
    w_oa = w_out[0][:ATTN_WIDTH].reshape(N_KV_HEADS, Q_PER_KV, HEAD_DIM, D_MODEL)
    w_oa = w_oa.transpose(1, 0, 2, 3).reshape(ATTN_WIDTH, D_MODEL)
    head_of = jnp.arange(Q_END) // HEAD_DIM
    ssm_w1, ssm_w2, ssm_dec = _ssm_weights(ssm_lambda_re[0], ssm_lambda_im[0], ssm_log_dt[0],
                                           ssm_b_re[0], ssm_b_im[0], ssm_c_re[0], ssm_c_im[0],
                                           ssm_d[0])
    shared = dict(
        ln_emb_g=row(ln_emb_g), ln_emb_b=row(ln_emb_b),
        w_in=jnp.concatenate([w_q, w_in[0][:, Q_END:]], axis=1).astype(BF16),
        qg=row(jnp.tile(q_norm_g[0], N_Q_HEADS)), kg=row(jnp.tile(k_norm_g[0], N_KV_HEADS)),
        bones=(head_of[:, None] == head_of[None, :]).astype(BF16),
        ssm_w1=ssm_w1, ssm_w2=ssm_w2, ssm_dec=ssm_dec,
        w_glu=w_glu[0].astype(BF16), b_glu=row(b_glu[0]),
        w_out_attn=w_oa.astype(BF16), w_out_ssm=w_out[0][ATTN_WIDTH:].astype(BF16),
        ln1_g=row(ln1_g[0]), ln1_b=row(ln1_b[0]),
        w_router=jnp.pad(w_router[0], ((0, 0), (0, LANES - N_EXPERTS))).astype(BF16),
        w_gate=w_gate[0].astype(BF16), w_up=w_up[0].astype(BF16), w_down=w_down[0].astype(BF16),
        ln2_g=row(ln2_g[0]), ln2_b=row(ln2_b[0]),
    )
    meta_pad = jnp.pad(meta_tokens.astype(F32), ((0, TAIL - N_META), (0, 0)))[None]
    return (_run_trunk(x_prompt, meta_pad, shared), _run_trunk(x_sample, meta_pad, shared))
```

```python
import functools
import math

import jax
import jax.numpy as jnp
from jax import lax
from jax.experimental import pallas as pl
from jax.experimental.pallas import tpu as pltpu

F32 = jnp.float32
BF16 = jnp.bfloat16

D_MODEL = 1024
N_META = 16
GRID_W = 64
ATTN_WIDTH = 512
SSM_WIDTH = 512
HEAD_DIM = 64
N_Q_HEADS = 8
N_KV_HEADS = 2
Q_PER_KV = 4
KV_WIDTH = 128
ROPE_AXIS_DIM = 32
ROPE_THETA = 10000.0
ATTN_SCALE = HEAD_DIM ** -0.5
RMS_EPS = 1e-6
SSM_GROUP = 16
N_SSM_GROUPS = 32
SSM_STATE = 64
Q_END = ATTN_WIDTH
K_END = Q_END + KV_WIDTH
V_END = K_END + KV_WIDTH
IN_WIDTH = V_END + SSM_WIDTH
N_EXPERTS = 16
EXPERT_FF = 2048
EC_CAPACITY_FACTOR = 2
LN_EPS = 1e-5
DEPTH = 1
DEEPNORM_ALPHA = (2 * DEPTH) ** 0.25

LANES = 128
SUBLANES = 8
TAIL = LANES
PAD_ROWS = TAIL - N_META
CHUNK = 16
N_PAIRS = N_SSM_GROUPS // 2
PAIR_W = 2 * CHUNK * SSM_GROUP
PAIR_STATE = 2 * SSM_STATE
EMBED_TILE = 512
ROW_TILE = 384
Q_TILE = 128
KV_TILE = EMBED_TILE
SLOT_TILE = 384
FF_TILE = 512
SSM_ROW_TILE = 256
DEC_ROWS = 48
Q_PRESCALE = ATTN_SCALE * math.log2(math.e)
NEG_BIG = -1e30
VMEM_LIMIT = 56 * 1024 * 1024


def _cparams(sem):
    return pltpu.CompilerParams(dimension_semantics=sem, vmem_limit_bytes=VMEM_LIMIT)


def _const_spec(*shape):
    return pl.BlockSpec(shape, lambda *idx: (0,) * len(shape))


def _layer_norm_rows(x, g, b):
    mu = jnp.mean(x, axis=-1, keepdims=True)
    xc = x - mu
    var = jnp.mean(xc * xc, axis=-1, keepdims=True)
    return xc * lax.rsqrt(var + LN_EPS) * g + b


def _head_rms(t, gain, bones):
    sq = t * t
    hi = sq.astype(BF16)
    lo = (sq - hi.astype(F32)).astype(BF16)
    ss = (jnp.dot(hi, bones, preferred_element_type=F32)
          + jnp.dot(lo, bones, preferred_element_type=F32))
    return t * lax.rsqrt(ss * (1.0 / HEAD_DIM) + RMS_EPS) * gain


def _rope_slab(t, cos, sin_signed):
    lane = lax.broadcasted_iota(jnp.int32, t.shape, 1)
    first = (lane % ROPE_AXIS_DIM) < (ROPE_AXIS_DIM // 2)
    partner = jnp.where(first, pltpu.roll(t, LANES - 16, 1), pltpu.roll(t, 16, 1))
    return t * cos + partner * sin_signed


def _embed_body(x_ref, g_ref, b_ref, w_ref, qg_ref, kg_ref, bones_ref, cos_ref, sin_ref,
                h_ref, q_ref, k_ref, vt0_ref, vt1_ref, u_ref):
    h = _layer_norm_rows(x_ref[0], g_ref[...], b_ref[...])
    h_ref[0] = h
    proj = jnp.dot(h.astype(BF16), w_ref[...], preferred_element_type=F32)
    cos = cos_ref[...]
    sin = sin_ref[...]
    qn = _head_rms(proj[:, :Q_END], qg_ref[...], bones_ref[...])
    for s in range(ATTN_WIDTH // LANES):
        sl = slice(s * LANES, (s + 1) * LANES)
        q_ref[0, :, sl] = (_rope_slab(qn[:, sl], cos, sin) * Q_PRESCALE).astype(BF16)
    kn = _head_rms(proj[:, Q_END:K_END], kg_ref[...], bones_ref[:KV_WIDTH, :KV_WIDTH])
    k_ref[0] = _rope_slab(kn, cos, sin).astype(BF16)
    vt = proj[:, K_END:V_END].T
    row = lax.broadcasted_iota(jnp.int32, vt.shape, 0)
    vt0_ref[0, 0] = jnp.where(row < HEAD_DIM, vt, 1.0).astype(BF16)
    vt1_ref[0, 0] = jnp.where(row >= HEAD_DIM, vt, 1.0).astype(BF16)
    u_ref[0] = proj[:, V_END:].astype(BF16)


def _embed_tail_body(x_ref, g_ref, b_ref, w_ref, qg_ref, kg_ref, bones_ref, cos_ref, sin_ref,
                     h_in, q_in, k_in, u_in, h_ref, q_ref, k_ref, vt0_ref, vt1_ref, u_ref):
    del h_in, q_in, k_in, u_in
    _embed_body(x_ref, g_ref, b_ref, w_ref, qg_ref, kg_ref, bones_ref, cos_ref, sin_ref,
                h_ref, q_ref, k_ref, vt0_ref, vt1_ref, u_ref)


def _embed(x, meta_pad, ln_g, ln_b, w_in, qg, kg, bones, cos, sin, L, Lp):
    B = x.shape[0]
    TM = EMBED_TILE
    nj = L // TM
    w_specs = [_const_spec(1, D_MODEL), _const_spec(1, D_MODEL), _const_spec(D_MODEL, IN_WIDTH),
               _const_spec(1, Q_END), _const_spec(1, KV_WIDTH), _const_spec(Q_END, Q_END)]

    def out_shapes(vt_tiles, vt_width):
        vt = jax.ShapeDtypeStruct((B, vt_tiles, KV_WIDTH, vt_width), BF16)
        return [
            jax.ShapeDtypeStruct((B, Lp, D_MODEL), F32),
            jax.ShapeDtypeStruct((B, Lp, Q_END), BF16),
            jax.ShapeDtypeStruct((B, Lp, KV_WIDTH), BF16),
            vt, vt,
            jax.ShapeDtypeStruct((B, Lp, SSM_WIDTH), BF16),
        ]

    def out_specs(tm, row_block):
        return [
            pl.BlockSpec((1, tm, D_MODEL), lambda b, j: (b, row_block(j), 0)),
            pl.BlockSpec((1, tm, Q_END), lambda b, j: (b, row_block(j), 0)),
            pl.BlockSpec((1, tm, KV_WIDTH), lambda b, j: (b, row_block(j), 0)),
            pl.BlockSpec((1, 1, KV_WIDTH, tm), lambda b, j: (b, j, 0, 0)),
            pl.BlockSpec((1, 1, KV_WIDTH, tm), lambda b, j: (b, j, 0, 0)),
            pl.BlockSpec((1, tm, SSM_WIDTH), lambda b, j: (b, row_block(j), 0)),
        ]

    main = pl.pallas_call(
        _embed_body,
        grid=(B, nj),
        in_specs=[pl.BlockSpec((1, TM, D_MODEL), lambda b, j: (b, j, 0))] + w_specs
        + [pl.BlockSpec((TM, LANES), lambda b, j: (j, 0))] * 2,
        out_specs=out_specs(TM, lambda j: j),
        out_shape=out_shapes(nj, TM),
        compiler_params=_cparams(("parallel", "parallel")),
        name="embed_main",
    )
    h0, q, k, vt0, vt1, u = main(x, ln_g, ln_b, w_in, qg, kg, bones, cos, sin)

    jt = Lp // TAIL - 1
    tail = pl.pallas_call(
        _embed_tail_body,
        grid=(B, 1),
        in_specs=[pl.BlockSpec((1, TAIL, D_MODEL), lambda b, j: (0, 0, 0))] + w_specs
        + [pl.BlockSpec((TAIL, LANES), lambda b, j: (jt, 0))] * 2
        + [pl.BlockSpec(memory_space=pl.ANY)] * 4,
        out_specs=out_specs(TAIL, lambda j: jt),
        out_shape=out_shapes(1, TAIL),
        input_output_aliases={9: 0, 10: 1, 11: 2, 12: 5},
        compiler_params=_cparams(("parallel", "arbitrary")),
        name="embed_tail",
    )
    h0, q, k, vt0_tail, vt1_tail, u = tail(meta_pad, ln_g, ln_b, w_in, qg, kg, bones, cos, sin,
                                           h0, q, k, u)
    return h0, q, k, (vt0, vt1, vt0_tail, vt1_tail), u


def _attn_body(q_ref, k_ref, vt0_ref, vt1_ref, vt0t_ref, vt1t_ref, o_ref, qt_ref, m_ref, acc_ref,
               *, L, n_kv_tiles):
    tq = Q_TILE
    cols = Q_PER_KV * tq
    row = lax.broadcasted_iota(jnp.int32, (KV_WIDTH, tq), 0)
    for r in range(Q_PER_KV):
        slab_t = q_ref[0, :, r * LANES:(r + 1) * LANES].astype(F32).T
        for g in range(N_KV_HEADS):
            in_group = (row >= g * HEAD_DIM) & (row < (g + 1) * HEAD_DIM)
            qt_ref[g, :, r * tq:(r + 1) * tq] = jnp.where(in_group, slab_t, 0.0).astype(BF16)
    m_ref[...] = jnp.full(m_ref.shape, NEG_BIG, F32)
    acc_ref[...] = jnp.zeros(acc_ref.shape, F32)

    def online_step(k_tile, vt_tiles, n_valid):
        for g in range(N_KV_HEADS):
            s = jnp.dot(k_tile, qt_ref[g], preferred_element_type=F32)
            if n_valid is not None:
                key = lax.broadcasted_iota(jnp.int32, s.shape, 0)
                s = jnp.where(key < n_valid, s, NEG_BIG)
            m_old = m_ref[g]
            m_new = jnp.maximum(m_old, jnp.max(s, axis=0, keepdims=True))
            alpha = jnp.exp2(m_old[0:1, :] - m_new[0:1, :])
            p = jnp.exp2(s - m_new[0:1, :])
            acc_ref[g] = alpha * acc_ref[g] + jnp.dot(vt_tiles[g], p.astype(BF16),
                                                      preferred_element_type=F32)
            m_ref[g] = m_new

    def kv_step(c, carry):
        start = pl.multiple_of(c * KV_TILE, KV_TILE)
        online_step(k_ref[0, pl.ds(start, KV_TILE), :], (vt0_ref[0, c], vt1_ref[0, c]), None)
        return carry

    lax.fori_loop(0, n_kv_tiles, kv_step, 0)
    online_step(k_ref[0, L:L + TAIL, :], (vt0t_ref[0, 0], vt1t_ref[0, 0]), N_META)

    outs = []
    for g in range(N_KV_HEADS):
        other = (1 - g) * HEAD_DIM
        outs.append(acc_ref[g] / acc_ref[g, other:other + 1, :])
    row = lax.broadcasted_iota(jnp.int32, (KV_WIDTH, cols), 0)
    comb = jnp.where(row < HEAD_DIM, outs[0], outs[1])
    for r in range(Q_PER_KV):
        o_ref[0, :, r * LANES:(r + 1) * LANES] = comb[:, r * tq:(r + 1) * tq].T.astype(BF16)


def _attention(q, k, vts, L, Lp):
    B = q.shape[0]
    nkt = L // KV_TILE
    cols = Q_PER_KV * Q_TILE
    vt_spec = pl.BlockSpec((1, nkt, KV_WIDTH, KV_TILE), lambda b, j: (b, 0, 0, 0))
    vt_tail_spec = pl.BlockSpec((1, 1, KV_WIDTH, TAIL), lambda b, j: (b, 0, 0, 0))
    return pl.pallas_call(
        functools.partial(_attn_body, L=L, n_kv_tiles=nkt),
        grid=(B, Lp // Q_TILE),
        in_specs=[
            pl.BlockSpec((1, Q_TILE, Q_END), lambda b, j: (b, j, 0)),
            pl.BlockSpec((1, Lp, KV_WIDTH), lambda b, j: (b, 0, 0)),
            vt_spec, vt_spec, vt_tail_spec, vt_tail_spec,
        ],
        out_specs=pl.BlockSpec((1, Q_TILE, Q_END), lambda b, j: (b, j, 0)),
        out_shape=jax.ShapeDtypeStruct((B, Lp, Q_END), BF16),
        scratch_shapes=[
            pltpu.VMEM((N_KV_HEADS, KV_WIDTH, cols), BF16),
            pltpu.VMEM((N_KV_HEADS, SUBLANES, cols), F32),
            pltpu.VMEM((N_KV_HEADS, KV_WIDTH, cols), F32),
        ],
        compiler_params=_cparams(("parallel", "arbitrary")),
        name="attention",
    )(q, k, *vts)


def _ssm_weights(lam_re, lam_im, log_dt, b_re, b_im, c_re, c_im, d_skip):
    hp = lax.Precision.HIGHEST
    dt = jnp.exp(log_dt.astype(F32))[..., None]
    lr = lam_re.astype(F32)
    li = lam_im.astype(F32)

    def apow(n):
        n = jnp.asarray(n, F32)
        mag = jnp.exp(lr[..., None] * dt[..., None] * n)
        ang = li[..., None] * dt[..., None] * n
        return mag * jnp.cos(ang), mag * jnp.sin(ang)

    a1r, a1i = apow(jnp.ones((1,), F32))
    a1r, a1i = a1r[..., 0], a1i[..., 0]
    nr = a1r - 1.0
    den = lr * lr + li * li
    f_r = (nr * lr + a1i * li) / den
    f_i = (a1i * lr - nr * li) / den
    br = b_re.astype(F32)
    bi = b_im.astype(F32)
    bb_r = f_r[..., None] * br - f_i[..., None] * bi
    bb_i = f_r[..., None] * bi + f_i[..., None] * br
    cr = c_re.astype(F32)
    ci = c_im.astype(F32)

    tau = jnp.arange(CHUNK + 1, dtype=F32)
    pr, pi = apow(tau)
    ab_r = pr[..., None] * bb_r[:, :, :, None, :] - pi[..., None] * bb_i[:, :, :, None, :]
    ab_i = pr[..., None] * bb_i[:, :, :, None, :] + pi[..., None] * bb_r[:, :, :, None, :]
    kern = (jnp.einsum("dgop,dgptc->dgtco", cr, ab_r, precision=hp)
            - jnp.einsum("dgop,dgptc->dgtco", ci, ab_i, precision=hp))
    s_idx = jnp.arange(CHUNK)[:, None]
    t_idx = jnp.arange(CHUNK)[None, :]
    lag_f = jnp.clip(t_idx - s_idx, 0, CHUNK)
    lag_b = jnp.clip(s_idx - t_idx, 0, CHUNK)
    m_f = jnp.where((t_idx >= s_idx)[..., None, None], kern[0][:, lag_f], 0.0)
    m_b = jnp.where((s_idx >= t_idx)[..., None, None], kern[1][:, lag_b], 0.0)
    eye_t = jnp.eye(CHUNK, dtype=F32)[None, :, :, None, None]
    eye_c = jnp.eye(SSM_GROUP, dtype=F32)[None, None, None]
    dsk = d_skip.astype(F32).reshape(N_SSM_GROUPS, 1, 1, SSM_GROUP, 1)
    m_all = m_f + m_b + eye_t * eye_c * dsk
    m_all = m_all.transpose(0, 1, 3, 2, 4).reshape(N_SSM_GROUPS, 256, 256)

    def state_in(d, expo):
        er, ei = pr[d][..., expo], pi[d][..., expo]
        wr = er[..., None] * bb_r[d][:, :, None, :] - ei[..., None] * bb_i[d][:, :, None, :]
        wi = er[..., None] * bb_i[d][:, :, None, :] + ei[..., None] * bb_r[d][:, :, None, :]
        to_rows = lambda w: w.transpose(0, 2, 3, 1).reshape(N_SSM_GROUPS, 256, SSM_STATE)
        return to_rows(wr), to_rows(wi)

    sf_r, sf_i = state_in(0, jnp.arange(CHUNK - 1, -1, -1))
    sb_r, sb_i = state_in(1, jnp.arange(CHUNK))

    def state_out(d, expo):
        er, ei = pr[d][..., expo], pi[d][..., expo]
        wr = cr[d].transpose(0, 2, 1)[:, :, None, :] * er[..., None] \
            - ci[d].transpose(0, 2, 1)[:, :, None, :] * ei[..., None]
        wi = cr[d].transpose(0, 2, 1)[:, :, None, :] * ei[..., None] \
            + ci[d].transpose(0, 2, 1)[:, :, None, :] * er[..., None]
        flat = lambda w: w.reshape(N_SSM_GROUPS, SSM_STATE, 256)
        return flat(wr), flat(-wi)

    of_r, of_i = state_out(0, jnp.arange(1, CHUNK + 1))
    ob_r, ob_i = state_out(1, jnp.arange(CHUNK, 0, -1))

    def pair_rows(w):
        return w.reshape(N_PAIRS, 2, *w.shape[1:])

    z256 = jnp.zeros((N_PAIRS, 256, 256), F32)
    mp = pair_rows(m_all)
    w1_y = jnp.concatenate([jnp.concatenate([mp[:, 0], z256], axis=2),
                            jnp.concatenate([z256, mp[:, 1]], axis=2)], axis=1)

    def pair_cols_in(w):
        wp = pair_rows(w)
        z = jnp.zeros_like(wp[:, 0])
        return jnp.concatenate([jnp.concatenate([wp[:, 0], z], axis=2),
                                jnp.concatenate([z, wp[:, 1]], axis=2)], axis=1)

    w1 = jnp.concatenate([w1_y, pair_cols_in(sf_r), pair_cols_in(sf_i),
                          pair_cols_in(sb_r), pair_cols_in(sb_i)], axis=2)

    def pair_rows_out(w):
        wp = pair_rows(w)
        z = jnp.zeros_like(wp[:, 0])
        return jnp.concatenate([jnp.concatenate([wp[:, 0], z], axis=2),
                                jnp.concatenate([z, wp[:, 1]], axis=2)], axis=1)

    w2 = jnp.concatenate([pair_rows_out(of_r), pair_rows_out(of_i),
                          pair_rows_out(ob_r), pair_rows_out(ob_i)], axis=1)

    qr, qi = apow(CHUNK * jnp.arange(SUBLANES + 1, dtype=F32))
    zero = jnp.zeros_like(qr[..., 0])

    def dec_rows(d, q_expo):
        rows = [qr[d][..., 1], qi[d][..., 1], qr[d][..., 2], qi[d][..., 2],
                qr[d][..., 4], qi[d][..., 4], zero[d], zero[d]]
        rows += [qr[d][..., n] for n in q_expo] + [qi[d][..., n] for n in q_expo]
        return jnp.stack(rows, axis=1)

    dec = jnp.concatenate([dec_rows(0, range(1, SUBLANES + 1)),
                           dec_rows(1, range(SUBLANES, 0, -1))], axis=1)
    dec = dec.reshape(N_PAIRS, 2, DEC_ROWS, SSM_STATE).transpose(0, 2, 1, 3)
    dec = dec.reshape(N_PAIRS, DEC_ROWS, PAIR_STATE)
    return w1.astype(BF16), w2.astype(BF16), dec


def _ssm_body(u_ref, w1_ref, w2_ref, dec_ref, z_ref, sh_ref, *, B, n_chunks):
    R = B * n_chunks
    tiles = [(r0, min(SSM_ROW_TILE, R - r0)) for r0 in range(0, R, SSM_ROW_TILE)]
    PS = PAIR_STATE
    n_blocks = n_chunks // SUBLANES

    for r0, n in tiles:
        t = jnp.dot(u_ref[0, r0:r0 + n, :], w1_ref[0], preferred_element_type=F32)
        z_ref[0, r0:r0 + n, :] = t[:, :PAIR_W]
        sh_ref[r0:r0 + n, :] = t[:, PAIR_W:]

    sub = lax.broadcasted_iota(jnp.int32, (SUBLANES, PS), 0)

    def cmul(ar, ai, xr, xi):
        return ar * xr - ai * xi, ar * xi + ai * xr

    def block_scan(sr, si, cr, ci, base, reverse):
        xr, xi = sr, si
        for k, d in enumerate((1, 2, 4)):
            pr_ = dec_ref[0, base + 2 * k:base + 2 * k + 1, :]
            pi_ = dec_ref[0, base + 2 * k + 1:base + 2 * k + 2, :]
            keep = (sub < SUBLANES - d) if reverse else (sub >= d)
            shift = SUBLANES - d if reverse else d
            yr = jnp.where(keep, pltpu.roll(xr, shift, 0), 0.0)
            yi = jnp.where(keep, pltpu.roll(xi, shift, 0), 0.0)
            mr, mi = cmul(pr_, pi_, yr, yi)
            xr, xi = xr + mr, xi + mi
        qr_ = dec_ref[0, base + 8:base + 16, :]
        qi_ = dec_ref[0, base + 16:base + 24, :]
        mr, mi = cmul(qr_, qi_, cr, ci)
        xr, xi = xr + mr, xi + mi
        edge = SUBLANES - 1 if reverse else 0
        shift = SUBLANES - 1 if reverse else 1
        er = jnp.where(sub == edge, cr, pltpu.roll(xr, shift, 0))
        ei = jnp.where(sub == edge, ci, pltpu.roll(xi, shift, 0))
        last = 0 if reverse else SUBLANES - 1
        nr = jnp.broadcast_to(xr[last:last + 1, :], (SUBLANES, PS))
        ni = jnp.broadcast_to(xi[last:last + 1, :], (SUBLANES, PS))
        return er, ei, nr, ni

    def scan_step(i, carry):
        new = []
        for b in range(B):
            cfr, cfi, cbr, cbi = carry[4 * b:4 * b + 4]
            rf = pl.multiple_of(b * n_chunks + i * SUBLANES, SUBLANES)
            rb = pl.multiple_of(b * n_chunks + (n_blocks - 1 - i) * SUBLANES, SUBLANES)
            f_rows, b_rows = pl.ds(rf, SUBLANES), pl.ds(rb, SUBLANES)
            er, ei, cfr, cfi = block_scan(sh_ref[f_rows, 0:PS], sh_ref[f_rows, PS:2 * PS],
                                          cfr, cfi, 0, False)
            sh_ref[f_rows, 0:PS] = er
            sh_ref[f_rows, PS:2 * PS] = ei
            er, ei, cbr, cbi = block_scan(sh_ref[b_rows, 2 * PS:3 * PS], sh_ref[b_rows, 3 * PS:4 * PS],
                                          cbr, cbi, DEC_ROWS // 2, True)
            sh_ref[b_rows, 2 * PS:3 * PS] = er
            sh_ref[b_rows, 3 * PS:4 * PS] = ei
            new += [cfr, cfi, cbr, cbi]
        return tuple(new)

    zero = jnp.zeros((SUBLANES, PS), F32)
    lax.fori_loop(0, n_blocks, scan_step, (zero,) * (4 * B))

    for r0, n in tiles:
        y = z_ref[0, r0:r0 + n, :] + jnp.dot(sh_ref[r0:r0 + n, :].astype(BF16), w2_ref[0],
                                             preferred_element_type=F32)
        z_ref[0, r0:r0 + n, :] = jax.nn.gelu(y)


def _ssm(u_pairs, w1, w2, dec, B, n_chunks):
    assert n_chunks % SUBLANES == 0
    R = B * n_chunks
    return pl.pallas_call(
        functools.partial(_ssm_body, B=B, n_chunks=n_chunks),
        grid=(N_PAIRS,),
        in_specs=[
            pl.BlockSpec((1, R, PAIR_W), lambda p: (p, 0, 0)),
            pl.BlockSpec((1, PAIR_W, 2 * PAIR_W), lambda p: (p, 0, 0)),
            pl.BlockSpec((1, PAIR_W, PAIR_W), lambda p: (p, 0, 0)),
            pl.BlockSpec((1, DEC_ROWS, PAIR_STATE), lambda p: (p, 0, 0)),
        ],
        out_specs=pl.BlockSpec((1, R, PAIR_W), lambda p: (p, 0, 0)),
        out_shape=jax.ShapeDtypeStruct((N_PAIRS, R, PAIR_W), F32),
        scratch_shapes=[pltpu.VMEM((R, 4 * PAIR_STATE), F32)],
        compiler_params=_cparams(("parallel",)),
        name="ssm",
    )(u_pairs, w1, w2, dec)


def _to_chunk_pairs(u, L):
    B, Lp, _ = u.shape
    nc, ncp = L // CHUNK, Lp // CHUNK
    uc = u.reshape(B, ncp, CHUNK, N_PAIRS, 2, SSM_GROUP)
    seq = jnp.concatenate([uc[:, nc:nc + 1], uc[:, :nc],
                           jnp.zeros((B, ncp - nc - 1) + uc.shape[2:], u.dtype)], axis=1)
    return seq.transpose(3, 0, 1, 4, 2, 5).reshape(N_PAIRS, B * ncp, PAIR_W)


def _from_chunk_pairs(z, B, L, Lp):
    nc, ncp = L // CHUNK, Lp // CHUNK
    zc = z.reshape(N_PAIRS, B, ncp, 2, CHUNK, SSM_GROUP)
    seq = zc.transpose(1, 2, 4, 0, 3, 5)
    back = jnp.concatenate([seq[:, 1:nc + 1], seq[:, 0:1], seq[:, nc + 1:]], axis=1)
    return back.reshape(B, Lp, SSM_WIDTH)


def _mix_body(o_ref, z_ref, h0_ref, wglu_ref, bglu_ref, woa_ref, wos_ref, g_ref, b_ref, wr_ref,
              h1_ref, acc_ref, aff_ref, *, L, Lp):
    z = z_ref[...]
    gate = jax.nn.sigmoid(jnp.dot(z.astype(BF16), wglu_ref[...], preferred_element_type=F32)
                          + bglu_ref[...])
    ssm_out = (z * gate).astype(BF16)
    mix = (jnp.dot(o_ref[...], woa_ref[...], preferred_element_type=F32)
           + jnp.dot(ssm_out, wos_ref[...], preferred_element_type=F32))
    h1 = _layer_norm_rows(DEEPNORM_ALPHA * h0_ref[...] + mix, g_ref[...], b_ref[...])
    h1_ref[...] = h1
    acc_ref[...] = DEEPNORM_ALPHA * h1
    logits = jnp.dot(h1.astype(BF16), wr_ref[...], preferred_element_type=F32)
    lane = lax.broadcasted_iota(jnp.int32, logits.shape, 1)
    logits = jnp.where(lane < N_EXPERTS, logits, NEG_BIG)
    e = jnp.exp(logits - jnp.max(logits, axis=-1, keepdims=True))
    aff = (e / jnp.sum(e, axis=-1, keepdims=True)).T[:N_EXPERTS, :]
    tiles_per_seq = Lp // ROW_TILE
    pos = (pl.program_id(0) % tiles_per_seq) * ROW_TILE + lax.broadcasted_iota(
        jnp.int32, aff.shape, 1)
    aff_ref[...] = jnp.where(pos < L + N_META, aff, -1.0)


def _mix(o, z, h0, wglu, bglu, woa, wos, g, b, wr, L, Lp):
    N = o.shape[0]
    TM = ROW_TILE
    row = lambda w: pl.BlockSpec((TM, w), lambda i: (i, 0))
    return pl.pallas_call(
        functools.partial(_mix_body, L=L, Lp=Lp),
        grid=(N // TM,),
        in_specs=[row(ATTN_WIDTH), row(SSM_WIDTH), row(D_MODEL),
                  _const_spec(SSM_WIDTH, SSM_WIDTH), _const_spec(1, SSM_WIDTH),
                  _const_spec(ATTN_WIDTH, D_MODEL), _const_spec(SSM_WIDTH, D_MODEL),
                  _const_spec(1, D_MODEL), _const_spec(1, D_MODEL), _const_spec(D_MODEL, LANES)],
        out_specs=[row(D_MODEL), row(D_MODEL), pl.BlockSpec((N_EXPERTS, TM), lambda i: (0, i))],
        out_shape=[jax.ShapeDtypeStruct((N, D_MODEL), F32),
                   jax.ShapeDtypeStruct((N, D_MODEL), F32),
                   jax.ShapeDtypeStruct((N_EXPERTS, N), F32)],
        compiler_params=_cparams(("parallel",)),
        name="mix",
    )(o, z, h0, wglu, bglu, woa, wos, g, b, wr)


def _select_body(aff_ref, ord_ref, idx_ref, *, capacity, n_slot_tiles, seq_len, pad_start):
    nbp = aff_ref.shape[1]
    ri = lax.broadcasted_iota(jnp.int32, (LANES, LANES), 0)
    ci = lax.broadcasted_iota(jnp.int32, (LANES, LANES), 1)
    strict_upper = (ri < ci).astype(BF16)
    incl_upper = (ri <= ci).astype(BF16)
    ones = jnp.ones((LANES, LANES), BF16)
    order = ord_ref[...]
    blk = lax.broadcasted_iota(jnp.int32, (nbp, LANES), 0).astype(F32)
    blk_hi = jnp.floor(blk * (1.0 / 16.0))
    blk_lo = blk - 16.0 * blk_hi
    dot = functools.partial(jnp.dot, preferred_element_type=F32)
    dot_nt = functools.partial(lax.dot_general, dimension_numbers=(((1,), (1,)), ((), ())),
                               preferred_element_type=F32)

    def per_expert(e, carry):
        bits = pltpu.bitcast(aff_ref[e], jnp.int32)
        t = jnp.zeros((1, 1), jnp.int32)
        for bit in range(30, -1, -1):
            cand = t | (1 << bit)
            cnt = jnp.sum((bits >= cand).astype(jnp.int32), keepdims=True)
            t = jnp.where(cnt >= capacity, cand, t)
        gt = bits > t
        eq = bits == t
        need = (capacity - jnp.sum(gt.astype(jnp.int32), keepdims=True)).astype(F32)
        eqb = eq.astype(BF16)
        tie_rank = dot(order, dot(eqb, ones).astype(BF16)) + dot(eqb, strict_upper)
        sel = (gt | (eq & (tie_rank < need))).astype(BF16)
        csum_in_blk = dot(sel, incl_upper)
        blk_tot = dot(sel, ones)
        blk_off = dot(order, blk_tot.astype(BF16))
        off_hi = jnp.floor(blk_off * (1.0 / 64.0))
        off_lo = blk_off - 64.0 * off_hi
        table = jnp.concatenate([csum_in_blk, off_hi, off_lo, blk_hi, blk_lo],
                                axis=1).astype(BF16)
        tot_l = dot_nt(ones[:SUBLANES], sel)
        off_l = dot_nt(tot_l.astype(BF16), order)[0:1, :]
        tot_l = tot_l[0:1, :]

        def per_tile(st, carry2):
            j = (st * SLOT_TILE + lax.broadcasted_iota(jnp.int32, (SLOT_TILE, 1), 0)).astype(F32)
            onehot = ((off_l <= j) & (j < off_l + tot_l)).astype(BF16)
            got = dot(onehot, table)
            local = j - (64.0 * got[:, LANES:2 * LANES] + got[:, 2 * LANES:3 * LANES])
            lane_idx = dot((got[:, :LANES] <= local).astype(BF16), ones)
            block = 16.0 * got[:, 3 * LANES:4 * LANES] + got[:, 4 * LANES:5 * LANES]
            k = j - capacity
            seq = sum((k >= PAD_ROWS * i).astype(F32) for i in range(1, SLOT_TILE // PAD_ROWS + 1))
            pad_tok = seq * (seq_len - PAD_ROWS) + pad_start + k
            tok = jnp.where(j < capacity, block * LANES + lane_idx, pad_tok)
            idx_ref[e, st] = tok.T[0:1, :].astype(jnp.int32)
            return carry2

        lax.fori_loop(0, n_slot_tiles, per_tile, 0)
        return carry

    lax.fori_loop(0, N_EXPERTS, per_expert, 0)


def _select(aff_blocks, order, capacity, n_slot_tiles, seq_len, pad_start):
    nbp = aff_blocks.shape[1]
    return pl.pallas_call(
        functools.partial(_select_body, capacity=capacity, n_slot_tiles=n_slot_tiles,
                          seq_len=seq_len, pad_start=pad_start),
        grid=(1,),
        in_specs=[_const_spec(N_EXPERTS, nbp, LANES), _const_spec(nbp, nbp)],
        out_specs=_const_spec(N_EXPERTS, n_slot_tiles, 1, SLOT_TILE),
        out_shape=jax.ShapeDtypeStruct((N_EXPERTS, n_slot_tiles, 1, SLOT_TILE), jnp.int32),
        compiler_params=_cparams(("arbitrary",)),
        name="select",
    )(aff_blocks, order)


def _moe_body(idx_ref, idx_next_ref, h1_hbm, wg_ref, wu_ref, wd_ref, wr_ref, acc_in, acc_hbm,
              xbuf, abuf, sem_x, sem_a, sem_s):
    del acc_in
    T = SLOT_TILE
    n_c = pl.num_programs(1)
    step = pl.program_id(0) * n_c + pl.program_id(1)
    n_steps = pl.num_programs(0) * n_c
    slot = step % 2

    def start_rows(copy_of_row, idx):
        def body(j, carry):
            copy_of_row(idx[0, 0, 0, j], j).start()
            return carry
        lax.fori_loop(0, T, body, 0, unroll=8)

    def x_row(buf):
        return lambda row, j: pltpu.make_async_copy(
            h1_hbm.at[pl.ds(row, 1)], xbuf.at[buf, pl.ds(j, 1)], sem_x.at[buf])

    def acc_row_in(row, j):
        return pltpu.make_async_copy(acc_hbm.at[pl.ds(row, 1)], abuf.at[pl.ds(j, 1)], sem_a)

    def acc_row_out(row, j):
        return pltpu.make_async_copy(abuf.at[pl.ds(j, 1)], acc_hbm.at[pl.ds(row, 1)], sem_s)

    def wait_scatter():
        pltpu.make_async_copy(abuf, acc_hbm.at[pl.ds(0, T)], sem_s).wait()

    @pl.when(step == 0)
    def _():
        start_rows(x_row(0), idx_ref)

    @pl.when(step > 0)
    def _():
        wait_scatter()

    start_rows(acc_row_in, idx_ref)

    @pl.when(step + 1 < n_steps)
    def _():
        start_rows(x_row(1 - slot), idx_next_ref)

    pltpu.make_async_copy(h1_hbm.at[pl.ds(0, T)], xbuf.at[slot], sem_x.at[slot]).wait()
    x = xbuf[slot].astype(BF16)
    logits = jnp.dot(x, wr_ref[...], preferred_element_type=F32)
    lane = lax.broadcasted_iota(jnp.int32, logits.shape, 1)
    logits = jnp.where(lane < N_EXPERTS, logits, NEG_BIG)
    p = jnp.exp(logits - jnp.max(logits, axis=-1, keepdims=True))
    gate = (jnp.sum(jnp.where(lane == pl.program_id(0), p, 0.0), axis=-1, keepdims=True)
            / jnp.sum(p, axis=-1, keepdims=True))
    y = jnp.zeros((T, D_MODEL), F32)
    for f in range(EXPERT_FF // FF_TILE):
        fs = slice(f * FF_TILE, (f + 1) * FF_TILE)
        hg = jnp.dot(x, wg_ref[0, :, fs], preferred_element_type=F32)
        hu = jnp.dot(x, wu_ref[0, :, fs], preferred_element_type=F32)
        hid = (jax.nn.silu(hg) * hu).astype(BF16)
        y = y + jnp.dot(hid, wd_ref[0, fs, :], preferred_element_type=F32)
    pltpu.make_async_copy(acc_hbm.at[pl.ds(0, T)], abuf, sem_a).wait()
    abuf[...] = abuf[...] + y * gate
    start_rows(acc_row_out, idx_ref)

    @pl.when(step == n_steps - 1)
    def _():
        wait_scatter()


def _moe(idx, h1, acc, wg, wu, wd, wr, n_slot_tiles):
    T = SLOT_TILE
    n_c = n_slot_tiles

    def next_block(e, c):
        return (jnp.minimum(e + (c + 1) // n_c, N_EXPERTS - 1), (c + 1) % n_c, 0, 0)

    return pl.pallas_call(
        _moe_body,
        grid=(N_EXPERTS, n_c),
        in_specs=[
            pl.BlockSpec((1, 1, 1, T), lambda e, c: (e, c, 0, 0), memory_space=pltpu.SMEM),
            pl.BlockSpec((1, 1, 1, T), next_block, memory_space=pltpu.SMEM),
            pl.BlockSpec(memory_space=pl.ANY),
            pl.BlockSpec((1, D_MODEL, EXPERT_FF), lambda e, c: (e, 0, 0)),
            pl.BlockSpec((1, D_MODEL, EXPERT_FF), lambda e, c: (e, 0, 0)),
            pl.BlockSpec((1, EXPERT_FF, D_MODEL), lambda e, c: (e, 0, 0)),
            _const_spec(D_MODEL, LANES),
            pl.BlockSpec(memory_space=pl.ANY),
        ],
        out_specs=pl.BlockSpec(memory_space=pl.ANY),
        out_shape=jax.ShapeDtypeStruct(acc.shape, F32),
        scratch_shapes=[pltpu.VMEM((2, T, D_MODEL), F32), pltpu.VMEM((T, D_MODEL), F32),
                        pltpu.SemaphoreType.DMA((2,)), pltpu.SemaphoreType.DMA(()),
                        pltpu.SemaphoreType.DMA(())],
        input_output_aliases={7: 0},
        compiler_params=pltpu.CompilerParams(dimension_semantics=("arbitrary", "arbitrary"),
                                             vmem_limit_bytes=VMEM_LIMIT,
                                             disable_bounds_checks=True),
        name="expert_ffn",
    )(idx, idx, h1, wg, wu, wd, wr, acc)


def _final_body(a_ref, g_ref, b_ref, o_ref):
    o_ref[0] = _layer_norm_rows(a_ref[0], g_ref[...], b_ref[...])


def _final_norm(acc, g, b, L):
    B = acc.shape[0]
    TM = EMBED_TILE
    return pl.pallas_call(
        _final_body,
        grid=(B, L // TM),
        in_specs=[pl.BlockSpec((1, TM, D_MODEL), lambda b_, j: (b_, j, 0)),
                  _const_spec(1, D_MODEL), _const_spec(1, D_MODEL)],
        out_specs=pl.BlockSpec((1, TM, D_MODEL), lambda b_, j: (b_, j, 0)),
        out_shape=jax.ShapeDtypeStruct((B, L, D_MODEL), F32),
        compiler_params=_cparams(("parallel", "parallel")),
        name="final_norm",
    )(acc, g, b)


def _rope_tables(L, Lp):
    rows = L // GRID_W
    t = jnp.arange(L, dtype=jnp.int32)
    m = jnp.arange(N_META, dtype=jnp.int32)
    pad = jnp.zeros((Lp - L - N_META,), jnp.int32)
    row = jnp.concatenate([t // GRID_W - rows // 2, jnp.full((N_META,), -(rows // 2) - 1, jnp.int32),
                           pad])
    col = jnp.concatenate([t % GRID_W - GRID_W // 2, m - GRID_W // 2, pad])
    inv_freq = ROPE_THETA ** (-jnp.arange(0, ROPE_AXIS_DIM, 2, dtype=F32) / ROPE_AXIS_DIM)
    ang_r = row.astype(F32)[:, None] * inv_freq
    ang_c = col.astype(F32)[:, None] * inv_freq
    cos = jnp.concatenate([jnp.cos(ang_r)] * 2 + [jnp.cos(ang_c)] * 2, axis=1)
    sin = jnp.concatenate([-jnp.sin(ang_r), jnp.sin(ang_r), -jnp.sin(ang_c), jnp.sin(ang_c)], axis=1)
    return jnp.tile(cos, (1, 2)), jnp.tile(sin, (1, 2))


def _block_order(B, Lp, nbp):
    nbb = Lp // LANES
    r = jnp.arange(nbp)
    b, jb = r // nbb, r % nbb
    rank = jnp.where(r < B * nbb, b * nbb + jnp.where(jb == nbb - 1, 0, jb + 1), r)
    return (rank[None, :] < rank[:, None]).astype(BF16)


def _run_trunk(x, meta_pad, shared):
    B, L, _ = x.shape
    Lp = L + TAIL
    N = B * Lp
    cos, sin = _rope_tables(L, Lp)
    h0, q, k, vts, u = _embed(x, meta_pad, shared["ln_emb_g"], shared["ln_emb_b"], shared["w_in"],
                              shared["qg"], shared["kg"], shared["bones"], cos, sin, L, Lp)
    o = _attention(q, k, vts, L, Lp)
    z = _ssm(_to_chunk_pairs(u, L), shared["ssm_w1"], shared["ssm_w2"], shared["ssm_dec"],
             B, Lp // CHUNK)
    z = _from_chunk_pairs(z, B, L, Lp)
    h1, acc, aff = _mix(o.reshape(N, ATTN_WIDTH), z.reshape(N, SSM_WIDTH), h0.reshape(N, D_MODEL),
                        shared["w_glu"], shared["b_glu"], shared["w_out_attn"], shared["w_out_ssm"],
                        shared["ln1_g"], shared["ln1_b"], shared["w_router"], L, Lp)

    capacity = EC_CAPACITY_FACTOR * B * (L + N_META) // N_EXPERTS
    n_slot_tiles = -(-capacity // SLOT_TILE)
    nb = N // LANES
    nbp = -(-nb // LANES) * LANES
    aff_blocks = jnp.pad(aff.reshape(N_EXPERTS, nb, LANES), ((0, 0), (0, nbp - nb), (0, 0)),
                         constant_values=-1.0)
    assert n_slot_tiles * SLOT_TILE - capacity <= B * PAD_ROWS
    idx = _select(aff_blocks, _block_order(B, Lp, nbp), capacity, n_slot_tiles, Lp, L + N_META)
    acc = _moe(idx, h1, acc, shared["w_gate"], shared["w_up"], shared["w_down"],
               shared["w_router"], n_slot_tiles)
    return _final_norm(acc.reshape(B, Lp, D_MODEL), shared["ln2_g"], shared["ln2_b"], L)


def kernel(x_prompt, x_sample, meta_tokens, ln_emb_g, ln_emb_b, w_in, q_norm_g, k_norm_g, ssm_lambda_re, ssm_lambda_im, ssm_log_dt, ssm_b_re, ssm_b_im, ssm_c_re, ssm_c_im, ssm_d, w_glu, b_glu, w_out, ln1_g, ln1_b, w_router, w_gate, w_up, w_down, ln2_g, ln2_b):
    row = lambda a: a.reshape(1, -1).astype(F32)
    w_q = w_in[0][:, :Q_END].reshape(D_MODEL, N_KV_HEADS, Q_PER_KV, HEAD_DIM)
    w_q = w_q.transpose(0, 2, 1, 3).reshape(D_MODEL, Q_END)
    w_oa = w_out[0][:ATTN_WIDTH].reshape(N_KV_HEADS, Q_PER_KV, HEAD_DIM, D_MODEL)
    w_oa = w_oa.transpose(1, 0, 2, 3).reshape(ATTN_WIDTH, D_MODEL)
    head_of = jnp.arange(Q_END) // HEAD_DIM
    ssm_w1, ssm_w2, ssm_dec = _ssm_weights(ssm_lambda_re[0], ssm_lambda_im[0], ssm_log_dt[0],
                                           ssm_b_re[0], ssm_b_im[0], ssm_c_re[0], ssm_c_im[0],
                                           ssm_d[0])
    shared = dict(
        ln_emb_g=row(ln_emb_g), ln_emb_b=row(ln_emb_b),
        w_in=jnp.concatenate([w_q, w_in[0][:, Q_END:]], axis=1).astype(BF16),
        qg=row(jnp.tile(q_norm_g[0], N_Q_HEADS)), kg=row(jnp.tile(k_norm_g[0], N_KV_HEADS)),
        bones=(head_of[:, None] == head_of[None, :]).astype(BF16),
        ssm_w1=ssm_w1, ssm_w2=ssm_w2, ssm_dec=ssm_dec,
        w_glu=w_glu[0].astype(BF16), b_glu=row(b_glu[0]),
        w_out_attn=w_oa.astype(BF16), w_out_ssm=w_out[0][ATTN_WIDTH:].astype(BF16),
        ln1_g=row(ln1_g[0]), ln1_b=row(ln1_b[0]),
        w_router=jnp.pad(w_router[0], ((0, 0), (0, LANES - N_EXPERTS))).astype(BF16),
        w_gate=w_gate[0].astype(BF16), w_up=w_up[0].astype(BF16), w_down=w_down[0].astype(BF16),
        ln2_g=row(ln2_g[0]), ln2_b=row(ln2_b[0]),
    )
    meta_pad = jnp.pad(meta_tokens.astype(F32), ((0, TAIL - N_META), (0, 0)))[None]
    return (_run_trunk(x_prompt, meta_pad, shared), _run_trunk(x_sample, meta_pad, shared))
```

```python
import functools
import math

import jax
import jax.numpy as jnp
from jax import lax
from jax.experimental import pallas as pl
from jax.experimental.pallas import tpu as pltpu

F32 = jnp.float32
BF16 = jnp.bfloat16

D_MODEL = 1024
N_META = 16
GRID_W = 64
ATTN_WIDTH = 512
SSM_WIDTH = 512
HEAD_DIM = 64
N_Q_HEADS = 8
N_KV_HEADS = 2
Q_PER_KV = 4
KV_WIDTH = 128
ROPE_AXIS_DIM = 32
ROPE_THETA = 10000.0
ATTN_SCALE = HEAD_DIM ** -0.5
RMS_EPS = 1e-6
SSM_GROUP = 16
N_SSM_GROUPS = 32
SSM_STATE = 64
Q_END = ATTN_WIDTH
K_END = Q_END + KV_WIDTH
V_END = K_END + KV_WIDTH
IN_WIDTH = V_END + SSM_WIDTH
N_EXPERTS = 16
EXPERT_FF = 2048
EC_CAPACITY_FACTOR = 2
LN_EPS = 1e-5
DEPTH = 1
DEEPNORM_ALPHA = (2 * DEPTH) ** 0.25

LANES = 128
SUBLANES = 8
TAIL = LANES
PAD_ROWS = TAIL - N_META
CHUNK = 16
N_PAIRS = N_SSM_GROUPS // 2
PAIR_W = 2 * CHUNK * SSM_GROUP
PAIR_STATE = 2 * SSM_STATE
EMBED_TILE = 512
ROW_TILE = 384
Q_TILE = 128
KV_TILE = EMBED_TILE
KEY_SUBTILE = 256
SLOT_TILE = 384
FF_TILE = 512
SSM_ROW_TILE = 256
DEC_ROWS = 48
Q_PRESCALE = ATTN_SCALE * math.log2(math.e)
NEG_BIG = -1e30
VMEM_LIMIT = 56 * 1024 * 1024


def _cparams(sem):
    return pltpu.CompilerParams(dimension_semantics=sem, vmem_limit_bytes=VMEM_LIMIT)


def _const_spec(*shape):
    return pl.BlockSpec(shape, lambda *idx: (0,) * len(shape))


def _layer_norm_rows(x, g, b):
    mu = jnp.mean(x, axis=-1, keepdims=True)
    xc = x - mu
    var = jnp.mean(xc * xc, axis=-1, keepdims=True)
    return xc * lax.rsqrt(var + LN_EPS) * g + b


def _head_rms(t, gain, bones):
    sq = t * t
    hi = sq.astype(BF16)
    lo = (sq - hi.astype(F32)).astype(BF16)
    ss = (jnp.dot(hi, bones, preferred_element_type=F32)
          + jnp.dot(lo, bones, preferred_element_type=F32))
    return t * lax.rsqrt(ss * (1.0 / HEAD_DIM) + RMS_EPS) * gain


def _rope_slab(t, cos, sin_signed):
    lane = lax.broadcasted_iota(jnp.int32, t.shape, 1)
    first = (lane % ROPE_AXIS_DIM) < (ROPE_AXIS_DIM // 2)
    partner = jnp.where(first, pltpu.roll(t, LANES - 16, 1), pltpu.roll(t, 16, 1))
    return t * cos + partner * sin_signed


def _lane_piece_gather(pieces, lane16):
    out = None
    for j, (arr, src) in enumerate(pieces):
        shift = (SSM_GROUP * j - src) % LANES
        moved = pltpu.roll(arr, shift, 1) if shift else arr
        out = moved if out is None else jnp.where(lane16 == j, moved, out)
    return out


def _embed_body(x_ref, g_ref, b_ref, w_ref, qg_ref, kg_ref, bones_ref, cos_ref, sin_ref,
                h_ref, q_ref, k_ref, vt0_ref, vt1_ref, u_ref, u_scr):
    h = _layer_norm_rows(x_ref[0], g_ref[...], b_ref[...])
    h_ref[0] = h
    proj = jnp.dot(h.astype(BF16), w_ref[...], preferred_element_type=F32)
    cos = cos_ref[...]
    sin = sin_ref[...]
    qn = _head_rms(proj[:, :Q_END], qg_ref[...], bones_ref[...])
    for s in range(ATTN_WIDTH // LANES):
        sl = slice(s * LANES, (s + 1) * LANES)
        q_ref[0, :, sl] = (_rope_slab(qn[:, sl], cos, sin) * Q_PRESCALE).astype(BF16)
    kn = _head_rms(proj[:, Q_END:K_END], kg_ref[...], bones_ref[:KV_WIDTH, :KV_WIDTH])
    k_ref[0] = _rope_slab(kn, cos, sin).astype(BF16)
    vt = proj[:, K_END:V_END].T
    row = lax.broadcasted_iota(jnp.int32, vt.shape, 0)
    vt0_ref[0, 0] = jnp.where(row < HEAD_DIM, vt, 1.0).astype(BF16)
    vt1_ref[0, 0] = jnp.where(row >= HEAD_DIM, vt, 1.0).astype(BF16)
    n_rows = u_scr.shape[1] // CHUNK
    for v in range(SSM_WIDTH // LANES):
        u_scr[v] = proj[:, V_END + v * LANES:V_END + (v + 1) * LANES]
    by_token = [[u_scr[v, pl.ds(t, n_rows, stride=CHUNK), :] for v in range(SSM_WIDTH // LANES)]
                for t in range(CHUNK)]
    lane16 = lax.broadcasted_iota(jnp.int32, (n_rows, LANES), 1) // SSM_GROUP
    for p in range(N_PAIRS):
        for gi in range(2):
            src = (p % 4) * 2 * SSM_GROUP + gi * SSM_GROUP
            for h in range(CHUNK // SUBLANES):
                pieces = [(by_token[SUBLANES * h + j][p // 4], src) for j in range(SUBLANES)]
                lo = gi * CHUNK * SSM_GROUP + h * LANES
                u_ref[p, 0, :, lo:lo + LANES] = _lane_piece_gather(pieces, lane16)


def _embed_tail_body(x_ref, g_ref, b_ref, w_ref, qg_ref, kg_ref, bones_ref, cos_ref, sin_ref,
                     h_in, q_in, k_in, u_in, h_ref, q_ref, k_ref, vt0_ref, vt1_ref, u_ref, u_scr):
    del h_in, q_in, k_in, u_in
    _embed_body(x_ref, g_ref, b_ref, w_ref, qg_ref, kg_ref, bones_ref, cos_ref, sin_ref,
                h_ref, q_ref, k_ref, vt0_ref, vt1_ref, u_ref, u_scr)


def _embed(x, meta_pad, ln_g, ln_b, w_in, qg, kg, bones, cos, sin, L, Lp):
    B = x.shape[0]
    TM = EMBED_TILE
    nj = L // TM
    w_specs = [_const_spec(1, D_MODEL), _const_spec(1, D_MODEL), _const_spec(D_MODEL, IN_WIDTH),
               _const_spec(1, Q_END), _const_spec(1, KV_WIDTH), _const_spec(Q_END, Q_END)]

    def out_shapes(vt_tiles, vt_width):
        vt = jax.ShapeDtypeStruct((B, vt_tiles, KV_WIDTH, vt_width), BF16)
        return [
            jax.ShapeDtypeStruct((B, Lp, D_MODEL), F32),
            jax.ShapeDtypeStruct((B, Lp, Q_END), BF16),
            jax.ShapeDtypeStruct((B, Lp, KV_WIDTH), BF16),
            vt, vt,
            jax.ShapeDtypeStruct((N_PAIRS, B, Lp // CHUNK, PAIR_W), F32),
        ]

    def out_specs(tm, row_block):
        return [
            pl.BlockSpec((1, tm, D_MODEL), lambda b, j: (b, row_block(j), 0)),
            pl.BlockSpec((1, tm, Q_END), lambda b, j: (b, row_block(j), 0)),
            pl.BlockSpec((1, tm, KV_WIDTH), lambda b, j: (b, row_block(j), 0)),
            pl.BlockSpec((1, 1, KV_WIDTH, tm), lambda b, j: (b, j, 0, 0)),
            pl.BlockSpec((1, 1, KV_WIDTH, tm), lambda b, j: (b, j, 0, 0)),
            pl.BlockSpec((N_PAIRS, 1, tm // CHUNK, PAIR_W), lambda b, j: (0, b, row_block(j), 0)),
        ]

    main = pl.pallas_call(
        _embed_body,
        grid=(B, nj),
        in_specs=[pl.BlockSpec((1, TM, D_MODEL), lambda b, j: (b, j, 0))] + w_specs
        + [pl.BlockSpec((TM, LANES), lambda b, j: (j, 0))] * 2,
        out_specs=out_specs(TM, lambda j: j),
        out_shape=out_shapes(nj, TM),
        scratch_shapes=[pltpu.VMEM((SSM_WIDTH // LANES, TM, LANES), F32)],
        compiler_params=_cparams(("parallel", "parallel")),
        name="embed_main",
    )
    h0, q, k, vt0, vt1, u = main(x, ln_g, ln_b, w_in, qg, kg, bones, cos, sin)

    jt = Lp // TAIL - 1
    tail = pl.pallas_call(
        _embed_tail_body,
        grid=(B, 1),
        in_specs=[pl.BlockSpec((1, TAIL, D_MODEL), lambda b, j: (0, 0, 0))] + w_specs
        + [pl.BlockSpec((TAIL, LANES), lambda b, j: (jt, 0))] * 2
        + [pl.BlockSpec(memory_space=pl.ANY)] * 4,
        out_specs=out_specs(TAIL, lambda j: jt),
        out_shape=out_shapes(1, TAIL),
        input_output_aliases={9: 0, 10: 1, 11: 2, 12: 5},
        scratch_shapes=[pltpu.VMEM((SSM_WIDTH // LANES, TAIL, LANES), F32)],
        compiler_params=_cparams(("parallel", "arbitrary")),
        name="embed_tail",
    )
    h0, q, k, vt0_tail, vt1_tail, u = tail(meta_pad, ln_g, ln_b, w_in, qg, kg, bones, cos, sin,
                                           h0, q, k, u)
    return h0, q, k, (vt0, vt1, vt0_tail, vt1_tail), u


def _attn_body(q_ref, k_ref, vt0_ref, vt1_ref, vt0t_ref, vt1t_ref, o_ref,
               qt_ref, s_ref, mc_ref, m_ref, acc_ref, *, L, n_kv_tiles):
    tq = Q_TILE
    cols = Q_PER_KV * tq
    row = lax.broadcasted_iota(jnp.int32, (KV_WIDTH, tq), 0)
    for r in range(Q_PER_KV):
        slab_t = q_ref[0, :, r * LANES:(r + 1) * LANES].astype(F32).T
        for g in range(N_KV_HEADS):
            in_group = (row >= g * HEAD_DIM) & (row < (g + 1) * HEAD_DIM)
            qt_ref[g, :, r * tq:(r + 1) * tq] = jnp.where(in_group, slab_t, 0.0).astype(BF16)
    m_ref[...] = jnp.full(m_ref.shape, NEG_BIG, F32)
    acc_ref[...] = jnp.zeros(acc_ref.shape, F32)

    def scores(slot, k_tile, first_valid=0):
        n = k_tile.shape[0]
        for g in range(N_KV_HEADS):
            s = jnp.dot(k_tile, qt_ref[g], preferred_element_type=F32)
            if first_valid:
                key = lax.broadcasted_iota(jnp.int32, s.shape, 0)
                s = jnp.where(key >= first_valid, s, NEG_BIG)
            s_ref[slot, g, 0:n, :] = s
            mc_ref[slot, g] = jnp.broadcast_to(jnp.max(s, axis=0, keepdims=True), (SUBLANES, cols))

    def softmax_values(slot, vt_tiles, n):
        for g in range(N_KV_HEADS):
            m_old = m_ref[g]
            m_new = jnp.maximum(m_old, mc_ref[slot, g])
            alpha = jnp.exp2(m_old[0:1, :] - m_new[0:1, :])
            acc = alpha * acc_ref[g]
            for k0 in range(0, n, KEY_SUBTILE):
                k1 = min(k0 + KEY_SUBTILE, n)
                p = jnp.exp2(s_ref[slot, g, k0:k1, :] - m_new[0:1, :]).astype(BF16)
                acc = acc + jnp.dot(vt_tiles[g][:, k0:k1], p, preferred_element_type=F32)
            acc_ref[g] = acc
            m_ref[g] = m_new

    scores(0, k_ref[0, 0:KV_TILE, :])

    def kv_step(i, carry):
        slot = i % 2
        start = pl.multiple_of((i + 1) * KV_TILE, KV_TILE)
        scores(1 - slot, k_ref[0, pl.ds(start, KV_TILE), :])
        softmax_values(slot, (vt0_ref[0, i], vt1_ref[0, i]), KV_TILE)
        return carry

    lax.fori_loop(0, n_kv_tiles - 1, kv_step, 0)
    last = n_kv_tiles - 1
    scores(1 - last % 2, k_ref[0, L:L + TAIL, :], PAD_ROWS)
    softmax_values(last % 2, (vt0_ref[0, last], vt1_ref[0, last]), KV_TILE)
    softmax_values(1 - last % 2, (vt0t_ref[0, 0], vt1t_ref[0, 0]), TAIL)

    outs = []
    for g in range(N_KV_HEADS):
        other = (1 - g) * HEAD_DIM
        outs.append(acc_ref[g] / acc_ref[g, other:other + 1, :])
    row = lax.broadcasted_iota(jnp.int32, (KV_WIDTH, cols), 0)
    comb = jnp.where(row < HEAD_DIM, outs[0], outs[1])
    for r in range(Q_PER_KV):
        o_ref[0, :, r * LANES:(r + 1) * LANES] = comb[:, r * tq:(r + 1) * tq].T.astype(BF16)


def _attention(q, k, vts, L, Lp):
    B = q.shape[0]
    nkt = L // KV_TILE
    cols = Q_PER_KV * Q_TILE
    vt_spec = pl.BlockSpec((1, nkt, KV_WIDTH, KV_TILE), lambda b, j: (b, 0, 0, 0))
    vt_tail_spec = pl.BlockSpec((1, 1, KV_WIDTH, TAIL), lambda b, j: (b, 0, 0, 0))
    return pl.pallas_call(
        functools.partial(_attn_body, L=L, n_kv_tiles=nkt),
        grid=(B, Lp // Q_TILE),
        in_specs=[
            pl.BlockSpec((1, Q_TILE, Q_END), lambda b, j: (b, j, 0)),
            pl.BlockSpec((1, Lp, KV_WIDTH), lambda b, j: (b, 0, 0)),
            vt_spec, vt_spec, vt_tail_spec, vt_tail_spec,
        ],
        out_specs=pl.BlockSpec((1, Q_TILE, Q_END), lambda b, j: (b, j, 0)),
        out_shape=jax.ShapeDtypeStruct((B, Lp, Q_END), BF16),
        scratch_shapes=[
            pltpu.VMEM((N_KV_HEADS, KV_WIDTH, cols), BF16),
            pltpu.VMEM((2, N_KV_HEADS, KV_TILE, cols), F32),
            pltpu.VMEM((2, N_KV_HEADS, SUBLANES, cols), F32),
            pltpu.VMEM((N_KV_HEADS, SUBLANES, cols), F32),
            pltpu.VMEM((N_KV_HEADS, KV_WIDTH, cols), F32),
        ],
        compiler_params=_cparams(("parallel", "arbitrary")),
        name="attention",
    )(q, k, *vts)


def _ssm_weights(lam_re, lam_im, log_dt, b_re, b_im, c_re, c_im, d_skip):
    hp = lax.Precision.HIGHEST
    dt = jnp.exp(log_dt.astype(F32))[..., None]
    lr = lam_re.astype(F32)
    li = lam_im.astype(F32)

    def apow(n):
        n = jnp.asarray(n, F32)
        mag = jnp.exp(lr[..., None] * dt[..., None] * n)
        ang = li[..., None] * dt[..., None] * n
        return mag * jnp.cos(ang), mag * jnp.sin(ang)

    a1r, a1i = apow(jnp.ones((1,), F32))
    a1r, a1i = a1r[..., 0], a1i[..., 0]
    nr = a1r - 1.0
    den = lr * lr + li * li
    f_r = (nr * lr + a1i * li) / den
    f_i = (a1i * lr - nr * li) / den
    br = b_re.astype(F32)
    bi = b_im.astype(F32)
    bb_r = f_r[..., None] * br - f_i[..., None] * bi
    bb_i = f_r[..., None] * bi + f_i[..., None] * br
    cr = c_re.astype(F32)
    ci = c_im.astype(F32)

    tau = jnp.arange(CHUNK + 1, dtype=F32)
    pr, pi = apow(tau)
    ab_r = pr[..., None] * bb_r[:, :, :, None, :] - pi[..., None] * bb_i[:, :, :, None, :]
    ab_i = pr[..., None] * bb_i[:, :, :, None, :] + pi[..., None] * bb_r[:, :, :, None, :]
    kern = (jnp.einsum("dgop,dgptc->dgtco", cr, ab_r, precision=hp)
            - jnp.einsum("dgop,dgptc->dgtco", ci, ab_i, precision=hp))
    s_idx = jnp.arange(CHUNK)[:, None]
    t_idx = jnp.arange(CHUNK)[None, :]
    lag_f = jnp.clip(t_idx - s_idx, 0, CHUNK)
    lag_b = jnp.clip(s_idx - t_idx, 0, CHUNK)
    m_f = jnp.where((t_idx >= s_idx)[..., None, None], kern[0][:, lag_f], 0.0)
    m_b = jnp.where((s_idx >= t_idx)[..., None, None], kern[1][:, lag_b], 0.0)
    eye_t = jnp.eye(CHUNK, dtype=F32)[None, :, :, None, None]
    eye_c = jnp.eye(SSM_GROUP, dtype=F32)[None, None, None]
    dsk = d_skip.astype(F32).reshape(N_SSM_GROUPS, 1, 1, SSM_GROUP, 1)
    m_all = m_f + m_b + eye_t * eye_c * dsk
    m_all = m_all.transpose(0, 1, 3, 2, 4).reshape(N_SSM_GROUPS, 256, 256)

    def state_in(d, expo):
        er, ei = pr[d][..., expo], pi[d][..., expo]
        wr = er[..., None] * bb_r[d][:, :, None, :] - ei[..., None] * bb_i[d][:, :, None, :]
        wi = er[..., None] * bb_i[d][:, :, None, :] + ei[..., None] * bb_r[d][:, :, None, :]
        to_rows = lambda w: w.transpose(0, 2, 3, 1).reshape(N_SSM_GROUPS, 256, SSM_STATE)
        return to_rows(wr), to_rows(wi)

    sf_r, sf_i = state_in(0, jnp.arange(CHUNK - 1, -1, -1))
    sb_r, sb_i = state_in(1, jnp.arange(CHUNK))

    def state_out(d, expo):
        er, ei = pr[d][..., expo], pi[d][..., expo]
        wr = cr[d].transpose(0, 2, 1)[:, :, None, :] * er[..., None] \
            - ci[d].transpose(0, 2, 1)[:, :, None, :] * ei[..., None]
        wi = cr[d].transpose(0, 2, 1)[:, :, None, :] * ei[..., None] \
            + ci[d].transpose(0, 2, 1)[:, :, None, :] * er[..., None]
        flat = lambda w: w.reshape(N_SSM_GROUPS, SSM_STATE, 256)
        return flat(wr), flat(-wi)

    of_r, of_i = state_out(0, jnp.arange(1, CHUNK + 1))
    ob_r, ob_i = state_out(1, jnp.arange(CHUNK, 0, -1))

    def pair_rows(w):
        return w.reshape(N_PAIRS, 2, *w.shape[1:])

    z256 = jnp.zeros((N_PAIRS, 256, 256), F32)
    mp = pair_rows(m_all)
    w1_y = jnp.concatenate([jnp.concatenate([mp[:, 0], z256], axis=2),
                            jnp.concatenate([z256, mp[:, 1]], axis=2)], axis=1)

    def pair_cols_in(w):
        wp = pair_rows(w)
        z = jnp.zeros_like(wp[:, 0])
        return jnp.concatenate([jnp.concatenate([wp[:, 0], z], axis=2),
                                jnp.concatenate([z, wp[:, 1]], axis=2)], axis=1)

    w1 = jnp.concatenate([w1_y, pair_cols_in(sf_r), pair_cols_in(sf_i),
                          pair_cols_in(sb_r), pair_cols_in(sb_i)], axis=2)

    def pair_rows_out(w):
        wp = pair_rows(w)
        z = jnp.zeros_like(wp[:, 0])
        return jnp.concatenate([jnp.concatenate([wp[:, 0], z], axis=2),
                                jnp.concatenate([z, wp[:, 1]], axis=2)], axis=1)

    w2 = jnp.concatenate([pair_rows_out(of_r), pair_rows_out(of_i),
                          pair_rows_out(ob_r), pair_rows_out(ob_i)], axis=1)

    qr, qi = apow(CHUNK * jnp.arange(SUBLANES + 1, dtype=F32))
    zero = jnp.zeros_like(qr[..., 0])

    def dec_rows(d, q_expo):
        rows = [qr[d][..., 1], qi[d][..., 1], qr[d][..., 2], qi[d][..., 2],
                qr[d][..., 4], qi[d][..., 4], zero[d], zero[d]]
        rows += [qr[d][..., n] for n in q_expo] + [qi[d][..., n] for n in q_expo]
        return jnp.stack(rows, axis=1)

    dec = jnp.concatenate([dec_rows(0, range(1, SUBLANES + 1)),
                           dec_rows(1, range(SUBLANES, 0, -1))], axis=1)
    dec = dec.reshape(N_PAIRS, 2, DEC_ROWS, SSM_STATE).transpose(0, 2, 1, 3)
    dec = dec.reshape(N_PAIRS, DEC_ROWS, PAIR_STATE)
    return w1.astype(BF16), w2.astype(BF16), dec


def _ssm_body(u_ref, w1_ref, w2_ref, dec_ref, z_ref, sh_ref, *, B, n_chunks):
    R = B * n_chunks
    tiles = [(r0, min(SSM_ROW_TILE, R - r0)) for r0 in range(0, R, SSM_ROW_TILE)]
    PS = PAIR_STATE
    n_blocks = n_chunks // SUBLANES

    for r0, n in tiles:
        t = jnp.dot(u_ref[0, r0:r0 + n, :].astype(BF16), w1_ref[0], preferred_element_type=F32)
        z_ref[0, r0:r0 + n, :] = t[:, :PAIR_W]
        sh_ref[r0:r0 + n, :] = t[:, PAIR_W:]

    sub = lax.broadcasted_iota(jnp.int32, (SUBLANES, PS), 0)

    def cmul(ar, ai, xr, xi):
        return ar * xr - ai * xi, ar * xi + ai * xr

    def block_scan(sr, si, cr, ci, base, reverse):
        xr, xi = sr, si
        for k, d in enumerate((1, 2, 4)):
            pr_ = dec_ref[0, base + 2 * k:base + 2 * k + 1, :]
            pi_ = dec_ref[0, base + 2 * k + 1:base + 2 * k + 2, :]
            keep = (sub < SUBLANES - d) if reverse else (sub >= d)
            shift = SUBLANES - d if reverse else d
            yr = jnp.where(keep, pltpu.roll(xr, shift, 0), 0.0)
            yi = jnp.where(keep, pltpu.roll(xi, shift, 0), 0.0)
            mr, mi = cmul(pr_, pi_, yr, yi)
            xr, xi = xr + mr, xi + mi
        qr_ = dec_ref[0, base + 8:base + 16, :]
        qi_ = dec_ref[0, base + 16:base + 24, :]
        mr, mi = cmul(qr_, qi_, cr, ci)
        xr, xi = xr + mr, xi + mi
        edge = SUBLANES - 1 if reverse else 0
        shift = SUBLANES - 1 if reverse else 1
        er = jnp.where(sub == edge, cr, pltpu.roll(xr, shift, 0))
        ei = jnp.where(sub == edge, ci, pltpu.roll(xi, shift, 0))
        last = 0 if reverse else SUBLANES - 1
        nr = jnp.broadcast_to(xr[last:last + 1, :], (SUBLANES, PS))
        ni = jnp.broadcast_to(xi[last:last + 1, :], (SUBLANES, PS))
        return er, ei, nr, ni

    def scan_step(i, carry):
        new = []
        for b in range(B):
            cfr, cfi, cbr, cbi = carry[4 * b:4 * b + 4]
            tile_f = (i + n_blocks - 1) % n_blocks
            tile_b = (2 * n_blocks - 2 - i) % n_blocks
            rf = pl.multiple_of(b * n_chunks + tile_f * SUBLANES, SUBLANES)
            rb = pl.multiple_of(b * n_chunks + tile_b * SUBLANES, SUBLANES)
            f_rows, b_rows = pl.ds(rf, SUBLANES), pl.ds(rb, SUBLANES)
            er, ei, cfr, cfi = block_scan(sh_ref[f_rows, 0:PS], sh_ref[f_rows, PS:2 * PS],
                                          cfr, cfi, 0, False)
            sh_ref[f_rows, 0:PS] = er
            sh_ref[f_rows, PS:2 * PS] = ei
            er, ei, cbr, cbi = block_scan(sh_ref[b_rows, 2 * PS:3 * PS], sh_ref[b_rows, 3 * PS:4 * PS],
                                          cbr, cbi, DEC_ROWS // 2, True)
            sh_ref[b_rows, 2 * PS:3 * PS] = er
            sh_ref[b_rows, 3 * PS:4 * PS] = ei
            new += [cfr, cfi, cbr, cbi]
        return tuple(new)

    zero = jnp.zeros((SUBLANES, PS), F32)
    lax.fori_loop(0, n_blocks, scan_step, (zero,) * (4 * B))

    for r0, n in tiles:
        y = z_ref[0, r0:r0 + n, :] + jnp.dot(sh_ref[r0:r0 + n, :].astype(BF16), w2_ref[0],
                                             preferred_element_type=F32)
        z_ref[0, r0:r0 + n, :] = jax.nn.gelu(y)


def _ssm(u_pairs, w1, w2, dec, B, n_chunks):
    assert n_chunks % SUBLANES == 0
    R = B * n_chunks
    return pl.pallas_call(
        functools.partial(_ssm_body, B=B, n_chunks=n_chunks),
        grid=(N_PAIRS,),
        in_specs=[
            pl.BlockSpec((1, R, PAIR_W), lambda p: (p, 0, 0)),
            pl.BlockSpec((1, PAIR_W, 2 * PAIR_W), lambda p: (p, 0, 0)),
            pl.BlockSpec((1, PAIR_W, PAIR_W), lambda p: (p, 0, 0)),
            pl.BlockSpec((1, DEC_ROWS, PAIR_STATE), lambda p: (p, 0, 0)),
        ],
        out_specs=pl.BlockSpec((1, R, PAIR_W), lambda p: (p, 0, 0)),
        out_shape=jax.ShapeDtypeStruct((N_PAIRS, R, PAIR_W), F32),
        scratch_shapes=[pltpu.VMEM((R, 4 * PAIR_STATE), F32)],
        compiler_params=_cparams(("parallel",)),
        name="ssm",
    )(u_pairs, w1, w2, dec)


def _mix_body(o_ref, zp_ref, h0_ref, wglu_ref, bglu_ref, woa_ref, wos_ref, g_ref, b_ref, wr_ref,
              h1_ref, acc_ref, aff_ref, z_scr, *, L, Lp):
    n_rows = zp_ref.shape[1]
    lane16 = lax.broadcasted_iota(jnp.int32, (n_rows, LANES), 1) // SSM_GROUP
    for t in range(CHUNK):
        src = (t % SUBLANES) * SSM_GROUP
        for v in range(SSM_WIDTH // LANES):
            pieces = []
            for j in range(SUBLANES):
                p, gi = 4 * v + j // 2, j % 2
                lo = gi * CHUNK * SSM_GROUP + (t // SUBLANES) * LANES
                pieces.append((zp_ref[p, :, lo:lo + LANES], src))
            z_scr[v, pl.ds(t, n_rows, stride=CHUNK), :] = _lane_piece_gather(pieces, lane16)
    z = jnp.concatenate([z_scr[v] for v in range(SSM_WIDTH // LANES)], axis=1)
    gate = jax.nn.sigmoid(jnp.dot(z.astype(BF16), wglu_ref[...], preferred_element_type=F32)
                          + bglu_ref[...])
    ssm_out = (z * gate).astype(BF16)
    mix = (jnp.dot(o_ref[...], woa_ref[...], preferred_element_type=F32)
           + jnp.dot(ssm_out, wos_ref[...], preferred_element_type=F32))
    h1 = _layer_norm_rows(DEEPNORM_ALPHA * h0_ref[...] + mix, g_ref[...], b_ref[...])
    h1_ref[...] = h1
    acc_ref[...] = DEEPNORM_ALPHA * h1
    logits = jnp.dot(h1.astype(BF16), wr_ref[...], preferred_element_type=F32)
    lane = lax.broadcasted_iota(jnp.int32, logits.shape, 1)
    logits = jnp.where(lane < N_EXPERTS, logits, NEG_BIG)
    e = jnp.exp(logits - jnp.max(logits, axis=-1, keepdims=True))
    aff = (e / jnp.sum(e, axis=-1, keepdims=True)).T[:N_EXPERTS, :]
    tiles_per_seq = Lp // ROW_TILE
    pos = (pl.program_id(0) % tiles_per_seq) * ROW_TILE + lax.broadcasted_iota(
        jnp.int32, aff.shape, 1)
    aff_ref[...] = jnp.where((pos < L) | (pos >= L + PAD_ROWS), aff, -1.0)


def _mix(o, z, h0, wglu, bglu, woa, wos, g, b, wr, L, Lp):
    N = o.shape[0]
    TM = ROW_TILE
    row = lambda w: pl.BlockSpec((TM, w), lambda i: (i, 0))
    return pl.pallas_call(
        functools.partial(_mix_body, L=L, Lp=Lp),
        grid=(N // TM,),
        in_specs=[row(ATTN_WIDTH), pl.BlockSpec((N_PAIRS, TM // CHUNK, PAIR_W), lambda i: (0, i, 0)),
                  row(D_MODEL),
                  _const_spec(SSM_WIDTH, SSM_WIDTH), _const_spec(1, SSM_WIDTH),
                  _const_spec(ATTN_WIDTH, D_MODEL), _const_spec(SSM_WIDTH, D_MODEL),
                  _const_spec(1, D_MODEL), _const_spec(1, D_MODEL), _const_spec(D_MODEL, LANES)],
        out_specs=[row(D_MODEL), row(D_MODEL), pl.BlockSpec((N_EXPERTS, TM), lambda i: (0, i))],
        out_shape=[jax.ShapeDtypeStruct((N, D_MODEL), F32),
                   jax.ShapeDtypeStruct((N, D_MODEL), F32),
                   jax.ShapeDtypeStruct((N_EXPERTS, N), F32)],
        scratch_shapes=[pltpu.VMEM((SSM_WIDTH // LANES, TM, LANES), F32)],
        compiler_params=_cparams(("parallel",)),
        name="mix",
    )(o, z, h0, wglu, bglu, woa, wos, g, b, wr)


def _select_body(aff_ref, ord_ref, idx_ref, *, capacity, n_slot_tiles, seq_len, pad_start):
    nbp = aff_ref.shape[1]
    ri = lax.broadcasted_iota(jnp.int32, (LANES, LANES), 0)
    ci = lax.broadcasted_iota(jnp.int32, (LANES, LANES), 1)
    strict_upper = (ri < ci).astype(BF16)
    incl_upper = (ri <= ci).astype(BF16)
    ones = jnp.ones((LANES, LANES), BF16)
    order = ord_ref[...]
    blk = lax.broadcasted_iota(jnp.int32, (nbp, LANES), 0).astype(F32)
    blk_hi = jnp.floor(blk * (1.0 / 16.0))
    blk_lo = blk - 16.0 * blk_hi
    dot = functools.partial(jnp.dot, preferred_element_type=F32)
    dot_nt = functools.partial(lax.dot_general, dimension_numbers=(((1,), (1,)), ((), ())),
                               preferred_element_type=F32)

    def per_expert(e, carry):
        bits = pltpu.bitcast(aff_ref[e], jnp.int32)
        t = jnp.zeros((1, 1), jnp.int32)
        for bit in range(30, -1, -1):
            cand = t | (1 << bit)
            cnt = jnp.sum((bits >= cand).astype(jnp.int32), keepdims=True)
            t = jnp.where(cnt >= capacity, cand, t)
        gt = bits > t
        eq = bits == t
        need = (capacity - jnp.sum(gt.astype(jnp.int32), keepdims=True)).astype(F32)
        eqb = eq.astype(BF16)
        tie_rank = dot(order, dot(eqb, ones).astype(BF16)) + dot(eqb, strict_upper)
        sel = (gt | (eq & (tie_rank < need))).astype(BF16)
        csum_in_blk = dot(sel, incl_upper)
        blk_tot = dot(sel, ones)
        blk_off = dot(order, blk_tot.astype(BF16))
        off_hi = jnp.floor(blk_off * (1.0 / 64.0))
        off_lo = blk_off - 64.0 * off_hi
        table = jnp.concatenate([csum_in_blk, off_hi, off_lo, blk_hi, blk_lo],
                                axis=1).astype(BF16)
        tot_l = dot_nt(ones[:SUBLANES], sel)
        off_l = dot_nt(tot_l.astype(BF16), order)[0:1, :]
        tot_l = tot_l[0:1, :]

        def per_tile(st, carry2):
            j = (st * SLOT_TILE + lax.broadcasted_iota(jnp.int32, (SLOT_TILE, 1), 0)).astype(F32)
            onehot = ((off_l <= j) & (j < off_l + tot_l)).astype(BF16)
            got = dot(onehot, table)
            local = j - (64.0 * got[:, LANES:2 * LANES] + got[:, 2 * LANES:3 * LANES])
            lane_idx = dot((got[:, :LANES] <= local).astype(BF16), ones)
            block = 16.0 * got[:, 3 * LANES:4 * LANES] + got[:, 4 * LANES:5 * LANES]
            k = j - capacity
            seq = sum((k >= PAD_ROWS * i).astype(F32) for i in range(1, SLOT_TILE // PAD_ROWS + 1))
            pad_tok = seq * (seq_len - PAD_ROWS) + pad_start + k
            tok = jnp.where(j < capacity, block * LANES + lane_idx, pad_tok)
            idx_ref[e, st] = tok.T[0:1, :].astype(jnp.int32)
            return carry2

        lax.fori_loop(0, n_slot_tiles, per_tile, 0)
        return carry

    lax.fori_loop(0, N_EXPERTS, per_expert, 0)


def _select(aff_blocks, order, capacity, n_slot_tiles, seq_len, pad_start):
    nbp = aff_blocks.shape[1]
    return pl.pallas_call(
        functools.partial(_select_body, capacity=capacity, n_slot_tiles=n_slot_tiles,
                          seq_len=seq_len, pad_start=pad_start),
        grid=(1,),
        in_specs=[_const_spec(N_EXPERTS, nbp, LANES), _const_spec(nbp, nbp)],
        out_specs=_const_spec(N_EXPERTS, n_slot_tiles, 1, SLOT_TILE),
        out_shape=jax.ShapeDtypeStruct((N_EXPERTS, n_slot_tiles, 1, SLOT_TILE), jnp.int32),
        compiler_params=_cparams(("arbitrary",)),
        name="select",
    )(aff_blocks, order)


def _moe_body(idx_ref, idx_next_ref, h1_hbm, wg_ref, wu_ref, wd_ref, wr_ref, acc_in, acc_hbm,
              xbuf, abuf, sem_x, sem_a, sem_s):
    del acc_in
    T = SLOT_TILE
    n_c = pl.num_programs(1)
    step = pl.program_id(0) * n_c + pl.program_id(1)
    n_steps = pl.num_programs(0) * n_c
    slot = step % 2

    def start_rows(copy_of_row, idx):
        def body(j, carry):
            copy_of_row(idx[0, 0, 0, j], j).start()
            return carry
        lax.fori_loop(0, T, body, 0, unroll=8)

    def x_row(buf):
        return lambda row, j: pltpu.make_async_copy(
            h1_hbm.at[pl.ds(row, 1)], xbuf.at[buf, pl.ds(j, 1)], sem_x.at[buf])

    def acc_row_in(row, j):
        return pltpu.make_async_copy(acc_hbm.at[pl.ds(row, 1)], abuf.at[pl.ds(j, 1)], sem_a)

    def acc_row_out(row, j):
        return pltpu.make_async_copy(abuf.at[pl.ds(j, 1)], acc_hbm.at[pl.ds(row, 1)], sem_s)

    def wait_scatter():
        pltpu.make_async_copy(abuf, acc_hbm.at[pl.ds(0, T)], sem_s).wait()

    @pl.when(step == 0)
    def _():
        start_rows(x_row(0), idx_ref)

    @pl.when(step > 0)
    def _():
        wait_scatter()

    start_rows(acc_row_in, idx_ref)

    @pl.when(step + 1 < n_steps)
    def _():
        start_rows(x_row(1 - slot), idx_next_ref)

    pltpu.make_async_copy(h1_hbm.at[pl.ds(0, T)], xbuf.at[slot], sem_x.at[slot]).wait()
    x = xbuf[slot].astype(BF16)
    logits = jnp.dot(x, wr_ref[...], preferred_element_type=F32)
    lane = lax.broadcasted_iota(jnp.int32, logits.shape, 1)
    logits = jnp.where(lane < N_EXPERTS, logits, NEG_BIG)
    p = jnp.exp(logits - jnp.max(logits, axis=-1, keepdims=True))
    gate = (jnp.sum(jnp.where(lane == pl.program_id(0), p, 0.0), axis=-1, keepdims=True)
            / jnp.sum(p, axis=-1, keepdims=True))
    y = jnp.zeros((T, D_MODEL), F32)
    for f in range(EXPERT_FF // FF_TILE):
        fs = slice(f * FF_TILE, (f + 1) * FF_TILE)
        hg = jnp.dot(x, wg_ref[0, :, fs], preferred_element_type=F32)
        hu = jnp.dot(x, wu_ref[0, :, fs], preferred_element_type=F32)
        hid = (jax.nn.silu(hg) * hu).astype(BF16)
        y = y + jnp.dot(hid, wd_ref[0, fs, :], preferred_element_type=F32)
    pltpu.make_async_copy(acc_hbm.at[pl.ds(0, T)], abuf, sem_a).wait()
    abuf[...] = abuf[...] + y * gate
    start_rows(acc_row_out, idx_ref)

    @pl.when(step == n_steps - 1)
    def _():
        wait_scatter()


def _moe(idx, h1, acc, wg, wu, wd, wr, n_slot_tiles):
    T = SLOT_TILE
    n_c = n_slot_tiles

    def next_block(e, c):
        return (jnp.minimum(e + (c + 1) // n_c, N_EXPERTS - 1), (c + 1) % n_c, 0, 0)

    return pl.pallas_call(
        _moe_body,
        grid=(N_EXPERTS, n_c),
        in_specs=[
            pl.BlockSpec((1, 1, 1, T), lambda e, c: (e, c, 0, 0), memory_space=pltpu.SMEM),
            pl.BlockSpec((1, 1, 1, T), next_block, memory_space=pltpu.SMEM),
            pl.BlockSpec(memory_space=pl.ANY),
            pl.BlockSpec((1, D_MODEL, EXPERT_FF), lambda e, c: (e, 0, 0)),
            pl.BlockSpec((1, D_MODEL, EXPERT_FF), lambda e, c: (e, 0, 0)),
            pl.BlockSpec((1, EXPERT_FF, D_MODEL), lambda e, c: (e, 0, 0)),
            _const_spec(D_MODEL, LANES),
            pl.BlockSpec(memory_space=pl.ANY),
        ],
        out_specs=pl.BlockSpec(memory_space=pl.ANY),
        out_shape=jax.ShapeDtypeStruct(acc.shape, F32),
        scratch_shapes=[pltpu.VMEM((2, T, D_MODEL), F32), pltpu.VMEM((T, D_MODEL), F32),
                        pltpu.SemaphoreType.DMA((2,)), pltpu.SemaphoreType.DMA(()),
                        pltpu.SemaphoreType.DMA(())],
        input_output_aliases={7: 0},
        compiler_params=pltpu.CompilerParams(dimension_semantics=("arbitrary", "arbitrary"),
                                             vmem_limit_bytes=VMEM_LIMIT,
                                             disable_bounds_checks=True),
        name="expert_ffn",
    )(idx, idx, h1, wg, wu, wd, wr, acc)


def _final_body(a_ref, g_ref, b_ref, o_ref):
    o_ref[0] = _layer_norm_rows(a_ref[0], g_ref[...], b_ref[...])


def _final_norm(acc, g, b, L):
    B = acc.shape[0]
    TM = EMBED_TILE
    return pl.pallas_call(
        _final_body,
        grid=(B, L // TM),
        in_specs=[pl.BlockSpec((1, TM, D_MODEL), lambda b_, j: (b_, j, 0)),
                  _const_spec(1, D_MODEL), _const_spec(1, D_MODEL)],
        out_specs=pl.BlockSpec((1, TM, D_MODEL), lambda b_, j: (b_, j, 0)),
        out_shape=jax.ShapeDtypeStruct((B, L, D_MODEL), F32),
        compiler_params=_cparams(("parallel", "parallel")),
        name="final_norm",
    )(acc, g, b)


def _rope_tables(L, Lp):
    rows = L // GRID_W
    t = jnp.arange(L, dtype=jnp.int32)
    m = jnp.arange(N_META, dtype=jnp.int32)
    pad = jnp.zeros((Lp - L - N_META,), jnp.int32)
    row = jnp.concatenate([t // GRID_W - rows // 2, pad,
                           jnp.full((N_META,), -(rows // 2) - 1, jnp.int32)])
    col = jnp.concatenate([t % GRID_W - GRID_W // 2, pad, m - GRID_W // 2])
    inv_freq = ROPE_THETA ** (-jnp.arange(0, ROPE_AXIS_DIM, 2, dtype=F32) / ROPE_AXIS_DIM)
    ang_r = row.astype(F32)[:, None] * inv_freq
    ang_c = col.astype(F32)[:, None] * inv_freq
    cos = jnp.concatenate([jnp.cos(ang_r)] * 2 + [jnp.cos(ang_c)] * 2, axis=1)
    sin = jnp.concatenate([-jnp.sin(ang_r), jnp.sin(ang_r), -jnp.sin(ang_c), jnp.sin(ang_c)], axis=1)
    return jnp.tile(cos, (1, 2)), jnp.tile(sin, (1, 2))


def _block_order(B, Lp, nbp):
    nbb = Lp // LANES
    r = jnp.arange(nbp)
    b, jb = r // nbb, r % nbb
    rank = jnp.where(r < B * nbb, b * nbb + jnp.where(jb == nbb - 1, 0, jb + 1), r)
    return (rank[None, :] < rank[:, None]).astype(BF16)


def _run_trunk(x, meta_pad, shared):
    B, L, _ = x.shape
    Lp = L + TAIL
    N = B * Lp
    cos, sin = _rope_tables(L, Lp)
    h0, q, k, vts, u = _embed(x, meta_pad, shared["ln_emb_g"], shared["ln_emb_b"], shared["w_in"],
                              shared["qg"], shared["kg"], shared["bones"], cos, sin, L, Lp)
    o = _attention(q, k, vts, L, Lp)
    z = _ssm(u.reshape(N_PAIRS, N // CHUNK, PAIR_W), shared["ssm_w1"], shared["ssm_w2"],
             shared["ssm_dec"], B, Lp // CHUNK)
    h1, acc, aff = _mix(o.reshape(N, ATTN_WIDTH), z, h0.reshape(N, D_MODEL),
                        shared["w_glu"], shared["b_glu"], shared["w_out_attn"], shared["w_out_ssm"],
                        shared["ln1_g"], shared["ln1_b"], shared["w_router"], L, Lp)

    capacity = EC_CAPACITY_FACTOR * B * (L + N_META) // N_EXPERTS
    n_slot_tiles = -(-capacity // SLOT_TILE)
    nb = N // LANES
    nbp = -(-nb // LANES) * LANES
    aff_blocks = jnp.pad(aff.reshape(N_EXPERTS, nb, LANES), ((0, 0), (0, nbp - nb), (0, 0)),
                         constant_values=-1.0)
    assert n_slot_tiles * SLOT_TILE - capacity <= B * PAD_ROWS
    idx = _select(aff_blocks, _block_order(B, Lp, nbp), capacity, n_slot_tiles, Lp, L)
    acc = _moe(idx, h1, acc, shared["w_gate"], shared["w_up"], shared["w_down"],
               shared["w_router"], n_slot_tiles)
    return _final_norm(acc.reshape(B, Lp, D_MODEL), shared["ln2_g"], shared["ln2_b"], L)


def kernel(x_prompt, x_sample, meta_tokens, ln_emb_g, ln_emb_b, w_in, q_norm_g, k_norm_g, ssm_lambda_re, ssm_lambda_im, ssm_log_dt, ssm_b_re, ssm_b_im, ssm_c_re, ssm_c_im, ssm_d, w_glu, b_glu, w_out, ln1_g, ln1_b, w_router, w_gate, w_up, w_down, ln2_g, ln2_b):
    row = lambda a: a.reshape(1, -1).astype(F32)
    w_q = w_in[0][:, :Q_END].reshape(D_MODEL, N_KV_HEADS, Q_PER_KV, HEAD_DIM)
    w_q = w_q.transpose(0, 2, 1, 3).reshape(D_MODEL, Q_END)
    w_oa = w_out[0][:ATTN_WIDTH].reshape(N_KV_HEADS, Q_PER_KV, HEAD_DIM, D_MODEL)
    w_oa = w_oa.transpose(1, 0, 2, 3).reshape(ATTN_WIDTH, D_MODEL)
    head_of = jnp.arange(Q_END) // HEAD_DIM
    ssm_w1, ssm_w2, ssm_dec = _ssm_weights(ssm_lambda_re[0], ssm_lambda_im[0], ssm_log_dt[0],
                                           ssm_b_re[0], ssm_b_im[0], ssm_c_re[0], ssm_c_im[0],
                                           ssm_d[0])
    shared = dict(
        ln_emb_g=row(ln_emb_g), ln_emb_b=row(ln_emb_b),
        w_in=jnp.concatenate([w_q, w_in[0][:, Q_END:]], axis=1).astype(BF16),
        qg=row(jnp.tile(q_norm_g[0], N_Q_HEADS)), kg=row(jnp.tile(k_norm_g[0], N_KV_HEADS)),
        bones=(head_of[:, None] == head_of[None, :]).astype(BF16),
        ssm_w1=ssm_w1, ssm_w2=ssm_w2, ssm_dec=ssm_dec,
        w_glu=w_glu[0].astype(BF16), b_glu=row(b_glu[0]),
        w_out_attn=w_oa.astype(BF16), w_out_ssm=w_out[0][ATTN_WIDTH:].astype(BF16),
        ln1_g=row(ln1_g[0]), ln1_b=row(ln1_b[0]),
        w_router=jnp.pad(w_router[0], ((0, 0), (0, LANES - N_EXPERTS))).astype(BF16),
        w_gate=w_gate[0].astype(BF16), w_up=w_up[0].astype(BF16), w_down=w_down[0].astype(BF16),
        ln2_g=row(ln2_g[0]), ln2_b=row(ln2_b[0]),
    )
    meta_pad = jnp.pad(meta_tokens.astype(F32), ((PAD_ROWS, 0), (0, 0)))[None]
    return (_run_trunk(x_prompt, meta_pad, shared), _run_trunk(x_sample, meta_pad, shared))
```

```python
import functools
import math

import jax
import jax.numpy as jnp
from jax import lax
from jax.experimental import pallas as pl
from jax.experimental.pallas import tpu as pltpu

F32 = jnp.float32
BF16 = jnp.bfloat16

D_MODEL = 1024
N_META = 16
GRID_W = 64
ATTN_WIDTH = 512
SSM_WIDTH = 512
HEAD_DIM = 64
N_Q_HEADS = 8
N_KV_HEADS = 2
Q_PER_KV = 4
KV_WIDTH = 128
ROPE_AXIS_DIM = 32
ROPE_THETA = 10000.0
ATTN_SCALE = HEAD_DIM ** -0.5
RMS_EPS = 1e-6
SSM_GROUP = 16
N_SSM_GROUPS = 32
SSM_STATE = 64
Q_END = ATTN_WIDTH
K_END = Q_END + KV_WIDTH
V_END = K_END + KV_WIDTH
IN_WIDTH = V_END + SSM_WIDTH
N_EXPERTS = 16
EXPERT_FF = 2048
EC_CAPACITY_FACTOR = 2
LN_EPS = 1e-5
DEPTH = 1
DEEPNORM_ALPHA = (2 * DEPTH) ** 0.25

LANES = 128
SUBLANES = 8
ROW_SLABS = D_MODEL // LANES
TAIL = LANES
PAD_ROWS = TAIL - N_META
CHUNK = 16
N_PAIRS = N_SSM_GROUPS // 2
PAIR_W = 2 * CHUNK * SSM_GROUP
PAIR_STATE = 2 * SSM_STATE
EMBED_TILE = 512
ROW_TILE = 384
Q_TILE = 128
KV_TILE = EMBED_TILE
KEY_SUBTILE = 256
SLOT_TILE = 384
FF_TILE = 512
SSM_ROW_TILE = 256
DEC_ROWS = 48
Q_PRESCALE = ATTN_SCALE * math.log2(math.e)
NEG_BIG = -1e30
VMEM_LIMIT = 56 * 1024 * 1024


def _cparams(sem):
    return pltpu.CompilerParams(dimension_semantics=sem, vmem_limit_bytes=VMEM_LIMIT)


def _const_spec(*shape):
    return pl.BlockSpec(shape, lambda *idx: (0,) * len(shape))


def _layer_norm_rows(x, g, b):
    mu = jnp.mean(x, axis=-1, keepdims=True)
    xc = x - mu
    var = jnp.mean(xc * xc, axis=-1, keepdims=True)
    return xc * lax.rsqrt(var + LN_EPS) * g + b


def _head_rms(t, gain, bones):
    sq = t * t
    hi = sq.astype(BF16)
    lo = (sq - hi.astype(F32)).astype(BF16)
    ss = (jnp.dot(hi, bones, preferred_element_type=F32)
          + jnp.dot(lo, bones, preferred_element_type=F32))
    return t * lax.rsqrt(ss * (1.0 / HEAD_DIM) + RMS_EPS) * gain


def _rope_slab(t, cos, sin_signed):
    lane = lax.broadcasted_iota(jnp.int32, t.shape, 1)
    first = (lane % ROPE_AXIS_DIM) < (ROPE_AXIS_DIM // 2)
    partner = jnp.where(first, pltpu.roll(t, LANES - 16, 1), pltpu.roll(t, 16, 1))
    return t * cos + partner * sin_signed


def _lane_piece_gather(pieces, lane16):
    out = None
    for j, (arr, src) in enumerate(pieces):
        shift = (SSM_GROUP * j - src) % LANES
        moved = pltpu.roll(arr, shift, 1) if shift else arr
        out = moved if out is None else jnp.where(lane16 == j, moved, out)
    return out


def _embed_body(x_ref, g_ref, b_ref, w_ref, qg_ref, kg_ref, bones_ref, cos_ref, sin_ref,
                h_ref, q_ref, k_ref, vt0_ref, vt1_ref, u_ref, u_scr):
    h = _layer_norm_rows(x_ref[0], g_ref[...], b_ref[...])
    h_ref[0] = h
    proj = jnp.dot(h.astype(BF16), w_ref[...], preferred_element_type=F32)
    cos = cos_ref[...]
    sin = sin_ref[...]
    qn = _head_rms(proj[:, :Q_END], qg_ref[...], bones_ref[...])
    for s in range(ATTN_WIDTH // LANES):
        sl = slice(s * LANES, (s + 1) * LANES)
        q_ref[0, :, sl] = (_rope_slab(qn[:, sl], cos, sin) * Q_PRESCALE).astype(BF16)
    kn = _head_rms(proj[:, Q_END:K_END], kg_ref[...], bones_ref[:KV_WIDTH, :KV_WIDTH])
    k_ref[0] = _rope_slab(kn, cos, sin).astype(BF16)
    vt = proj[:, K_END:V_END].T
    row = lax.broadcasted_iota(jnp.int32, vt.shape, 0)
    vt0_ref[0, 0] = jnp.where(row < HEAD_DIM, vt, 1.0).astype(BF16)
    vt1_ref[0, 0] = jnp.where(row >= HEAD_DIM, vt, 1.0).astype(BF16)
    n_rows = u_scr.shape[1] // CHUNK
    for v in range(SSM_WIDTH // LANES):
        u_scr[v] = proj[:, V_END + v * LANES:V_END + (v + 1) * LANES]
    by_token = [[u_scr[v, pl.ds(t, n_rows, stride=CHUNK), :] for v in range(SSM_WIDTH // LANES)]
                for t in range(CHUNK)]
    lane16 = lax.broadcasted_iota(jnp.int32, (n_rows, LANES), 1) // SSM_GROUP
    for p in range(N_PAIRS):
        for gi in range(2):
            src = (p % 4) * 2 * SSM_GROUP + gi * SSM_GROUP
            for h in range(CHUNK // SUBLANES):
                pieces = [(by_token[SUBLANES * h + j][p // 4], src) for j in range(SUBLANES)]
                lo = gi * CHUNK * SSM_GROUP + h * LANES
                u_ref[p, 0, :, lo:lo + LANES] = _lane_piece_gather(pieces, lane16)


def _embed_tail_body(x_ref, g_ref, b_ref, w_ref, qg_ref, kg_ref, bones_ref, cos_ref, sin_ref,
                     h_in, q_in, k_in, u_in, h_ref, q_ref, k_ref, vt0_ref, vt1_ref, u_ref, u_scr):
    del h_in, q_in, k_in, u_in
    _embed_body(x_ref, g_ref, b_ref, w_ref, qg_ref, kg_ref, bones_ref, cos_ref, sin_ref,
                h_ref, q_ref, k_ref, vt0_ref, vt1_ref, u_ref, u_scr)


def _embed(x, meta_pad, ln_g, ln_b, w_in, qg, kg, bones, cos, sin, L, Lp):
    B = x.shape[0]
    TM = EMBED_TILE
    nj = L // TM
    w_specs = [_const_spec(1, D_MODEL), _const_spec(1, D_MODEL), _const_spec(D_MODEL, IN_WIDTH),
               _const_spec(1, Q_END), _const_spec(1, KV_WIDTH), _const_spec(Q_END, Q_END)]

    def out_shapes(vt_tiles, vt_width):
        vt = jax.ShapeDtypeStruct((B, vt_tiles, KV_WIDTH, vt_width), BF16)
        return [
            jax.ShapeDtypeStruct((B, Lp, D_MODEL), F32),
            jax.ShapeDtypeStruct((B, Lp, Q_END), BF16),
            jax.ShapeDtypeStruct((B, Lp, KV_WIDTH), BF16),
            vt, vt,
            jax.ShapeDtypeStruct((N_PAIRS, B, Lp // CHUNK, PAIR_W), F32),
        ]

    def out_specs(tm, row_block):
        return [
            pl.BlockSpec((1, tm, D_MODEL), lambda b, j: (b, row_block(j), 0)),
            pl.BlockSpec((1, tm, Q_END), lambda b, j: (b, row_block(j), 0)),
            pl.BlockSpec((1, tm, KV_WIDTH), lambda b, j: (b, row_block(j), 0)),
            pl.BlockSpec((1, 1, KV_WIDTH, tm), lambda b, j: (b, j, 0, 0)),
            pl.BlockSpec((1, 1, KV_WIDTH, tm), lambda b, j: (b, j, 0, 0)),
            pl.BlockSpec((N_PAIRS, 1, tm // CHUNK, PAIR_W), lambda b, j: (0, b, row_block(j), 0)),
        ]

    main = pl.pallas_call(
        _embed_body,
        grid=(B, nj),
        in_specs=[pl.BlockSpec((1, TM, D_MODEL), lambda b, j: (b, j, 0))] + w_specs
        + [pl.BlockSpec((TM, LANES), lambda b, j: (j, 0))] * 2,
        out_specs=out_specs(TM, lambda j: j),
        out_shape=out_shapes(nj, TM),
        scratch_shapes=[pltpu.VMEM((SSM_WIDTH // LANES, TM, LANES), F32)],
        compiler_params=_cparams(("parallel", "parallel")),
        name="embed_main",
    )
    h0, q, k, vt0, vt1, u = main(x, ln_g, ln_b, w_in, qg, kg, bones, cos, sin)

    jt = Lp // TAIL - 1
    tail = pl.pallas_call(
        _embed_tail_body,
        grid=(B, 1),
        in_specs=[pl.BlockSpec((1, TAIL, D_MODEL), lambda b, j: (0, 0, 0))] + w_specs
        + [pl.BlockSpec((TAIL, LANES), lambda b, j: (jt, 0))] * 2
        + [pl.BlockSpec(memory_space=pl.ANY)] * 4,
        out_specs=out_specs(TAIL, lambda j: jt),
        out_shape=out_shapes(1, TAIL),
        input_output_aliases={9: 0, 10: 1, 11: 2, 12: 5},
        scratch_shapes=[pltpu.VMEM((SSM_WIDTH // LANES, TAIL, LANES), F32)],
        compiler_params=_cparams(("parallel", "arbitrary")),
        name="embed_tail",
    )
    h0, q, k, vt0_tail, vt1_tail, u = tail(meta_pad, ln_g, ln_b, w_in, qg, kg, bones, cos, sin,
                                           h0, q, k, u)
    return h0, q, k, (vt0, vt1, vt0_tail, vt1_tail), u


def _attn_body(q_ref, k_ref, vt0_ref, vt1_ref, vt0t_ref, vt1t_ref, o_ref,
               qt_ref, s0_ref, s1_ref, mc0_ref, mc1_ref, m_ref, acc_ref, *, L, n_kv_tiles):
    tq = Q_TILE
    cols = Q_PER_KV * tq
    row = lax.broadcasted_iota(jnp.int32, (KV_WIDTH, tq), 0)
    for r in range(Q_PER_KV):
        slab_t = q_ref[0, :, r * LANES:(r + 1) * LANES].astype(F32).T
        for g in range(N_KV_HEADS):
            in_group = (row >= g * HEAD_DIM) & (row < (g + 1) * HEAD_DIM)
            qt_ref[g, :, r * tq:(r + 1) * tq] = jnp.where(in_group, slab_t, 0.0).astype(BF16)
    m_ref[...] = jnp.full(m_ref.shape, NEG_BIG, F32)
    acc_ref[...] = jnp.zeros(acc_ref.shape, F32)

    s_bufs = ((s0_ref, mc0_ref), (s1_ref, mc1_ref))

    def scores(slot, k_tile, first_valid=0):
        s_ref, mc_ref = s_bufs[slot]
        n = k_tile.shape[0]
        for g in range(N_KV_HEADS):
            s = jnp.dot(k_tile, qt_ref[g], preferred_element_type=F32)
            if first_valid:
                key = lax.broadcasted_iota(jnp.int32, s.shape, 0)
                s = jnp.where(key >= first_valid, s, NEG_BIG)
            s_ref[g, 0:n, :] = s
            mc_ref[g] = jnp.broadcast_to(jnp.max(s, axis=0, keepdims=True), (SUBLANES, cols))

    def softmax_values(slot, vt_tiles, n):
        s_ref, mc_ref = s_bufs[slot]
        for g in range(N_KV_HEADS):
            m_old = m_ref[g]
            m_new = jnp.maximum(m_old, mc_ref[g])
            alpha = jnp.exp2(m_old[0:1, :] - m_new[0:1, :])
            acc = alpha * acc_ref[g]
            for k0 in range(0, n, KEY_SUBTILE):
                k1 = min(k0 + KEY_SUBTILE, n)
                p = jnp.exp2(s_ref[g, k0:k1, :] - m_new[0:1, :]).astype(BF16)
                acc = acc + jnp.dot(vt_tiles[g][:, k0:k1], p, preferred_element_type=F32)
            acc_ref[g] = acc
            m_ref[g] = m_new

    def k_tile(i):
        return k_ref[0, pl.ds(pl.multiple_of(i * KV_TILE, KV_TILE), KV_TILE), :]

    def vt_tiles(i):
        return vt0_ref[0, i], vt1_ref[0, i]

    scores(0, k_tile(0))

    def kv_pair(i, carry):
        softmax_values(0, vt_tiles(2 * i), KV_TILE)
        scores(1, k_tile(2 * i + 1))
        softmax_values(1, vt_tiles(2 * i + 1), KV_TILE)
        scores(0, k_tile(2 * i + 2))
        return carry

    lax.fori_loop(0, n_kv_tiles // 2 - 1, kv_pair, 0)
    last = n_kv_tiles - 1
    softmax_values(0, vt_tiles(last - 1), KV_TILE)
    scores(1, k_tile(last))
    softmax_values(1, vt_tiles(last), KV_TILE)
    scores(0, k_ref[0, L:L + TAIL, :], PAD_ROWS)
    softmax_values(0, (vt0t_ref[0, 0], vt1t_ref[0, 0]), TAIL)

    outs = []
    for g in range(N_KV_HEADS):
        other = (1 - g) * HEAD_DIM
        outs.append(acc_ref[g] / acc_ref[g, other:other + 1, :])
    row = lax.broadcasted_iota(jnp.int32, (KV_WIDTH, cols), 0)
    comb = jnp.where(row < HEAD_DIM, outs[0], outs[1])
    for r in range(Q_PER_KV):
        o_ref[0, :, r * LANES:(r + 1) * LANES] = comb[:, r * tq:(r + 1) * tq].T.astype(BF16)


def _attention(q, k, vts, L, Lp):
    B = q.shape[0]
    nkt = L // KV_TILE
    assert nkt % 2 == 0
    cols = Q_PER_KV * Q_TILE
    vt_spec = pl.BlockSpec((1, nkt, KV_WIDTH, KV_TILE), lambda b, j: (b, 0, 0, 0))
    vt_tail_spec = pl.BlockSpec((1, 1, KV_WIDTH, TAIL), lambda b, j: (b, 0, 0, 0))
    return pl.pallas_call(
        functools.partial(_attn_body, L=L, n_kv_tiles=nkt),
        grid=(B, Lp // Q_TILE),
        in_specs=[
            pl.BlockSpec((1, Q_TILE, Q_END), lambda b, j: (b, j, 0)),
            pl.BlockSpec((1, Lp, KV_WIDTH), lambda b, j: (b, 0, 0)),
            vt_spec, vt_spec, vt_tail_spec, vt_tail_spec,
        ],
        out_specs=pl.BlockSpec((1, Q_TILE, Q_END), lambda b, j: (b, j, 0)),
        out_shape=jax.ShapeDtypeStruct((B, Lp, Q_END), BF16),
        scratch_shapes=[
            pltpu.VMEM((N_KV_HEADS, KV_WIDTH, cols), BF16),
            pltpu.VMEM((N_KV_HEADS, KV_TILE, cols), F32),
            pltpu.VMEM((N_KV_HEADS, KV_TILE, cols), F32),
            pltpu.VMEM((N_KV_HEADS, SUBLANES, cols), F32),
            pltpu.VMEM((N_KV_HEADS, SUBLANES, cols), F32),
            pltpu.VMEM((N_KV_HEADS, SUBLANES, cols), F32),
            pltpu.VMEM((N_KV_HEADS, KV_WIDTH, cols), F32),
        ],
        compiler_params=_cparams(("parallel", "arbitrary")),
        name="attention",
    )(q, k, *vts)


def _ssm_weights(lam_re, lam_im, log_dt, b_re, b_im, c_re, c_im, d_skip):
    hp = lax.Precision.HIGHEST
    dt = jnp.exp(log_dt.astype(F32))[..., None]
    lr = lam_re.astype(F32)
    li = lam_im.astype(F32)

    def apow(n):
        n = jnp.asarray(n, F32)
        mag = jnp.exp(lr[..., None] * dt[..., None] * n)
        ang = li[..., None] * dt[..., None] * n
        return mag * jnp.cos(ang), mag * jnp.sin(ang)

    a1r, a1i = apow(jnp.ones((1,), F32))
    a1r, a1i = a1r[..., 0], a1i[..., 0]
    nr = a1r - 1.0
    den = lr * lr + li * li
    f_r = (nr * lr + a1i * li) / den
    f_i = (a1i * lr - nr * li) / den
    br = b_re.astype(F32)
    bi = b_im.astype(F32)
    bb_r = f_r[..., None] * br - f_i[..., None] * bi
    bb_i = f_r[..., None] * bi + f_i[..., None] * br
    cr = c_re.astype(F32)
    ci = c_im.astype(F32)

    tau = jnp.arange(CHUNK + 1, dtype=F32)
    pr, pi = apow(tau)
    ab_r = pr[..., None] * bb_r[:, :, :, None, :] - pi[..., None] * bb_i[:, :, :, None, :]
    ab_i = pr[..., None] * bb_i[:, :, :, None, :] + pi[..., None] * bb_r[:, :, :, None, :]
    kern = (jnp.einsum("dgop,dgptc->dgtco", cr, ab_r, precision=hp)
            - jnp.einsum("dgop,dgptc->dgtco", ci, ab_i, precision=hp))
    s_idx = jnp.arange(CHUNK)[:, None]
    t_idx = jnp.arange(CHUNK)[None, :]
    lag_f = jnp.clip(t_idx - s_idx, 0, CHUNK)
    lag_b = jnp.clip(s_idx - t_idx, 0, CHUNK)
    m_f = jnp.where((t_idx >= s_idx)[..., None, None], kern[0][:, lag_f], 0.0)
    m_b = jnp.where((s_idx >= t_idx)[..., None, None], kern[1][:, lag_b], 0.0)
    eye_t = jnp.eye(CHUNK, dtype=F32)[None, :, :, None, None]
    eye_c = jnp.eye(SSM_GROUP, dtype=F32)[None, None, None]
    dsk = d_skip.astype(F32).reshape(N_SSM_GROUPS, 1, 1, SSM_GROUP, 1)
    m_all = m_f + m_b + eye_t * eye_c * dsk
    m_all = m_all.transpose(0, 1, 3, 2, 4).reshape(N_SSM_GROUPS, 256, 256)

    def state_in(d, expo):
        er, ei = pr[d][..., expo], pi[d][..., expo]
        wr = er[..., None] * bb_r[d][:, :, None, :] - ei[..., None] * bb_i[d][:, :, None, :]
        wi = er[..., None] * bb_i[d][:, :, None, :] + ei[..., None] * bb_r[d][:, :, None, :]
        to_rows = lambda w: w.transpose(0, 2, 3, 1).reshape(N_SSM_GROUPS, 256, SSM_STATE)
        return to_rows(wr), to_rows(wi)

    sf_r, sf_i = state_in(0, jnp.arange(CHUNK - 1, -1, -1))
    sb_r, sb_i = state_in(1, jnp.arange(CHUNK))

    def state_out(d, expo):
        er, ei = pr[d][..., expo], pi[d][..., expo]
        wr = cr[d].transpose(0, 2, 1)[:, :, None, :] * er[..., None] \
            - ci[d].transpose(0, 2, 1)[:, :, None, :] * ei[..., None]
        wi = cr[d].transpose(0, 2, 1)[:, :, None, :] * ei[..., None] \
            + ci[d].transpose(0, 2, 1)[:, :, None, :] * er[..., None]
        flat = lambda w: w.reshape(N_SSM_GROUPS, SSM_STATE, 256)
        return flat(wr), flat(-wi)

    of_r, of_i = state_out(0, jnp.arange(1, CHUNK + 1))
    ob_r, ob_i = state_out(1, jnp.arange(CHUNK, 0, -1))

    def pair_rows(w):
        return w.reshape(N_PAIRS, 2, *w.shape[1:])

    z256 = jnp.zeros((N_PAIRS, 256, 256), F32)
    mp = pair_rows(m_all)
    w1_y = jnp.concatenate([jnp.concatenate([mp[:, 0], z256], axis=2),
                            jnp.concatenate([z256, mp[:, 1]], axis=2)], axis=1)

    def pair_cols_in(w):
        wp = pair_rows(w)
        z = jnp.zeros_like(wp[:, 0])
        return jnp.concatenate([jnp.concatenate([wp[:, 0], z], axis=2),
                                jnp.concatenate([z, wp[:, 1]], axis=2)], axis=1)

    w1 = jnp.concatenate([w1_y, pair_cols_in(sf_r), pair_cols_in(sf_i),
                          pair_cols_in(sb_r), pair_cols_in(sb_i)], axis=2)

    def pair_rows_out(w):
        wp = pair_rows(w)
        z = jnp.zeros_like(wp[:, 0])
        return jnp.concatenate([jnp.concatenate([wp[:, 0], z], axis=2),
                                jnp.concatenate([z, wp[:, 1]], axis=2)], axis=1)

    w2 = jnp.concatenate([pair_rows_out(of_r), pair_rows_out(of_i),
                          pair_rows_out(ob_r), pair_rows_out(ob_i)], axis=1)

    qr, qi = apow(CHUNK * jnp.arange(SUBLANES + 1, dtype=F32))
    zero = jnp.zeros_like(qr[..., 0])

    def dec_rows(d, q_expo):
        rows = [qr[d][..., 1], qi[d][..., 1], qr[d][..., 2], qi[d][..., 2],
                qr[d][..., 4], qi[d][..., 4], zero[d], zero[d]]
        rows += [qr[d][..., n] for n in q_expo] + [qi[d][..., n] for n in q_expo]
        return jnp.stack(rows, axis=1)

    dec = jnp.concatenate([dec_rows(0, range(1, SUBLANES + 1)),
                           dec_rows(1, range(SUBLANES, 0, -1))], axis=1)
    dec = dec.reshape(N_PAIRS, 2, DEC_ROWS, SSM_STATE).transpose(0, 2, 1, 3)
    dec = dec.reshape(N_PAIRS, DEC_ROWS, PAIR_STATE)
    return w1.astype(BF16), w2.astype(BF16), dec


def _ssm_body(u_ref, w1_ref, w2_ref, dec_ref, z_ref, sh_ref, *, B, n_chunks):
    R = B * n_chunks
    tiles = [(r0, min(SSM_ROW_TILE, R - r0)) for r0 in range(0, R, SSM_ROW_TILE)]
    PS = PAIR_STATE
    n_blocks = n_chunks // SUBLANES

    for r0, n in tiles:
        t = jnp.dot(u_ref[0, r0:r0 + n, :].astype(BF16), w1_ref[0], preferred_element_type=F32)
        z_ref[0, r0:r0 + n, :] = t[:, :PAIR_W]
        sh_ref[r0:r0 + n, :] = t[:, PAIR_W:]

    sub = lax.broadcasted_iota(jnp.int32, (SUBLANES, PS), 0)

    def cmul(ar, ai, xr, xi):
        return ar * xr - ai * xi, ar * xi + ai * xr

    def block_scan(sr, si, cr, ci, base, reverse):
        xr, xi = sr, si
        for k, d in enumerate((1, 2, 4)):
            pr_ = dec_ref[0, base + 2 * k:base + 2 * k + 1, :]
            pi_ = dec_ref[0, base + 2 * k + 1:base + 2 * k + 2, :]
            keep = (sub < SUBLANES - d) if reverse else (sub >= d)
            shift = SUBLANES - d if reverse else d
            yr = jnp.where(keep, pltpu.roll(xr, shift, 0), 0.0)
            yi = jnp.where(keep, pltpu.roll(xi, shift, 0), 0.0)
            mr, mi = cmul(pr_, pi_, yr, yi)
            xr, xi = xr + mr, xi + mi
        qr_ = dec_ref[0, base + 8:base + 16, :]
        qi_ = dec_ref[0, base + 16:base + 24, :]
        mr, mi = cmul(qr_, qi_, cr, ci)
        xr, xi = xr + mr, xi + mi
        edge = SUBLANES - 1 if reverse else 0
        shift = SUBLANES - 1 if reverse else 1
        er = jnp.where(sub == edge, cr, pltpu.roll(xr, shift, 0))
        ei = jnp.where(sub == edge, ci, pltpu.roll(xi, shift, 0))
        last = 0 if reverse else SUBLANES - 1
        nr = jnp.broadcast_to(xr[last:last + 1, :], (SUBLANES, PS))
        ni = jnp.broadcast_to(xi[last:last + 1, :], (SUBLANES, PS))
        return er, ei, nr, ni

    def scan_step(i, carry):
        new = []
        for b in range(B):
            cfr, cfi, cbr, cbi = carry[4 * b:4 * b + 4]
            tile_f = (i + n_blocks - 1) % n_blocks
            tile_b = (2 * n_blocks - 2 - i) % n_blocks
            rf = pl.multiple_of(b * n_chunks + tile_f * SUBLANES, SUBLANES)
            rb = pl.multiple_of(b * n_chunks + tile_b * SUBLANES, SUBLANES)
            f_rows, b_rows = pl.ds(rf, SUBLANES), pl.ds(rb, SUBLANES)
            er, ei, cfr, cfi = block_scan(sh_ref[f_rows, 0:PS], sh_ref[f_rows, PS:2 * PS],
                                          cfr, cfi, 0, False)
            sh_ref[f_rows, 0:PS] = er
            sh_ref[f_rows, PS:2 * PS] = ei
            er, ei, cbr, cbi = block_scan(sh_ref[b_rows, 2 * PS:3 * PS], sh_ref[b_rows, 3 * PS:4 * PS],
                                          cbr, cbi, DEC_ROWS // 2, True)
            sh_ref[b_rows, 2 * PS:3 * PS] = er
            sh_ref[b_rows, 3 * PS:4 * PS] = ei
            new += [cfr, cfi, cbr, cbi]
        return tuple(new)

    zero = jnp.zeros((SUBLANES, PS), F32)
    lax.fori_loop(0, n_blocks, scan_step, (zero,) * (4 * B))

    for r0, n in tiles:
        y = z_ref[0, r0:r0 + n, :] + jnp.dot(sh_ref[r0:r0 + n, :].astype(BF16), w2_ref[0],
                                             preferred_element_type=F32)
        z_ref[0, r0:r0 + n, :] = jax.nn.gelu(y)


def _ssm(u_pairs, w1, w2, dec, B, n_chunks):
    assert n_chunks % SUBLANES == 0
    R = B * n_chunks
    return pl.pallas_call(
        functools.partial(_ssm_body, B=B, n_chunks=n_chunks),
        grid=(N_PAIRS,),
        in_specs=[
            pl.BlockSpec((1, R, PAIR_W), lambda p: (p, 0, 0)),
            pl.BlockSpec((1, PAIR_W, 2 * PAIR_W), lambda p: (p, 0, 0)),
            pl.BlockSpec((1, PAIR_W, PAIR_W), lambda p: (p, 0, 0)),
            pl.BlockSpec((1, DEC_ROWS, PAIR_STATE), lambda p: (p, 0, 0)),
        ],
        out_specs=pl.BlockSpec((1, R, PAIR_W), lambda p: (p, 0, 0)),
        out_shape=jax.ShapeDtypeStruct((N_PAIRS, R, PAIR_W), F32),
        scratch_shapes=[pltpu.VMEM((R, 4 * PAIR_STATE), F32)],
        compiler_params=_cparams(("parallel",)),
        name="ssm",
    )(u_pairs, w1, w2, dec)


def _mix_body(o_ref, zp_ref, h0_ref, wglu_ref, bglu_ref, woa_ref, wos_ref, g_ref, b_ref, wr_ref,
              h1_ref, acc_ref, aff_ref, z_scr, *, L, Lp):
    n_rows = zp_ref.shape[1]
    lane16 = lax.broadcasted_iota(jnp.int32, (n_rows, LANES), 1) // SSM_GROUP
    for t in range(CHUNK):
        src = (t % SUBLANES) * SSM_GROUP
        for v in range(SSM_WIDTH // LANES):
            pieces = []
            for j in range(SUBLANES):
                p, gi = 4 * v + j // 2, j % 2
                lo = gi * CHUNK * SSM_GROUP + (t // SUBLANES) * LANES
                pieces.append((zp_ref[p, :, lo:lo + LANES], src))
            z_scr[v, pl.ds(t, n_rows, stride=CHUNK), :] = _lane_piece_gather(pieces, lane16)
    z = jnp.concatenate([z_scr[v] for v in range(SSM_WIDTH // LANES)], axis=1)
    gate = jax.nn.sigmoid(jnp.dot(z.astype(BF16), wglu_ref[...], preferred_element_type=F32)
                          + bglu_ref[...])
    ssm_out = (z * gate).astype(BF16)
    mix = (jnp.dot(o_ref[...], woa_ref[...], preferred_element_type=F32)
           + jnp.dot(ssm_out, wos_ref[...], preferred_element_type=F32))
    h1 = _layer_norm_rows(DEEPNORM_ALPHA * h0_ref[...] + mix, g_ref[...], b_ref[...])
    for s in range(ROW_SLABS):
        sl = slice(s * LANES, (s + 1) * LANES)
        h1_ref[:, s, :] = h1[:, sl]
        acc_ref[:, s, :] = DEEPNORM_ALPHA * h1[:, sl]
    logits = jnp.dot(h1.astype(BF16), wr_ref[...], preferred_element_type=F32)
    lane = lax.broadcasted_iota(jnp.int32, logits.shape, 1)
    logits = jnp.where(lane < N_EXPERTS, logits, NEG_BIG)
    e = jnp.exp(logits - jnp.max(logits, axis=-1, keepdims=True))
    aff = (e / jnp.sum(e, axis=-1, keepdims=True)).T[:N_EXPERTS, :]
    tiles_per_seq = Lp // ROW_TILE
    pos = (pl.program_id(0) % tiles_per_seq) * ROW_TILE + lax.broadcasted_iota(
        jnp.int32, aff.shape, 1)
    aff_ref[...] = jnp.where((pos < L) | (pos >= L + PAD_ROWS), aff, -1.0)


def _mix(o, z, h0, wglu, bglu, woa, wos, g, b, wr, L, Lp):
    N = o.shape[0]
    TM = ROW_TILE
    row = lambda w: pl.BlockSpec((TM, w), lambda i: (i, 0))
    tok_rows = pl.BlockSpec((TM, ROW_SLABS, LANES), lambda i: (i, 0, 0))
    return pl.pallas_call(
        functools.partial(_mix_body, L=L, Lp=Lp),
        grid=(N // TM,),
        in_specs=[row(ATTN_WIDTH), pl.BlockSpec((N_PAIRS, TM // CHUNK, PAIR_W), lambda i: (0, i, 0)),
                  row(D_MODEL),
                  _const_spec(SSM_WIDTH, SSM_WIDTH), _const_spec(1, SSM_WIDTH),
                  _const_spec(ATTN_WIDTH, D_MODEL), _const_spec(SSM_WIDTH, D_MODEL),
                  _const_spec(1, D_MODEL), _const_spec(1, D_MODEL), _const_spec(D_MODEL, LANES)],
        out_specs=[tok_rows, tok_rows, pl.BlockSpec((N_EXPERTS, TM), lambda i: (0, i))],
        out_shape=[jax.ShapeDtypeStruct((N, ROW_SLABS, LANES), F32),
                   jax.ShapeDtypeStruct((N, ROW_SLABS, LANES), F32),
                   jax.ShapeDtypeStruct((N_EXPERTS, N), F32)],
        scratch_shapes=[pltpu.VMEM((SSM_WIDTH // LANES, TM, LANES), F32)],
        compiler_params=_cparams(("parallel",)),
        name="mix",
    )(o, z, h0, wglu, bglu, woa, wos, g, b, wr)


def _select_body(aff_ref, ord_ref, idx_ref, *, capacity, n_slot_tiles, seq_len, pad_start):
    nbp = aff_ref.shape[1]
    ri = lax.broadcasted_iota(jnp.int32, (LANES, LANES), 0)
    ci = lax.broadcasted_iota(jnp.int32, (LANES, LANES), 1)
    strict_upper = (ri < ci).astype(BF16)
    incl_upper = (ri <= ci).astype(BF16)
    ones = jnp.ones((LANES, LANES), BF16)
    order = ord_ref[...]
    blk = lax.broadcasted_iota(jnp.int32, (nbp, LANES), 0).astype(F32)
    blk_hi = jnp.floor(blk * (1.0 / 16.0))
    blk_lo = blk - 16.0 * blk_hi
    dot = functools.partial(jnp.dot, preferred_element_type=F32)
    dot_nt = functools.partial(lax.dot_general, dimension_numbers=(((1,), (1,)), ((), ())),
                               preferred_element_type=F32)

    def per_expert(e, carry):
        bits = pltpu.bitcast(aff_ref[e], jnp.int32)
        t = jnp.zeros((1, 1), jnp.int32)
        for bit in range(30, -1, -1):
            cand = t | (1 << bit)
            cnt = jnp.sum((bits >= cand).astype(jnp.int32), keepdims=True)
            t = jnp.where(cnt >= capacity, cand, t)
        gt = bits > t
        eq = bits == t
        need = (capacity - jnp.sum(gt.astype(jnp.int32), keepdims=True)).astype(F32)
        eqb = eq.astype(BF16)
        tie_rank = dot(order, dot(eqb, ones).astype(BF16)) + dot(eqb, strict_upper)
        sel = (gt | (eq & (tie_rank < need))).astype(BF16)
        csum_in_blk = dot(sel, incl_upper)
        blk_tot = dot(sel, ones)
        blk_off = dot(order, blk_tot.astype(BF16))
        off_hi = jnp.floor(blk_off * (1.0 / 64.0))
        off_lo = blk_off - 64.0 * off_hi
        table = jnp.concatenate([csum_in_blk, off_hi, off_lo, blk_hi, blk_lo],
                                axis=1).astype(BF16)
        tot_l = dot_nt(ones[:SUBLANES], sel)
        off_l = dot_nt(tot_l.astype(BF16), order)[0:1, :]
        tot_l = tot_l[0:1, :]

        def per_tile(st, carry2):
            j = (st * SLOT_TILE + lax.broadcasted_iota(jnp.int32, (SLOT_TILE, 1), 0)).astype(F32)
            onehot = ((off_l <= j) & (j < off_l + tot_l)).astype(BF16)
            got = dot(onehot, table)
            local = j - (64.0 * got[:, LANES:2 * LANES] + got[:, 2 * LANES:3 * LANES])
            lane_idx = dot((got[:, :LANES] <= local).astype(BF16), ones)
            block = 16.0 * got[:, 3 * LANES:4 * LANES] + got[:, 4 * LANES:5 * LANES]
            k = j - capacity
            seq = sum((k >= PAD_ROWS * i).astype(F32) for i in range(1, SLOT_TILE // PAD_ROWS + 1))
            pad_tok = seq * (seq_len - PAD_ROWS) + pad_start + k
            tok = jnp.where(j < capacity, block * LANES + lane_idx, pad_tok)
            idx_ref[e, st] = tok.T[0:1, :].astype(jnp.int32)
            return carry2

        lax.fori_loop(0, n_slot_tiles, per_tile, 0)
        return carry

    lax.fori_loop(0, N_EXPERTS, per_expert, 0)


def _select(aff_blocks, order, capacity, n_slot_tiles, seq_len, pad_start):
    nbp = aff_blocks.shape[1]
    return pl.pallas_call(
        functools.partial(_select_body, capacity=capacity, n_slot_tiles=n_slot_tiles,
                          seq_len=seq_len, pad_start=pad_start),
        grid=(1,),
        in_specs=[_const_spec(N_EXPERTS, nbp, LANES), _const_spec(nbp, nbp)],
        out_specs=_const_spec(N_EXPERTS, n_slot_tiles, 1, SLOT_TILE),
        out_shape=jax.ShapeDtypeStruct((N_EXPERTS, n_slot_tiles, 1, SLOT_TILE), jnp.int32),
        compiler_params=_cparams(("arbitrary",)),
        name="select",
    )(aff_blocks, order)


def _moe_body(idx_ref, idx_next_ref, h1_hbm, wg_ref, wu_ref, wd_ref, wr_ref, acc_in, acc_hbm,
              xbuf, abuf, sem_x, sem_a, sem_s):
    del acc_in
    T = SLOT_TILE
    n_c = pl.num_programs(1)
    step = pl.program_id(0) * n_c + pl.program_id(1)
    n_steps = pl.num_programs(0) * n_c
    slot = step % 2

    def start_rows(copy_of_row, idx):
        def body(j, carry):
            copy_of_row(idx[0, 0, 0, j], j).start()
            return carry
        lax.fori_loop(0, T, body, 0, unroll=8)

    def x_row(buf):
        return lambda row, j: pltpu.make_async_copy(h1_hbm.at[row], xbuf.at[buf, j], sem_x.at[buf])

    def acc_row_in(row, j):
        return pltpu.make_async_copy(acc_hbm.at[row], abuf.at[j], sem_a)

    def acc_row_out(row, j):
        return pltpu.make_async_copy(abuf.at[j], acc_hbm.at[row], sem_s)

    def wait_scatter():
        pltpu.make_async_copy(abuf, acc_hbm.at[pl.ds(0, T)], sem_s).wait()

    @pl.when(step == 0)
    def _():
        start_rows(x_row(0), idx_ref)

    @pl.when(step > 0)
    def _():
        wait_scatter()

    start_rows(acc_row_in, idx_ref)

    @pl.when(step + 1 < n_steps)
    def _():
        start_rows(x_row(1 - slot), idx_next_ref)

    pltpu.make_async_copy(h1_hbm.at[pl.ds(0, T)], xbuf.at[slot], sem_x.at[slot]).wait()
    x = jnp.concatenate([xbuf[slot, :, s, :] for s in range(ROW_SLABS)], axis=1).astype(BF16)
    logits = jnp.dot(x, wr_ref[...], preferred_element_type=F32)
    lane = lax.broadcasted_iota(jnp.int32, logits.shape, 1)
    logits = jnp.where(lane < N_EXPERTS, logits, NEG_BIG)
    p = jnp.exp(logits - jnp.max(logits, axis=-1, keepdims=True))
    gate = (jnp.sum(jnp.where(lane == pl.program_id(0), p, 0.0), axis=-1, keepdims=True)
            / jnp.sum(p, axis=-1, keepdims=True))
    y = jnp.zeros((T, D_MODEL), F32)
    for f in range(EXPERT_FF // FF_TILE):
        fs = slice(f * FF_TILE, (f + 1) * FF_TILE)
        hg = jnp.dot(x, wg_ref[0, :, fs], preferred_element_type=F32)
        hu = jnp.dot(x, wu_ref[0, :, fs], preferred_element_type=F32)
        hid = (jax.nn.silu(hg) * hu).astype(BF16)
        y = y + jnp.dot(hid, wd_ref[0, fs, :], preferred_element_type=F32)
    pltpu.make_async_copy(acc_hbm.at[pl.ds(0, T)], abuf, sem_a).wait()
    y = y * gate
    for s in range(ROW_SLABS):
        abuf[:, s, :] = abuf[:, s, :] + y[:, s * LANES:(s + 1) * LANES]
    start_rows(acc_row_out, idx_ref)

    @pl.when(step == n_steps - 1)
    def _():
        wait_scatter()


def _moe(idx, h1, acc, wg, wu, wd, wr, n_slot_tiles):
    T = SLOT_TILE
    n_c = n_slot_tiles

    def next_block(e, c):
        return (jnp.minimum(e + (c + 1) // n_c, N_EXPERTS - 1), (c + 1) % n_c, 0, 0)

    return pl.pallas_call(
        _moe_body,
        grid=(N_EXPERTS, n_c),
        in_specs=[
            pl.BlockSpec((1, 1, 1, T), lambda e, c: (e, c, 0, 0), memory_space=pltpu.SMEM),
            pl.BlockSpec((1, 1, 1, T), next_block, memory_space=pltpu.SMEM),
            pl.BlockSpec(memory_space=pl.ANY),
            pl.BlockSpec((1, D_MODEL, EXPERT_FF), lambda e, c: (e, 0, 0)),
            pl.BlockSpec((1, D_MODEL, EXPERT_FF), lambda e, c: (e, 0, 0)),
            pl.BlockSpec((1, EXPERT_FF, D_MODEL), lambda e, c: (e, 0, 0)),
            _const_spec(D_MODEL, LANES),
            pl.BlockSpec(memory_space=pl.ANY),
        ],
        out_specs=pl.BlockSpec(memory_space=pl.ANY),
        out_shape=jax.ShapeDtypeStruct(acc.shape, F32),
        scratch_shapes=[pltpu.VMEM((2, T, ROW_SLABS, LANES), F32),
                        pltpu.VMEM((T, ROW_SLABS, LANES), F32),
                        pltpu.SemaphoreType.DMA((2,)), pltpu.SemaphoreType.DMA(()),
                        pltpu.SemaphoreType.DMA(())],
        input_output_aliases={7: 0},
        compiler_params=pltpu.CompilerParams(dimension_semantics=("arbitrary", "arbitrary"),
                                             vmem_limit_bytes=VMEM_LIMIT,
                                             disable_bounds_checks=True),
        name="expert_ffn",
    )(idx, idx, h1, wg, wu, wd, wr, acc)


def _final_body(a_ref, g_ref, b_ref, o_ref):
    x = jnp.concatenate([a_ref[0, :, s, :] for s in range(ROW_SLABS)], axis=1)
    o_ref[0] = _layer_norm_rows(x, g_ref[...], b_ref[...])


def _final_norm(acc, g, b, L):
    B = acc.shape[0]
    TM = EMBED_TILE
    return pl.pallas_call(
        _final_body,
        grid=(B, L // TM),
        in_specs=[pl.BlockSpec((1, TM, ROW_SLABS, LANES), lambda b_, j: (b_, j, 0, 0)),
                  _const_spec(1, D_MODEL), _const_spec(1, D_MODEL)],
        out_specs=pl.BlockSpec((1, TM, D_MODEL), lambda b_, j: (b_, j, 0)),
        out_shape=jax.ShapeDtypeStruct((B, L, D_MODEL), F32),
        compiler_params=_cparams(("parallel", "parallel")),
        name="final_norm",
    )(acc, g, b)


def _rope_tables(L, Lp):
    rows = L // GRID_W
    t = jnp.arange(L, dtype=jnp.int32)
    m = jnp.arange(N_META, dtype=jnp.int32)
    pad = jnp.zeros((Lp - L - N_META,), jnp.int32)
    row = jnp.concatenate([t // GRID_W - rows // 2, pad,
                           jnp.full((N_META,), -(rows // 2) - 1, jnp.int32)])
    col = jnp.concatenate([t % GRID_W - GRID_W // 2, pad, m - GRID_W // 2])
    inv_freq = ROPE_THETA ** (-jnp.arange(0, ROPE_AXIS_DIM, 2, dtype=F32) / ROPE_AXIS_DIM)
    ang_r = row.astype(F32)[:, None] * inv_freq
    ang_c = col.astype(F32)[:, None] * inv_freq
    cos = jnp.concatenate([jnp.cos(ang_r)] * 2 + [jnp.cos(ang_c)] * 2, axis=1)
    sin = jnp.concatenate([-jnp.sin(ang_r), jnp.sin(ang_r), -jnp.sin(ang_c), jnp.sin(ang_c)], axis=1)
    return jnp.tile(cos, (1, 2)), jnp.tile(sin, (1, 2))


def _block_order(B, Lp, nbp):
    nbb = Lp // LANES
    r = jnp.arange(nbp)
    b, jb = r // nbb, r % nbb
    rank = jnp.where(r < B * nbb, b * nbb + jnp.where(jb == nbb - 1, 0, jb + 1), r)
    return (rank[None, :] < rank[:, None]).astype(BF16)


def _run_trunk(x, meta_pad, shared):
    B, L, _ = x.shape
    Lp = L + TAIL
    N = B * Lp
    cos, sin = _rope_tables(L, Lp)
    h0, q, k, vts, u = _embed(x, meta_pad, shared["ln_emb_g"], shared["ln_emb_b"], shared["w_in"],
                              shared["qg"], shared["kg"], shared["bones"], cos, sin, L, Lp)
    o = _attention(q, k, vts, L, Lp)
    z = _ssm(u.reshape(N_PAIRS, N // CHUNK, PAIR_W), shared["ssm_w1"], shared["ssm_w2"],
             shared["ssm_dec"], B, Lp // CHUNK)
    h1, acc, aff = _mix(o.reshape(N, ATTN_WIDTH), z, h0.reshape(N, D_MODEL),
                        shared["w_glu"], shared["b_glu"], shared["w_out_attn"], shared["w_out_ssm"],
                        shared["ln1_g"], shared["ln1_b"], shared["w_router"], L, Lp)

    capacity = EC_CAPACITY_FACTOR * B * (L + N_META) // N_EXPERTS
    n_slot_tiles = -(-capacity // SLOT_TILE)
    nb = N // LANES
    nbp = -(-nb // LANES) * LANES
    aff_blocks = jnp.pad(aff.reshape(N_EXPERTS, nb, LANES), ((0, 0), (0, nbp - nb), (0, 0)),
                         constant_values=-1.0)
    assert n_slot_tiles * SLOT_TILE - capacity <= B * PAD_ROWS
    idx = _select(aff_blocks, _block_order(B, Lp, nbp), capacity, n_slot_tiles, Lp, L)
    acc = _moe(idx, h1, acc, shared["w_gate"], shared["w_up"], shared["w_down"],
               shared["w_router"], n_slot_tiles)
    return _final_norm(acc.reshape(B, Lp, ROW_SLABS, LANES), shared["ln2_g"], shared["ln2_b"], L)


def kernel(x_prompt, x_sample, meta_tokens, ln_emb_g, ln_emb_b, w_in, q_norm_g, k_norm_g, ssm_lambda_re, ssm_lambda_im, ssm_log_dt, ssm_b_re, ssm_b_im, ssm_c_re, ssm_c_im, ssm_d, w_glu, b_glu, w_out, ln1_g, ln1_b, w_router, w_gate, w_up, w_down, ln2_g, ln2_b):
    row = lambda a: a.reshape(1, -1).astype(F32)
    w_q = w_in[0][:, :Q_END].reshape(D_MODEL, N_KV_HEADS, Q_PER_KV, HEAD_DIM)
    w_q = w_q.transpose(0, 2, 1, 3).reshape(D_MODEL, Q_END)
    w_oa = w_out[0][:ATTN_WIDTH].reshape(N_KV_HEADS, Q_PER_KV, HEAD_DIM, D_MODEL)
    w_oa = w_oa.transpose(1, 0, 2, 3).reshape(ATTN_WIDTH, D_MODEL)
    head_of = jnp.arange(Q_END) // HEAD_DIM
    ssm_w1, ssm_w2, ssm_dec = _ssm_weights(ssm_lambda_re[0], ssm_lambda_im[0], ssm_log_dt[0],
                                           ssm_b_re[0], ssm_b_im[0], ssm_c_re[0], ssm_c_im[0],
                                           ssm_d[0])
    shared = dict(
        ln_emb_g=row(ln_emb_g), ln_emb_b=row(ln_emb_b),
        w_in=jnp.concatenate([w_q, w_in[0][:, Q_END:]], axis=1).astype(BF16),
        qg=row(jnp.tile(q_norm_g[0], N_Q_HEADS)), kg=row(jnp.tile(k_norm_g[0], N_KV_HEADS)),
        bones=(head_of[:, None] == head_of[None, :]).astype(BF16),
        ssm_w1=ssm_w1, ssm_w2=ssm_w2, ssm_dec=ssm_dec,
        w_glu=w_glu[0].astype(BF16), b_glu=row(b_glu[0]),
        w_out_attn=w_oa.astype(BF16), w_out_ssm=w_out[0][ATTN_WIDTH:].astype(BF16),
        ln1_g=row(ln1_g[0]), ln1_b=row(ln1_b[0]),
        w_router=jnp.pad(w_router[0], ((0, 0), (0, LANES - N_EXPERTS))).astype(BF16),
        w_gate=w_gate[0].astype(BF16), w_up=w_up[0].astype(BF16), w_down=w_down[0].astype(BF16),
        ln2_g=row(ln2_g[0]), ln2_b=row(ln2_b[0]),
    )
    meta_pad = jnp.pad(meta_tokens.astype(F32), ((PAD_ROWS, 0), (0, 0)))[None]
    return (_run_trunk(x_prompt, meta_pad, shared), _run_trunk(x_sample, meta_pad, shared))
```

```python
import functools
import math

import jax
import jax.numpy as jnp
from jax import lax
from jax.experimental import pallas as pl
from jax.experimental.pallas import tpu as pltpu

F32 = jnp.float32
BF16 = jnp.bfloat16

D_MODEL = 1024
N_META = 16
GRID_W = 64
ATTN_WIDTH = 512
SSM_WIDTH = 512
HEAD_DIM = 64
N_Q_HEADS = 8
N_KV_HEADS = 2
Q_PER_KV = 4
KV_WIDTH = 128
ROPE_AXIS_DIM = 32
ROPE_THETA = 10000.0
ATTN_SCALE = HEAD_DIM ** -0.5
RMS_EPS = 1e-6
SSM_GROUP = 16
N_SSM_GROUPS = 32
SSM_STATE = 64
Q_END = ATTN_WIDTH
K_END = Q_END + KV_WIDTH
V_END = K_END + KV_WIDTH
IN_WIDTH = V_END + SSM_WIDTH
N_EXPERTS = 16
EXPERT_FF = 2048
EC_CAPACITY_FACTOR = 2
LN_EPS = 1e-5
DEPTH = 1
DEEPNORM_ALPHA = (2 * DEPTH) ** 0.25

LANES = 128
SUBLANES = 8
ROW_SLABS = D_MODEL // LANES
TAIL = LANES
PAD_ROWS = TAIL - N_META
CHUNK = 16
N_PAIRS = N_SSM_GROUPS // 2
PAIR_W = 2 * CHUNK * SSM_GROUP
PAIR_STATE = 2 * SSM_STATE
EMBED_TILE = 512
ROW_TILE = 384
Q_TILE = 128
KV_TILE = EMBED_TILE
KEY_SUBTILE = 256
KV_UNROLL = 4
SLOT_TILE = 384
FF_TILE = 512
SSM_ROW_TILE = 256
DEC_ROWS = 48
VT_ROWS = HEAD_DIM + 16
Q_PRESCALE = ATTN_SCALE * math.log2(math.e)
NEG_BIG = -1e30
VMEM_LIMIT = 56 * 1024 * 1024


def _cparams(sem):
    return pltpu.CompilerParams(dimension_semantics=sem, vmem_limit_bytes=VMEM_LIMIT)


def _const_spec(*shape):
    return pl.BlockSpec(shape, lambda *idx: (0,) * len(shape))


def _layer_norm_rows(x, g, b):
    mu = jnp.mean(x, axis=-1, keepdims=True)
    xc = x - mu
    var = jnp.mean(xc * xc, axis=-1, keepdims=True)
    return xc * lax.rsqrt(var + LN_EPS) * g + b


def _head_rms(t, gain, bones):
    sq = t * t
    hi = sq.astype(BF16)
    lo = (sq - hi.astype(F32)).astype(BF16)
    ss = (jnp.dot(hi, bones, preferred_element_type=F32)
          + jnp.dot(lo, bones, preferred_element_type=F32))
    return t * lax.rsqrt(ss * (1.0 / HEAD_DIM) + RMS_EPS) * gain


def _rope_slab(t, cos, sin_signed):
    lane = lax.broadcasted_iota(jnp.int32, t.shape, 1)
    first = (lane % ROPE_AXIS_DIM) < (ROPE_AXIS_DIM // 2)
    partner = jnp.where(first, pltpu.roll(t, LANES - 16, 1), pltpu.roll(t, 16, 1))
    return t * cos + partner * sin_signed


def _lane_piece_gather(pieces, lane16):
    out = None
    for j, (arr, src) in enumerate(pieces):
        shift = (SSM_GROUP * j - src) % LANES
        moved = pltpu.roll(arr, shift, 1) if shift else arr
        out = moved if out is None else jnp.where(lane16 == j, moved, out)
    return out


def _embed_body(x_ref, g_ref, b_ref, w_ref, qg_ref, kg_ref, bones_ref, cos_ref, sin_ref,
                h_ref, q_ref, k_ref, vt0_ref, vt1_ref, u_ref, u_scr):
    h = _layer_norm_rows(x_ref[0], g_ref[...], b_ref[...])
    h_ref[0] = h
    proj = jnp.dot(h.astype(BF16), w_ref[...], preferred_element_type=F32)
    cos = cos_ref[...]
    sin = sin_ref[...]
    qn = _head_rms(proj[:, :Q_END], qg_ref[...], bones_ref[...])
    for s in range(ATTN_WIDTH // LANES):
        sl = slice(s * LANES, (s + 1) * LANES)
        q_ref[0, :, sl] = (_rope_slab(qn[:, sl], cos, sin) * Q_PRESCALE).astype(BF16)
    kn = _head_rms(proj[:, Q_END:K_END], kg_ref[...], bones_ref[:KV_WIDTH, :KV_WIDTH])
    k_ref[0] = _rope_slab(kn, cos, sin).astype(BF16)
    vt = proj[:, K_END:V_END].T
    ones = jnp.ones((VT_ROWS - HEAD_DIM, vt.shape[1]), F32)
    vt0_ref[0, 0] = jnp.concatenate([vt[:HEAD_DIM], ones], axis=0).astype(BF16)
    vt1_ref[0, 0] = jnp.concatenate([vt[HEAD_DIM:], ones], axis=0).astype(BF16)
    n_rows = u_scr.shape[1] // CHUNK
    for v in range(SSM_WIDTH // LANES):
        u_scr[v] = proj[:, V_END + v * LANES:V_END + (v + 1) * LANES]
    by_token = [[u_scr[v, pl.ds(t, n_rows, stride=CHUNK), :] for v in range(SSM_WIDTH // LANES)]
                for t in range(CHUNK)]
    lane16 = lax.broadcasted_iota(jnp.int32, (n_rows, LANES), 1) // SSM_GROUP
    for p in range(N_PAIRS):
        for gi in range(2):
            src = (p % 4) * 2 * SSM_GROUP + gi * SSM_GROUP
            for h in range(CHUNK // SUBLANES):
                pieces = [(by_token[SUBLANES * h + j][p // 4], src) for j in range(SUBLANES)]
                lo = gi * CHUNK * SSM_GROUP + h * LANES
                u_ref[p, 0, :, lo:lo + LANES] = _lane_piece_gather(pieces, lane16)


def _embed_tail_body(x_ref, g_ref, b_ref, w_ref, qg_ref, kg_ref, bones_ref, cos_ref, sin_ref,
                     h_in, q_in, k_in, u_in, h_ref, q_ref, k_ref, vt0_ref, vt1_ref, u_ref, u_scr):
    del h_in, q_in, k_in, u_in
    _embed_body(x_ref, g_ref, b_ref, w_ref, qg_ref, kg_ref, bones_ref, cos_ref, sin_ref,
                h_ref, q_ref, k_ref, vt0_ref, vt1_ref, u_ref, u_scr)


def _embed(x, meta_pad, ln_g, ln_b, w_in, qg, kg, bones, cos, sin, L, Lp):
    B = x.shape[0]
    TM = EMBED_TILE
    nj = L // TM
    w_specs = [_const_spec(1, D_MODEL), _const_spec(1, D_MODEL), _const_spec(D_MODEL, IN_WIDTH),
               _const_spec(1, Q_END), _const_spec(1, KV_WIDTH), _const_spec(Q_END, Q_END)]

    def out_shapes(vt_tiles, vt_width):
        vt = jax.ShapeDtypeStruct((B, vt_tiles, VT_ROWS, vt_width), BF16)
        return [
            jax.ShapeDtypeStruct((B, Lp, D_MODEL), F32),
            jax.ShapeDtypeStruct((B, Lp, Q_END), BF16),
            jax.ShapeDtypeStruct((B, Lp, KV_WIDTH), BF16),
            vt, vt,
            jax.ShapeDtypeStruct((N_PAIRS, B, Lp // CHUNK, PAIR_W), F32),
        ]

    def out_specs(tm, row_block):
        return [
            pl.BlockSpec((1, tm, D_MODEL), lambda b, j: (b, row_block(j), 0)),
            pl.BlockSpec((1, tm, Q_END), lambda b, j: (b, row_block(j), 0)),
            pl.BlockSpec((1, tm, KV_WIDTH), lambda b, j: (b, row_block(j), 0)),
            pl.BlockSpec((1, 1, VT_ROWS, tm), lambda b, j: (b, j, 0, 0)),
            pl.BlockSpec((1, 1, VT_ROWS, tm), lambda b, j: (b, j, 0, 0)),
            pl.BlockSpec((N_PAIRS, 1, tm // CHUNK, PAIR_W), lambda b, j: (0, b, row_block(j), 0)),
        ]

    main = pl.pallas_call(
        _embed_body,
        grid=(B, nj),
        in_specs=[pl.BlockSpec((1, TM, D_MODEL), lambda b, j: (b, j, 0))] + w_specs
        + [pl.BlockSpec((TM, LANES), lambda b, j: (j, 0))] * 2,
        out_specs=out_specs(TM, lambda j: j),
        out_shape=out_shapes(nj, TM),
        scratch_shapes=[pltpu.VMEM((SSM_WIDTH // LANES, TM, LANES), F32)],
        compiler_params=_cparams(("parallel", "parallel")),
        name="embed_main",
    )
    h0, q, k, vt0, vt1, u = main(x, ln_g, ln_b, w_in, qg, kg, bones, cos, sin)

    jt = Lp // TAIL - 1
    tail = pl.pallas_call(
        _embed_tail_body,
        grid=(B, 1),
        in_specs=[pl.BlockSpec((1, TAIL, D_MODEL), lambda b, j: (0, 0, 0))] + w_specs
        + [pl.BlockSpec((TAIL, LANES), lambda b, j: (jt, 0))] * 2
        + [pl.BlockSpec(memory_space=pl.ANY)] * 4,
        out_specs=out_specs(TAIL, lambda j: jt),
        out_shape=out_shapes(1, TAIL),
        input_output_aliases={9: 0, 10: 1, 11: 2, 12: 5},
        scratch_shapes=[pltpu.VMEM((SSM_WIDTH // LANES, TAIL, LANES), F32)],
        compiler_params=_cparams(("parallel", "arbitrary")),
        name="embed_tail",
    )
    h0, q, k, vt0_tail, vt1_tail, u = tail(meta_pad, ln_g, ln_b, w_in, qg, kg, bones, cos, sin,
                                           h0, q, k, u)
    return h0, q, k, (vt0, vt1, vt0_tail, vt1_tail), u


def _attn_body(q_ref, k_ref, vt0_ref, vt1_ref, vt0t_ref, vt1t_ref, o_ref,
               qt_ref, s0_ref, s1_ref, mc0_ref, mc1_ref, m_ref, acc_ref, *, L, n_kv_tiles):
    tq = Q_TILE
    cols = Q_PER_KV * tq
    row = lax.broadcasted_iota(jnp.int32, (KV_WIDTH, tq), 0)
    for r in range(Q_PER_KV):
        slab_t = q_ref[0, :, r * LANES:(r + 1) * LANES].astype(F32).T
        for g in range(N_KV_HEADS):
            in_group = (row >= g * HEAD_DIM) & (row < (g + 1) * HEAD_DIM)
            qt_ref[g, :, r * tq:(r + 1) * tq] = jnp.where(in_group, slab_t, 0.0).astype(BF16)
    m_ref[...] = jnp.full(m_ref.shape, NEG_BIG, F32)
    acc_ref[...] = jnp.zeros(acc_ref.shape, F32)

    s_bufs = ((s0_ref, mc0_ref), (s1_ref, mc1_ref))

    def scores(slot, k_tile, first_valid=0):
        s_ref, mc_ref = s_bufs[slot]
        n = k_tile.shape[0]
        for g in range(N_KV_HEADS):
            s = jnp.dot(k_tile, qt_ref[g], preferred_element_type=F32)
            if first_valid:
                key = lax.broadcasted_iota(jnp.int32, s.shape, 0)
                s = jnp.where(key >= first_valid, s, NEG_BIG)
            s_ref[g, 0:n, :] = s
            mc_ref[g] = jnp.broadcast_to(jnp.max(s, axis=0, keepdims=True), (SUBLANES, cols))

    def softmax_values(slot, vt_tiles, n):
        s_ref, mc_ref = s_bufs[slot]
        for g in range(N_KV_HEADS):
            m_old = m_ref[g]
            m_new = jnp.maximum(m_old, mc_ref[g])
            alpha = jnp.exp2(m_old[0:1, :] - m_new[0:1, :])
            acc = alpha * acc_ref[g]
            for k0 in range(0, n, KEY_SUBTILE):
                k1 = min(k0 + KEY_SUBTILE, n)
                p = jnp.exp2(s_ref[g, k0:k1, :] - m_new[0:1, :]).astype(BF16)
                acc = acc + jnp.dot(vt_tiles[g][:, k0:k1], p, preferred_element_type=F32)
            acc_ref[g] = acc
            m_ref[g] = m_new

    def k_tile(i):
        return k_ref[0, pl.ds(pl.multiple_of(i * KV_TILE, KV_TILE), KV_TILE), :]

    def vt_tiles(i):
        return vt0_ref[0, i], vt1_ref[0, i]

    scores(0, k_tile(0))

    def run_tiles(first, count, next_scores):
        for t in range(count):
            softmax_values(t % 2, vt_tiles(first + t), KV_TILE)
            if t + 1 < count:
                scores((t + 1) % 2, k_tile(first + t + 1))
            else:
                next_scores()

    def kv_group(i, carry):
        first = i * KV_UNROLL
        run_tiles(first, KV_UNROLL, lambda: scores(0, k_tile(first + KV_UNROLL)))
        return carry

    n_groups = n_kv_tiles // KV_UNROLL
    lax.fori_loop(0, n_groups - 1, kv_group, 0)
    run_tiles((n_groups - 1) * KV_UNROLL, KV_UNROLL,
              lambda: scores(0, k_ref[0, L:L + TAIL, :], PAD_ROWS))
    softmax_values(0, (vt0t_ref[0, 0], vt1t_ref[0, 0]), TAIL)

    comb = jnp.concatenate(
        [acc_ref[g, 0:HEAD_DIM, :] / acc_ref[g, HEAD_DIM:HEAD_DIM + 1, :] for g in range(N_KV_HEADS)],
        axis=0)
    for r in range(Q_PER_KV):
        o_ref[0, :, r * LANES:(r + 1) * LANES] = comb[:, r * tq:(r + 1) * tq].T.astype(BF16)


def _attention(q, k, vts, L, Lp):
    B = q.shape[0]
    nkt = L // KV_TILE
    assert nkt % KV_UNROLL == 0
    cols = Q_PER_KV * Q_TILE
    vt_spec = pl.BlockSpec((1, nkt, VT_ROWS, KV_TILE), lambda b, j: (b, 0, 0, 0))
    vt_tail_spec = pl.BlockSpec((1, 1, VT_ROWS, TAIL), lambda b, j: (b, 0, 0, 0))
    return pl.pallas_call(
        functools.partial(_attn_body, L=L, n_kv_tiles=nkt),
        grid=(B, Lp // Q_TILE),
        in_specs=[
            pl.BlockSpec((1, Q_TILE, Q_END), lambda b, j: (b, j, 0)),
            pl.BlockSpec((1, Lp, KV_WIDTH), lambda b, j: (b, 0, 0)),
            vt_spec, vt_spec, vt_tail_spec, vt_tail_spec,
        ],
        out_specs=pl.BlockSpec((1, Q_TILE, Q_END), lambda b, j: (b, j, 0)),
        out_shape=jax.ShapeDtypeStruct((B, Lp, Q_END), BF16),
        scratch_shapes=[
            pltpu.VMEM((N_KV_HEADS, KV_WIDTH, cols), BF16),
            pltpu.VMEM((N_KV_HEADS, KV_TILE, cols), F32),
            pltpu.VMEM((N_KV_HEADS, KV_TILE, cols), F32),
            pltpu.VMEM((N_KV_HEADS, SUBLANES, cols), F32),
            pltpu.VMEM((N_KV_HEADS, SUBLANES, cols), F32),
            pltpu.VMEM((N_KV_HEADS, SUBLANES, cols), F32),
            pltpu.VMEM((N_KV_HEADS, VT_ROWS, cols), F32),
        ],
        compiler_params=_cparams(("parallel", "arbitrary")),
        name="attention",
    )(q, k, *vts)


def _ssm_weights(lam_re, lam_im, log_dt, b_re, b_im, c_re, c_im, d_skip):
    hp = lax.Precision.HIGHEST
    dt = jnp.exp(log_dt.astype(F32))[..., None]
    lr = lam_re.astype(F32)
    li = lam_im.astype(F32)

    def apow(n):
        n = jnp.asarray(n, F32)
        mag = jnp.exp(lr[..., None] * dt[..., None] * n)
        ang = li[..., None] * dt[..., None] * n
        return mag * jnp.cos(ang), mag * jnp.sin(ang)

    a1r, a1i = apow(jnp.ones((1,), F32))
    a1r, a1i = a1r[..., 0], a1i[..., 0]
    nr = a1r - 1.0
    den = lr * lr + li * li
    f_r = (nr * lr + a1i * li) / den
    f_i = (a1i * lr - nr * li) / den
    br = b_re.astype(F32)
    bi = b_im.astype(F32)
    bb_r = f_r[..., None] * br - f_i[..., None] * bi
    bb_i = f_r[..., None] * bi + f_i[..., None] * br
    cr = c_re.astype(F32)
    ci = c_im.astype(F32)

    tau = jnp.arange(CHUNK + 1, dtype=F32)
    pr, pi = apow(tau)
    ab_r = pr[..., None] * bb_r[:, :, :, None, :] - pi[..., None] * bb_i[:, :, :, None, :]
    ab_i = pr[..., None] * bb_i[:, :, :, None, :] + pi[..., None] * bb_r[:, :, :, None, :]
    kern = (jnp.einsum("dgop,dgptc->dgtco", cr, ab_r, precision=hp)
            - jnp.einsum("dgop,dgptc->dgtco", ci, ab_i, precision=hp))
    s_idx = jnp.arange(CHUNK)[:, None]
    t_idx = jnp.arange(CHUNK)[None, :]
    lag_f = jnp.clip(t_idx - s_idx, 0, CHUNK)
    lag_b = jnp.clip(s_idx - t_idx, 0, CHUNK)
    m_f = jnp.where((t_idx >= s_idx)[..., None, None], kern[0][:, lag_f], 0.0)
    m_b = jnp.where((s_idx >= t_idx)[..., None, None], kern[1][:, lag_b], 0.0)
    eye_t = jnp.eye(CHUNK, dtype=F32)[None, :, :, None, None]
    eye_c = jnp.eye(SSM_GROUP, dtype=F32)[None, None, None]
    dsk = d_skip.astype(F32).reshape(N_SSM_GROUPS, 1, 1, SSM_GROUP, 1)
    m_all = m_f + m_b + eye_t * eye_c * dsk
    m_all = m_all.transpose(0, 1, 3, 2, 4).reshape(N_SSM_GROUPS, 256, 256)

    def state_in(d, expo):
        er, ei = pr[d][..., expo], pi[d][..., expo]
        wr = er[..., None] * bb_r[d][:, :, None, :] - ei[..., None] * bb_i[d][:, :, None, :]
        wi = er[..., None] * bb_i[d][:, :, None, :] + ei[..., None] * bb_r[d][:, :, None, :]
        to_rows = lambda w: w.transpose(0, 2, 3, 1).reshape(N_SSM_GROUPS, 256, SSM_STATE)
        return to_rows(wr), to_rows(wi)

    sf_r, sf_i = state_in(0, jnp.arange(CHUNK - 1, -1, -1))
    sb_r, sb_i = state_in(1, jnp.arange(CHUNK))

    def state_out(d, expo):
        er, ei = pr[d][..., expo], pi[d][..., expo]
        wr = cr[d].transpose(0, 2, 1)[:, :, None, :] * er[..., None] \
            - ci[d].transpose(0, 2, 1)[:, :, None, :] * ei[..., None]
        wi = cr[d].transpose(0, 2, 1)[:, :, None, :] * ei[..., None] \
            + ci[d].transpose(0, 2, 1)[:, :, None, :] * er[..., None]
        flat = lambda w: w.reshape(N_SSM_GROUPS, SSM_STATE, 256)
        return flat(wr), flat(-wi)

    of_r, of_i = state_out(0, jnp.arange(1, CHUNK + 1))
    ob_r, ob_i = state_out(1, jnp.arange(CHUNK, 0, -1))

    def pair_rows(w):
        return w.reshape(N_PAIRS, 2, *w.shape[1:])

    z256 = jnp.zeros((N_PAIRS, 256, 256), F32)
    mp = pair_rows(m_all)
    w1_y = jnp.concatenate([jnp.concatenate([mp[:, 0], z256], axis=2),
                            jnp.concatenate([z256, mp[:, 1]], axis=2)], axis=1)

    def pair_cols_in(w):
        wp = pair_rows(w)
        z = jnp.zeros_like(wp[:, 0])
        return jnp.concatenate([jnp.concatenate([wp[:, 0], z], axis=2),
                                jnp.concatenate([z, wp[:, 1]], axis=2)], axis=1)

    w1 = jnp.concatenate([w1_y, pair_cols_in(sf_r), pair_cols_in(sf_i),
                          pair_cols_in(sb_r), pair_cols_in(sb_i)], axis=2)

    def pair_rows_out(w):
        wp = pair_rows(w)
        z = jnp.zeros_like(wp[:, 0])
        return jnp.concatenate([jnp.concatenate([wp[:, 0], z], axis=2),
                                jnp.concatenate([z, wp[:, 1]], axis=2)], axis=1)

    w2 = jnp.concatenate([pair_rows_out(of_r), pair_rows_out(of_i),
                          pair_rows_out(ob_r), pair_rows_out(ob_i)], axis=1)

    qr, qi = apow(CHUNK * jnp.arange(SUBLANES + 1, dtype=F32))
    zero = jnp.zeros_like(qr[..., 0])

    def dec_rows(d, q_expo):
        rows = [qr[d][..., 1], qi[d][..., 1], qr[d][..., 2], qi[d][..., 2],
                qr[d][..., 4], qi[d][..., 4], zero[d], zero[d]]
        rows += [qr[d][..., n] for n in q_expo] + [qi[d][..., n] for n in q_expo]
        return jnp.stack(rows, axis=1)

    dec = jnp.concatenate([dec_rows(0, range(1, SUBLANES + 1)),
                           dec_rows(1, range(SUBLANES, 0, -1))], axis=1)
    dec = dec.reshape(N_PAIRS, 2, DEC_ROWS, SSM_STATE).transpose(0, 2, 1, 3)
    dec = dec.reshape(N_PAIRS, DEC_ROWS, PAIR_STATE)
    return w1.astype(BF16), w2.astype(BF16), dec


def _ssm_body(u_ref, w1_ref, w2_ref, dec_ref, z_ref, sh_ref, *, B, n_chunks):
    R = B * n_chunks
    tiles = [(r0, min(SSM_ROW_TILE, R - r0)) for r0 in range(0, R, SSM_ROW_TILE)]
    PS = PAIR_STATE
    n_blocks = n_chunks // SUBLANES

    for r0, n in tiles:
        t = jnp.dot(u_ref[0, r0:r0 + n, :].astype(BF16), w1_ref[0], preferred_element_type=F32)
        z_ref[0, r0:r0 + n, :] = t[:, :PAIR_W]
        sh_ref[r0:r0 + n, :] = t[:, PAIR_W:]

    sub = lax.broadcasted_iota(jnp.int32, (SUBLANES, PS), 0)

    def cmul(ar, ai, xr, xi):
        return ar * xr - ai * xi, ar * xi + ai * xr

    def block_scan(sr, si, cr, ci, base, reverse):
        xr, xi = sr, si
        for k, d in enumerate((1, 2, 4)):
            pr_ = dec_ref[0, base + 2 * k:base + 2 * k + 1, :]
            pi_ = dec_ref[0, base + 2 * k + 1:base + 2 * k + 2, :]
            keep = (sub < SUBLANES - d) if reverse else (sub >= d)
            shift = SUBLANES - d if reverse else d
            yr = jnp.where(keep, pltpu.roll(xr, shift, 0), 0.0)
            yi = jnp.where(keep, pltpu.roll(xi, shift, 0), 0.0)
            mr, mi = cmul(pr_, pi_, yr, yi)
            xr, xi = xr + mr, xi + mi
        qr_ = dec_ref[0, base + 8:base + 16, :]
        qi_ = dec_ref[0, base + 16:base + 24, :]
        mr, mi = cmul(qr_, qi_, cr, ci)
        xr, xi = xr + mr, xi + mi
        edge = SUBLANES - 1 if reverse else 0
        shift = SUBLANES - 1 if reverse else 1
        er = jnp.where(sub == edge, cr, pltpu.roll(xr, shift, 0))
        ei = jnp.where(sub == edge, ci, pltpu.roll(xi, shift, 0))
        last = 0 if reverse else SUBLANES - 1
        nr = jnp.broadcast_to(xr[last:last + 1, :], (SUBLANES, PS))
        ni = jnp.broadcast_to(xi[last:last + 1, :], (SUBLANES, PS))
        return er, ei, nr, ni

    def scan_step(i, carry):
        new = []
        for b in range(B):
            cfr, cfi, cbr, cbi = carry[4 * b:4 * b + 4]
            tile_f = (i + n_blocks - 1) % n_blocks
            tile_b = (2 * n_blocks - 2 - i) % n_blocks
            rf = pl.multiple_of(b * n_chunks + tile_f * SUBLANES, SUBLANES)
            rb = pl.multiple_of(b * n_chunks + tile_b * SUBLANES, SUBLANES)
            f_rows, b_rows = pl.ds(rf, SUBLANES), pl.ds(rb, SUBLANES)
            er, ei, cfr, cfi = block_scan(sh_ref[f_rows, 0:PS], sh_ref[f_rows, PS:2 * PS],
                                          cfr, cfi, 0, False)
            sh_ref[f_rows, 0:PS] = er
            sh_ref[f_rows, PS:2 * PS] = ei
            er, ei, cbr, cbi = block_scan(sh_ref[b_rows, 2 * PS:3 * PS], sh_ref[b_rows, 3 * PS:4 * PS],
                                          cbr, cbi, DEC_ROWS // 2, True)
            sh_ref[b_rows, 2 * PS:3 * PS] = er
            sh_ref[b_rows, 3 * PS:4 * PS] = ei
            new += [cfr, cfi, cbr, cbi]
        return tuple(new)

    zero = jnp.zeros((SUBLANES, PS), F32)
    lax.fori_loop(0, n_blocks, scan_step, (zero,) * (4 * B))

    for r0, n in tiles:
        y = z_ref[0, r0:r0 + n, :] + jnp.dot(sh_ref[r0:r0 + n, :].astype(BF16), w2_ref[0],
                                             preferred_element_type=F32)
        z_ref[0, r0:r0 + n, :] = jax.nn.gelu(y)


def _ssm(u_pairs, w1, w2, dec, B, n_chunks):
    assert n_chunks % SUBLANES == 0
    R = B * n_chunks
    return pl.pallas_call(
        functools.partial(_ssm_body, B=B, n_chunks=n_chunks),
        grid=(N_PAIRS,),
        in_specs=[
            pl.BlockSpec((1, R, PAIR_W), lambda p: (p, 0, 0)),
            pl.BlockSpec((1, PAIR_W, 2 * PAIR_W), lambda p: (p, 0, 0)),
            pl.BlockSpec((1, PAIR_W, PAIR_W), lambda p: (p, 0, 0)),
            pl.BlockSpec((1, DEC_ROWS, PAIR_STATE), lambda p: (p, 0, 0)),
        ],
        out_specs=pl.BlockSpec((1, R, PAIR_W), lambda p: (p, 0, 0)),
        out_shape=jax.ShapeDtypeStruct((N_PAIRS, R, PAIR_W), F32),
        scratch_shapes=[pltpu.VMEM((R, 4 * PAIR_STATE), F32)],
        compiler_params=_cparams(("parallel",)),
        name="ssm",
    )(u_pairs, w1, w2, dec)


def _mix_body(o_ref, zp_ref, h0_ref, wglu_ref, bglu_ref, woa_ref, wos_ref, g_ref, b_ref, wr_ref,
              h1_ref, acc_ref, aff_ref, z_scr, *, L, Lp):
    n_rows = zp_ref.shape[1]
    lane16 = lax.broadcasted_iota(jnp.int32, (n_rows, LANES), 1) // SSM_GROUP
    for t in range(CHUNK):
        src = (t % SUBLANES) * SSM_GROUP
        for v in range(SSM_WIDTH // LANES):
            pieces = []
            for j in range(SUBLANES):
                p, gi = 4 * v + j // 2, j % 2
                lo = gi * CHUNK * SSM_GROUP + (t // SUBLANES) * LANES
                pieces.append((zp_ref[p, :, lo:lo + LANES], src))
            z_scr[v, pl.ds(t, n_rows, stride=CHUNK), :] = _lane_piece_gather(pieces, lane16)
    z = jnp.concatenate([z_scr[v] for v in range(SSM_WIDTH // LANES)], axis=1)
    gate = jax.nn.sigmoid(jnp.dot(z.astype(BF16), wglu_ref[...], preferred_element_type=F32)
                          + bglu_ref[...])
    ssm_out = (z * gate).astype(BF16)
    mix = (jnp.dot(o_ref[...], woa_ref[...], preferred_element_type=F32)
           + jnp.dot(ssm_out, wos_ref[...], preferred_element_type=F32))
    h1 = _layer_norm_rows(DEEPNORM_ALPHA * h0_ref[...] + mix, g_ref[...], b_ref[...])
    for s in range(ROW_SLABS):
        sl = slice(s * LANES, (s + 1) * LANES)
        h1_ref[:, s, :] = h1[:, sl]
        acc_ref[:, s, :] = DEEPNORM_ALPHA * h1[:, sl]
    logits = jnp.dot(h1.astype(BF16), wr_ref[...], preferred_element_type=F32)
    lane = lax.broadcasted_iota(jnp.int32, logits.shape, 1)
    logits = jnp.where(lane < N_EXPERTS, logits, NEG_BIG)
    e = jnp.exp(logits - jnp.max(logits, axis=-1, keepdims=True))
    aff = (e / jnp.sum(e, axis=-1, keepdims=True)).T[:N_EXPERTS, :]
    tiles_per_seq = Lp // ROW_TILE
    pos = (pl.program_id(0) % tiles_per_seq) * ROW_TILE + lax.broadcasted_iota(
        jnp.int32, aff.shape, 1)
    aff_ref[...] = jnp.where((pos < L) | (pos >= L + PAD_ROWS), aff, -1.0)


def _mix(o, z, h0, wglu, bglu, woa, wos, g, b, wr, L, Lp):
    N = o.shape[0]
    TM = ROW_TILE
    row = lambda w: pl.BlockSpec((TM, w), lambda i: (i, 0))
    tok_rows = pl.BlockSpec((TM, ROW_SLABS, LANES), lambda i: (i, 0, 0))
    return pl.pallas_call(
        functools.partial(_mix_body, L=L, Lp=Lp),
        grid=(N // TM,),
        in_specs=[row(ATTN_WIDTH), pl.BlockSpec((N_PAIRS, TM // CHUNK, PAIR_W), lambda i: (0, i, 0)),
                  row(D_MODEL),
                  _const_spec(SSM_WIDTH, SSM_WIDTH), _const_spec(1, SSM_WIDTH),
                  _const_spec(ATTN_WIDTH, D_MODEL), _const_spec(SSM_WIDTH, D_MODEL),
                  _const_spec(1, D_MODEL), _const_spec(1, D_MODEL), _const_spec(D_MODEL, LANES)],
        out_specs=[tok_rows, tok_rows, pl.BlockSpec((N_EXPERTS, TM), lambda i: (0, i))],
        out_shape=[jax.ShapeDtypeStruct((N, ROW_SLABS, LANES), F32),
                   jax.ShapeDtypeStruct((N, ROW_SLABS, LANES), F32),
                   jax.ShapeDtypeStruct((N_EXPERTS, N), F32)],
        scratch_shapes=[pltpu.VMEM((SSM_WIDTH // LANES, TM, LANES), F32)],
        compiler_params=_cparams(("parallel",)),
        name="mix",
    )(o, z, h0, wglu, bglu, woa, wos, g, b, wr)


def _select_body(aff_ref, ord_ref, idx_ref, *, capacity, n_slot_tiles, seq_len, pad_start):
    nbp = aff_ref.shape[1]
    ri = lax.broadcasted_iota(jnp.int32, (LANES, LANES), 0)
    ci = lax.broadcasted_iota(jnp.int32, (LANES, LANES), 1)
    strict_upper = (ri < ci).astype(BF16)
    incl_upper = (ri <= ci).astype(BF16)
    ones = jnp.ones((LANES, LANES), BF16)
    order = ord_ref[...]
    blk = lax.broadcasted_iota(jnp.int32, (nbp, LANES), 0).astype(F32)
    blk_hi = jnp.floor(blk * (1.0 / 16.0))
    blk_lo = blk - 16.0 * blk_hi
    dot = functools.partial(jnp.dot, preferred_element_type=F32)
    dot_nt = functools.partial(lax.dot_general, dimension_numbers=(((1,), (1,)), ((), ())),
                               preferred_element_type=F32)

    def per_expert(e, carry):
        bits = pltpu.bitcast(aff_ref[e], jnp.int32)
        t = jnp.zeros((1, 1), jnp.int32)
        for bit in range(30, -1, -1):
            cand = t | (1 << bit)
            cnt = jnp.sum((bits >= cand).astype(jnp.int32), keepdims=True)
            t = jnp.where(cnt >= capacity, cand, t)
        gt = bits > t
        eq = bits == t
        need = (capacity - jnp.sum(gt.astype(jnp.int32), keepdims=True)).astype(F32)
        eqb = eq.astype(BF16)
        tie_rank = dot(order, dot(eqb, ones).astype(BF16)) + dot(eqb, strict_upper)
        sel = (gt | (eq & (tie_rank < need))).astype(BF16)
        csum_in_blk = dot(sel, incl_upper)
        blk_tot = dot(sel, ones)
        blk_off = dot(order, blk_tot.astype(BF16))
        off_hi = jnp.floor(blk_off * (1.0 / 64.0))
        off_lo = blk_off - 64.0 * off_hi
        table = jnp.concatenate([csum_in_blk, off_hi, off_lo, blk_hi, blk_lo],
                                axis=1).astype(BF16)
        tot_l = dot_nt(ones[:SUBLANES], sel)
        off_l = dot_nt(tot_l.astype(BF16), order)[0:1, :]
        tot_l = tot_l[0:1, :]

        def per_tile(st, carry2):
            j = (st * SLOT_TILE + lax.broadcasted_iota(jnp.int32, (SLOT_TILE, 1), 0)).astype(F32)
            onehot = ((off_l <= j) & (j < off_l + tot_l)).astype(BF16)
            got = dot(onehot, table)
            local = j - (64.0 * got[:, LANES:2 * LANES] + got[:, 2 * LANES:3 * LANES])
            lane_idx = dot((got[:, :LANES] <= local).astype(BF16), ones)
            block = 16.0 * got[:, 3 * LANES:4 * LANES] + got[:, 4 * LANES:5 * LANES]
            k = j - capacity
            seq = sum((k >= PAD_ROWS * i).astype(F32) for i in range(1, SLOT_TILE // PAD_ROWS + 1))
            pad_tok = seq * (seq_len - PAD_ROWS) + pad_start + k
            tok = jnp.where(j < capacity, block * LANES + lane_idx, pad_tok)
            idx_ref[e, st] = tok.T[0:1, :].astype(jnp.int32)
            return carry2

        lax.fori_loop(0, n_slot_tiles, per_tile, 0)
        return carry

    lax.fori_loop(0, N_EXPERTS, per_expert, 0)


def _select(aff_blocks, order, capacity, n_slot_tiles, seq_len, pad_start):
    nbp = aff_blocks.shape[1]
    return pl.pallas_call(
        functools.partial(_select_body, capacity=capacity, n_slot_tiles=n_slot_tiles,
                          seq_len=seq_len, pad_start=pad_start),
        grid=(1,),
        in_specs=[_const_spec(N_EXPERTS, nbp, LANES), _const_spec(nbp, nbp)],
        out_specs=_const_spec(N_EXPERTS, n_slot_tiles, 1, SLOT_TILE),
        out_shape=jax.ShapeDtypeStruct((N_EXPERTS, n_slot_tiles, 1, SLOT_TILE), jnp.int32),
        compiler_params=_cparams(("arbitrary",)),
        name="select",
    )(aff_blocks, order)


def _moe_body(idx_ref, idx_next_ref, h1_hbm, wg_ref, wu_ref, wd_ref, wr_ref, acc_in, acc_hbm,
              xbuf, abuf, sem_x, sem_a, sem_s):
    del acc_in
    T = SLOT_TILE
    n_c = pl.num_programs(1)
    step = pl.program_id(0) * n_c + pl.program_id(1)
    n_steps = pl.num_programs(0) * n_c
    slot = step % 2

    def start_rows(copy_of_row, idx):
        def body(j, carry):
            copy_of_row(idx[0, 0, 0, j], j).start()
            return carry
        lax.fori_loop(0, T, body, 0, unroll=8)

    def start_rows_inline(copy_of_row, idx, lo, hi):
        for j in range(lo, hi):
            copy_of_row(idx[0, 0, 0, j], j).start()

    def x_row(buf):
        return lambda row, j: pltpu.make_async_copy(h1_hbm.at[row], xbuf.at[buf, j], sem_x.at[buf])

    def acc_row_in(row, j):
        return pltpu.make_async_copy(acc_hbm.at[row], abuf.at[j], sem_a)

    def acc_row_out(row, j):
        return pltpu.make_async_copy(abuf.at[j], acc_hbm.at[row], sem_s)

    def wait_scatter():
        pltpu.make_async_copy(abuf, acc_hbm.at[pl.ds(0, T)], sem_s).wait()

    @pl.when(step == 0)
    def _():
        start_rows(x_row(0), idx_ref)

    @pl.when(step > 0)
    def _():
        wait_scatter()

    pltpu.make_async_copy(h1_hbm.at[pl.ds(0, T)], xbuf.at[slot], sem_x.at[slot]).wait()
    x = jnp.concatenate([xbuf[slot, :, s, :] for s in range(ROW_SLABS)], axis=1).astype(BF16)
    logits = jnp.dot(x, wr_ref[...], preferred_element_type=F32)
    lane = lax.broadcasted_iota(jnp.int32, logits.shape, 1)
    logits = jnp.where(lane < N_EXPERTS, logits, NEG_BIG)
    p = jnp.exp(logits - jnp.max(logits, axis=-1, keepdims=True))
    gate = (jnp.sum(jnp.where(lane == pl.program_id(0), p, 0.0), axis=-1, keepdims=True)
            / jnp.sum(p, axis=-1, keepdims=True))
    y = jnp.zeros((T, D_MODEL), F32)
    n_f = EXPERT_FF // FF_TILE
    half = 2 * T // n_f
    for f in range(n_f):
        fs = slice(f * FF_TILE, (f + 1) * FF_TILE)
        hg = jnp.dot(x, wg_ref[0, :, fs], preferred_element_type=F32)
        hu = jnp.dot(x, wu_ref[0, :, fs], preferred_element_type=F32)
        hid = (jax.nn.silu(hg) * hu).astype(BF16)
        y = y + jnp.dot(hid, wd_ref[0, fs, :], preferred_element_type=F32)
        if f < n_f // 2:
            start_rows_inline(acc_row_in, idx_ref, f * half, (f + 1) * half)
        else:
            start_rows_inline(x_row(1 - slot), idx_next_ref, (f - n_f // 2) * half,
                              (f - n_f // 2 + 1) * half)
    pltpu.make_async_copy(acc_hbm.at[pl.ds(0, T)], abuf, sem_a).wait()
    y = y * gate
    for s in range(ROW_SLABS):
        abuf[:, s, :] = abuf[:, s, :] + y[:, s * LANES:(s + 1) * LANES]
    start_rows(acc_row_out, idx_ref)

    @pl.when(step == n_steps - 1)
    def _():
        wait_scatter()
        pltpu.make_async_copy(h1_hbm.at[pl.ds(0, T)], xbuf.at[1 - slot], sem_x.at[1 - slot]).wait()


def _moe(idx, h1, acc, wg, wu, wd, wr, n_slot_tiles):
    T = SLOT_TILE
    n_c = n_slot_tiles

    def next_block(e, c):
        return (jnp.minimum(e + (c + 1) // n_c, N_EXPERTS - 1), (c + 1) % n_c, 0, 0)

    return pl.pallas_call(
        _moe_body,
        grid=(N_EXPERTS, n_c),
        in_specs=[
            pl.BlockSpec((1, 1, 1, T), lambda e, c: (e, c, 0, 0), memory_space=pltpu.SMEM),
            pl.BlockSpec((1, 1, 1, T), next_block, memory_space=pltpu.SMEM),
            pl.BlockSpec(memory_space=pl.ANY),
            pl.BlockSpec((1, D_MODEL, EXPERT_FF), lambda e, c: (e, 0, 0)),
            pl.BlockSpec((1, D_MODEL, EXPERT_FF), lambda e, c: (e, 0, 0)),
            pl.BlockSpec((1, EXPERT_FF, D_MODEL), lambda e, c: (e, 0, 0)),
            _const_spec(D_MODEL, LANES),
            pl.BlockSpec(memory_space=pl.ANY),
        ],
        out_specs=pl.BlockSpec(memory_space=pl.ANY),
        out_shape=jax.ShapeDtypeStruct(acc.shape, F32),
        scratch_shapes=[pltpu.VMEM((2, T, ROW_SLABS, LANES), F32),
                        pltpu.VMEM((T, ROW_SLABS, LANES), F32),
                        pltpu.SemaphoreType.DMA((2,)), pltpu.SemaphoreType.DMA(()),
                        pltpu.SemaphoreType.DMA(())],
        input_output_aliases={7: 0},
        compiler_params=pltpu.CompilerParams(dimension_semantics=("arbitrary", "arbitrary"),
                                             vmem_limit_bytes=VMEM_LIMIT,
                                             disable_bounds_checks=True),
        name="expert_ffn",
    )(idx, idx, h1, wg, wu, wd, wr, acc)


def _final_body(a_ref, g_ref, b_ref, o_ref):
    x = jnp.concatenate([a_ref[0, :, s, :] for s in range(ROW_SLABS)], axis=1)
    o_ref[0] = _layer_norm_rows(x, g_ref[...], b_ref[...])


def _final_norm(acc, g, b, L):
    B = acc.shape[0]
    TM = EMBED_TILE
    return pl.pallas_call(
        _final_body,
        grid=(B, L // TM),
        in_specs=[pl.BlockSpec((1, TM, ROW_SLABS, LANES), lambda b_, j: (b_, j, 0, 0)),
                  _const_spec(1, D_MODEL), _const_spec(1, D_MODEL)],
        out_specs=pl.BlockSpec((1, TM, D_MODEL), lambda b_, j: (b_, j, 0)),
        out_shape=jax.ShapeDtypeStruct((B, L, D_MODEL), F32),
        compiler_params=_cparams(("parallel", "parallel")),
        name="final_norm",
    )(acc, g, b)


def _rope_tables(L, Lp):
    rows = L // GRID_W
    t = jnp.arange(L, dtype=jnp.int32)
    m = jnp.arange(N_META, dtype=jnp.int32)
    pad = jnp.zeros((Lp - L - N_META,), jnp.int32)
    row = jnp.concatenate([t // GRID_W - rows // 2, pad,
                           jnp.full((N_META,), -(rows // 2) - 1, jnp.int32)])
    col = jnp.concatenate([t % GRID_W - GRID_W // 2, pad, m - GRID_W // 2])
    inv_freq = ROPE_THETA ** (-jnp.arange(0, ROPE_AXIS_DIM, 2, dtype=F32) / ROPE_AXIS_DIM)
    ang_r = row.astype(F32)[:, None] * inv_freq
    ang_c = col.astype(F32)[:, None] * inv_freq
    cos = jnp.concatenate([jnp.cos(ang_r)] * 2 + [jnp.cos(ang_c)] * 2, axis=1)
    sin = jnp.concatenate([-jnp.sin(ang_r), jnp.sin(ang_r), -jnp.sin(ang_c), jnp.sin(ang_c)], axis=1)
    return jnp.tile(cos, (1, 2)), jnp.tile(sin, (1, 2))


def _block_order(B, Lp, nbp):
    nbb = Lp // LANES
    r = jnp.arange(nbp)
    b, jb = r // nbb, r % nbb
    rank = jnp.where(r < B * nbb, b * nbb + jnp.where(jb == nbb - 1, 0, jb + 1), r)
    return (rank[None, :] < rank[:, None]).astype(BF16)


def _run_trunk(x, meta_pad, shared):
    B, L, _ = x.shape
    Lp = L + TAIL
    N = B * Lp
    cos, sin = _rope_tables(L, Lp)
    h0, q, k, vts, u = _embed(x, meta_pad, shared["ln_emb_g"], shared["ln_emb_b"], shared["w_in"],
                              shared["qg"], shared["kg"], shared["bones"], cos, sin, L, Lp)
    o = _attention(q, k, vts, L, Lp)
    z = _ssm(u.reshape(N_PAIRS, N // CHUNK, PAIR_W), shared["ssm_w1"], shared["ssm_w2"],
             shared["ssm_dec"], B, Lp // CHUNK)
    h1, acc, aff = _mix(o.reshape(N, ATTN_WIDTH), z, h0.reshape(N, D_MODEL),
                        shared["w_glu"], shared["b_glu"], shared["w_out_attn"], shared["w_out_ssm"],
                        shared["ln1_g"], shared["ln1_b"], shared["w_router"], L, Lp)

    capacity = EC_CAPACITY_FACTOR * B * (L + N_META) // N_EXPERTS
    n_slot_tiles = -(-capacity // SLOT_TILE)
    nb = N // LANES
    nbp = -(-nb // LANES) * LANES
    aff_blocks = jnp.pad(aff.reshape(N_EXPERTS, nb, LANES), ((0, 0), (0, nbp - nb), (0, 0)),
                         constant_values=-1.0)
    assert n_slot_tiles * SLOT_TILE - capacity <= B * PAD_ROWS
    idx = _select(aff_blocks, _block_order(B, Lp, nbp), capacity, n_slot_tiles, Lp, L)
    acc = _moe(idx, h1, acc, shared["w_gate"], shared["w_up"], shared["w_down"],
               shared["w_router"], n_slot_tiles)
    return _final_norm(acc.reshape(B, Lp, ROW_SLABS, LANES), shared["ln2_g"], shared["ln2_b"], L)


def kernel(x_prompt, x_sample, meta_tokens, ln_emb_g, ln_emb_b, w_in, q_norm_g, k_norm_g, ssm_lambda_re, ssm_lambda_im, ssm_log_dt, ssm_b_re, ssm_b_im, ssm_c_re, ssm_c_im, ssm_d, w_glu, b_glu, w_out, ln1_g, ln1_b, w_router, w_gate, w_up, w_down, ln2_g, ln2_b):
    row = lambda a: a.reshape(1, -1).astype(F32)
    w_q = w_in[0][:, :Q_END].reshape(D_MODEL, N_KV_HEADS, Q_PER_KV, HEAD_DIM)
    w_q = w_q.transpose(0, 2, 1, 3).reshape(D_MODEL, Q_END)
    w_oa = w_out[0][:ATTN_WIDTH].reshape(N_KV_HEADS, Q_PER_KV, HEAD_DIM, D_MODEL)
    w_oa = w_oa.transpose(1, 0, 2, 3).reshape(ATTN_WIDTH, D_MODEL)
    head_of = jnp.arange(Q_END) // HEAD_DIM
    ssm_w1, ssm_w2, ssm_dec = _ssm_weights(ssm_lambda_re[0], ssm_lambda_im[0], ssm_log_dt[0],
                                           ssm_b_re[0], ssm_b_im[0], ssm_c_re[0], ssm_c_im[0],
                                           ssm_d[0])
    shared = dict(
        ln_emb_g=row(ln_emb_g), ln_emb_b=row(ln_emb_b),
        w_in=jnp.concatenate([w_q, w_in[0][:, Q_END:]], axis=1).astype(BF16),
        qg=row(jnp.tile(q_norm_g[0], N_Q_HEADS)), kg=row(jnp.tile(k_norm_g[0], N_KV_HEADS)),
        bones=(head_of[:, None] == head_of[None, :]).astype(BF16),
        ssm_w1=ssm_w1, ssm_w2=ssm_w2, ssm_dec=ssm_dec,
        w_glu=w_glu[0].astype(BF16), b_glu=row(b_glu[0]),
        w_out_attn=w_oa.astype(BF16), w_out_ssm=w_out[0][ATTN_WIDTH:].astype(BF16),
        ln1_g=row(ln1_g[0]), ln1_b=row(ln1_b[0]),
        w_router=jnp.pad(w_router[0], ((0, 0), (0, LANES - N_EXPERTS))).astype(BF16),
        w_gate=w_gate[0].astype(BF16), w_up=w_up[0].astype(BF16), w_down=w_down[0].astype(BF16),
        ln2_g=row(ln2_g[0]), ln2_b=row(ln2_b[0]),
    )
    meta_pad = jnp.pad(meta_tokens.astype(F32), ((PAD_ROWS, 0), (0, 0)))[None]
    return (_run_trunk(x_prompt, meta_pad, shared), _run_trunk(x_sample, meta_pad, shared))
```

```python
import functools
import math

import jax
import jax.numpy as jnp
from jax import lax
from jax.experimental import pallas as pl
from jax.experimental.pallas import tpu as pltpu

F32 = jnp.float32
BF16 = jnp.bfloat16

D_MODEL = 1024
N_META = 16
GRID_W = 64
ATTN_WIDTH = 512
SSM_WIDTH = 512
HEAD_DIM = 64
N_Q_HEADS = 8
N_KV_HEADS = 2
Q_PER_KV = 4
KV_WIDTH = 128
ROPE_AXIS_DIM = 32
ROPE_THETA = 10000.0
ATTN_SCALE = HEAD_DIM ** -0.5
RMS_EPS = 1e-6
SSM_GROUP = 16
N_SSM_GROUPS = 32
SSM_STATE = 64
Q_END = ATTN_WIDTH
K_END = Q_END + KV_WIDTH
V_END = K_END + KV_WIDTH
IN_WIDTH = V_END + SSM_WIDTH
N_EXPERTS = 16
EXPERT_FF = 2048
EC_CAPACITY_FACTOR = 2
LN_EPS = 1e-5
DEPTH = 1
DEEPNORM_ALPHA = (2 * DEPTH) ** 0.25

LANES = 128
SUBLANES = 8
ROW_SLABS = D_MODEL // LANES
TAIL = LANES
PAD_ROWS = TAIL - N_META
CHUNK = 16
N_PAIRS = N_SSM_GROUPS // 2
PAIR_W = 2 * CHUNK * SSM_GROUP
PAIR_STATE = 2 * SSM_STATE
EMBED_TILE = 512
ROW_TILE = 384
Q_TILE = ROW_TILE
KV_TILE = EMBED_TILE
KEY_SUBTILE = 256
KV_UNROLL = 4
SLOT_TILE = 384
MOE_TILE_MAX = 528
FF_TILE = 512
SSM_ROW_TILE = 256
DEC_ROWS = 48
VT_ROWS = HEAD_DIM + 16
Q_PRESCALE = ATTN_SCALE * math.log2(math.e)
NEG_BIG = -1e30
SAFE_EXP2_RANGE = 60.0
VMEM_LIMIT = 56 * 1024 * 1024


def _cparams(sem):
    return pltpu.CompilerParams(dimension_semantics=sem, vmem_limit_bytes=VMEM_LIMIT)


def _const_spec(*shape):
    return pl.BlockSpec(shape, lambda *idx: (0,) * len(shape))


def _layer_norm_rows(x, g, b):
    mu = jnp.mean(x, axis=-1, keepdims=True)
    xc = x - mu
    var = jnp.mean(xc * xc, axis=-1, keepdims=True)
    return xc * lax.rsqrt(var + LN_EPS) * g + b


def _head_rms(t, gain, bones):
    sq = t * t
    hi = sq.astype(BF16)
    lo = (sq - hi.astype(F32)).astype(BF16)
    ss = (jnp.dot(hi, bones, preferred_element_type=F32)
          + jnp.dot(lo, bones, preferred_element_type=F32))
    return t * lax.rsqrt(ss * (1.0 / HEAD_DIM) + RMS_EPS) * gain


def _rope_slab(t, cos, sin_signed):
    lane = lax.broadcasted_iota(jnp.int32, t.shape, 1)
    first = (lane % ROPE_AXIS_DIM) < (ROPE_AXIS_DIM // 2)
    partner = jnp.where(first, pltpu.roll(t, LANES - 16, 1), pltpu.roll(t, 16, 1))
    return t * cos + partner * sin_signed


def _lane_piece_gather(pieces, lane16):
    out = None
    for j, (arr, src) in enumerate(pieces):
        shift = (SSM_GROUP * j - src) % LANES
        moved = pltpu.roll(arr, shift, 1) if shift else arr
        out = moved if out is None else jnp.where(lane16 == j, moved, out)
    return out


def _embed_body(x_ref, g_ref, b_ref, w_ref, qg_ref, kg_ref, bones_ref, cos_ref, sin_ref,
                h_ref, q_ref, k_ref, vt0_ref, vt1_ref, u_ref, u_scr):
    h = _layer_norm_rows(x_ref[0], g_ref[...], b_ref[...])
    h_ref[0] = h
    proj = jnp.dot(h.astype(BF16), w_ref[...], preferred_element_type=F32)
    cos = cos_ref[...]
    sin = sin_ref[...]
    qn = _head_rms(proj[:, :Q_END], qg_ref[...], bones_ref[...])
    for s in range(ATTN_WIDTH // LANES):
        sl = slice(s * LANES, (s + 1) * LANES)
        q_ref[0, :, sl] = (_rope_slab(qn[:, sl], cos, sin) * Q_PRESCALE).astype(BF16)
    kn = _head_rms(proj[:, Q_END:K_END], kg_ref[...], bones_ref[:KV_WIDTH, :KV_WIDTH])
    k_ref[0] = _rope_slab(kn, cos, sin).astype(BF16)
    vt = proj[:, K_END:V_END].T
    ones = jnp.ones((VT_ROWS - HEAD_DIM, vt.shape[1]), F32)
    vt0_ref[0, 0] = jnp.concatenate([vt[:HEAD_DIM], ones], axis=0).astype(BF16)
    vt1_ref[0, 0] = jnp.concatenate([vt[HEAD_DIM:], ones], axis=0).astype(BF16)
    n_rows = u_scr.shape[1] // CHUNK
    for v in range(SSM_WIDTH // LANES):
        u_scr[v] = proj[:, V_END + v * LANES:V_END + (v + 1) * LANES]
    by_token = [[u_scr[v, pl.ds(t, n_rows, stride=CHUNK), :] for v in range(SSM_WIDTH // LANES)]
                for t in range(CHUNK)]
    lane16 = lax.broadcasted_iota(jnp.int32, (n_rows, LANES), 1) // SSM_GROUP
    for p in range(N_PAIRS):
        for gi in range(2):
            src = (p % 4) * 2 * SSM_GROUP + gi * SSM_GROUP
            for h in range(CHUNK // SUBLANES):
                pieces = [(by_token[SUBLANES * h + j][p // 4], src) for j in range(SUBLANES)]
                lo = gi * CHUNK * SSM_GROUP + h * LANES
                u_ref[p, 0, :, lo:lo + LANES] = _lane_piece_gather(pieces, lane16)


def _embed_tail_body(x_ref, g_ref, b_ref, w_ref, qg_ref, kg_ref, bones_ref, cos_ref, sin_ref,
                     h_in, q_in, k_in, u_in, h_ref, q_ref, k_ref, vt0_ref, vt1_ref, u_ref, u_scr):
    del h_in, q_in, k_in, u_in
    _embed_body(x_ref, g_ref, b_ref, w_ref, qg_ref, kg_ref, bones_ref, cos_ref, sin_ref,
                h_ref, q_ref, k_ref, vt0_ref, vt1_ref, u_ref, u_scr)


def _embed(x, meta_pad, ln_g, ln_b, w_in, qg, kg, bones, cos, sin, L, Lp):
    B = x.shape[0]
    TM = EMBED_TILE
    nj = L // TM
    w_specs = [_const_spec(1, D_MODEL), _const_spec(1, D_MODEL), _const_spec(D_MODEL, IN_WIDTH),
               _const_spec(1, Q_END), _const_spec(1, KV_WIDTH), _const_spec(Q_END, Q_END)]

    def out_shapes(vt_tiles, vt_width):
        vt = jax.ShapeDtypeStruct((B, vt_tiles, VT_ROWS, vt_width), BF16)
        return [
            jax.ShapeDtypeStruct((B, Lp, D_MODEL), F32),
            jax.ShapeDtypeStruct((B, Lp, Q_END), BF16),
            jax.ShapeDtypeStruct((B, Lp, KV_WIDTH), BF16),
            vt, vt,
            jax.ShapeDtypeStruct((N_PAIRS, B, Lp // CHUNK, PAIR_W), F32),
        ]

    def out_specs(tm, row_block):
        return [
            pl.BlockSpec((1, tm, D_MODEL), lambda b, j: (b, row_block(j), 0)),
            pl.BlockSpec((1, tm, Q_END), lambda b, j: (b, row_block(j), 0)),
            pl.BlockSpec((1, tm, KV_WIDTH), lambda b, j: (b, row_block(j), 0)),
            pl.BlockSpec((1, 1, VT_ROWS, tm), lambda b, j: (b, j, 0, 0)),
            pl.BlockSpec((1, 1, VT_ROWS, tm), lambda b, j: (b, j, 0, 0)),
            pl.BlockSpec((N_PAIRS, 1, tm // CHUNK, PAIR_W), lambda b, j: (0, b, row_block(j), 0)),
        ]

    main = pl.pallas_call(
        _embed_body,
        grid=(B, nj),
        in_specs=[pl.BlockSpec((1, TM, D_MODEL), lambda b, j: (b, j, 0))] + w_specs
        + [pl.BlockSpec((TM, LANES), lambda b, j: (j, 0))] * 2,
        out_specs=out_specs(TM, lambda j: j),
        out_shape=out_shapes(nj, TM),
        scratch_shapes=[pltpu.VMEM((SSM_WIDTH // LANES, TM, LANES), F32)],
        compiler_params=_cparams(("parallel", "parallel")),
        name="embed_main",
    )
    h0, q, k, vt0, vt1, u = main(x, ln_g, ln_b, w_in, qg, kg, bones, cos, sin)

    jt = Lp // TAIL - 1
    tail = pl.pallas_call(
        _embed_tail_body,
        grid=(B, 1),
        in_specs=[pl.BlockSpec((1, TAIL, D_MODEL), lambda b, j: (0, 0, 0))] + w_specs
        + [pl.BlockSpec((TAIL, LANES), lambda b, j: (jt, 0))] * 2
        + [pl.BlockSpec(memory_space=pl.ANY)] * 4,
        out_specs=out_specs(TAIL, lambda j: jt),
        out_shape=out_shapes(1, TAIL),
        input_output_aliases={9: 0, 10: 1, 11: 2, 12: 5},
        scratch_shapes=[pltpu.VMEM((SSM_WIDTH // LANES, TAIL, LANES), F32)],
        compiler_params=_cparams(("parallel", "arbitrary")),
        name="embed_tail",
    )
    h0, q, k, vt0_tail, vt1_tail, u = tail(meta_pad, ln_g, ln_b, w_in, qg, kg, bones, cos, sin,
                                           h0, q, k, u)
    return h0, q, k, (vt0, vt1, vt0_tail, vt1_tail), u


def _attn_body(q_ref, k_ref, vt0_ref, vt1_ref, vt0t_ref, vt1t_ref, o_ref,
               qt_ref, s0_ref, s1_ref, mc0_ref, mc1_ref, m_ref, acc_ref, *, L, n_kv_tiles):
    cols = Q_PER_KV * Q_TILE
    _attn_load_qt(q_ref, qt_ref)
    m_ref[...] = jnp.full(m_ref.shape, NEG_BIG, F32)
    acc_ref[...] = jnp.zeros(acc_ref.shape, F32)

    s_bufs = ((s0_ref, mc0_ref), (s1_ref, mc1_ref))

    def scores(slot, k_tile, first_valid=0):
        s_ref, mc_ref = s_bufs[slot]
        n = k_tile.shape[0]
        for g in range(N_KV_HEADS):
            s = jnp.dot(k_tile, qt_ref[g], preferred_element_type=F32)
            if first_valid:
                key = lax.broadcasted_iota(jnp.int32, s.shape, 0)
                s = jnp.where(key >= first_valid, s, NEG_BIG)
            s_ref[g, 0:n, :] = s
            mc_ref[g] = jnp.broadcast_to(jnp.max(s, axis=0, keepdims=True), (SUBLANES, cols))

    def softmax_values(slot, vt_tiles, n):
        s_ref, mc_ref = s_bufs[slot]
        for g in range(N_KV_HEADS):
            m_old = m_ref[g]
            m_new = jnp.maximum(m_old, mc_ref[g])
            alpha = jnp.exp2(m_old[0:1, :] - m_new[0:1, :])
            acc = alpha * acc_ref[g]
            for k0 in range(0, n, KEY_SUBTILE):
                k1 = min(k0 + KEY_SUBTILE, n)
                p = jnp.exp2(s_ref[g, k0:k1, :] - m_new[0:1, :]).astype(BF16)
                acc = acc + jnp.dot(vt_tiles[g][:, k0:k1], p, preferred_element_type=F32)
            acc_ref[g] = acc
            m_ref[g] = m_new

    def k_tile(i):
        return k_ref[0, pl.ds(pl.multiple_of(i * KV_TILE, KV_TILE), KV_TILE), :]

    def vt_tiles(i):
        return vt0_ref[0, i], vt1_ref[0, i]

    scores(0, k_tile(0))

    def run_tiles(first, count, next_scores):
        for t in range(count):
            softmax_values(t % 2, vt_tiles(first + t), KV_TILE)
            if t + 1 < count:
                scores((t + 1) % 2, k_tile(first + t + 1))
            else:
                next_scores()

    def kv_group(i, carry):
        first = i * KV_UNROLL
        run_tiles(first, KV_UNROLL, lambda: scores(0, k_tile(first + KV_UNROLL)))
        return carry

    n_groups = n_kv_tiles // KV_UNROLL
    lax.fori_loop(0, n_groups - 1, kv_group, 0)
    run_tiles((n_groups - 1) * KV_UNROLL, KV_UNROLL,
              lambda: scores(0, k_ref[0, L:L + TAIL, :], PAD_ROWS))
    softmax_values(0, (vt0t_ref[0, 0], vt1t_ref[0, 0]), TAIL)

    _attn_store_out(acc_ref, o_ref)


def _attn_bounded_body(q_ref, k_ref, vt0_ref, vt1_ref, vt0t_ref, vt1t_ref, o_ref,
                       qt_ref, p0_ref, p1_ref, acc_ref, *, L, n_kv_tiles):
    _attn_load_qt(q_ref, qt_ref)
    acc_ref[...] = jnp.zeros(acc_ref.shape, F32)
    p_bufs = (p0_ref, p1_ref)

    def key_blocks(k_tile, vt_tiles, accs, first_valid=0):
        n = k_tile.shape[0]
        accs = list(accs)
        for i, k0 in enumerate(range(0, n, KEY_SUBTILE)):
            k1 = min(k0 + KEY_SUBTILE, n)
            p_ref = p_bufs[i % 2]
            for g in range(N_KV_HEADS):
                s = jnp.dot(k_tile[k0:k1], qt_ref[g], preferred_element_type=F32)
                if first_valid:
                    key = k0 + lax.broadcasted_iota(jnp.int32, s.shape, 0)
                    s = jnp.where(key >= first_valid, s, NEG_BIG)
                p_ref[g, 0:k1 - k0, :] = jnp.exp2(s).astype(BF16)
            for g in range(N_KV_HEADS):
                accs[g] = accs[g] + jnp.dot(vt_tiles[g][:, k0:k1], p_ref[g, 0:k1 - k0, :],
                                            preferred_element_type=F32)
        return accs

    def kv_group(i, carry):
        accs = [acc_ref[g] for g in range(N_KV_HEADS)]
        for t in range(KV_UNROLL):
            tile = i * KV_UNROLL + t
            start = pl.multiple_of(tile * KV_TILE, KV_TILE)
            accs = key_blocks(k_ref[0, pl.ds(start, KV_TILE), :], (vt0_ref[0, tile], vt1_ref[0, tile]),
                              accs)
        for g in range(N_KV_HEADS):
            acc_ref[g] = accs[g]
        return carry

    lax.fori_loop(0, n_kv_tiles // KV_UNROLL, kv_group, 0)
    accs = key_blocks(k_ref[0, L:L + TAIL, :], (vt0t_ref[0, 0], vt1t_ref[0, 0]),
                      [acc_ref[g] for g in range(N_KV_HEADS)], PAD_ROWS)
    for g in range(N_KV_HEADS):
        acc_ref[g] = accs[g]
    _attn_store_out(acc_ref, o_ref)


def _attn_load_qt(q_ref, qt_ref):
    tq = Q_TILE
    row = lax.broadcasted_iota(jnp.int32, (KV_WIDTH, tq), 0)
    for r in range(Q_PER_KV):
        slab_t = q_ref[0, :, r * LANES:(r + 1) * LANES].astype(F32).T
        for g in range(N_KV_HEADS):
            in_group = (row >= g * HEAD_DIM) & (row < (g + 1) * HEAD_DIM)
            qt_ref[g, :, r * tq:(r + 1) * tq] = jnp.where(in_group, slab_t, 0.0).astype(BF16)


def _attn_store_out(acc_ref, o_ref):
    tq = Q_TILE
    comb = jnp.concatenate(
        [acc_ref[g, 0:HEAD_DIM, :] / acc_ref[g, HEAD_DIM:HEAD_DIM + 1, :] for g in range(N_KV_HEADS)],
        axis=0)
    for r in range(Q_PER_KV):
        o_ref[0, :, r * LANES:(r + 1) * LANES] = comb[:, r * tq:(r + 1) * tq].T.astype(BF16)


def _attention(q, k, vts, L, Lp, score_bound):
    B = q.shape[0]
    nkt = L // KV_TILE
    assert nkt % KV_UNROLL == 0
    cols = Q_PER_KV * Q_TILE
    vt_spec = pl.BlockSpec((1, nkt, VT_ROWS, KV_TILE), lambda b, j: (b, 0, 0, 0))
    vt_tail_spec = pl.BlockSpec((1, 1, VT_ROWS, TAIL), lambda b, j: (b, 0, 0, 0))
    call = functools.partial(
        pl.pallas_call,
        grid=(B, Lp // Q_TILE),
        in_specs=[
            pl.BlockSpec((1, Q_TILE, Q_END), lambda b, j: (b, j, 0)),
            pl.BlockSpec((1, Lp, KV_WIDTH), lambda b, j: (b, 0, 0)),
            vt_spec, vt_spec, vt_tail_spec, vt_tail_spec,
        ],
        out_specs=pl.BlockSpec((1, Q_TILE, Q_END), lambda b, j: (b, j, 0)),
        out_shape=jax.ShapeDtypeStruct((B, Lp, Q_END), BF16),
        compiler_params=_cparams(("parallel", "arbitrary")),
    )
    bounded = call(
        functools.partial(_attn_bounded_body, L=L, n_kv_tiles=nkt),
        scratch_shapes=[
            pltpu.VMEM((N_KV_HEADS, KV_WIDTH, cols), BF16),
            pltpu.VMEM((N_KV_HEADS, KEY_SUBTILE, cols), BF16),
            pltpu.VMEM((N_KV_HEADS, KEY_SUBTILE, cols), BF16),
            pltpu.VMEM((N_KV_HEADS, VT_ROWS, cols), F32),
        ],
        name="attention_bounded",
    )
    general = call(
        functools.partial(_attn_body, L=L, n_kv_tiles=nkt),
        scratch_shapes=[
            pltpu.VMEM((N_KV_HEADS, KV_WIDTH, cols), BF16),
            pltpu.VMEM((N_KV_HEADS, KV_TILE, cols), F32),
            pltpu.VMEM((N_KV_HEADS, KV_TILE, cols), F32),
            pltpu.VMEM((N_KV_HEADS, SUBLANES, cols), F32),
            pltpu.VMEM((N_KV_HEADS, SUBLANES, cols), F32),
            pltpu.VMEM((N_KV_HEADS, SUBLANES, cols), F32),
            pltpu.VMEM((N_KV_HEADS, VT_ROWS, cols), F32),
        ],
        name="attention",
    )
    return lax.cond(score_bound <= SAFE_EXP2_RANGE, bounded, general, q, k, *vts)


def _ssm_weights(lam_re, lam_im, log_dt, b_re, b_im, c_re, c_im, d_skip):
    hp = lax.Precision.HIGHEST
    dt = jnp.exp(log_dt.astype(F32))[..., None]
    lr = lam_re.astype(F32)
    li = lam_im.astype(F32)

    def apow(n):
        n = jnp.asarray(n, F32)
        mag = jnp.exp(lr[..., None] * dt[..., None] * n)
        ang = li[..., None] * dt[..., None] * n
        return mag * jnp.cos(ang), mag * jnp.sin(ang)

    a1r, a1i = apow(jnp.ones((1,), F32))
    a1r, a1i = a1r[..., 0], a1i[..., 0]
    nr = a1r - 1.0
    den = lr * lr + li * li
    f_r = (nr * lr + a1i * li) / den
    f_i = (a1i * lr - nr * li) / den
    br = b_re.astype(F32)
    bi = b_im.astype(F32)
    bb_r = f_r[..., None] * br - f_i[..., None] * bi
    bb_i = f_r[..., None] * bi + f_i[..., None] * br
    cr = c_re.astype(F32)
    ci = c_im.astype(F32)

    tau = jnp.arange(CHUNK + 1, dtype=F32)
    pr, pi = apow(tau)
    ab_r = pr[..., None] * bb_r[:, :, :, None, :] - pi[..., None] * bb_i[:, :, :, None, :]
    ab_i = pr[..., None] * bb_i[:, :, :, None, :] + pi[..., None] * bb_r[:, :, :, None, :]
    kern = (jnp.einsum("dgop,dgptc->dgtco", cr, ab_r, precision=hp)
            - jnp.einsum("dgop,dgptc->dgtco", ci, ab_i, precision=hp))
    s_idx = jnp.arange(CHUNK)[:, None]
    t_idx = jnp.arange(CHUNK)[None, :]
    lag_f = jnp.clip(t_idx - s_idx, 0, CHUNK)
    lag_b = jnp.clip(s_idx - t_idx, 0, CHUNK)
    m_f = jnp.where((t_idx >= s_idx)[..., None, None], kern[0][:, lag_f], 0.0)
    m_b = jnp.where((s_idx >= t_idx)[..., None, None], kern[1][:, lag_b], 0.0)
    eye_t = jnp.eye(CHUNK, dtype=F32)[None, :, :, None, None]
    eye_c = jnp.eye(SSM_GROUP, dtype=F32)[None, None, None]
    dsk = d_skip.astype(F32).reshape(N_SSM_GROUPS, 1, 1, SSM_GROUP, 1)
    m_all = m_f + m_b + eye_t * eye_c * dsk
    m_all = m_all.transpose(0, 1, 3, 2, 4).reshape(N_SSM_GROUPS, 256, 256)

    def state_in(d, expo):
        er, ei = pr[d][..., expo], pi[d][..., expo]
        wr = er[..., None] * bb_r[d][:, :, None, :] - ei[..., None] * bb_i[d][:, :, None, :]
        wi = er[..., None] * bb_i[d][:, :, None, :] + ei[..., None] * bb_r[d][:, :, None, :]
        to_rows = lambda w: w.transpose(0, 2, 3, 1).reshape(N_SSM_GROUPS, 256, SSM_STATE)
        return to_rows(wr), to_rows(wi)

    sf_r, sf_i = state_in(0, jnp.arange(CHUNK - 1, -1, -1))
    sb_r, sb_i = state_in(1, jnp.arange(CHUNK))

    def state_out(d, expo):
        er, ei = pr[d][..., expo], pi[d][..., expo]
        wr = cr[d].transpose(0, 2, 1)[:, :, None, :] * er[..., None] \
            - ci[d].transpose(0, 2, 1)[:, :, None, :] * ei[..., None]
        wi = cr[d].transpose(0, 2, 1)[:, :, None, :] * ei[..., None] \
            + ci[d].transpose(0, 2, 1)[:, :, None, :] * er[..., None]
        flat = lambda w: w.reshape(N_SSM_GROUPS, SSM_STATE, 256)
        return flat(wr), flat(-wi)

    of_r, of_i = state_out(0, jnp.arange(1, CHUNK + 1))
    ob_r, ob_i = state_out(1, jnp.arange(CHUNK, 0, -1))

    def pair_rows(w):
        return w.reshape(N_PAIRS, 2, *w.shape[1:])

    z256 = jnp.zeros((N_PAIRS, 256, 256), F32)
    mp = pair_rows(m_all)
    w1_y = jnp.concatenate([jnp.concatenate([mp[:, 0], z256], axis=2),
                            jnp.concatenate([z256, mp[:, 1]], axis=2)], axis=1)

    def pair_cols_in(w):
        wp = pair_rows(w)
        z = jnp.zeros_like(wp[:, 0])
        return jnp.concatenate([jnp.concatenate([wp[:, 0], z], axis=2),
                                jnp.concatenate([z, wp[:, 1]], axis=2)], axis=1)

    w1 = jnp.concatenate([w1_y, pair_cols_in(sf_r), pair_cols_in(sf_i),
                          pair_cols_in(sb_r), pair_cols_in(sb_i)], axis=2)

    def pair_rows_out(w):
        wp = pair_rows(w)
        z = jnp.zeros_like(wp[:, 0])
        return jnp.concatenate([jnp.concatenate([wp[:, 0], z], axis=2),
                                jnp.concatenate([z, wp[:, 1]], axis=2)], axis=1)

    w2 = jnp.concatenate([pair_rows_out(of_r), pair_rows_out(of_i),
                          pair_rows_out(ob_r), pair_rows_out(ob_i)], axis=1)

    qr, qi = apow(CHUNK * jnp.arange(SUBLANES + 1, dtype=F32))
    zero = jnp.zeros_like(qr[..., 0])

    def dec_rows(d, q_expo):
        rows = [qr[d][..., 1], qi[d][..., 1], qr[d][..., 2], qi[d][..., 2],
                qr[d][..., 4], qi[d][..., 4], zero[d], zero[d]]
        rows += [qr[d][..., n] for n in q_expo] + [qi[d][..., n] for n in q_expo]
        return jnp.stack(rows, axis=1)

    dec = jnp.concatenate([dec_rows(0, range(1, SUBLANES + 1)),
                           dec_rows(1, range(SUBLANES, 0, -1))], axis=1)
    dec = dec.reshape(N_PAIRS, 2, DEC_ROWS, SSM_STATE).transpose(0, 2, 1, 3)
    dec = dec.reshape(N_PAIRS, DEC_ROWS, PAIR_STATE)
    return w1.astype(BF16), w2.astype(BF16), dec


def _ssm_body(u_ref, w1_ref, w2_ref, dec_ref, z_ref, sh_ref, *, B, n_chunks):
    R = B * n_chunks
    tiles = [(r0, min(SSM_ROW_TILE, R - r0)) for r0 in range(0, R, SSM_ROW_TILE)]
    PS = PAIR_STATE
    n_blocks = n_chunks // SUBLANES

    for r0, n in tiles:
        t = jnp.dot(u_ref[0, r0:r0 + n, :].astype(BF16), w1_ref[0], preferred_element_type=F32)
        z_ref[0, r0:r0 + n, :] = t[:, :PAIR_W]
        sh_ref[r0:r0 + n, :] = t[:, PAIR_W:]

    sub = lax.broadcasted_iota(jnp.int32, (SUBLANES, PS), 0)

    def cmul(ar, ai, xr, xi):
        return ar * xr - ai * xi, ar * xi + ai * xr

    def block_scan(sr, si, cr, ci, base, reverse):
        xr, xi = sr, si
        for k, d in enumerate((1, 2, 4)):
            pr_ = dec_ref[0, base + 2 * k:base + 2 * k + 1, :]
            pi_ = dec_ref[0, base + 2 * k + 1:base + 2 * k + 2, :]
            keep = (sub < SUBLANES - d) if reverse else (sub >= d)
            shift = SUBLANES - d if reverse else d
            yr = jnp.where(keep, pltpu.roll(xr, shift, 0), 0.0)
            yi = jnp.where(keep, pltpu.roll(xi, shift, 0), 0.0)
            mr, mi = cmul(pr_, pi_, yr, yi)
            xr, xi = xr + mr, xi + mi
        qr_ = dec_ref[0, base + 8:base + 16, :]
        qi_ = dec_ref[0, base + 16:base + 24, :]
        mr, mi = cmul(qr_, qi_, cr, ci)
        xr, xi = xr + mr, xi + mi
        edge = SUBLANES - 1 if reverse else 0
        shift = SUBLANES - 1 if reverse else 1
        er = jnp.where(sub == edge, cr, pltpu.roll(xr, shift, 0))
        ei = jnp.where(sub == edge, ci, pltpu.roll(xi, shift, 0))
        last = 0 if reverse else SUBLANES - 1
        nr = jnp.broadcast_to(xr[last:last + 1, :], (SUBLANES, PS))
        ni = jnp.broadcast_to(xi[last:last + 1, :], (SUBLANES, PS))
        return er, ei, nr, ni

    def scan_step(i, carry):
        new = []
        for b in range(B):
            cfr, cfi, cbr, cbi = carry[4 * b:4 * b + 4]
            tile_f = (i + n_blocks - 1) % n_blocks
            tile_b = (2 * n_blocks - 2 - i) % n_blocks
            rf = pl.multiple_of(b * n_chunks + tile_f * SUBLANES, SUBLANES)
            rb = pl.multiple_of(b * n_chunks + tile_b * SUBLANES, SUBLANES)
            f_rows, b_rows = pl.ds(rf, SUBLANES), pl.ds(rb, SUBLANES)
            er, ei, cfr, cfi = block_scan(sh_ref[f_rows, 0:PS], sh_ref[f_rows, PS:2 * PS],
                                          cfr, cfi, 0, False)
            sh_ref[f_rows, 0:PS] = er
            sh_ref[f_rows, PS:2 * PS] = ei
            er, ei, cbr, cbi = block_scan(sh_ref[b_rows, 2 * PS:3 * PS], sh_ref[b_rows, 3 * PS:4 * PS],
                                          cbr, cbi, DEC_ROWS // 2, True)
            sh_ref[b_rows, 2 * PS:3 * PS] = er
            sh_ref[b_rows, 3 * PS:4 * PS] = ei
            new += [cfr, cfi, cbr, cbi]
        return tuple(new)

    zero = jnp.zeros((SUBLANES, PS), F32)
    lax.fori_loop(0, n_blocks, scan_step, (zero,) * (4 * B))

    for r0, n in tiles:
        y = z_ref[0, r0:r0 + n, :] + jnp.dot(sh_ref[r0:r0 + n, :].astype(BF16), w2_ref[0],
                                             preferred_element_type=F32)
        z_ref[0, r0:r0 + n, :] = jax.nn.gelu(y)


def _ssm(u_pairs, w1, w2, dec, B, n_chunks):
    assert n_chunks % SUBLANES == 0
    R = B * n_chunks
    return pl.pallas_call(
        functools.partial(_ssm_body, B=B, n_chunks=n_chunks),
        grid=(N_PAIRS,),
        in_specs=[
            pl.BlockSpec((1, R, PAIR_W), lambda p: (p, 0, 0)),
            pl.BlockSpec((1, PAIR_W, 2 * PAIR_W), lambda p: (p, 0, 0)),
            pl.BlockSpec((1, PAIR_W, PAIR_W), lambda p: (p, 0, 0)),
            pl.BlockSpec((1, DEC_ROWS, PAIR_STATE), lambda p: (p, 0, 0)),
        ],
        out_specs=pl.BlockSpec((1, R, PAIR_W), lambda p: (p, 0, 0)),
        out_shape=jax.ShapeDtypeStruct((N_PAIRS, R, PAIR_W), F32),
        scratch_shapes=[pltpu.VMEM((R, 4 * PAIR_STATE), F32)],
        compiler_params=_cparams(("parallel",)),
        name="ssm",
    )(u_pairs, w1, w2, dec)


def _mix_body(o_ref, zp_ref, h0_ref, wglu_ref, bglu_ref, woa_ref, wos_ref, g_ref, b_ref, wr_ref,
              h1_ref, acc_ref, aff_ref, z_scr, *, L, Lp):
    n_rows = zp_ref.shape[1]
    lane16 = lax.broadcasted_iota(jnp.int32, (n_rows, LANES), 1) // SSM_GROUP
    for t in range(CHUNK):
        src = (t % SUBLANES) * SSM_GROUP
        for v in range(SSM_WIDTH // LANES):
            pieces = []
            for j in range(SUBLANES):
                p, gi = 4 * v + j // 2, j % 2
                lo = gi * CHUNK * SSM_GROUP + (t // SUBLANES) * LANES
                pieces.append((zp_ref[p, :, lo:lo + LANES], src))
            z_scr[v, pl.ds(t, n_rows, stride=CHUNK), :] = _lane_piece_gather(pieces, lane16)
    z = jnp.concatenate([z_scr[v] for v in range(SSM_WIDTH // LANES)], axis=1)
    gate = jax.nn.sigmoid(jnp.dot(z.astype(BF16), wglu_ref[...], preferred_element_type=F32)
                          + bglu_ref[...])
    ssm_out = (z * gate).astype(BF16)
    mix = (jnp.dot(o_ref[...], woa_ref[...], preferred_element_type=F32)
           + jnp.dot(ssm_out, wos_ref[...], preferred_element_type=F32))
    h1 = _layer_norm_rows(DEEPNORM_ALPHA * h0_ref[...] + mix, g_ref[...], b_ref[...])
    for s in range(ROW_SLABS):
        acc_ref[:, s, :] = DEEPNORM_ALPHA * h1[:, s * LANES:(s + 1) * LANES]
    h1_bits = pltpu.bitcast(h1.astype(BF16).astype(F32), jnp.uint32)
    for s in range(ROW_SLABS // 2):
        lo = h1_bits[:, s * LANES:(s + 1) * LANES]
        hi = h1_bits[:, (s + ROW_SLABS // 2) * LANES:(s + ROW_SLABS // 2 + 1) * LANES]
        h1_ref[:, s, :] = hi | lax.shift_right_logical(lo, jnp.uint32(16))
    logits = jnp.dot(h1.astype(BF16), wr_ref[...], preferred_element_type=F32)
    lane = lax.broadcasted_iota(jnp.int32, logits.shape, 1)
    logits = jnp.where(lane < N_EXPERTS, logits, NEG_BIG)
    e = jnp.exp(logits - jnp.max(logits, axis=-1, keepdims=True))
    aff = (e / jnp.sum(e, axis=-1, keepdims=True)).T[:N_EXPERTS, :]
    tiles_per_seq = Lp // ROW_TILE
    pos = (pl.program_id(0) % tiles_per_seq) * ROW_TILE + lax.broadcasted_iota(
        jnp.int32, aff.shape, 1)
    aff_ref[...] = jnp.where((pos < L) | (pos >= L + PAD_ROWS), aff, -1.0)


def _mix(o, z, h0, wglu, bglu, woa, wos, g, b, wr, L, Lp):
    N = o.shape[0]
    TM = ROW_TILE
    row = lambda w: pl.BlockSpec((TM, w), lambda i: (i, 0))
    tok_rows = pl.BlockSpec((TM, ROW_SLABS, LANES), lambda i: (i, 0, 0))
    return pl.pallas_call(
        functools.partial(_mix_body, L=L, Lp=Lp),
        grid=(N // TM,),
        in_specs=[row(ATTN_WIDTH), pl.BlockSpec((N_PAIRS, TM // CHUNK, PAIR_W), lambda i: (0, i, 0)),
                  row(D_MODEL),
                  _const_spec(SSM_WIDTH, SSM_WIDTH), _const_spec(1, SSM_WIDTH),
                  _const_spec(ATTN_WIDTH, D_MODEL), _const_spec(SSM_WIDTH, D_MODEL),
                  _const_spec(1, D_MODEL), _const_spec(1, D_MODEL), _const_spec(D_MODEL, LANES)],
        out_specs=[pl.BlockSpec((TM, ROW_SLABS // 2, LANES), lambda i: (i, 0, 0)), tok_rows,
                   pl.BlockSpec((N_EXPERTS, TM), lambda i: (0, i))],
        out_shape=[jax.ShapeDtypeStruct((N, ROW_SLABS // 2, LANES), jnp.uint32),
                   jax.ShapeDtypeStruct((N, ROW_SLABS, LANES), F32),
                   jax.ShapeDtypeStruct((N_EXPERTS, N), F32)],
        scratch_shapes=[pltpu.VMEM((SSM_WIDTH // LANES, TM, LANES), F32)],
        compiler_params=_cparams(("parallel",)),
        name="mix",
    )(o, z, h0, wglu, bglu, woa, wos, g, b, wr)


def _select_body(aff_ref, ord_ref, idx_ref, thr_ref, *, capacity, n_slot_tiles, seq_len, pad_start):
    nbp = aff_ref.shape[1]
    ri = lax.broadcasted_iota(jnp.int32, (LANES, LANES), 0)
    ci = lax.broadcasted_iota(jnp.int32, (LANES, LANES), 1)
    strict_upper = (ri < ci).astype(BF16)
    incl_upper = (ri <= ci).astype(BF16)
    ones = jnp.ones((LANES, LANES), BF16)
    order = ord_ref[...]
    blk = lax.broadcasted_iota(jnp.int32, (nbp, LANES), 0).astype(F32)
    blk_hi = jnp.floor(blk * (1.0 / 16.0))
    blk_lo = blk - 16.0 * blk_hi
    dot = functools.partial(jnp.dot, preferred_element_type=F32)
    dot_nt = functools.partial(lax.dot_general, dimension_numbers=(((1,), (1,)), ((), ())),
                               preferred_element_type=F32)

    def search_bit(i, ts):
        bit = jnp.left_shift(jnp.int32(1), 30 - i)
        new = []
        for e in range(N_EXPERTS):
            cand = ts[e] | bit
            bits = pltpu.bitcast(aff_ref[e], jnp.int32)
            cnt = jnp.sum((bits >= cand).astype(jnp.int32), keepdims=True)
            new.append(jnp.where(cnt >= capacity, cand, ts[e]))
        return tuple(new)

    thresholds = lax.fori_loop(0, 31, search_bit, (jnp.zeros((1, 1), jnp.int32),) * N_EXPERTS)
    for e in range(N_EXPERTS):
        thr_ref[e] = jnp.broadcast_to(thresholds[e], (SUBLANES, LANES))

    def per_expert(e, carry):
        bits = pltpu.bitcast(aff_ref[e], jnp.int32)
        t = thr_ref[e][0:1, 0:1]
        gt = bits > t
        eq = bits == t
        need = (capacity - jnp.sum(gt.astype(jnp.int32), keepdims=True)).astype(F32)
        eqb = eq.astype(BF16)
        tie_rank = dot(order, dot(eqb, ones).astype(BF16)) + dot(eqb, strict_upper)
        sel = (gt | (eq & (tie_rank < need))).astype(BF16)
        csum_in_blk = dot(sel, incl_upper)
        blk_tot = dot(sel, ones)
        blk_off = dot(order, blk_tot.astype(BF16))
        off_hi = jnp.floor(blk_off * (1.0 / 64.0))
        off_lo = blk_off - 64.0 * off_hi
        table = jnp.concatenate([csum_in_blk, off_hi, off_lo, blk_hi, blk_lo],
                                axis=1).astype(BF16)
        tot_l = dot_nt(ones[:SUBLANES], sel)
        off_l = dot_nt(tot_l.astype(BF16), order)[0:1, :]
        tot_l = tot_l[0:1, :]

        def per_tile(st, carry2):
            j = (st * SLOT_TILE + lax.broadcasted_iota(jnp.int32, (SLOT_TILE, 1), 0)).astype(F32)
            onehot = ((off_l <= j) & (j < off_l + tot_l)).astype(BF16)
            got = dot(onehot, table)
            local = j - (64.0 * got[:, LANES:2 * LANES] + got[:, 2 * LANES:3 * LANES])
            lane_idx = dot((got[:, :LANES] <= local).astype(BF16), ones)
            block = 16.0 * got[:, 3 * LANES:4 * LANES] + got[:, 4 * LANES:5 * LANES]
            k = j - capacity
            seq = sum((k >= PAD_ROWS * i).astype(F32) for i in range(1, SLOT_TILE // PAD_ROWS + 1))
            pad_tok = seq * (seq_len - PAD_ROWS) + pad_start + k
            tok = jnp.where(j < capacity, block * LANES + lane_idx, pad_tok)
            idx_ref[e, st] = tok.T[0:1, :].astype(jnp.int32)
            return carry2

        lax.fori_loop(0, n_slot_tiles, per_tile, 0)
        return carry

    lax.fori_loop(0, N_EXPERTS, per_expert, 0)


def _select(aff_blocks, order, capacity, n_slot_tiles, seq_len, pad_start):
    nbp = aff_blocks.shape[1]
    return pl.pallas_call(
        functools.partial(_select_body, capacity=capacity, n_slot_tiles=n_slot_tiles,
                          seq_len=seq_len, pad_start=pad_start),
        grid=(1,),
        in_specs=[_const_spec(N_EXPERTS, nbp, LANES), _const_spec(nbp, nbp)],
        out_specs=_const_spec(N_EXPERTS, n_slot_tiles, 1, SLOT_TILE),
        out_shape=jax.ShapeDtypeStruct((N_EXPERTS, n_slot_tiles, 1, SLOT_TILE), jnp.int32),
        scratch_shapes=[pltpu.VMEM((N_EXPERTS, SUBLANES, LANES), jnp.int32)],
        compiler_params=_cparams(("arbitrary",)),
        name="select",
    )(aff_blocks, order)


def _moe_body(idx_ref, idx_next_ref, h1_hbm, wg_ref, wu_ref, wd_ref, wr_ref, acc_in, acc_hbm,
              xbuf, abuf, sem_x, sem_a, sem_s):
    del acc_in
    T = xbuf.shape[1]
    n_c = pl.num_programs(1)
    step = pl.program_id(0) * n_c + pl.program_id(1)
    n_steps = pl.num_programs(0) * n_c
    slot = step % 2

    def start_rows(copy_of_row, idx):
        def body(j, carry):
            copy_of_row(idx[0, 0, 0, j], j).start()
            return carry
        lax.fori_loop(0, T, body, 0, unroll=8)

    def x_row(buf):
        return lambda row, j: pltpu.make_async_copy(h1_hbm.at[row], xbuf.at[buf, j], sem_x.at[buf])

    def acc_row_in(row, j):
        return pltpu.make_async_copy(acc_hbm.at[row], abuf.at[j], sem_a)

    def acc_row_out(row, j):
        return pltpu.make_async_copy(abuf.at[j], acc_hbm.at[row], sem_s)

    def wait_scatter():
        pltpu.make_async_copy(abuf, acc_hbm.at[pl.ds(0, T)], sem_s).wait()

    @pl.when(step == 0)
    def _():
        start_rows(x_row(0), idx_ref)

    @pl.when(step > 0)
    def _():
        wait_scatter()

    start_rows(acc_row_in, idx_ref)

    @pl.when(step + 1 < n_steps)
    def _():
        start_rows(x_row(1 - slot), idx_next_ref)

    pltpu.make_async_copy(h1_hbm.at[pl.ds(0, T)], xbuf.at[slot], sem_x.at[slot]).wait()
    packed = [xbuf[slot, :, s, :] for s in range(ROW_SLABS // 2)]
    lo = [pltpu.bitcast(lax.shift_left(u, jnp.uint32(16)), F32) for u in packed]
    hi = [pltpu.bitcast(u & jnp.uint32(0xFFFF0000), F32) for u in packed]
    x = jnp.concatenate(lo + hi, axis=1).astype(BF16)
    logits = jnp.dot(x, wr_ref[...], preferred_element_type=F32)
    lane = lax.broadcasted_iota(jnp.int32, logits.shape, 1)
    logits = jnp.where(lane < N_EXPERTS, logits, NEG_BIG)
    p = jnp.exp(logits - jnp.max(logits, axis=-1, keepdims=True))
    gate = (jnp.sum(jnp.where(lane == pl.program_id(0), p, 0.0), axis=-1, keepdims=True)
            / jnp.sum(p, axis=-1, keepdims=True))
    y = jnp.zeros((T, D_MODEL), F32)
    for f in range(EXPERT_FF // FF_TILE):
        fs = slice(f * FF_TILE, (f + 1) * FF_TILE)
        hg = jnp.dot(x, wg_ref[0, :, fs], preferred_element_type=F32)
        hu = jnp.dot(x, wu_ref[0, :, fs], preferred_element_type=F32)
        hid = (jax.nn.silu(hg) * hu).astype(BF16)
        y = y + jnp.dot(hid, wd_ref[0, fs, :], preferred_element_type=F32)
    pltpu.make_async_copy(acc_hbm.at[pl.ds(0, T)], abuf, sem_a).wait()
    y = y * gate
    for s in range(ROW_SLABS):
        abuf[:, s, :] = abuf[:, s, :] + y[:, s * LANES:(s + 1) * LANES]
    start_rows(acc_row_out, idx_ref)

    @pl.when(step == n_steps - 1)
    def _():
        wait_scatter()


def _moe(idx, h1, acc, wg, wu, wd, wr):
    n_slots = idx.shape[1] * idx.shape[3]
    n_c = next(n for n in range(1, n_slots) if n_slots % n == 0 and n_slots // n <= MOE_TILE_MAX
               and (n_slots // n) % 16 == 0)
    T = n_slots // n_c
    idx = idx.reshape(N_EXPERTS, n_c, 1, T)

    def next_block(e, c):
        return (jnp.minimum(e + (c + 1) // n_c, N_EXPERTS - 1), (c + 1) % n_c, 0, 0)

    return pl.pallas_call(
        _moe_body,
        grid=(N_EXPERTS, n_c),
        in_specs=[
            pl.BlockSpec((1, 1, 1, T), lambda e, c: (e, c, 0, 0), memory_space=pltpu.SMEM),
            pl.BlockSpec((1, 1, 1, T), next_block, memory_space=pltpu.SMEM),
            pl.BlockSpec(memory_space=pl.ANY),
            pl.BlockSpec((1, D_MODEL, EXPERT_FF), lambda e, c: (e, 0, 0)),
            pl.BlockSpec((1, D_MODEL, EXPERT_FF), lambda e, c: (e, 0, 0)),
            pl.BlockSpec((1, EXPERT_FF, D_MODEL), lambda e, c: (e, 0, 0)),
            _const_spec(D_MODEL, LANES),
            pl.BlockSpec(memory_space=pl.ANY),
        ],
        out_specs=pl.BlockSpec(memory_space=pl.ANY),
        out_shape=jax.ShapeDtypeStruct(acc.shape, F32),
        scratch_shapes=[pltpu.VMEM((2, T, ROW_SLABS // 2, LANES), jnp.uint32),
                        pltpu.VMEM((T, ROW_SLABS, LANES), F32),
                        pltpu.SemaphoreType.DMA((2,)), pltpu.SemaphoreType.DMA(()),
                        pltpu.SemaphoreType.DMA(())],
        input_output_aliases={7: 0},
        compiler_params=pltpu.CompilerParams(dimension_semantics=("arbitrary", "arbitrary"),
                                             vmem_limit_bytes=VMEM_LIMIT,
                                             disable_bounds_checks=True),
        name="expert_ffn",
    )(idx, idx, h1, wg, wu, wd, wr, acc)


def _final_body(a_ref, g_ref, b_ref, o_ref):
    x = jnp.concatenate([a_ref[0, :, s, :] for s in range(ROW_SLABS)], axis=1)
    o_ref[0] = _layer_norm_rows(x, g_ref[...], b_ref[...])


def _final_norm(acc, g, b, L):
    B = acc.shape[0]
    TM = EMBED_TILE
    return pl.pallas_call(
        _final_body,
        grid=(B, L // TM),
        in_specs=[pl.BlockSpec((1, TM, ROW_SLABS, LANES), lambda b_, j: (b_, j, 0, 0)),
                  _const_spec(1, D_MODEL), _const_spec(1, D_MODEL)],
        out_specs=pl.BlockSpec((1, TM, D_MODEL), lambda b_, j: (b_, j, 0)),
        out_shape=jax.ShapeDtypeStruct((B, L, D_MODEL), F32),
        compiler_params=_cparams(("parallel", "parallel")),
        name="final_norm",
    )(acc, g, b)


def _rope_tables(L, Lp):
    rows = L // GRID_W
    t = jnp.arange(L, dtype=jnp.int32)
    m = jnp.arange(N_META, dtype=jnp.int32)
    pad = jnp.zeros((Lp - L - N_META,), jnp.int32)
    row = jnp.concatenate([t // GRID_W - rows // 2, pad,
                           jnp.full((N_META,), -(rows // 2) - 1, jnp.int32)])
    col = jnp.concatenate([t % GRID_W - GRID_W // 2, pad, m - GRID_W // 2])
    inv_freq = ROPE_THETA ** (-jnp.arange(0, ROPE_AXIS_DIM, 2, dtype=F32) / ROPE_AXIS_DIM)
    ang_r = row.astype(F32)[:, None] * inv_freq
    ang_c = col.astype(F32)[:, None] * inv_freq
    cos = jnp.concatenate([jnp.cos(ang_r)] * 2 + [jnp.cos(ang_c)] * 2, axis=1)
    sin = jnp.concatenate([-jnp.sin(ang_r), jnp.sin(ang_r), -jnp.sin(ang_c), jnp.sin(ang_c)], axis=1)
    return jnp.tile(cos, (1, 2)), jnp.tile(sin, (1, 2))


def _block_order(B, Lp, nbp):
    nbb = Lp // LANES
    r = jnp.arange(nbp)
    b, jb = r // nbb, r % nbb
    rank = jnp.where(r < B * nbb, b * nbb + jnp.where(jb == nbb - 1, 0, jb + 1), r)
    return (rank[None, :] < rank[:, None]).astype(BF16)


def _run_trunk(x, meta_pad, shared):
    B, L, _ = x.shape
    Lp = L + TAIL
    N = B * Lp
    cos, sin = _rope_tables(L, Lp)
    h0, q, k, vts, u = _embed(x, meta_pad, shared["ln_emb_g"], shared["ln_emb_b"], shared["w_in"],
                              shared["qg"], shared["kg"], shared["bones"], cos, sin, L, Lp)
    o = _attention(q, k, vts, L, Lp, shared["score_bound"])
    z = _ssm(u.reshape(N_PAIRS, N // CHUNK, PAIR_W), shared["ssm_w1"], shared["ssm_w2"],
             shared["ssm_dec"], B, Lp // CHUNK)
    h1, acc, aff = _mix(o.reshape(N, ATTN_WIDTH), z, h0.reshape(N, D_MODEL),
                        shared["w_glu"], shared["b_glu"], shared["w_out_attn"], shared["w_out_ssm"],
                        shared["ln1_g"], shared["ln1_b"], shared["w_router"], L, Lp)

    capacity = EC_CAPACITY_FACTOR * B * (L + N_META) // N_EXPERTS
    n_slot_tiles = -(-capacity // SLOT_TILE)
    nb = N // LANES
    nbp = -(-nb // LANES) * LANES
    aff_blocks = jnp.pad(aff.reshape(N_EXPERTS, nb, LANES), ((0, 0), (0, nbp - nb), (0, 0)),
                         constant_values=-1.0)
    assert n_slot_tiles * SLOT_TILE - capacity <= B * PAD_ROWS
    idx = _select(aff_blocks, _block_order(B, Lp, nbp), capacity, n_slot_tiles, Lp, L)
    acc = _moe(idx, h1, acc, shared["w_gate"], shared["w_up"], shared["w_down"], shared["w_router"])
    return _final_norm(acc.reshape(B, Lp, ROW_SLABS, LANES), shared["ln2_g"], shared["ln2_b"], L)


def kernel(x_prompt, x_sample, meta_tokens, ln_emb_g, ln_emb_b, w_in, q_norm_g, k_norm_g, ssm_lambda_re, ssm_lambda_im, ssm_log_dt, ssm_b_re, ssm_b_im, ssm_c_re, ssm_c_im, ssm_d, w_glu, b_glu, w_out, ln1_g, ln1_b, w_router, w_gate, w_up, w_down, ln2_g, ln2_b):
    row = lambda a: a.reshape(1, -1).astype(F32)
    w_q = w_in[0][:, :Q_END].reshape(D_MODEL, N_KV_HEADS, Q_PER_KV, HEAD_DIM)
    w_q = w_q.transpose(0, 2, 1, 3).reshape(D_MODEL, Q_END)
    w_oa = w_out[0][:ATTN_WIDTH].reshape(N_KV_HEADS, Q_PER_KV, HEAD_DIM, D_MODEL)
    w_oa = w_oa.transpose(1, 0, 2, 3).reshape(ATTN_WIDTH, D_MODEL)
    head_of = jnp.arange(Q_END) // HEAD_DIM
    ssm_w1, ssm_w2, ssm_dec = _ssm_weights(ssm_lambda_re[0], ssm_lambda_im[0], ssm_log_dt[0],
                                           ssm_b_re[0], ssm_b_im[0], ssm_c_re[0], ssm_c_im[0],
                                           ssm_d[0])
    shared = dict(
        ln_emb_g=row(ln_emb_g), ln_emb_b=row(ln_emb_b),
        w_in=jnp.concatenate([w_q, w_in[0][:, Q_END:]], axis=1).astype(BF16),
        qg=row(jnp.tile(q_norm_g[0], N_Q_HEADS)), kg=row(jnp.tile(k_norm_g[0], N_KV_HEADS)),
        bones=(head_of[:, None] == head_of[None, :]).astype(BF16),
        score_bound=(1.02 * HEAD_DIM * Q_PRESCALE * jnp.max(jnp.abs(q_norm_g[0]))
                     * jnp.max(jnp.abs(k_norm_g[0]))).astype(F32),
        ssm_w1=ssm_w1, ssm_w2=ssm_w2, ssm_dec=ssm_dec,
        w_glu=w_glu[0].astype(BF16), b_glu=row(b_glu[0]),
        w_out_attn=w_oa.astype(BF16), w_out_ssm=w_out[0][ATTN_WIDTH:].astype(BF16),
        ln1_g=row(ln1_g[0]), ln1_b=row(ln1_b[0]),
        w_router=jnp.pad(w_router[0], ((0, 0), (0, LANES - N_EXPERTS))).astype(BF16),
        w_gate=w_gate[0].astype(BF16), w_up=w_up[0].astype(BF16), w_down=w_down[0].astype(BF16),
        ln2_g=row(ln2_g[0]), ln2_b=row(ln2_b[0]),
    )
    meta_pad = jnp.pad(meta_tokens.astype(F32), ((PAD_ROWS, 0), (0, 0)))[None]
    return (_run_trunk(x_prompt, meta_pad, shared), _run_trunk(x_sample, meta_pad, shared))
```

```python
import functools
import math

import jax
import jax.numpy as jnp
from jax import lax
from jax.experimental import pallas as pl
from jax.experimental.pallas import tpu as pltpu

F32 = jnp.float32
BF16 = jnp.bfloat16

D_MODEL = 1024
N_META = 16
GRID_W = 64
ATTN_WIDTH = 512
SSM_WIDTH = 512
HEAD_DIM = 64
N_Q_HEADS = 8
N_KV_HEADS = 2
Q_PER_KV = 4
KV_WIDTH = 128
ROPE_AXIS_DIM = 32
ROPE_THETA = 10000.0
ATTN_SCALE = HEAD_DIM ** -0.5
RMS_EPS = 1e-6
SSM_GROUP = 16
N_SSM_GROUPS = 32
SSM_STATE = 64
Q_END = ATTN_WIDTH
K_END = Q_END + KV_WIDTH
V_END = K_END + KV_WIDTH
IN_WIDTH = V_END + SSM_WIDTH
N_EXPERTS = 16
EXPERT_FF = 2048
EC_CAPACITY_FACTOR = 2
LN_EPS = 1e-5
DEPTH = 1
DEEPNORM_ALPHA = (2 * DEPTH) ** 0.25

LANES = 128
SUBLANES = 8
ROW_SLABS = D_MODEL // LANES
TAIL = LANES
PAD_ROWS = TAIL - N_META
CHUNK = 16
N_PAIRS = N_SSM_GROUPS // 2
PAIR_W = 2 * CHUNK * SSM_GROUP
PAIR_STATE = 2 * SSM_STATE
EMBED_TILE = 512
ROW_TILE = 384
Q_TILE = ROW_TILE
KV_TILE = EMBED_TILE
KEY_SUBTILE = 256
KV_UNROLL = 4
SLOT_TILE = 384
MOE_TILE_MAX = 528
FF_TILE = 512
SSM_ROW_TILE = 256
DEC_ROWS = 48
VT_ROWS = HEAD_DIM + 16
Q_PRESCALE = ATTN_SCALE * math.log2(math.e)
NEG_BIG = -1e30
SAFE_EXP2_RANGE = 60.0
VMEM_LIMIT = 56 * 1024 * 1024


def _cparams(sem):
    return pltpu.CompilerParams(dimension_semantics=sem, vmem_limit_bytes=VMEM_LIMIT)


def _const_spec(*shape):
    return pl.BlockSpec(shape, lambda *idx: (0,) * len(shape))


def _layer_norm_rows(x, g, b):
    mu = jnp.mean(x, axis=-1, keepdims=True)
    xc = x - mu
    var = jnp.mean(xc * xc, axis=-1, keepdims=True)
    return xc * lax.rsqrt(var + LN_EPS) * g + b


def _head_rms(t, gain, bones):
    sq = t * t
    hi = sq.astype(BF16)
    lo = (sq - hi.astype(F32)).astype(BF16)
    ss = (jnp.dot(hi, bones, preferred_element_type=F32)
          + jnp.dot(lo, bones, preferred_element_type=F32))
    return t * lax.rsqrt(ss * (1.0 / HEAD_DIM) + RMS_EPS) * gain


def _rope_slab(t, cos, sin_signed):
    lane = lax.broadcasted_iota(jnp.int32, t.shape, 1)
    first = (lane % ROPE_AXIS_DIM) < (ROPE_AXIS_DIM // 2)
    partner = jnp.where(first, pltpu.roll(t, LANES - 16, 1), pltpu.roll(t, 16, 1))
    return t * cos + partner * sin_signed


def _lane_piece_gather(pieces, lane16):
    out = None
    for j, (arr, src) in enumerate(pieces):
        shift = (SSM_GROUP * j - src) % LANES
        moved = pltpu.roll(arr, shift, 1) if shift else arr
        out = moved if out is None else jnp.where(lane16 == j, moved, out)
    return out


def _embed_body(x_ref, g_ref, b_ref, w_ref, qg_ref, kg_ref, bones_ref, cos_ref, sin_ref,
                h_ref, q_ref, k_ref, vt0_ref, vt1_ref, u_ref, u_scr):
    h = _layer_norm_rows(x_ref[0], g_ref[...], b_ref[...])
    h_ref[0] = h
    proj = jnp.dot(h.astype(BF16), w_ref[...], preferred_element_type=F32)
    cos = cos_ref[...]
    sin = sin_ref[...]
    qn = _head_rms(proj[:, :Q_END], qg_ref[...], bones_ref[...])
    for s in range(ATTN_WIDTH // LANES):
        sl = slice(s * LANES, (s + 1) * LANES)
        q_ref[0, :, sl] = (_rope_slab(qn[:, sl], cos, sin) * Q_PRESCALE).astype(BF16)
    kn = _head_rms(proj[:, Q_END:K_END], kg_ref[...], bones_ref[:KV_WIDTH, :KV_WIDTH])
    k_ref[0] = _rope_slab(kn, cos, sin).astype(BF16)
    vt = proj[:, K_END:V_END].T
    ones = jnp.ones((VT_ROWS - HEAD_DIM, vt.shape[1]), F32)
    vt0_ref[0, 0] = jnp.concatenate([vt[:HEAD_DIM], ones], axis=0).astype(BF16)
    vt1_ref[0, 0] = jnp.concatenate([vt[HEAD_DIM:], ones], axis=0).astype(BF16)
    n_rows = u_scr.shape[1] // CHUNK
    for v in range(SSM_WIDTH // LANES):
        u_scr[v] = proj[:, V_END + v * LANES:V_END + (v + 1) * LANES]
    by_token = [[u_scr[v, pl.ds(t, n_rows, stride=CHUNK), :] for v in range(SSM_WIDTH // LANES)]
                for t in range(CHUNK)]
    lane16 = lax.broadcasted_iota(jnp.int32, (n_rows, LANES), 1) // SSM_GROUP
    for p in range(N_PAIRS):
        for gi in range(2):
            src = (p % 4) * 2 * SSM_GROUP + gi * SSM_GROUP
            for h in range(CHUNK // SUBLANES):
                pieces = [(by_token[SUBLANES * h + j][p // 4], src) for j in range(SUBLANES)]
                lo = gi * CHUNK * SSM_GROUP + h * LANES
                u_ref[p, 0, :, lo:lo + LANES] = _lane_piece_gather(pieces, lane16)


def _embed_tail_body(x_ref, g_ref, b_ref, w_ref, qg_ref, kg_ref, bones_ref, cos_ref, sin_ref,
                     h_in, q_in, k_in, u_in, h_ref, q_ref, k_ref, vt0_ref, vt1_ref, u_ref, u_scr):
    del h_in, q_in, k_in, u_in
    _embed_body(x_ref, g_ref, b_ref, w_ref, qg_ref, kg_ref, bones_ref, cos_ref, sin_ref,
                h_ref, q_ref, k_ref, vt0_ref, vt1_ref, u_ref, u_scr)


def _embed(x, meta_pad, ln_g, ln_b, w_in, qg, kg, bones, cos, sin, L, Lp):
    B = x.shape[0]
    TM = EMBED_TILE
    nj = L // TM
    w_specs = [_const_spec(1, D_MODEL), _const_spec(1, D_MODEL), _const_spec(D_MODEL, IN_WIDTH),
               _const_spec(1, Q_END), _const_spec(1, KV_WIDTH), _const_spec(Q_END, Q_END)]

    def out_shapes(vt_tiles, vt_width):
        vt = jax.ShapeDtypeStruct((B, vt_tiles, VT_ROWS, vt_width), BF16)
        return [
            jax.ShapeDtypeStruct((B, Lp, D_MODEL), F32),
            jax.ShapeDtypeStruct((B, Lp, Q_END), BF16),
            jax.ShapeDtypeStruct((B, Lp, KV_WIDTH), BF16),
            vt, vt,
            jax.ShapeDtypeStruct((N_PAIRS, B, Lp // CHUNK, PAIR_W), F32),
        ]

    def out_specs(tm, row_block):
        return [
            pl.BlockSpec((1, tm, D_MODEL), lambda b, j: (b, row_block(j), 0)),
            pl.BlockSpec((1, tm, Q_END), lambda b, j: (b, row_block(j), 0)),
            pl.BlockSpec((1, tm, KV_WIDTH), lambda b, j: (b, row_block(j), 0)),
            pl.BlockSpec((1, 1, VT_ROWS, tm), lambda b, j: (b, j, 0, 0)),
            pl.BlockSpec((1, 1, VT_ROWS, tm), lambda b, j: (b, j, 0, 0)),
            pl.BlockSpec((N_PAIRS, 1, tm // CHUNK, PAIR_W), lambda b, j: (0, b, row_block(j), 0)),
        ]

    main = pl.pallas_call(
        _embed_body,
        grid=(B, nj),
        in_specs=[pl.BlockSpec((1, TM, D_MODEL), lambda b, j: (b, j, 0))] + w_specs
        + [pl.BlockSpec((TM, LANES), lambda b, j: (j, 0))] * 2,
        out_specs=out_specs(TM, lambda j: j),
        out_shape=out_shapes(nj, TM),
        scratch_shapes=[pltpu.VMEM((SSM_WIDTH // LANES, TM, LANES), F32)],
        compiler_params=_cparams(("parallel", "parallel")),
        name="embed_main",
    )
    h0, q, k, vt0, vt1, u = main(x, ln_g, ln_b, w_in, qg, kg, bones, cos, sin)

    jt = Lp // TAIL - 1
    tail = pl.pallas_call(
        _embed_tail_body,
        grid=(B, 1),
        in_specs=[pl.BlockSpec((1, TAIL, D_MODEL), lambda b, j: (0, 0, 0))] + w_specs
        + [pl.BlockSpec((TAIL, LANES), lambda b, j: (jt, 0))] * 2
        + [pl.BlockSpec(memory_space=pl.ANY)] * 4,
        out_specs=out_specs(TAIL, lambda j: jt),
        out_shape=out_shapes(1, TAIL),
        input_output_aliases={9: 0, 10: 1, 11: 2, 12: 5},
        scratch_shapes=[pltpu.VMEM((SSM_WIDTH // LANES, TAIL, LANES), F32)],
        compiler_params=_cparams(("parallel", "arbitrary")),
        name="embed_tail",
    )
    h0, q, k, vt0_tail, vt1_tail, u = tail(meta_pad, ln_g, ln_b, w_in, qg, kg, bones, cos, sin,
                                           h0, q, k, u)
    return h0, q, k, (vt0, vt1, vt0_tail, vt1_tail), u


def _attn_body(q_ref, k_ref, vt0_ref, vt1_ref, vt0t_ref, vt1t_ref, o_ref,
               qt_ref, s0_ref, s1_ref, mc0_ref, mc1_ref, m_ref, acc_ref, *, L, n_kv_tiles):
    cols = Q_PER_KV * Q_TILE
    _attn_load_qt(q_ref, qt_ref)
    m_ref[...] = jnp.full(m_ref.shape, NEG_BIG, F32)
    acc_ref[...] = jnp.zeros(acc_ref.shape, F32)

    s_bufs = ((s0_ref, mc0_ref), (s1_ref, mc1_ref))

    def scores(slot, k_tile, first_valid=0):
        s_ref, mc_ref = s_bufs[slot]
        n = k_tile.shape[0]
        for g in range(N_KV_HEADS):
            s = jnp.dot(k_tile, qt_ref[g], preferred_element_type=F32)
            if first_valid:
                key = lax.broadcasted_iota(jnp.int32, s.shape, 0)
                s = jnp.where(key >= first_valid, s, NEG_BIG)
            s_ref[g, 0:n, :] = s
            mc_ref[g] = jnp.broadcast_to(jnp.max(s, axis=0, keepdims=True), (SUBLANES, cols))

    def softmax_values(slot, vt_tiles, n):
        s_ref, mc_ref = s_bufs[slot]
        for g in range(N_KV_HEADS):
            m_old = m_ref[g]
            m_new = jnp.maximum(m_old, mc_ref[g])
            alpha = jnp.exp2(m_old[0:1, :] - m_new[0:1, :])
            acc = alpha * acc_ref[g]
            for k0 in range(0, n, KEY_SUBTILE):
                k1 = min(k0 + KEY_SUBTILE, n)
                p = jnp.exp2(s_ref[g, k0:k1, :] - m_new[0:1, :]).astype(BF16)
                acc = acc + jnp.dot(vt_tiles[g][:, k0:k1], p, preferred_element_type=F32)
            acc_ref[g] = acc
            m_ref[g] = m_new

    def k_tile(i):
        return k_ref[0, pl.ds(pl.multiple_of(i * KV_TILE, KV_TILE), KV_TILE), :]

    def vt_tiles(i):
        return vt0_ref[0, i], vt1_ref[0, i]

    scores(0, k_tile(0))

    def run_tiles(first, count, next_scores):
        for t in range(count):
            softmax_values(t % 2, vt_tiles(first + t), KV_TILE)
            if t + 1 < count:
                scores((t + 1) % 2, k_tile(first + t + 1))
            else:
                next_scores()

    def kv_group(i, carry):
        first = i * KV_UNROLL
        run_tiles(first, KV_UNROLL, lambda: scores(0, k_tile(first + KV_UNROLL)))
        return carry

    n_groups = n_kv_tiles // KV_UNROLL
    lax.fori_loop(0, n_groups - 1, kv_group, 0)
    run_tiles((n_groups - 1) * KV_UNROLL, KV_UNROLL,
              lambda: scores(0, k_ref[0, L:L + TAIL, :], PAD_ROWS))
    softmax_values(0, (vt0t_ref[0, 0], vt1t_ref[0, 0]), TAIL)

    _attn_store_out(acc_ref, o_ref)


def _attn_bounded_body(q_ref, k_ref, vt0_ref, vt1_ref, vt0t_ref, vt1t_ref, o_ref,
                       qt_ref, p0_ref, p1_ref, acc_ref, *, L, n_kv_tiles):
    _attn_load_qt(q_ref, qt_ref)
    acc_ref[...] = jnp.zeros(acc_ref.shape, F32)
    p_bufs = (p0_ref, p1_ref)

    def key_blocks(k_tile, vt_tiles, accs, first_valid=0):
        n = k_tile.shape[0]
        accs = list(accs)
        for i, k0 in enumerate(range(0, n, KEY_SUBTILE)):
            k1 = min(k0 + KEY_SUBTILE, n)
            p_ref = p_bufs[i % 2]
            for g in range(N_KV_HEADS):
                s = jnp.dot(k_tile[k0:k1], qt_ref[g], preferred_element_type=F32)
                if first_valid:
                    key = k0 + lax.broadcasted_iota(jnp.int32, s.shape, 0)
                    s = jnp.where(key >= first_valid, s, NEG_BIG)
                p_ref[g, 0:k1 - k0, :] = jnp.exp2(s).astype(BF16)
            for g in range(N_KV_HEADS):
                accs[g] = accs[g] + jnp.dot(vt_tiles[g][:, k0:k1], p_ref[g, 0:k1 - k0, :],
                                            preferred_element_type=F32)
        return accs

    def kv_group(i, carry):
        accs = [acc_ref[g] for g in range(N_KV_HEADS)]
        for t in range(KV_UNROLL):
            tile = i * KV_UNROLL + t
            start = pl.multiple_of(tile * KV_TILE, KV_TILE)
            accs = key_blocks(k_ref[0, pl.ds(start, KV_TILE), :], (vt0_ref[0, tile], vt1_ref[0, tile]),
                              accs)
        for g in range(N_KV_HEADS):
            acc_ref[g] = accs[g]
        return carry

    lax.fori_loop(0, n_kv_tiles // KV_UNROLL, kv_group, 0)
    accs = key_blocks(k_ref[0, L:L + TAIL, :], (vt0t_ref[0, 0], vt1t_ref[0, 0]),
                      [acc_ref[g] for g in range(N_KV_HEADS)], PAD_ROWS)
    for g in range(N_KV_HEADS):
        acc_ref[g] = accs[g]
    _attn_store_out(acc_ref, o_ref)


def _attn_load_qt(q_ref, qt_ref):
    tq = Q_TILE
    row = lax.broadcasted_iota(jnp.int32, (KV_WIDTH, tq), 0)
    for r in range(Q_PER_KV):
        slab_t = q_ref[0, :, r * LANES:(r + 1) * LANES].astype(F32).T
        for g in range(N_KV_HEADS):
            in_group = (row >= g * HEAD_DIM) & (row < (g + 1) * HEAD_DIM)
            qt_ref[g, :, r * tq:(r + 1) * tq] = jnp.where(in_group, slab_t, 0.0).astype(BF16)


def _attn_store_out(acc_ref, o_ref):
    tq = Q_TILE
    comb = jnp.concatenate(
        [acc_ref[g, 0:HEAD_DIM, :] / acc_ref[g, HEAD_DIM:HEAD_DIM + 1, :] for g in range(N_KV_HEADS)],
        axis=0)
    for r in range(Q_PER_KV):
        o_ref[0, :, r * LANES:(r + 1) * LANES] = comb[:, r * tq:(r + 1) * tq].T.astype(BF16)


def _attention(q, k, vts, L, Lp, score_bound):
    B = q.shape[0]
    nkt = L // KV_TILE
    assert nkt % KV_UNROLL == 0
    cols = Q_PER_KV * Q_TILE
    vt_spec = pl.BlockSpec((1, nkt, VT_ROWS, KV_TILE), lambda b, j: (b, 0, 0, 0))
    vt_tail_spec = pl.BlockSpec((1, 1, VT_ROWS, TAIL), lambda b, j: (b, 0, 0, 0))
    call = functools.partial(
        pl.pallas_call,
        grid=(B, Lp // Q_TILE),
        in_specs=[
            pl.BlockSpec((1, Q_TILE, Q_END), lambda b, j: (b, j, 0)),
            pl.BlockSpec((1, Lp, KV_WIDTH), lambda b, j: (b, 0, 0)),
            vt_spec, vt_spec, vt_tail_spec, vt_tail_spec,
        ],
        out_specs=pl.BlockSpec((1, Q_TILE, Q_END), lambda b, j: (b, j, 0)),
        out_shape=jax.ShapeDtypeStruct((B, Lp, Q_END), BF16),
        compiler_params=_cparams(("parallel", "arbitrary")),
    )
    bounded = call(
        functools.partial(_attn_bounded_body, L=L, n_kv_tiles=nkt),
        scratch_shapes=[
            pltpu.VMEM((N_KV_HEADS, KV_WIDTH, cols), BF16),
            pltpu.VMEM((N_KV_HEADS, KEY_SUBTILE, cols), BF16),
            pltpu.VMEM((N_KV_HEADS, KEY_SUBTILE, cols), BF16),
            pltpu.VMEM((N_KV_HEADS, VT_ROWS, cols), F32),
        ],
        name="attention_bounded",
    )
    general = call(
        functools.partial(_attn_body, L=L, n_kv_tiles=nkt),
        scratch_shapes=[
            pltpu.VMEM((N_KV_HEADS, KV_WIDTH, cols), BF16),
            pltpu.VMEM((N_KV_HEADS, KV_TILE, cols), F32),
            pltpu.VMEM((N_KV_HEADS, KV_TILE, cols), F32),
            pltpu.VMEM((N_KV_HEADS, SUBLANES, cols), F32),
            pltpu.VMEM((N_KV_HEADS, SUBLANES, cols), F32),
            pltpu.VMEM((N_KV_HEADS, SUBLANES, cols), F32),
            pltpu.VMEM((N_KV_HEADS, VT_ROWS, cols), F32),
        ],
        name="attention",
    )
    return lax.cond(score_bound <= SAFE_EXP2_RANGE, bounded, general, q, k, *vts)


def _ssm_weights(lam_re, lam_im, log_dt, b_re, b_im, c_re, c_im, d_skip):
    hp = lax.Precision.HIGHEST
    dt = jnp.exp(log_dt.astype(F32))[..., None]
    lr = lam_re.astype(F32)
    li = lam_im.astype(F32)

    def apow(n):
        n = jnp.asarray(n, F32)
        mag = jnp.exp(lr[..., None] * dt[..., None] * n)
        ang = li[..., None] * dt[..., None] * n
        return mag * jnp.cos(ang), mag * jnp.sin(ang)

    a1r, a1i = apow(jnp.ones((1,), F32))
    a1r, a1i = a1r[..., 0], a1i[..., 0]
    nr = a1r - 1.0
    den = lr * lr + li * li
    f_r = (nr * lr + a1i * li) / den
    f_i = (a1i * lr - nr * li) / den
    br = b_re.astype(F32)
    bi = b_im.astype(F32)
    bb_r = f_r[..., None] * br - f_i[..., None] * bi
    bb_i = f_r[..., None] * bi + f_i[..., None] * br
    cr = c_re.astype(F32)
    ci = c_im.astype(F32)

    tau = jnp.arange(CHUNK + 1, dtype=F32)
    pr, pi = apow(tau)
    ab_r = pr[..., None] * bb_r[:, :, :, None, :] - pi[..., None] * bb_i[:, :, :, None, :]
    ab_i = pr[..., None] * bb_i[:, :, :, None, :] + pi[..., None] * bb_r[:, :, :, None, :]
    kern = (jnp.einsum("dgop,dgptc->dgtco", cr, ab_r, precision=hp)
            - jnp.einsum("dgop,dgptc->dgtco", ci, ab_i, precision=hp))
    s_idx = jnp.arange(CHUNK)[:, None]
    t_idx = jnp.arange(CHUNK)[None, :]
    lag_f = jnp.clip(t_idx - s_idx, 0, CHUNK)
    lag_b = jnp.clip(s_idx - t_idx, 0, CHUNK)
    m_f = jnp.where((t_idx >= s_idx)[..., None, None], kern[0][:, lag_f], 0.0)
    m_b = jnp.where((s_idx >= t_idx)[..., None, None], kern[1][:, lag_b], 0.0)
    eye_t = jnp.eye(CHUNK, dtype=F32)[None, :, :, None, None]
    eye_c = jnp.eye(SSM_GROUP, dtype=F32)[None, None, None]
    dsk = d_skip.astype(F32).reshape(N_SSM_GROUPS, 1, 1, SSM_GROUP, 1)
    m_all = m_f + m_b + eye_t * eye_c * dsk
    m_all = m_all.transpose(0, 1, 3, 2, 4).reshape(N_SSM_GROUPS, 256, 256)

    def state_in(d, expo):
        er, ei = pr[d][..., expo], pi[d][..., expo]
        wr = er[..., None] * bb_r[d][:, :, None, :] - ei[..., None] * bb_i[d][:, :, None, :]
        wi = er[..., None] * bb_i[d][:, :, None, :] + ei[..., None] * bb_r[d][:, :, None, :]
        to_rows = lambda w: w.transpose(0, 2, 3, 1).reshape(N_SSM_GROUPS, 256, SSM_STATE)
        return to_rows(wr), to_rows(wi)

    sf_r, sf_i = state_in(0, jnp.arange(CHUNK - 1, -1, -1))
    sb_r, sb_i = state_in(1, jnp.arange(CHUNK))

    def state_out(d, expo):
        er, ei = pr[d][..., expo], pi[d][..., expo]
        wr = cr[d].transpose(0, 2, 1)[:, :, None, :] * er[..., None] \
            - ci[d].transpose(0, 2, 1)[:, :, None, :] * ei[..., None]
        wi = cr[d].transpose(0, 2, 1)[:, :, None, :] * ei[..., None] \
            + ci[d].transpose(0, 2, 1)[:, :, None, :] * er[..., None]
        flat = lambda w: w.reshape(N_SSM_GROUPS, SSM_STATE, 256)
        return flat(wr), flat(-wi)

    of_r, of_i = state_out(0, jnp.arange(1, CHUNK + 1))
    ob_r, ob_i = state_out(1, jnp.arange(CHUNK, 0, -1))

    def pair_rows(w):
        return w.reshape(N_PAIRS, 2, *w.shape[1:])

    z256 = jnp.zeros((N_PAIRS, 256, 256), F32)
    mp = pair_rows(m_all)
    w1_y = jnp.concatenate([jnp.concatenate([mp[:, 0], z256], axis=2),
                            jnp.concatenate([z256, mp[:, 1]], axis=2)], axis=1)

    def pair_cols_in(w):
        wp = pair_rows(w)
        z = jnp.zeros_like(wp[:, 0])
        return jnp.concatenate([jnp.concatenate([wp[:, 0], z], axis=2),
                                jnp.concatenate([z, wp[:, 1]], axis=2)], axis=1)

    w1 = jnp.concatenate([w1_y, pair_cols_in(sf_r), pair_cols_in(sf_i),
                          pair_cols_in(sb_r), pair_cols_in(sb_i)], axis=2)

    def pair_rows_out(w):
        wp = pair_rows(w)
        z = jnp.zeros_like(wp[:, 0])
        return jnp.concatenate([jnp.concatenate([wp[:, 0], z], axis=2),
                                jnp.concatenate([z, wp[:, 1]], axis=2)], axis=1)

    w2 = jnp.concatenate([pair_rows_out(of_r), pair_rows_out(of_i),
                          pair_rows_out(ob_r), pair_rows_out(ob_i)], axis=1)

    qr, qi = apow(CHUNK * jnp.arange(SUBLANES + 1, dtype=F32))
    zero = jnp.zeros_like(qr[..., 0])

    def dec_rows(d, q_expo):
        rows = [qr[d][..., 1], qi[d][..., 1], qr[d][..., 2], qi[d][..., 2],
                qr[d][..., 4], qi[d][..., 4], zero[d], zero[d]]
        rows += [qr[d][..., n] for n in q_expo] + [qi[d][..., n] for n in q_expo]
        return jnp.stack(rows, axis=1)

    dec = jnp.concatenate([dec_rows(0, range(1, SUBLANES + 1)),
                           dec_rows(1, range(SUBLANES, 0, -1))], axis=1)
    dec = dec.reshape(N_PAIRS, 2, DEC_ROWS, SSM_STATE).transpose(0, 2, 1, 3)
    dec = dec.reshape(N_PAIRS, DEC_ROWS, PAIR_STATE)
    return w1.astype(BF16), w2.astype(BF16), dec


def _ssm_body(u_ref, w1_ref, w2_ref, dec_ref, z_ref, sh_ref, *, B, n_chunks):
    R = B * n_chunks
    tiles = [(r0, min(SSM_ROW_TILE, R - r0)) for r0 in range(0, R, SSM_ROW_TILE)]
    PS = PAIR_STATE
    n_blocks = n_chunks // SUBLANES

    for r0, n in tiles:
        t = jnp.dot(u_ref[0, r0:r0 + n, :].astype(BF16), w1_ref[0], preferred_element_type=F32)
        z_ref[0, r0:r0 + n, :] = t[:, :PAIR_W]
        sh_ref[r0:r0 + n, :] = t[:, PAIR_W:]

    sub = lax.broadcasted_iota(jnp.int32, (SUBLANES, PS), 0)

    def cmul(ar, ai, xr, xi):
        return ar * xr - ai * xi, ar * xi + ai * xr

    def block_scan(sr, si, cr, ci, base, reverse):
        xr, xi = sr, si
        for k, d in enumerate((1, 2, 4)):
            pr_ = dec_ref[0, base + 2 * k:base + 2 * k + 1, :]
            pi_ = dec_ref[0, base + 2 * k + 1:base + 2 * k + 2, :]
            keep = (sub < SUBLANES - d) if reverse else (sub >= d)
            shift = SUBLANES - d if reverse else d
            yr = jnp.where(keep, pltpu.roll(xr, shift, 0), 0.0)
            yi = jnp.where(keep, pltpu.roll(xi, shift, 0), 0.0)
            mr, mi = cmul(pr_, pi_, yr, yi)
            xr, xi = xr + mr, xi + mi
        qr_ = dec_ref[0, base + 8:base + 16, :]
        qi_ = dec_ref[0, base + 16:base + 24, :]
        mr, mi = cmul(qr_, qi_, cr, ci)
        xr, xi = xr + mr, xi + mi
        edge = SUBLANES - 1 if reverse else 0
        shift = SUBLANES - 1 if reverse else 1
        er = jnp.where(sub == edge, cr, pltpu.roll(xr, shift, 0))
        ei = jnp.where(sub == edge, ci, pltpu.roll(xi, shift, 0))
        last = 0 if reverse else SUBLANES - 1
        nr = jnp.broadcast_to(xr[last:last + 1, :], (SUBLANES, PS))
        ni = jnp.broadcast_to(xi[last:last + 1, :], (SUBLANES, PS))
        return er, ei, nr, ni

    def scan_step(i, carry):
        new = []
        for b in range(B):
            cfr, cfi, cbr, cbi = carry[4 * b:4 * b + 4]
            tile_f = (i + n_blocks - 1) % n_blocks
            tile_b = (2 * n_blocks - 2 - i) % n_blocks
            rf = pl.multiple_of(b * n_chunks + tile_f * SUBLANES, SUBLANES)
            rb = pl.multiple_of(b * n_chunks + tile_b * SUBLANES, SUBLANES)
            f_rows, b_rows = pl.ds(rf, SUBLANES), pl.ds(rb, SUBLANES)
            er, ei, cfr, cfi = block_scan(sh_ref[f_rows, 0:PS], sh_ref[f_rows, PS:2 * PS],
                                          cfr, cfi, 0, False)
            sh_ref[f_rows, 0:PS] = er
            sh_ref[f_rows, PS:2 * PS] = ei
            er, ei, cbr, cbi = block_scan(sh_ref[b_rows, 2 * PS:3 * PS], sh_ref[b_rows, 3 * PS:4 * PS],
                                          cbr, cbi, DEC_ROWS // 2, True)
            sh_ref[b_rows, 2 * PS:3 * PS] = er
            sh_ref[b_rows, 3 * PS:4 * PS] = ei
            new += [cfr, cfi, cbr, cbi]
        return tuple(new)

    zero = jnp.zeros((SUBLANES, PS), F32)
    lax.fori_loop(0, n_blocks, scan_step, (zero,) * (4 * B))

    for r0, n in tiles:
        y = z_ref[0, r0:r0 + n, :] + jnp.dot(sh_ref[r0:r0 + n, :].astype(BF16), w2_ref[0],
                                             preferred_element_type=F32)
        z_ref[0, r0:r0 + n, :] = jax.nn.gelu(y)


def _ssm(u_pairs, w1, w2, dec, B, n_chunks):
    assert n_chunks % SUBLANES == 0
    R = B * n_chunks
    return pl.pallas_call(
        functools.partial(_ssm_body, B=B, n_chunks=n_chunks),
        grid=(N_PAIRS,),
        in_specs=[
            pl.BlockSpec((1, R, PAIR_W), lambda p: (p, 0, 0)),
            pl.BlockSpec((1, PAIR_W, 2 * PAIR_W), lambda p: (p, 0, 0)),
            pl.BlockSpec((1, PAIR_W, PAIR_W), lambda p: (p, 0, 0)),
            pl.BlockSpec((1, DEC_ROWS, PAIR_STATE), lambda p: (p, 0, 0)),
        ],
        out_specs=pl.BlockSpec((1, R, PAIR_W), lambda p: (p, 0, 0)),
        out_shape=jax.ShapeDtypeStruct((N_PAIRS, R, PAIR_W), F32),
        scratch_shapes=[pltpu.VMEM((R, 4 * PAIR_STATE), F32)],
        compiler_params=_cparams(("parallel",)),
        name="ssm",
    )(u_pairs, w1, w2, dec)


def _mix_body(o_ref, zp_ref, h0_ref, wglu_ref, bglu_ref, woa_ref, wos_ref, g_ref, b_ref, wr_ref,
              h1_ref, acc_ref, aff_ref, z_scr, *, L, Lp):
    n_rows = zp_ref.shape[1]
    lane16 = lax.broadcasted_iota(jnp.int32, (n_rows, LANES), 1) // SSM_GROUP
    for t in range(CHUNK):
        src = (t % SUBLANES) * SSM_GROUP
        for v in range(SSM_WIDTH // LANES):
            pieces = []
            for j in range(SUBLANES):
                p, gi = 4 * v + j // 2, j % 2
                lo = gi * CHUNK * SSM_GROUP + (t // SUBLANES) * LANES
                pieces.append((zp_ref[p, :, lo:lo + LANES], src))
            z_scr[v, pl.ds(t, n_rows, stride=CHUNK), :] = _lane_piece_gather(pieces, lane16)
    z = jnp.concatenate([z_scr[v] for v in range(SSM_WIDTH // LANES)], axis=1)
    gate = jax.nn.sigmoid(jnp.dot(z.astype(BF16), wglu_ref[...], preferred_element_type=F32)
                          + bglu_ref[...])
    ssm_out = (z * gate).astype(BF16)
    mix = (jnp.dot(o_ref[...], woa_ref[...], preferred_element_type=F32)
           + jnp.dot(ssm_out, wos_ref[...], preferred_element_type=F32))
    h1 = _layer_norm_rows(DEEPNORM_ALPHA * h0_ref[...] + mix, g_ref[...], b_ref[...])
    for s in range(ROW_SLABS):
        acc_ref[:, s, :] = DEEPNORM_ALPHA * h1[:, s * LANES:(s + 1) * LANES]
    h1_bits = pltpu.bitcast(h1.astype(BF16).astype(F32), jnp.uint32)
    for s in range(ROW_SLABS // 2):
        lo = h1_bits[:, s * LANES:(s + 1) * LANES]
        hi = h1_bits[:, (s + ROW_SLABS // 2) * LANES:(s + ROW_SLABS // 2 + 1) * LANES]
        h1_ref[:, s, :] = hi | lax.shift_right_logical(lo, jnp.uint32(16))
    logits = jnp.dot(h1.astype(BF16), wr_ref[...], preferred_element_type=F32)
    lane = lax.broadcasted_iota(jnp.int32, logits.shape, 1)
    logits = jnp.where(lane < N_EXPERTS, logits, NEG_BIG)
    e = jnp.exp(logits - jnp.max(logits, axis=-1, keepdims=True))
    aff = (e / jnp.sum(e, axis=-1, keepdims=True)).T[:N_EXPERTS, :]
    tiles_per_seq = Lp // ROW_TILE
    pos = (pl.program_id(0) % tiles_per_seq) * ROW_TILE + lax.broadcasted_iota(
        jnp.int32, aff.shape, 1)
    aff_ref[...] = jnp.where((pos < L) | (pos >= L + PAD_ROWS), aff, -1.0)


def _mix(o, z, h0, wglu, bglu, woa, wos, g, b, wr, L, Lp):
    N = o.shape[0]
    TM = ROW_TILE
    row = lambda w: pl.BlockSpec((TM, w), lambda i: (i, 0))
    tok_rows = pl.BlockSpec((TM, ROW_SLABS, LANES), lambda i: (i, 0, 0))
    return pl.pallas_call(
        functools.partial(_mix_body, L=L, Lp=Lp),
        grid=(N // TM,),
        in_specs=[row(ATTN_WIDTH), pl.BlockSpec((N_PAIRS, TM // CHUNK, PAIR_W), lambda i: (0, i, 0)),
                  row(D_MODEL),
                  _const_spec(SSM_WIDTH, SSM_WIDTH), _const_spec(1, SSM_WIDTH),
                  _const_spec(ATTN_WIDTH, D_MODEL), _const_spec(SSM_WIDTH, D_MODEL),
                  _const_spec(1, D_MODEL), _const_spec(1, D_MODEL), _const_spec(D_MODEL, LANES)],
        out_specs=[pl.BlockSpec((TM, ROW_SLABS // 2, LANES), lambda i: (i, 0, 0)), tok_rows,
                   pl.BlockSpec((N_EXPERTS, TM), lambda i: (0, i))],
        out_shape=[jax.ShapeDtypeStruct((N, ROW_SLABS // 2, LANES), jnp.uint32),
                   jax.ShapeDtypeStruct((N, ROW_SLABS, LANES), F32),
                   jax.ShapeDtypeStruct((N_EXPERTS, N), F32)],
        scratch_shapes=[pltpu.VMEM((SSM_WIDTH // LANES, TM, LANES), F32)],
        compiler_params=_cparams(("parallel",)),
        name="mix",
    )(o, z, h0, wglu, bglu, woa, wos, g, b, wr)


def _select_body(aff_ref, ord_ref, idx_ref, thr_ref, *, capacity, n_slot_tiles, seq_len, pad_start):
    nbp = aff_ref.shape[1]
    ri = lax.broadcasted_iota(jnp.int32, (LANES, LANES), 0)
    ci = lax.broadcasted_iota(jnp.int32, (LANES, LANES), 1)
    strict_upper = (ri < ci).astype(BF16)
    incl_upper = (ri <= ci).astype(BF16)
    ones = jnp.ones((LANES, LANES), BF16)
    order = ord_ref[...]
    blk = lax.broadcasted_iota(jnp.int32, (nbp, LANES), 0).astype(F32)
    blk_hi = jnp.floor(blk * (1.0 / 16.0))
    blk_lo = blk - 16.0 * blk_hi
    dot = functools.partial(jnp.dot, preferred_element_type=F32)
    dot_nt = functools.partial(lax.dot_general, dimension_numbers=(((1,), (1,)), ((), ())),
                               preferred_element_type=F32)

    def search_bit(i, ts):
        bit = jnp.left_shift(jnp.int32(1), 30 - i)
        new = []
        for e in range(N_EXPERTS):
            cand = ts[e] | bit
            bits = pltpu.bitcast(aff_ref[e], jnp.int32)
            cnt = jnp.sum((bits >= cand).astype(jnp.int32), keepdims=True)
            new.append(jnp.where(cnt >= capacity, cand, ts[e]))
        return tuple(new)

    thresholds = lax.fori_loop(0, 31, search_bit, (jnp.zeros((1, 1), jnp.int32),) * N_EXPERTS)
    for e in range(N_EXPERTS):
        thr_ref[e] = jnp.broadcast_to(thresholds[e], (SUBLANES, LANES))

    def per_expert(e, carry):
        bits = pltpu.bitcast(aff_ref[e], jnp.int32)
        t = thr_ref[e][0:1, 0:1]
        gt = bits > t
        eq = bits == t
        need = (capacity - jnp.sum(gt.astype(jnp.int32), keepdims=True)).astype(F32)
        eqb = eq.astype(BF16)
        tie_rank = dot(order, dot(eqb, ones).astype(BF16)) + dot(eqb, strict_upper)
        sel = (gt | (eq & (tie_rank < need))).astype(BF16)
        csum_in_blk = dot(sel, incl_upper)
        blk_tot = dot(sel, ones)
        blk_off = dot(order, blk_tot.astype(BF16))
        off_hi = jnp.floor(blk_off * (1.0 / 64.0))
        off_lo = blk_off - 64.0 * off_hi
        table = jnp.concatenate([csum_in_blk, off_hi, off_lo, blk_hi, blk_lo],
                                axis=1).astype(BF16)
        tot_l = dot_nt(ones[:SUBLANES], sel)
        off_l = dot_nt(tot_l.astype(BF16), order)[0:1, :]
        tot_l = tot_l[0:1, :]

        def per_tile(st, carry2):
            j = (st * SLOT_TILE + lax.broadcasted_iota(jnp.int32, (SLOT_TILE, 1), 0)).astype(F32)
            onehot = ((off_l <= j) & (j < off_l + tot_l)).astype(BF16)
            got = dot(onehot, table)
            local = j - (64.0 * got[:, LANES:2 * LANES] + got[:, 2 * LANES:3 * LANES])
            lane_idx = dot((got[:, :LANES] <= local).astype(BF16), ones)
            block = 16.0 * got[:, 3 * LANES:4 * LANES] + got[:, 4 * LANES:5 * LANES]
            k = j - capacity
            seq = sum((k >= PAD_ROWS * i).astype(F32) for i in range(1, SLOT_TILE // PAD_ROWS + 1))
            pad_tok = seq * (seq_len - PAD_ROWS) + pad_start + k
            tok = jnp.where(j < capacity, block * LANES + lane_idx, pad_tok)
            idx_ref[e, st] = tok.T[0:1, :].astype(jnp.int32)
            return carry2

        lax.fori_loop(0, n_slot_tiles, per_tile, 0)
        return carry

    lax.fori_loop(0, N_EXPERTS, per_expert, 0)


def _select(aff_blocks, order, capacity, n_slot_tiles, seq_len, pad_start):
    nbp = aff_blocks.shape[1]
    return pl.pallas_call(
        functools.partial(_select_body, capacity=capacity, n_slot_tiles=n_slot_tiles,
                          seq_len=seq_len, pad_start=pad_start),
        grid=(1,),
        in_specs=[_const_spec(N_EXPERTS, nbp, LANES), _const_spec(nbp, nbp)],
        out_specs=_const_spec(N_EXPERTS, n_slot_tiles, 1, SLOT_TILE),
        out_shape=jax.ShapeDtypeStruct((N_EXPERTS, n_slot_tiles, 1, SLOT_TILE), jnp.int32),
        scratch_shapes=[pltpu.VMEM((N_EXPERTS, SUBLANES, LANES), jnp.int32)],
        compiler_params=_cparams(("arbitrary",)),
        name="select",
    )(aff_blocks, order)


def _moe_body(idx_ref, idx_next_ref, h1_hbm, wg_ref, wu_ref, wd_ref, wr_ref, acc_in, acc_hbm,
              xbuf, abuf, sem_x, sem_a, sem_s):
    del acc_in
    T = xbuf.shape[1]
    n_c = pl.num_programs(1)
    step = pl.program_id(0) * n_c + pl.program_id(1)
    n_steps = pl.num_programs(0) * n_c
    slot = step % 2

    def start_rows(copy_of_row, idx):
        def body(j, carry):
            copy_of_row(idx[0, 0, 0, j], j).start()
            return carry
        lax.fori_loop(0, T, body, 0, unroll=8)

    def x_row(buf):
        return lambda row, j: pltpu.make_async_copy(h1_hbm.at[row], xbuf.at[buf, j], sem_x.at[buf])

    def acc_row_in(row, j):
        return pltpu.make_async_copy(acc_hbm.at[row], abuf.at[j], sem_a)

    def acc_row_out(row, j):
        return pltpu.make_async_copy(abuf.at[j], acc_hbm.at[row], sem_s)

    def wait_scatter():
        pltpu.make_async_copy(abuf, acc_hbm.at[pl.ds(0, T)], sem_s).wait()

    @pl.when(step == 0)
    def _():
        start_rows(x_row(0), idx_ref)

    @pl.when(step + 1 < n_steps)
    def _():
        start_rows(x_row(1 - slot), idx_next_ref)

    pltpu.make_async_copy(h1_hbm.at[pl.ds(0, T)], xbuf.at[slot], sem_x.at[slot]).wait()
    packed = [xbuf[slot, :, s, :] for s in range(ROW_SLABS // 2)]
    lo = [pltpu.bitcast(lax.shift_left(u, jnp.uint32(16)), F32) for u in packed]
    hi = [pltpu.bitcast(u & jnp.uint32(0xFFFF0000), F32) for u in packed]
    x = jnp.concatenate(lo + hi, axis=1).astype(BF16)
    logits = jnp.dot(x, wr_ref[...], preferred_element_type=F32)
    lane = lax.broadcasted_iota(jnp.int32, logits.shape, 1)
    logits = jnp.where(lane < N_EXPERTS, logits, NEG_BIG)
    p = jnp.exp(logits - jnp.max(logits, axis=-1, keepdims=True))
    gate = (jnp.sum(jnp.where(lane == pl.program_id(0), p, 0.0), axis=-1, keepdims=True)
            / jnp.sum(p, axis=-1, keepdims=True))
    y = jnp.zeros((T, D_MODEL), F32)
    for f in range(EXPERT_FF // FF_TILE):
        fs = slice(f * FF_TILE, (f + 1) * FF_TILE)
        hg = jnp.dot(x, wg_ref[0, :, fs], preferred_element_type=F32)
        hu = jnp.dot(x, wu_ref[0, :, fs], preferred_element_type=F32)
        hid = (jax.nn.silu(hg) * hu).astype(BF16)
        y = y + jnp.dot(hid, wd_ref[0, fs, :], preferred_element_type=F32)
        if f == 0:
            @pl.when(step > 0)
            def _():
                wait_scatter()

            start_rows(acc_row_in, idx_ref)
    pltpu.make_async_copy(acc_hbm.at[pl.ds(0, T)], abuf, sem_a).wait()
    y = y * gate
    for s in range(ROW_SLABS):
        abuf[:, s, :] = abuf[:, s, :] + y[:, s * LANES:(s + 1) * LANES]
    start_rows(acc_row_out, idx_ref)

    @pl.when(step == n_steps - 1)
    def _():
        wait_scatter()


def _moe(idx, h1, acc, wg, wu, wd, wr):
    n_slots = idx.shape[1] * idx.shape[3]
    n_c = next(n for n in range(1, n_slots) if n_slots % n == 0 and n_slots // n <= MOE_TILE_MAX
               and (n_slots // n) % 16 == 0)
    T = n_slots // n_c
    idx = idx.reshape(N_EXPERTS, n_c, 1, T)

    def next_block(e, c):
        return (jnp.minimum(e + (c + 1) // n_c, N_EXPERTS - 1), (c + 1) % n_c, 0, 0)

    return pl.pallas_call(
        _moe_body,
        grid=(N_EXPERTS, n_c),
        in_specs=[
            pl.BlockSpec((1, 1, 1, T), lambda e, c: (e, c, 0, 0), memory_space=pltpu.SMEM),
            pl.BlockSpec((1, 1, 1, T), next_block, memory_space=pltpu.SMEM),
            pl.BlockSpec(memory_space=pl.ANY),
            pl.BlockSpec((1, D_MODEL, EXPERT_FF), lambda e, c: (e, 0, 0)),
            pl.BlockSpec((1, D_MODEL, EXPERT_FF), lambda e, c: (e, 0, 0)),
            pl.BlockSpec((1, EXPERT_FF, D_MODEL), lambda e, c: (e, 0, 0)),
            _const_spec(D_MODEL, LANES),
            pl.BlockSpec(memory_space=pl.ANY),
        ],
        out_specs=pl.BlockSpec(memory_space=pl.ANY),
        out_shape=jax.ShapeDtypeStruct(acc.shape, F32),
        scratch_shapes=[pltpu.VMEM((2, T, ROW_SLABS // 2, LANES), jnp.uint32),
                        pltpu.VMEM((T, ROW_SLABS, LANES), F32),
                        pltpu.SemaphoreType.DMA((2,)), pltpu.SemaphoreType.DMA(()),
                        pltpu.SemaphoreType.DMA(())],
        input_output_aliases={7: 0},
        compiler_params=pltpu.CompilerParams(dimension_semantics=("arbitrary", "arbitrary"),
                                             vmem_limit_bytes=VMEM_LIMIT,
                                             disable_bounds_checks=True),
        name="expert_ffn",
    )(idx, idx, h1, wg, wu, wd, wr, acc)


def _final_body(a_ref, g_ref, b_ref, o_ref):
    x = jnp.concatenate([a_ref[0, :, s, :] for s in range(ROW_SLABS)], axis=1)
    o_ref[0] = _layer_norm_rows(x, g_ref[...], b_ref[...])


def _final_norm(acc, g, b, L):
    B = acc.shape[0]
    TM = EMBED_TILE
    return pl.pallas_call(
        _final_body,
        grid=(B, L // TM),
        in_specs=[pl.BlockSpec((1, TM, ROW_SLABS, LANES), lambda b_, j: (b_, j, 0, 0)),
                  _const_spec(1, D_MODEL), _const_spec(1, D_MODEL)],
        out_specs=pl.BlockSpec((1, TM, D_MODEL), lambda b_, j: (b_, j, 0)),
        out_shape=jax.ShapeDtypeStruct((B, L, D_MODEL), F32),
        compiler_params=_cparams(("parallel", "parallel")),
        name="final_norm",
    )(acc, g, b)


def _rope_tables(L, Lp):
    rows = L // GRID_W
    t = jnp.arange(L, dtype=jnp.int32)
    m = jnp.arange(N_META, dtype=jnp.int32)
    pad = jnp.zeros((Lp - L - N_META,), jnp.int32)
    row = jnp.concatenate([t // GRID_W - rows // 2, pad,
                           jnp.full((N_META,), -(rows // 2) - 1, jnp.int32)])
    col = jnp.concatenate([t % GRID_W - GRID_W // 2, pad, m - GRID_W // 2])
    inv_freq = ROPE_THETA ** (-jnp.arange(0, ROPE_AXIS_DIM, 2, dtype=F32) / ROPE_AXIS_DIM)
    ang_r = row.astype(F32)[:, None] * inv_freq
    ang_c = col.astype(F32)[:, None] * inv_freq
    cos = jnp.concatenate([jnp.cos(ang_r)] * 2 + [jnp.cos(ang_c)] * 2, axis=1)
    sin = jnp.concatenate([-jnp.sin(ang_r), jnp.sin(ang_r), -jnp.sin(ang_c), jnp.sin(ang_c)], axis=1)
    return jnp.tile(cos, (1, 2)), jnp.tile(sin, (1, 2))


def _block_order(B, Lp, nbp):
    nbb = Lp // LANES
    r = jnp.arange(nbp)
    b, jb = r // nbb, r % nbb
    rank = jnp.where(r < B * nbb, b * nbb + jnp.where(jb == nbb - 1, 0, jb + 1), r)
    return (rank[None, :] < rank[:, None]).astype(BF16)


def _run_trunk(x, meta_pad, shared):
    B, L, _ = x.shape
    Lp = L + TAIL
    N = B * Lp
    cos, sin = _rope_tables(L, Lp)
    h0, q, k, vts, u = _embed(x, meta_pad, shared["ln_emb_g"], shared["ln_emb_b"], shared["w_in"],
                              shared["qg"], shared["kg"], shared["bones"], cos, sin, L, Lp)
    o = _attention(q, k, vts, L, Lp, shared["score_bound"])
    z = _ssm(u.reshape(N_PAIRS, N // CHUNK, PAIR_W), shared["ssm_w1"], shared["ssm_w2"],
             shared["ssm_dec"], B, Lp // CHUNK)
    h1, acc, aff = _mix(o.reshape(N, ATTN_WIDTH), z, h0.reshape(N, D_MODEL),
                        shared["w_glu"], shared["b_glu"], shared["w_out_attn"], shared["w_out_ssm"],
                        shared["ln1_g"], shared["ln1_b"], shared["w_router"], L, Lp)

    capacity = EC_CAPACITY_FACTOR * B * (L + N_META) // N_EXPERTS
    n_slot_tiles = -(-capacity // SLOT_TILE)
    nb = N // LANES
    nbp = -(-nb // LANES) * LANES
    aff_blocks = jnp.pad(aff.reshape(N_EXPERTS, nb, LANES), ((0, 0), (0, nbp - nb), (0, 0)),
                         constant_values=-1.0)
    assert n_slot_tiles * SLOT_TILE - capacity <= B * PAD_ROWS
    idx = _select(aff_blocks, _block_order(B, Lp, nbp), capacity, n_slot_tiles, Lp, L)
    acc = _moe(idx, h1, acc, shared["w_gate"], shared["w_up"], shared["w_down"], shared["w_router"])
    return _final_norm(acc.reshape(B, Lp, ROW_SLABS, LANES), shared["ln2_g"], shared["ln2_b"], L)


def kernel(x_prompt, x_sample, meta_tokens, ln_emb_g, ln_emb_b, w_in, q_norm_g, k_norm_g, ssm_lambda_re, ssm_lambda_im, ssm_log_dt, ssm_b_re, ssm_b_im, ssm_c_re, ssm_c_im, ssm_d, w_glu, b_glu, w_out, ln1_g, ln1_b, w_router, w_gate, w_up, w_down, ln2_g, ln2_b):
    row = lambda a: a.reshape(1, -1).astype(F32)
    w_q = w_in[0][:, :Q_END].reshape(D_MODEL, N_KV_HEADS, Q_PER_KV, HEAD_DIM)
    w_q = w_q.transpose(0, 2, 1, 3).reshape(D_MODEL, Q_END)
    w_oa = w_out[0][:ATTN_WIDTH].reshape(N_KV_HEADS, Q_PER_KV, HEAD_DIM, D_MODEL)
    w_oa = w_oa.transpose(1, 0, 2, 3).reshape(ATTN_WIDTH, D_MODEL)
    head_of = jnp.arange(Q_END) // HEAD_DIM
    ssm_w1, ssm_w2, ssm_dec = _ssm_weights(ssm_lambda_re[0], ssm_lambda_im[0], ssm_log_dt[0],
                                           ssm_b_re[0], ssm_b_im[0], ssm_c_re[0], ssm_c_im[0],
                                           ssm_d[0])
    shared = dict(
        ln_emb_g=row(ln_emb_g), ln_emb_b=row(ln_emb_b),
        w_in=jnp.concatenate([w_q, w_in[0][:, Q_END:]], axis=1).astype(BF16),
        qg=row(jnp.tile(q_norm_g[0], N_Q_HEADS)), kg=row(jnp.tile(k_norm_g[0], N_KV_HEADS)),
        bones=(head_of[:, None] == head_of[None, :]).astype(BF16),
        score_bound=(1.02 * HEAD_DIM * Q_PRESCALE * jnp.max(jnp.abs(q_norm_g[0]))
                     * jnp.max(jnp.abs(k_norm_g[0]))).astype(F32),
        ssm_w1=ssm_w1, ssm_w2=ssm_w2, ssm_dec=ssm_dec,
        w_glu=w_glu[0].astype(BF16), b_glu=row(b_glu[0]),
        w_out_attn=w_oa.astype(BF16), w_out_ssm=w_out[0][ATTN_WIDTH:].astype(BF16),
        ln1_g=row(ln1_g[0]), ln1_b=row(ln1_b[0]),
        w_router=jnp.pad(w_router[0], ((0, 0), (0, LANES - N_EXPERTS))).astype(BF16),
        w_gate=w_gate[0].astype(BF16), w_up=w_up[0].astype(BF16), w_down=w_down[0].astype(BF16),
        ln2_g=row(ln2_g[0]), ln2_b=row(ln2_b[0]),
    )
    meta_pad = jnp.pad(meta_tokens.astype(F32), ((PAD_ROWS, 0), (0, 0)))[None]
    return (_run_trunk(x_prompt, meta_pad, shared), _run_trunk(x_sample, meta_pad, shared))
```

```python
import functools
import math

import jax
import jax.numpy as jnp
from jax import lax
from jax.experimental import pallas as pl
from jax.experimental.pallas import tpu as pltpu

F32 = jnp.float32
BF16 = jnp.bfloat16

D_MODEL = 1024
N_META = 16
GRID_W = 64
ATTN_WIDTH = 512
SSM_WIDTH = 512
HEAD_DIM = 64
N_Q_HEADS = 8
N_KV_HEADS = 2
Q_PER_KV = 4
KV_WIDTH = 128
ROPE_AXIS_DIM = 32
ROPE_THETA = 10000.0
ATTN_SCALE = HEAD_DIM ** -0.5
RMS_EPS = 1e-6
SSM_GROUP = 16
N_SSM_GROUPS = 32
SSM_STATE = 64
Q_END = ATTN_WIDTH
K_END = Q_END + KV_WIDTH
V_END = K_END + KV_WIDTH
IN_WIDTH = V_END + SSM_WIDTH
N_EXPERTS = 16
EXPERT_FF = 2048
EC_CAPACITY_FACTOR = 2
LN_EPS = 1e-5
DEPTH = 1
DEEPNORM_ALPHA = (2 * DEPTH) ** 0.25

LANES = 128
SUBLANES = 8
ROW_SLABS = D_MODEL // LANES
TAIL = LANES
PAD_ROWS = TAIL - N_META
CHUNK = 16
N_PAIRS = N_SSM_GROUPS // 2
PAIR_W = 2 * CHUNK * SSM_GROUP
PAIR_STATE = 2 * SSM_STATE
EMBED_TILE = 512
ROW_TILE = 384
Q_TILE = ROW_TILE
KV_TILE = EMBED_TILE
KEY_SUBTILE = 256
KV_UNROLL = 4
SLOT_TILE = 384
MOE_TILE_MAX = 528
FF_TILE = 512
SSM_ROW_TILE = 256
DEC_ROWS = 48
VT_ROWS = HEAD_DIM + 16
Q_PRESCALE = ATTN_SCALE * math.log2(math.e)
NEG_BIG = -1e30
SAFE_EXP2_RANGE = 60.0
VMEM_LIMIT = 56 * 1024 * 1024


def _cparams(sem):
    return pltpu.CompilerParams(dimension_semantics=sem, vmem_limit_bytes=VMEM_LIMIT)


def _const_spec(*shape):
    return pl.BlockSpec(shape, lambda *idx: (0,) * len(shape))


def _layer_norm_rows(x, g, b):
    mu = jnp.mean(x, axis=-1, keepdims=True)
    xc = x - mu
    var = jnp.mean(xc * xc, axis=-1, keepdims=True)
    return xc * lax.rsqrt(var + LN_EPS) * g + b


def _head_rms(t, gain, bones):
    sq = t * t
    hi = sq.astype(BF16)
    lo = (sq - hi.astype(F32)).astype(BF16)
    ss = (jnp.dot(hi, bones, preferred_element_type=F32)
          + jnp.dot(lo, bones, preferred_element_type=F32))
    return t * lax.rsqrt(ss * (1.0 / HEAD_DIM) + RMS_EPS) * gain


def _rope_slab(t, cos, sin_signed):
    lane = lax.broadcasted_iota(jnp.int32, t.shape, 1)
    first = (lane % ROPE_AXIS_DIM) < (ROPE_AXIS_DIM // 2)
    partner = jnp.where(first, pltpu.roll(t, LANES - 16, 1), pltpu.roll(t, 16, 1))
    return t * cos + partner * sin_signed


def _lane_piece_gather(pieces, lane16):
    out = None
    for j, (arr, src) in enumerate(pieces):
        shift = (SSM_GROUP * j - src) % LANES
        moved = pltpu.roll(arr, shift, 1) if shift else arr
        out = moved if out is None else jnp.where(lane16 == j, moved, out)
    return out


def _embed_body(x_ref, g_ref, b_ref, w_ref, qg_ref, kg_ref, bones_ref, cos_ref, sin_ref,
                h_ref, q_ref, k_ref, vt0_ref, vt1_ref, u_ref, u_scr):
    h = _layer_norm_rows(x_ref[0], g_ref[...], b_ref[...])
    h_ref[0] = h
    proj = jnp.dot(h.astype(BF16), w_ref[...], preferred_element_type=F32)
    cos = cos_ref[...]
    sin = sin_ref[...]
    qn = _head_rms(proj[:, :Q_END], qg_ref[...], bones_ref[...])
    for s in range(ATTN_WIDTH // LANES):
        sl = slice(s * LANES, (s + 1) * LANES)
        q_ref[0, :, sl] = (_rope_slab(qn[:, sl], cos, sin) * Q_PRESCALE).astype(BF16)
    kn = _head_rms(proj[:, Q_END:K_END], kg_ref[...], bones_ref[:KV_WIDTH, :KV_WIDTH])
    k_ref[0] = _rope_slab(kn, cos, sin).astype(BF16)
    vt = proj[:, K_END:V_END].T
    ones = jnp.ones((VT_ROWS - HEAD_DIM, vt.shape[1]), F32)
    vt0_ref[0, 0] = jnp.concatenate([vt[:HEAD_DIM], ones], axis=0).astype(BF16)
    vt1_ref[0, 0] = jnp.concatenate([vt[HEAD_DIM:], ones], axis=0).astype(BF16)
    n_rows = u_scr.shape[1] // CHUNK
    for v in range(SSM_WIDTH // LANES):
        u_scr[v] = proj[:, V_END + v * LANES:V_END + (v + 1) * LANES]
    by_token = [[u_scr[v, pl.ds(t, n_rows, stride=CHUNK), :] for v in range(SSM_WIDTH // LANES)]
                for t in range(CHUNK)]
    lane16 = lax.broadcasted_iota(jnp.int32, (n_rows, LANES), 1) // SSM_GROUP
    for p in range(N_PAIRS):
        for gi in range(2):
            src = (p % 4) * 2 * SSM_GROUP + gi * SSM_GROUP
            for h in range(CHUNK // SUBLANES):
                pieces = [(by_token[SUBLANES * h + j][p // 4], src) for j in range(SUBLANES)]
                lo = gi * CHUNK * SSM_GROUP + h * LANES
                u_ref[p, 0, :, lo:lo + LANES] = _lane_piece_gather(pieces, lane16)


def _embed_tail_body(x_ref, g_ref, b_ref, w_ref, qg_ref, kg_ref, bones_ref, cos_ref, sin_ref,
                     h_in, q_in, k_in, u_in, h_ref, q_ref, k_ref, vt0_ref, vt1_ref, u_ref, u_scr):
    del h_in, q_in, k_in, u_in
    _embed_body(x_ref, g_ref, b_ref, w_ref, qg_ref, kg_ref, bones_ref, cos_ref, sin_ref,
                h_ref, q_ref, k_ref, vt0_ref, vt1_ref, u_ref, u_scr)


def _embed(x, meta_pad, ln_g, ln_b, w_in, qg, kg, bones, cos, sin, L, Lp):
    B = x.shape[0]
    TM = EMBED_TILE
    nj = L // TM
    w_specs = [_const_spec(1, D_MODEL), _const_spec(1, D_MODEL), _const_spec(D_MODEL, IN_WIDTH),
               _const_spec(1, Q_END), _const_spec(1, KV_WIDTH), _const_spec(Q_END, Q_END)]

    def out_shapes(vt_tiles, vt_width):
        vt = jax.ShapeDtypeStruct((B, vt_tiles, VT_ROWS, vt_width), BF16)
        return [
            jax.ShapeDtypeStruct((B, Lp, D_MODEL), F32),
            jax.ShapeDtypeStruct((B, Lp, Q_END), BF16),
            jax.ShapeDtypeStruct((B, Lp, KV_WIDTH), BF16),
            vt, vt,
            jax.ShapeDtypeStruct((N_PAIRS, B, Lp // CHUNK, PAIR_W), F32),
        ]

    def out_specs(tm, row_block):
        return [
            pl.BlockSpec((1, tm, D_MODEL), lambda b, j: (b, row_block(j), 0)),
            pl.BlockSpec((1, tm, Q_END), lambda b, j: (b, row_block(j), 0)),
            pl.BlockSpec((1, tm, KV_WIDTH), lambda b, j: (b, row_block(j), 0)),
            pl.BlockSpec((1, 1, VT_ROWS, tm), lambda b, j: (b, j, 0, 0)),
            pl.BlockSpec((1, 1, VT_ROWS, tm), lambda b, j: (b, j, 0, 0)),
            pl.BlockSpec((N_PAIRS, 1, tm // CHUNK, PAIR_W), lambda b, j: (0, b, row_block(j), 0)),
        ]

    main = pl.pallas_call(
        _embed_body,
        grid=(B, nj),
        in_specs=[pl.BlockSpec((1, TM, D_MODEL), lambda b, j: (b, j, 0))] + w_specs
        + [pl.BlockSpec((TM, LANES), lambda b, j: (j, 0))] * 2,
        out_specs=out_specs(TM, lambda j: j),
        out_shape=out_shapes(nj, TM),
        scratch_shapes=[pltpu.VMEM((SSM_WIDTH // LANES, TM, LANES), F32)],
        compiler_params=_cparams(("parallel", "parallel")),
        name="embed_main",
    )
    h0, q, k, vt0, vt1, u = main(x, ln_g, ln_b, w_in, qg, kg, bones, cos, sin)

    jt = Lp // TAIL - 1
    tail = pl.pallas_call(
        _embed_tail_body,
        grid=(B, 1),
        in_specs=[pl.BlockSpec((1, TAIL, D_MODEL), lambda b, j: (0, 0, 0))] + w_specs
        + [pl.BlockSpec((TAIL, LANES), lambda b, j: (jt, 0))] * 2
        + [pl.BlockSpec(memory_space=pl.ANY)] * 4,
        out_specs=out_specs(TAIL, lambda j: jt),
        out_shape=out_shapes(1, TAIL),
        input_output_aliases={9: 0, 10: 1, 11: 2, 12: 5},
        scratch_shapes=[pltpu.VMEM((SSM_WIDTH // LANES, TAIL, LANES), F32)],
        compiler_params=_cparams(("parallel", "arbitrary")),
        name="embed_tail",
    )
    h0, q, k, vt0_tail, vt1_tail, u = tail(meta_pad, ln_g, ln_b, w_in, qg, kg, bones, cos, sin,
                                           h0, q, k, u)
    return h0, q, k, (vt0, vt1, vt0_tail, vt1_tail), u


def _attn_body(q_ref, k_ref, vt0_ref, vt1_ref, vt0t_ref, vt1t_ref, o_ref,
               qt_ref, s0_ref, s1_ref, mc0_ref, mc1_ref, m_ref, acc_ref, *, L, n_kv_tiles):
    cols = Q_PER_KV * Q_TILE
    _attn_load_qt(q_ref, qt_ref)
    m_ref[...] = jnp.full(m_ref.shape, NEG_BIG, F32)
    acc_ref[...] = jnp.zeros(acc_ref.shape, F32)

    s_bufs = ((s0_ref, mc0_ref), (s1_ref, mc1_ref))

    def scores(slot, k_tile, first_valid=0):
        s_ref, mc_ref = s_bufs[slot]
        n = k_tile.shape[0]
        for g in range(N_KV_HEADS):
            s = jnp.dot(k_tile, qt_ref[g], preferred_element_type=F32)
            if first_valid:
                key = lax.broadcasted_iota(jnp.int32, s.shape, 0)
                s = jnp.where(key >= first_valid, s, NEG_BIG)
            s_ref[g, 0:n, :] = s
            mc_ref[g] = jnp.broadcast_to(jnp.max(s, axis=0, keepdims=True), (SUBLANES, cols))

    def softmax_values(slot, vt_tiles, n):
        s_ref, mc_ref = s_bufs[slot]
        for g in range(N_KV_HEADS):
            m_old = m_ref[g]
            m_new = jnp.maximum(m_old, mc_ref[g])
            alpha = jnp.exp2(m_old[0:1, :] - m_new[0:1, :])
            acc = alpha * acc_ref[g]
            for k0 in range(0, n, KEY_SUBTILE):
                k1 = min(k0 + KEY_SUBTILE, n)
                p = jnp.exp2(s_ref[g, k0:k1, :] - m_new[0:1, :]).astype(BF16)
                acc = acc + jnp.dot(vt_tiles[g][:, k0:k1], p, preferred_element_type=F32)
            acc_ref[g] = acc
            m_ref[g] = m_new

    def k_tile(i):
        return k_ref[0, pl.ds(pl.multiple_of(i * KV_TILE, KV_TILE), KV_TILE), :]

    def vt_tiles(i):
        return vt0_ref[0, i], vt1_ref[0, i]

    scores(0, k_tile(0))

    def run_tiles(first, count, next_scores):
        for t in range(count):
            softmax_values(t % 2, vt_tiles(first + t), KV_TILE)
            if t + 1 < count:
                scores((t + 1) % 2, k_tile(first + t + 1))
            else:
                next_scores()

    def kv_group(i, carry):
        first = i * KV_UNROLL
        run_tiles(first, KV_UNROLL, lambda: scores(0, k_tile(first + KV_UNROLL)))
        return carry

    n_groups = n_kv_tiles // KV_UNROLL
    lax.fori_loop(0, n_groups - 1, kv_group, 0)
    run_tiles((n_groups - 1) * KV_UNROLL, KV_UNROLL,
              lambda: scores(0, k_ref[0, L:L + TAIL, :], PAD_ROWS))
    softmax_values(0, (vt0t_ref[0, 0], vt1t_ref[0, 0]), TAIL)

    _attn_store_out(acc_ref, o_ref)


def _attn_bounded_body(q_ref, k_ref, vt0_ref, vt1_ref, vt0t_ref, vt1t_ref, o_ref,
                       qt_ref, p0_ref, p1_ref, acc_ref, *, L, n_kv_tiles):
    _attn_load_qt(q_ref, qt_ref)
    acc_ref[...] = jnp.zeros(acc_ref.shape, F32)
    p_bufs = (p0_ref, p1_ref)

    def key_blocks(k_tile, vt_tiles, accs, first_valid=0):
        n = k_tile.shape[0]
        accs = list(accs)
        for i, k0 in enumerate(range(0, n, KEY_SUBTILE)):
            k1 = min(k0 + KEY_SUBTILE, n)
            p_ref = p_bufs[i % 2]
            for g in range(N_KV_HEADS):
                s = jnp.dot(k_tile[k0:k1], qt_ref[g], preferred_element_type=F32)
                if first_valid:
                    key = k0 + lax.broadcasted_iota(jnp.int32, s.shape, 0)
                    s = jnp.where(key >= first_valid, s, NEG_BIG)
                p_ref[g, 0:k1 - k0, :] = jnp.exp2(s).astype(BF16)
            for g in range(N_KV_HEADS):
                accs[g] = accs[g] + jnp.dot(vt_tiles[g][:, k0:k1], p_ref[g, 0:k1 - k0, :],
                                            preferred_element_type=F32)
        return accs

    def kv_group(i, carry):
        accs = [acc_ref[g] for g in range(N_KV_HEADS)]
        for t in range(KV_UNROLL):
            tile = i * KV_UNROLL + t
            start = pl.multiple_of(tile * KV_TILE, KV_TILE)
            accs = key_blocks(k_ref[0, pl.ds(start, KV_TILE), :], (vt0_ref[0, tile], vt1_ref[0, tile]),
                              accs)
        for g in range(N_KV_HEADS):
            acc_ref[g] = accs[g]
        return carry

    lax.fori_loop(0, n_kv_tiles // KV_UNROLL, kv_group, 0)
    accs = key_blocks(k_ref[0, L:L + TAIL, :], (vt0t_ref[0, 0], vt1t_ref[0, 0]),
                      [acc_ref[g] for g in range(N_KV_HEADS)], PAD_ROWS)
    for g in range(N_KV_HEADS):
        acc_ref[g] = accs[g]
    _attn_store_out(acc_ref, o_ref)


def _attn_load_qt(q_ref, qt_ref):
    tq = Q_TILE
    row = lax.broadcasted_iota(jnp.int32, (KV_WIDTH, tq), 0)
    for r in range(Q_PER_KV):
        slab_t = q_ref[0, :, r * LANES:(r + 1) * LANES].astype(F32).T
        for g in range(N_KV_HEADS):
            in_group = (row >= g * HEAD_DIM) & (row < (g + 1) * HEAD_DIM)
            qt_ref[g, :, r * tq:(r + 1) * tq] = jnp.where(in_group, slab_t, 0.0).astype(BF16)


def _attn_store_out(acc_ref, o_ref):
    tq = Q_TILE
    comb = jnp.concatenate(
        [acc_ref[g, 0:HEAD_DIM, :] / acc_ref[g, HEAD_DIM:HEAD_DIM + 1, :] for g in range(N_KV_HEADS)],
        axis=0)
    for r in range(Q_PER_KV):
        o_ref[0, :, r * LANES:(r + 1) * LANES] = comb[:, r * tq:(r + 1) * tq].T.astype(BF16)


def _attention(q, k, vts, L, Lp, score_bound):
    B = q.shape[0]
    nkt = L // KV_TILE
    assert nkt % KV_UNROLL == 0
    cols = Q_PER_KV * Q_TILE
    vt_spec = pl.BlockSpec((1, nkt, VT_ROWS, KV_TILE), lambda b, j: (b, 0, 0, 0))
    vt_tail_spec = pl.BlockSpec((1, 1, VT_ROWS, TAIL), lambda b, j: (b, 0, 0, 0))
    call = functools.partial(
        pl.pallas_call,
        grid=(B, Lp // Q_TILE),
        in_specs=[
            pl.BlockSpec((1, Q_TILE, Q_END), lambda b, j: (b, j, 0)),
            pl.BlockSpec((1, Lp, KV_WIDTH), lambda b, j: (b, 0, 0)),
            vt_spec, vt_spec, vt_tail_spec, vt_tail_spec,
        ],
        out_specs=pl.BlockSpec((1, Q_TILE, Q_END), lambda b, j: (b, j, 0)),
        out_shape=jax.ShapeDtypeStruct((B, Lp, Q_END), BF16),
        compiler_params=_cparams(("parallel", "arbitrary")),
    )
    bounded = call(
        functools.partial(_attn_bounded_body, L=L, n_kv_tiles=nkt),
        scratch_shapes=[
            pltpu.VMEM((N_KV_HEADS, KV_WIDTH, cols), BF16),
            pltpu.VMEM((N_KV_HEADS, KEY_SUBTILE, cols), BF16),
            pltpu.VMEM((N_KV_HEADS, KEY_SUBTILE, cols), BF16),
            pltpu.VMEM((N_KV_HEADS, VT_ROWS, cols), F32),
        ],
        name="attention_bounded",
    )
    general = call(
        functools.partial(_attn_body, L=L, n_kv_tiles=nkt),
        scratch_shapes=[
            pltpu.VMEM((N_KV_HEADS, KV_WIDTH, cols), BF16),
            pltpu.VMEM((N_KV_HEADS, KV_TILE, cols), F32),
            pltpu.VMEM((N_KV_HEADS, KV_TILE, cols), F32),
            pltpu.VMEM((N_KV_HEADS, SUBLANES, cols), F32),
            pltpu.VMEM((N_KV_HEADS, SUBLANES, cols), F32),
            pltpu.VMEM((N_KV_HEADS, SUBLANES, cols), F32),
            pltpu.VMEM((N_KV_HEADS, VT_ROWS, cols), F32),
        ],
        name="attention",
    )
    return lax.cond(score_bound <= SAFE_EXP2_RANGE, bounded, general, q, k, *vts)


def _ssm_weights(lam_re, lam_im, log_dt, b_re, b_im, c_re, c_im, d_skip):
    hp = lax.Precision.HIGHEST
    dt = jnp.exp(log_dt.astype(F32))[..., None]
    lr = lam_re.astype(F32)
    li = lam_im.astype(F32)

    def apow(n):
        n = jnp.asarray(n, F32)
        mag = jnp.exp(lr[..., None] * dt[..., None] * n)
        ang = li[..., None] * dt[..., None] * n
        return mag * jnp.cos(ang), mag * jnp.sin(ang)

    a1r, a1i = apow(jnp.ones((1,), F32))
    a1r, a1i = a1r[..., 0], a1i[..., 0]
    nr = a1r - 1.0
    den = lr * lr + li * li
    f_r = (nr * lr + a1i * li) / den
    f_i = (a1i * lr - nr * li) / den
    br = b_re.astype(F32)
    bi = b_im.astype(F32)
    bb_r = f_r[..., None] * br - f_i[..., None] * bi
    bb_i = f_r[..., None] * bi + f_i[..., None] * br
    cr = c_re.astype(F32)
    ci = c_im.astype(F32)

    tau = jnp.arange(CHUNK + 1, dtype=F32)
    pr, pi = apow(tau)
    ab_r = pr[..., None] * bb_r[:, :, :, None, :] - pi[..., None] * bb_i[:, :, :, None, :]
    ab_i = pr[..., None] * bb_i[:, :, :, None, :] + pi[..., None] * bb_r[:, :, :, None, :]
    kern = (jnp.einsum("dgop,dgptc->dgtco", cr, ab_r, precision=hp)
            - jnp.einsum("dgop,dgptc->dgtco", ci, ab_i, precision=hp))
    s_idx = jnp.arange(CHUNK)[:, None]
    t_idx = jnp.arange(CHUNK)[None, :]
    lag_f = jnp.clip(t_idx - s_idx, 0, CHUNK)
    lag_b = jnp.clip(s_idx - t_idx, 0, CHUNK)
    m_f = jnp.where((t_idx >= s_idx)[..., None, None], kern[0][:, lag_f], 0.0)
    m_b = jnp.where((s_idx >= t_idx)[..., None, None], kern[1][:, lag_b], 0.0)
    eye_t = jnp.eye(CHUNK, dtype=F32)[None, :, :, None, None]
    eye_c = jnp.eye(SSM_GROUP, dtype=F32)[None, None, None]
    dsk = d_skip.astype(F32).reshape(N_SSM_GROUPS, 1, 1, SSM_GROUP, 1)
    m_all = m_f + m_b + eye_t * eye_c * dsk
    m_all = m_all.transpose(0, 1, 3, 2, 4).reshape(N_SSM_GROUPS, 256, 256)

    def state_in(d, expo):
        er, ei = pr[d][..., expo], pi[d][..., expo]
        wr = er[..., None] * bb_r[d][:, :, None, :] - ei[..., None] * bb_i[d][:, :, None, :]
        wi = er[..., None] * bb_i[d][:, :, None, :] + ei[..., None] * bb_r[d][:, :, None, :]
        to_rows = lambda w: w.transpose(0, 2, 3, 1).reshape(N_SSM_GROUPS, 256, SSM_STATE)
        return to_rows(wr), to_rows(wi)

    sf_r, sf_i = state_in(0, jnp.arange(CHUNK - 1, -1, -1))
    sb_r, sb_i = state_in(1, jnp.arange(CHUNK))

    def state_out(d, expo):
        er, ei = pr[d][..., expo], pi[d][..., expo]
        wr = cr[d].transpose(0, 2, 1)[:, :, None, :] * er[..., None] \
            - ci[d].transpose(0, 2, 1)[:, :, None, :] * ei[..., None]
        wi = cr[d].transpose(0, 2, 1)[:, :, None, :] * ei[..., None] \
            + ci[d].transpose(0, 2, 1)[:, :, None, :] * er[..., None]
        flat = lambda w: w.reshape(N_SSM_GROUPS, SSM_STATE, 256)
        return flat(wr), flat(-wi)

    of_r, of_i = state_out(0, jnp.arange(1, CHUNK + 1))
    ob_r, ob_i = state_out(1, jnp.arange(CHUNK, 0, -1))

    def pair_rows(w):
        return w.reshape(N_PAIRS, 2, *w.shape[1:])

    z256 = jnp.zeros((N_PAIRS, 256, 256), F32)
    mp = pair_rows(m_all)
    w1_y = jnp.concatenate([jnp.concatenate([mp[:, 0], z256], axis=2),
                            jnp.concatenate([z256, mp[:, 1]], axis=2)], axis=1)

    def pair_cols_in(w):
        wp = pair_rows(w)
        z = jnp.zeros_like(wp[:, 0])
        return jnp.concatenate([jnp.concatenate([wp[:, 0], z], axis=2),
                                jnp.concatenate([z, wp[:, 1]], axis=2)], axis=1)

    w1 = jnp.concatenate([w1_y, pair_cols_in(sf_r), pair_cols_in(sf_i),
                          pair_cols_in(sb_r), pair_cols_in(sb_i)], axis=2)

    def pair_rows_out(w):
        wp = pair_rows(w)
        z = jnp.zeros_like(wp[:, 0])
        return jnp.concatenate([jnp.concatenate([wp[:, 0], z], axis=2),
                                jnp.concatenate([z, wp[:, 1]], axis=2)], axis=1)

    w2 = jnp.concatenate([pair_rows_out(of_r), pair_rows_out(of_i),
                          pair_rows_out(ob_r), pair_rows_out(ob_i)], axis=1)

    qr, qi = apow(CHUNK * jnp.arange(SUBLANES + 1, dtype=F32))
    zero = jnp.zeros_like(qr[..., 0])

    def dec_rows(d, q_expo):
        rows = [qr[d][..., 1], qi[d][..., 1], qr[d][..., 2], qi[d][..., 2],
                qr[d][..., 4], qi[d][..., 4], zero[d], zero[d]]
        rows += [qr[d][..., n] for n in q_expo] + [qi[d][..., n] for n in q_expo]
        return jnp.stack(rows, axis=1)

    dec = jnp.concatenate([dec_rows(0, range(1, SUBLANES + 1)),
                           dec_rows(1, range(SUBLANES, 0, -1))], axis=1)
    dec = dec.reshape(N_PAIRS, 2, DEC_ROWS, SSM_STATE).transpose(0, 2, 1, 3)
    dec = dec.reshape(N_PAIRS, DEC_ROWS, PAIR_STATE)
    return w1.astype(BF16), w2.astype(BF16), dec


def _ssm_body(u_ref, w1_ref, w2_ref, dec_ref, z_ref, sh_ref, *, B, n_chunks):
    R = B * n_chunks
    tiles = [(r0, min(SSM_ROW_TILE, R - r0)) for r0 in range(0, R, SSM_ROW_TILE)]
    PS = PAIR_STATE
    n_blocks = n_chunks // SUBLANES

    for r0, n in tiles:
        t = jnp.dot(u_ref[0, r0:r0 + n, :].astype(BF16), w1_ref[0], preferred_element_type=F32)
        z_ref[0, r0:r0 + n, :] = t[:, :PAIR_W]
        sh_ref[r0:r0 + n, :] = t[:, PAIR_W:]

    sub = lax.broadcasted_iota(jnp.int32, (SUBLANES, PS), 0)

    def cmul(ar, ai, xr, xi):
        return ar * xr - ai * xi, ar * xi + ai * xr

    def block_scan(sr, si, cr, ci, base, reverse):
        xr, xi = sr, si
        for k, d in enumerate((1, 2, 4)):
            pr_ = dec_ref[0, base + 2 * k:base + 2 * k + 1, :]
            pi_ = dec_ref[0, base + 2 * k + 1:base + 2 * k + 2, :]
            keep = (sub < SUBLANES - d) if reverse else (sub >= d)
            shift = SUBLANES - d if reverse else d
            yr = jnp.where(keep, pltpu.roll(xr, shift, 0), 0.0)
            yi = jnp.where(keep, pltpu.roll(xi, shift, 0), 0.0)
            mr, mi = cmul(pr_, pi_, yr, yi)
            xr, xi = xr + mr, xi + mi
        qr_ = dec_ref[0, base + 8:base + 16, :]
        qi_ = dec_ref[0, base + 16:base + 24, :]
        mr, mi = cmul(qr_, qi_, cr, ci)
        xr, xi = xr + mr, xi + mi
        edge = SUBLANES - 1 if reverse else 0
        shift = SUBLANES - 1 if reverse else 1
        er = jnp.where(sub == edge, cr, pltpu.roll(xr, shift, 0))
        ei = jnp.where(sub == edge, ci, pltpu.roll(xi, shift, 0))
        last = 0 if reverse else SUBLANES - 1
        nr = jnp.broadcast_to(xr[last:last + 1, :], (SUBLANES, PS))
        ni = jnp.broadcast_to(xi[last:last + 1, :], (SUBLANES, PS))
        return er, ei, nr, ni

    def scan_step(i, carry):
        new = []
        for b in range(B):
            cfr, cfi, cbr, cbi = carry[4 * b:4 * b + 4]
            tile_f = (i + n_blocks - 1) % n_blocks
            tile_b = (2 * n_blocks - 2 - i) % n_blocks
            rf = pl.multiple_of(b * n_chunks + tile_f * SUBLANES, SUBLANES)
            rb = pl.multiple_of(b * n_chunks + tile_b * SUBLANES, SUBLANES)
            f_rows, b_rows = pl.ds(rf, SUBLANES), pl.ds(rb, SUBLANES)
            er, ei, cfr, cfi = block_scan(sh_ref[f_rows, 0:PS], sh_ref[f_rows, PS:2 * PS],
                                          cfr, cfi, 0, False)
            sh_ref[f_rows, 0:PS] = er
            sh_ref[f_rows, PS:2 * PS] = ei
            er, ei, cbr, cbi = block_scan(sh_ref[b_rows, 2 * PS:3 * PS], sh_ref[b_rows, 3 * PS:4 * PS],
                                          cbr, cbi, DEC_ROWS // 2, True)
            sh_ref[b_rows, 2 * PS:3 * PS] = er
            sh_ref[b_rows, 3 * PS:4 * PS] = ei
            new += [cfr, cfi, cbr, cbi]
        return tuple(new)

    zero = jnp.zeros((SUBLANES, PS), F32)
    lax.fori_loop(0, n_blocks, scan_step, (zero,) * (4 * B))

    for r0, n in tiles:
        y = z_ref[0, r0:r0 + n, :] + jnp.dot(sh_ref[r0:r0 + n, :].astype(BF16), w2_ref[0],
                                             preferred_element_type=F32)
        z_ref[0, r0:r0 + n, :] = jax.nn.gelu(y)


def _ssm(u_pairs, w1, w2, dec, B, n_chunks):
    assert n_chunks % SUBLANES == 0
    R = B * n_chunks
    return pl.pallas_call(
        functools.partial(_ssm_body, B=B, n_chunks=n_chunks),
        grid=(N_PAIRS,),
        in_specs=[
            pl.BlockSpec((1, R, PAIR_W), lambda p: (p, 0, 0)),
            pl.BlockSpec((1, PAIR_W, 2 * PAIR_W), lambda p: (p, 0, 0)),
            pl.BlockSpec((1, PAIR_W, PAIR_W), lambda p: (p, 0, 0)),
            pl.BlockSpec((1, DEC_ROWS, PAIR_STATE), lambda p: (p, 0, 0)),
        ],
        out_specs=pl.BlockSpec((1, R, PAIR_W), lambda p: (p, 0, 0)),
        out_shape=jax.ShapeDtypeStruct((N_PAIRS, R, PAIR_W), F32),
        scratch_shapes=[pltpu.VMEM((R, 4 * PAIR_STATE), F32)],
        compiler_params=_cparams(("parallel",)),
        name="ssm",
    )(u_pairs, w1, w2, dec)


def _mix_body(o_ref, zp_ref, h0_ref, wglu_ref, bglu_ref, woa_ref, wos_ref, g_ref, b_ref, wr_ref,
              h1_ref, acc_ref, aff_ref, z_scr, *, L, Lp):
    n_rows = zp_ref.shape[1]
    lane16 = lax.broadcasted_iota(jnp.int32, (n_rows, LANES), 1) // SSM_GROUP
    for t in range(CHUNK):
        src = (t % SUBLANES) * SSM_GROUP
        for v in range(SSM_WIDTH // LANES):
            pieces = []
            for j in range(SUBLANES):
                p, gi = 4 * v + j // 2, j % 2
                lo = gi * CHUNK * SSM_GROUP + (t // SUBLANES) * LANES
                pieces.append((zp_ref[p, :, lo:lo + LANES], src))
            z_scr[v, pl.ds(t, n_rows, stride=CHUNK), :] = _lane_piece_gather(pieces, lane16)
    z = jnp.concatenate([z_scr[v] for v in range(SSM_WIDTH // LANES)], axis=1)
    gate = jax.nn.sigmoid(jnp.dot(z.astype(BF16), wglu_ref[...], preferred_element_type=F32)
                          + bglu_ref[...])
    ssm_out = (z * gate).astype(BF16)
    mix = (jnp.dot(o_ref[...], woa_ref[...], preferred_element_type=F32)
           + jnp.dot(ssm_out, wos_ref[...], preferred_element_type=F32))
    h1 = _layer_norm_rows(DEEPNORM_ALPHA * h0_ref[...] + mix, g_ref[...], b_ref[...])
    for s in range(ROW_SLABS):
        acc_ref[:, s, :] = DEEPNORM_ALPHA * h1[:, s * LANES:(s + 1) * LANES]
    h1_bits = pltpu.bitcast(h1.astype(BF16).astype(F32), jnp.uint32)
    for s in range(ROW_SLABS // 2):
        lo = h1_bits[:, s * LANES:(s + 1) * LANES]
        hi = h1_bits[:, (s + ROW_SLABS // 2) * LANES:(s + ROW_SLABS // 2 + 1) * LANES]
        h1_ref[:, s, :] = hi | lax.shift_right_logical(lo, jnp.uint32(16))
    logits = jnp.dot(h1.astype(BF16), wr_ref[...], preferred_element_type=F32)
    lane = lax.broadcasted_iota(jnp.int32, logits.shape, 1)
    logits = jnp.where(lane < N_EXPERTS, logits, NEG_BIG)
    e = jnp.exp(logits - jnp.max(logits, axis=-1, keepdims=True))
    aff = (e / jnp.sum(e, axis=-1, keepdims=True)).T[:N_EXPERTS, :]
    tiles_per_seq = Lp // ROW_TILE
    pos = (pl.program_id(0) % tiles_per_seq) * ROW_TILE + lax.broadcasted_iota(
        jnp.int32, aff.shape, 1)
    aff_ref[...] = jnp.where((pos < L) | (pos >= L + PAD_ROWS), aff, -1.0)


def _mix(o, z, h0, wglu, bglu, woa, wos, g, b, wr, L, Lp):
    N = o.shape[0]
    TM = ROW_TILE
    row = lambda w: pl.BlockSpec((TM, w), lambda i: (i, 0))
    tok_rows = pl.BlockSpec((TM, ROW_SLABS, LANES), lambda i: (i, 0, 0))
    return pl.pallas_call(
        functools.partial(_mix_body, L=L, Lp=Lp),
        grid=(N // TM,),
        in_specs=[row(ATTN_WIDTH), pl.BlockSpec((N_PAIRS, TM // CHUNK, PAIR_W), lambda i: (0, i, 0)),
                  row(D_MODEL),
                  _const_spec(SSM_WIDTH, SSM_WIDTH), _const_spec(1, SSM_WIDTH),
                  _const_spec(ATTN_WIDTH, D_MODEL), _const_spec(SSM_WIDTH, D_MODEL),
                  _const_spec(1, D_MODEL), _const_spec(1, D_MODEL), _const_spec(D_MODEL, LANES)],
        out_specs=[pl.BlockSpec((TM, ROW_SLABS // 2, LANES), lambda i: (i, 0, 0)), tok_rows,
                   pl.BlockSpec((N_EXPERTS, TM), lambda i: (0, i))],
        out_shape=[jax.ShapeDtypeStruct((N, ROW_SLABS // 2, LANES), jnp.uint32),
                   jax.ShapeDtypeStruct((N, ROW_SLABS, LANES), F32),
                   jax.ShapeDtypeStruct((N_EXPERTS, N), F32)],
        scratch_shapes=[pltpu.VMEM((SSM_WIDTH // LANES, TM, LANES), F32)],
        compiler_params=_cparams(("parallel",)),
        name="mix",
    )(o, z, h0, wglu, bglu, woa, wos, g, b, wr)


def _select_body(aff_ref, ord_ref, idx_ref, thr_ref, *, capacity, n_slot_tiles, seq_len, pad_start):
    nbp = aff_ref.shape[1]
    ri = lax.broadcasted_iota(jnp.int32, (LANES, LANES), 0)
    ci = lax.broadcasted_iota(jnp.int32, (LANES, LANES), 1)
    strict_upper = (ri < ci).astype(BF16)
    incl_upper = (ri <= ci).astype(BF16)
    ones = jnp.ones((LANES, LANES), BF16)
    order = ord_ref[...]
    blk = lax.broadcasted_iota(jnp.int32, (nbp, LANES), 0).astype(F32)
    blk_hi = jnp.floor(blk * (1.0 / 16.0))
    blk_lo = blk - 16.0 * blk_hi
    dot = functools.partial(jnp.dot, preferred_element_type=F32)
    dot_nt = functools.partial(lax.dot_general, dimension_numbers=(((1,), (1,)), ((), ())),
                               preferred_element_type=F32)

    def search_bit(i, ts):
        bit = jnp.left_shift(jnp.int32(1), 30 - i)
        new = []
        for e in range(N_EXPERTS):
            cand = ts[e] | bit
            bits = pltpu.bitcast(aff_ref[e], jnp.int32)
            cnt = jnp.sum((bits >= cand).astype(jnp.int32), keepdims=True)
            new.append(jnp.where(cnt >= capacity, cand, ts[e]))
        return tuple(new)

    thresholds = lax.fori_loop(0, 31, search_bit, (jnp.zeros((1, 1), jnp.int32),) * N_EXPERTS)
    for e in range(N_EXPERTS):
        thr_ref[e] = jnp.broadcast_to(thresholds[e], (SUBLANES, LANES))

    def per_expert(e, carry):
        bits = pltpu.bitcast(aff_ref[e], jnp.int32)
        t = thr_ref[e][0:1, 0:1]
        gt = bits > t
        eq = bits == t
        need = (capacity - jnp.sum(gt.astype(jnp.int32), keepdims=True)).astype(F32)
        eqb = eq.astype(BF16)
        tie_rank = dot(order, dot(eqb, ones).astype(BF16)) + dot(eqb, strict_upper)
        sel = (gt | (eq & (tie_rank < need))).astype(BF16)
        csum_in_blk = dot(sel, incl_upper)
        blk_tot = dot(sel, ones)
        blk_off = dot(order, blk_tot.astype(BF16))
        off_hi = jnp.floor(blk_off * (1.0 / 64.0))
        off_lo = blk_off - 64.0 * off_hi
        table = jnp.concatenate([csum_in_blk, off_hi, off_lo, blk_hi, blk_lo],
                                axis=1).astype(BF16)
        tot_l = dot_nt(ones[:SUBLANES], sel)
        off_l = dot_nt(tot_l.astype(BF16), order)[0:1, :]
        tot_l = tot_l[0:1, :]

        def per_tile(st, carry2):
            j = (st * SLOT_TILE + lax.broadcasted_iota(jnp.int32, (SLOT_TILE, 1), 0)).astype(F32)
            onehot = ((off_l <= j) & (j < off_l + tot_l)).astype(BF16)
            got = dot(onehot, table)
            local = j - (64.0 * got[:, LANES:2 * LANES] + got[:, 2 * LANES:3 * LANES])
            lane_idx = dot((got[:, :LANES] <= local).astype(BF16), ones)
            block = 16.0 * got[:, 3 * LANES:4 * LANES] + got[:, 4 * LANES:5 * LANES]
            k = j - capacity
            seq = sum((k >= PAD_ROWS * i).astype(F32) for i in range(1, SLOT_TILE // PAD_ROWS + 1))
            pad_tok = seq * (seq_len - PAD_ROWS) + pad_start + k
            tok = jnp.where(j < capacity, block * LANES + lane_idx, pad_tok)
            idx_ref[e, st] = tok.T[0:1, :].astype(jnp.int32)
            return carry2

        lax.fori_loop(0, n_slot_tiles, per_tile, 0)
        return carry

    lax.fori_loop(0, N_EXPERTS, per_expert, 0)


def _select(aff_blocks, order, capacity, n_slot_tiles, seq_len, pad_start):
    nbp = aff_blocks.shape[1]
    return pl.pallas_call(
        functools.partial(_select_body, capacity=capacity, n_slot_tiles=n_slot_tiles,
                          seq_len=seq_len, pad_start=pad_start),
        grid=(1,),
        in_specs=[_const_spec(N_EXPERTS, nbp, LANES), _const_spec(nbp, nbp)],
        out_specs=_const_spec(N_EXPERTS, n_slot_tiles, 1, SLOT_TILE),
        out_shape=jax.ShapeDtypeStruct((N_EXPERTS, n_slot_tiles, 1, SLOT_TILE), jnp.int32),
        scratch_shapes=[pltpu.VMEM((N_EXPERTS, SUBLANES, LANES), jnp.int32)],
        compiler_params=_cparams(("arbitrary",)),
        name="select",
    )(aff_blocks, order)


def _moe_body(idx_ref, idx_next_ref, h1_hbm, wg_ref, wu_ref, wd_ref, wr_ref, acc_in, acc_hbm,
              xbuf, abuf, hid_ref, sem_x, sem_a, sem_s):
    del acc_in
    T = xbuf.shape[1]
    n_c = pl.num_programs(1)
    step = pl.program_id(0) * n_c + pl.program_id(1)
    n_steps = pl.num_programs(0) * n_c
    slot = step % 2

    def start_rows(copy_of_row, idx):
        def body(j, carry):
            copy_of_row(idx[0, 0, 0, j], j).start()
            return carry
        lax.fori_loop(0, T, body, 0, unroll=8)

    def x_row(buf):
        return lambda row, j: pltpu.make_async_copy(h1_hbm.at[row], xbuf.at[buf, j], sem_x.at[buf])

    def acc_row_in(row, j):
        return pltpu.make_async_copy(acc_hbm.at[row], abuf.at[j], sem_a)

    def acc_row_out(row, j):
        return pltpu.make_async_copy(abuf.at[j], acc_hbm.at[row], sem_s)

    def wait_scatter():
        pltpu.make_async_copy(abuf, acc_hbm.at[pl.ds(0, T)], sem_s).wait()

    @pl.when(step == 0)
    def _():
        start_rows(x_row(0), idx_ref)

    @pl.when(step + 1 < n_steps)
    def _():
        start_rows(x_row(1 - slot), idx_next_ref)

    pltpu.make_async_copy(h1_hbm.at[pl.ds(0, T)], xbuf.at[slot], sem_x.at[slot]).wait()
    packed = [xbuf[slot, :, s, :] for s in range(ROW_SLABS // 2)]
    lo = [pltpu.bitcast(lax.shift_left(u, jnp.uint32(16)), F32) for u in packed]
    hi = [pltpu.bitcast(u & jnp.uint32(0xFFFF0000), F32) for u in packed]
    x = jnp.concatenate(lo + hi, axis=1).astype(BF16)
    logits = jnp.dot(x, wr_ref[...], preferred_element_type=F32)
    lane = lax.broadcasted_iota(jnp.int32, logits.shape, 1)
    logits = jnp.where(lane < N_EXPERTS, logits, NEG_BIG)
    p = jnp.exp(logits - jnp.max(logits, axis=-1, keepdims=True))
    gate = (jnp.sum(jnp.where(lane == pl.program_id(0), p, 0.0), axis=-1, keepdims=True)
            / jnp.sum(p, axis=-1, keepdims=True))
    for f in range(EXPERT_FF // FF_TILE):
        fs = slice(f * FF_TILE, (f + 1) * FF_TILE)
        hg = jnp.dot(x, wg_ref[0, :, fs], preferred_element_type=F32)
        hu = jnp.dot(x, wu_ref[0, :, fs], preferred_element_type=F32)
        hid_ref[:, fs] = (jax.nn.silu(hg) * hu).astype(BF16)
        if f == 0:
            @pl.when(step > 0)
            def _():
                wait_scatter()

            start_rows(acc_row_in, idx_ref)
    pltpu.make_async_copy(acc_hbm.at[pl.ds(0, T)], abuf, sem_a).wait()
    for n0 in range(0, D_MODEL, 2 * LANES):
        y = jnp.dot(hid_ref[...], wd_ref[0, :, n0:n0 + 2 * LANES], preferred_element_type=F32) * gate
        for h in range(2):
            s = n0 // LANES + h
            abuf[:, s, :] = abuf[:, s, :] + y[:, h * LANES:(h + 1) * LANES]
    start_rows(acc_row_out, idx_ref)

    @pl.when(step == n_steps - 1)
    def _():
        wait_scatter()


def _moe(idx, h1, acc, wg, wu, wd, wr):
    n_slots = idx.shape[1] * idx.shape[3]
    n_c = next(n for n in range(1, n_slots) if n_slots % n == 0 and n_slots // n <= MOE_TILE_MAX
               and (n_slots // n) % 16 == 0)
    T = n_slots // n_c
    idx = idx.reshape(N_EXPERTS, n_c, 1, T)

    def next_block(e, c):
        return (jnp.minimum(e + (c + 1) // n_c, N_EXPERTS - 1), (c + 1) % n_c, 0, 0)

    return pl.pallas_call(
        _moe_body,
        grid=(N_EXPERTS, n_c),
        in_specs=[
            pl.BlockSpec((1, 1, 1, T), lambda e, c: (e, c, 0, 0), memory_space=pltpu.SMEM),
            pl.BlockSpec((1, 1, 1, T), next_block, memory_space=pltpu.SMEM),
            pl.BlockSpec(memory_space=pl.ANY),
            pl.BlockSpec((1, D_MODEL, EXPERT_FF), lambda e, c: (e, 0, 0)),
            pl.BlockSpec((1, D_MODEL, EXPERT_FF), lambda e, c: (e, 0, 0)),
            pl.BlockSpec((1, EXPERT_FF, D_MODEL), lambda e, c: (e, 0, 0)),
            _const_spec(D_MODEL, LANES),
            pl.BlockSpec(memory_space=pl.ANY),
        ],
        out_specs=pl.BlockSpec(memory_space=pl.ANY),
        out_shape=jax.ShapeDtypeStruct(acc.shape, F32),
        scratch_shapes=[pltpu.VMEM((2, T, ROW_SLABS // 2, LANES), jnp.uint32),
                        pltpu.VMEM((T, ROW_SLABS, LANES), F32),
                        pltpu.VMEM((T, EXPERT_FF), BF16),
                        pltpu.SemaphoreType.DMA((2,)), pltpu.SemaphoreType.DMA(()),
                        pltpu.SemaphoreType.DMA(())],
        input_output_aliases={7: 0},
        compiler_params=pltpu.CompilerParams(dimension_semantics=("arbitrary", "arbitrary"),
                                             vmem_limit_bytes=VMEM_LIMIT,
                                             disable_bounds_checks=True),
        name="expert_ffn",
    )(idx, idx, h1, wg, wu, wd, wr, acc)


def _final_body(a_ref, g_ref, b_ref, o_ref):
    x = jnp.concatenate([a_ref[0, :, s, :] for s in range(ROW_SLABS)], axis=1)
    o_ref[0] = _layer_norm_rows(x, g_ref[...], b_ref[...])


def _final_norm(acc, g, b, L):
    B = acc.shape[0]
    TM = EMBED_TILE
    return pl.pallas_call(
        _final_body,
        grid=(B, L // TM),
        in_specs=[pl.BlockSpec((1, TM, ROW_SLABS, LANES), lambda b_, j: (b_, j, 0, 0)),
                  _const_spec(1, D_MODEL), _const_spec(1, D_MODEL)],
        out_specs=pl.BlockSpec((1, TM, D_MODEL), lambda b_, j: (b_, j, 0)),
        out_shape=jax.ShapeDtypeStruct((B, L, D_MODEL), F32),
        compiler_params=_cparams(("parallel", "parallel")),
        name="final_norm",
    )(acc, g, b)


def _rope_tables(L, Lp):
    rows = L // GRID_W
    t = jnp.arange(L, dtype=jnp.int32)
    m = jnp.arange(N_META, dtype=jnp.int32)
    pad = jnp.zeros((Lp - L - N_META,), jnp.int32)
    row = jnp.concatenate([t // GRID_W - rows // 2, pad,
                           jnp.full((N_META,), -(rows // 2) - 1, jnp.int32)])
    col = jnp.concatenate([t % GRID_W - GRID_W // 2, pad, m - GRID_W // 2])
    inv_freq = ROPE_THETA ** (-jnp.arange(0, ROPE_AXIS_DIM, 2, dtype=F32) / ROPE_AXIS_DIM)
    ang_r = row.astype(F32)[:, None] * inv_freq
    ang_c = col.astype(F32)[:, None] * inv_freq
    cos = jnp.concatenate([jnp.cos(ang_r)] * 2 + [jnp.cos(ang_c)] * 2, axis=1)
    sin = jnp.concatenate([-jnp.sin(ang_r), jnp.sin(ang_r), -jnp.sin(ang_c), jnp.sin(ang_c)], axis=1)
    return jnp.tile(cos, (1, 2)), jnp.tile(sin, (1, 2))


def _block_order(B, Lp, nbp):
    nbb = Lp // LANES
    r = jnp.arange(nbp)
    b, jb = r // nbb, r % nbb
    rank = jnp.where(r < B * nbb, b * nbb + jnp.where(jb == nbb - 1, 0, jb + 1), r)
    return (rank[None, :] < rank[:, None]).astype(BF16)


def _run_trunk(x, meta_pad, shared):
    B, L, _ = x.shape
    Lp = L + TAIL
    N = B * Lp
    cos, sin = _rope_tables(L, Lp)
    h0, q, k, vts, u = _embed(x, meta_pad, shared["ln_emb_g"], shared["ln_emb_b"], shared["w_in"],
                              shared["qg"], shared["kg"], shared["bones"], cos, sin, L, Lp)
    o = _attention(q, k, vts, L, Lp, shared["score_bound"])
    z = _ssm(u.reshape(N_PAIRS, N // CHUNK, PAIR_W), shared["ssm_w1"], shared["ssm_w2"],
             shared["ssm_dec"], B, Lp // CHUNK)
    h1, acc, aff = _mix(o.reshape(N, ATTN_WIDTH), z, h0.reshape(N, D_MODEL),
                        shared["w_glu"], shared["b_glu"], shared["w_out_attn"], shared["w_out_ssm"],
                        shared["ln1_g"], shared["ln1_b"], shared["w_router"], L, Lp)

    capacity = EC_CAPACITY_FACTOR * B * (L + N_META) // N_EXPERTS
    n_slot_tiles = -(-capacity // SLOT_TILE)
    nb = N // LANES
    nbp = -(-nb // LANES) * LANES
    aff_blocks = jnp.pad(aff.reshape(N_EXPERTS, nb, LANES), ((0, 0), (0, nbp - nb), (0, 0)),
                         constant_values=-1.0)
    assert n_slot_tiles * SLOT_TILE - capacity <= B * PAD_ROWS
    idx = _select(aff_blocks, _block_order(B, Lp, nbp), capacity, n_slot_tiles, Lp, L)
    acc = _moe(idx, h1, acc, shared["w_gate"], shared["w_up"], shared["w_down"], shared["w_router"])
    return _final_norm(acc.reshape(B, Lp, ROW_SLABS, LANES), shared["ln2_g"], shared["ln2_b"], L)


def kernel(x_prompt, x_sample, meta_tokens, ln_emb_g, ln_emb_b, w_in, q_norm_g, k_norm_g, ssm_lambda_re, ssm_lambda_im, ssm_log_dt, ssm_b_re, ssm_b_im, ssm_c_re, ssm_c_im, ssm_d, w_glu, b_glu, w_out, ln1_g, ln1_b, w_router, w_gate, w_up, w_down, ln2_g, ln2_b):
    row = lambda a: a.reshape(1, -1).astype(F32)
    w_q = w_in[0][:, :Q_END].reshape(D_MODEL, N_KV_HEADS, Q_PER_KV, HEAD_DIM)
    w_q = w_q.transpose(0, 2, 1, 3).reshape(D_MODEL, Q_END)
    w_oa = w_out[0][:ATTN_WIDTH].reshape(N_KV_HEADS, Q_PER_KV, HEAD_DIM, D_MODEL)
    w_oa = w_oa.transpose(1, 0, 2, 3).reshape(ATTN_WIDTH, D_MODEL)
    head_of = jnp.arange(Q_END) // HEAD_DIM
    ssm_w1, ssm_w2, ssm_dec = _ssm_weights(ssm_lambda_re[0], ssm_lambda_im[0], ssm_log_dt[0],
                                           ssm_b_re[0], ssm_b_im[0], ssm_c_re[0], ssm_c_im[0],
                                           ssm_d[0])
    shared = dict(
        ln_emb_g=row(ln_emb_g), ln_emb_b=row(ln_emb_b),
        w_in=jnp.concatenate([w_q, w_in[0][:, Q_END:]], axis=1).astype(BF16),
        qg=row(jnp.tile(q_norm_g[0], N_Q_HEADS)), kg=row(jnp.tile(k_norm_g[0], N_KV_HEADS)),
        bones=(head_of[:, None] == head_of[None, :]).astype(BF16),
        score_bound=(1.02 * HEAD_DIM * Q_PRESCALE * jnp.max(jnp.abs(q_norm_g[0]))
                     * jnp.max(jnp.abs(k_norm_g[0]))).astype(F32),
        ssm_w1=ssm_w1, ssm_w2=ssm_w2, ssm_dec=ssm_dec,
        w_glu=w_glu[0].astype(BF16), b_glu=row(b_glu[0]),
        w_out_attn=w_oa.astype(BF16), w_out_ssm=w_out[0][ATTN_WIDTH:].astype(BF16),
        ln1_g=row(ln1_g[0]), ln1_b=row(ln1_b[0]),
        w_router=jnp.pad(w_router[0], ((0, 0), (0, LANES - N_EXPERTS))).astype(BF16),
        w_gate=w_gate[0].astype(BF16), w_up=w_up[0].astype(BF16), w_down=w_down[0].astype(BF16),
        ln2_g=row(ln2_g[0]), ln2_b=row(ln2_b[0]),
    )
    meta_pad = jnp.pad(meta_tokens.astype(F32), ((PAD_ROWS, 0), (0, 0)))[None]
    return (_run_trunk(x_prompt, meta_pad, shared), _run_trunk(x_sample, meta_pad, shared))
```

```python
import functools
import math

import jax
import jax.numpy as jnp
from jax import lax
from jax.experimental import pallas as pl
from jax.experimental.pallas import tpu as pltpu

F32 = jnp.float32
BF16 = jnp.bfloat16

D_MODEL = 1024
N_META = 16
GRID_W = 64
ATTN_WIDTH = 512
SSM_WIDTH = 512
HEAD_DIM = 64
N_Q_HEADS = 8
N_KV_HEADS = 2
Q_PER_KV = 4
KV_WIDTH = 128
ROPE_AXIS_DIM = 32
ROPE_THETA = 10000.0
ATTN_SCALE = HEAD_DIM ** -0.5
RMS_EPS = 1e-6
SSM_GROUP = 16
N_SSM_GROUPS = 32
SSM_STATE = 64
Q_END = ATTN_WIDTH
K_END = Q_END + KV_WIDTH
V_END = K_END + KV_WIDTH
IN_WIDTH = V_END + SSM_WIDTH
N_EXPERTS = 16
EXPERT_FF = 2048
EC_CAPACITY_FACTOR = 2
LN_EPS = 1e-5
DEPTH = 1
DEEPNORM_ALPHA = (2 * DEPTH) ** 0.25

LANES = 128
SUBLANES = 8
ROW_SLABS = D_MODEL // LANES
TAIL = LANES
PAD_ROWS = TAIL - N_META
CHUNK = 16
N_PAIRS = N_SSM_GROUPS // 2
PAIR_W = 2 * CHUNK * SSM_GROUP
PAIR_STATE = 2 * SSM_STATE
EMBED_TILE = 512
ROW_TILE = 384
Q_TILE = ROW_TILE
KV_TILE = EMBED_TILE
KEY_SUBTILE = 256
KV_UNROLL = 4
SLOT_TILE = 384
MOE_TILE_MAX = 1056
FF_TILE = 512
SSM_ROW_TILE = 256
DEC_ROWS = 48
VT_ROWS = HEAD_DIM + 16
Q_PRESCALE = ATTN_SCALE * math.log2(math.e)
NEG_BIG = -1e30
SAFE_EXP2_RANGE = 60.0
VMEM_LIMIT = 56 * 1024 * 1024


def _cparams(sem):
    return pltpu.CompilerParams(dimension_semantics=sem, vmem_limit_bytes=VMEM_LIMIT)


def _const_spec(*shape):
    return pl.BlockSpec(shape, lambda *idx: (0,) * len(shape))


def _layer_norm_rows(x, g, b):
    mu = jnp.mean(x, axis=-1, keepdims=True)
    xc = x - mu
    var = jnp.mean(xc * xc, axis=-1, keepdims=True)
    return xc * lax.rsqrt(var + LN_EPS) * g + b


def _head_rms(t, gain, bones):
    sq = t * t
    hi = sq.astype(BF16)
    lo = (sq - hi.astype(F32)).astype(BF16)
    ss = (jnp.dot(hi, bones, preferred_element_type=F32)
          + jnp.dot(lo, bones, preferred_element_type=F32))
    return t * lax.rsqrt(ss * (1.0 / HEAD_DIM) + RMS_EPS) * gain


def _rope_slab(t, cos, sin_signed):
    lane = lax.broadcasted_iota(jnp.int32, t.shape, 1)
    first = (lane % ROPE_AXIS_DIM) < (ROPE_AXIS_DIM // 2)
    partner = jnp.where(first, pltpu.roll(t, LANES - 16, 1), pltpu.roll(t, 16, 1))
    return t * cos + partner * sin_signed


def _lane_piece_gather(pieces, lane16):
    out = None
    for j, (arr, src) in enumerate(pieces):
        shift = (SSM_GROUP * j - src) % LANES
        moved = pltpu.roll(arr, shift, 1) if shift else arr
        out = moved if out is None else jnp.where(lane16 == j, moved, out)
    return out


def _embed_body(x_ref, g_ref, b_ref, w_ref, qg_ref, kg_ref, bones_ref, cos_ref, sin_ref,
                h_ref, q_ref, k_ref, vt0_ref, vt1_ref, u_ref, u_scr):
    h = _layer_norm_rows(x_ref[0], g_ref[...], b_ref[...])
    h_ref[0] = h
    proj = jnp.dot(h.astype(BF16), w_ref[...], preferred_element_type=F32)
    cos = cos_ref[...]
    sin = sin_ref[...]
    qn = _head_rms(proj[:, :Q_END], qg_ref[...], bones_ref[...])
    for s in range(ATTN_WIDTH // LANES):
        sl = slice(s * LANES, (s + 1) * LANES)
        q_ref[0, :, sl] = (_rope_slab(qn[:, sl], cos, sin) * Q_PRESCALE).astype(BF16)
    kn = _head_rms(proj[:, Q_END:K_END], kg_ref[...], bones_ref[:KV_WIDTH, :KV_WIDTH])
    k_ref[0] = _rope_slab(kn, cos, sin).astype(BF16)
    vt = proj[:, K_END:V_END].T
    ones = jnp.ones((VT_ROWS - HEAD_DIM, vt.shape[1]), F32)
    vt0_ref[0, 0] = jnp.concatenate([vt[:HEAD_DIM], ones], axis=0).astype(BF16)
    vt1_ref[0, 0] = jnp.concatenate([vt[HEAD_DIM:], ones], axis=0).astype(BF16)
    n_rows = u_scr.shape[1] // CHUNK
    for v in range(SSM_WIDTH // LANES):
        u_scr[v] = proj[:, V_END + v * LANES:V_END + (v + 1) * LANES]
    by_token = [[u_scr[v, pl.ds(t, n_rows, stride=CHUNK), :] for v in range(SSM_WIDTH // LANES)]
                for t in range(CHUNK)]
    lane16 = lax.broadcasted_iota(jnp.int32, (n_rows, LANES), 1) // SSM_GROUP
    for p in range(N_PAIRS):
        for gi in range(2):
            src = (p % 4) * 2 * SSM_GROUP + gi * SSM_GROUP
            for h in range(CHUNK // SUBLANES):
                pieces = [(by_token[SUBLANES * h + j][p // 4], src) for j in range(SUBLANES)]
                lo = gi * CHUNK * SSM_GROUP + h * LANES
                u_ref[p, 0, :, lo:lo + LANES] = _lane_piece_gather(pieces, lane16)


def _embed_tail_body(x_ref, g_ref, b_ref, w_ref, qg_ref, kg_ref, bones_ref, cos_ref, sin_ref,
                     h_in, q_in, k_in, u_in, h_ref, q_ref, k_ref, vt0_ref, vt1_ref, u_ref, u_scr):
    del h_in, q_in, k_in, u_in
    _embed_body(x_ref, g_ref, b_ref, w_ref, qg_ref, kg_ref, bones_ref, cos_ref, sin_ref,
                h_ref, q_ref, k_ref, vt0_ref, vt1_ref, u_ref, u_scr)


def _embed(x, meta_pad, ln_g, ln_b, w_in, qg, kg, bones, cos, sin, L, Lp):
    B = x.shape[0]
    TM = EMBED_TILE
    nj = L // TM
    w_specs = [_const_spec(1, D_MODEL), _const_spec(1, D_MODEL), _const_spec(D_MODEL, IN_WIDTH),
               _const_spec(1, Q_END), _const_spec(1, KV_WIDTH), _const_spec(Q_END, Q_END)]

    def out_shapes(vt_tiles, vt_width):
        vt = jax.ShapeDtypeStruct((B, vt_tiles, VT_ROWS, vt_width), BF16)
        return [
            jax.ShapeDtypeStruct((B, Lp, D_MODEL), F32),
            jax.ShapeDtypeStruct((B, Lp, Q_END), BF16),
            jax.ShapeDtypeStruct((B, Lp, KV_WIDTH), BF16),
            vt, vt,
            jax.ShapeDtypeStruct((N_PAIRS, B, Lp // CHUNK, PAIR_W), F32),
        ]

    def out_specs(tm, row_block):
        return [
            pl.BlockSpec((1, tm, D_MODEL), lambda b, j: (b, row_block(j), 0)),
            pl.BlockSpec((1, tm, Q_END), lambda b, j: (b, row_block(j), 0)),
            pl.BlockSpec((1, tm, KV_WIDTH), lambda b, j: (b, row_block(j), 0)),
            pl.BlockSpec((1, 1, VT_ROWS, tm), lambda b, j: (b, j, 0, 0)),
            pl.BlockSpec((1, 1, VT_ROWS, tm), lambda b, j: (b, j, 0, 0)),
            pl.BlockSpec((N_PAIRS, 1, tm // CHUNK, PAIR_W), lambda b, j: (0, b, row_block(j), 0)),
        ]

    main = pl.pallas_call(
        _embed_body,
        grid=(B, nj),
        in_specs=[pl.BlockSpec((1, TM, D_MODEL), lambda b, j: (b, j, 0))] + w_specs
        + [pl.BlockSpec((TM, LANES), lambda b, j: (j, 0))] * 2,
        out_specs=out_specs(TM, lambda j: j),
        out_shape=out_shapes(nj, TM),
        scratch_shapes=[pltpu.VMEM((SSM_WIDTH // LANES, TM, LANES), F32)],
        compiler_params=_cparams(("parallel", "parallel")),
        name="embed_main",
    )
    h0, q, k, vt0, vt1, u = main(x, ln_g, ln_b, w_in, qg, kg, bones, cos, sin)

    jt = Lp // TAIL - 1
    tail = pl.pallas_call(
        _embed_tail_body,
        grid=(B, 1),
        in_specs=[pl.BlockSpec((1, TAIL, D_MODEL), lambda b, j: (0, 0, 0))] + w_specs
        + [pl.BlockSpec((TAIL, LANES), lambda b, j: (jt, 0))] * 2
        + [pl.BlockSpec(memory_space=pl.ANY)] * 4,
        out_specs=out_specs(TAIL, lambda j: jt),
        out_shape=out_shapes(1, TAIL),
        input_output_aliases={9: 0, 10: 1, 11: 2, 12: 5},
        scratch_shapes=[pltpu.VMEM((SSM_WIDTH // LANES, TAIL, LANES), F32)],
        compiler_params=_cparams(("parallel", "arbitrary")),
        name="embed_tail",
    )
    h0, q, k, vt0_tail, vt1_tail, u = tail(meta_pad, ln_g, ln_b, w_in, qg, kg, bones, cos, sin,
                                           h0, q, k, u)
    return h0, q, k, (vt0, vt1, vt0_tail, vt1_tail), u


def _attn_body(q_ref, k_ref, vt0_ref, vt1_ref, vt0t_ref, vt1t_ref, o_ref,
               qt_ref, s0_ref, s1_ref, mc0_ref, mc1_ref, m_ref, acc_ref, *, L, n_kv_tiles):
    cols = Q_PER_KV * Q_TILE
    _attn_load_qt(q_ref, qt_ref)
    m_ref[...] = jnp.full(m_ref.shape, NEG_BIG, F32)
    acc_ref[...] = jnp.zeros(acc_ref.shape, F32)

    s_bufs = ((s0_ref, mc0_ref), (s1_ref, mc1_ref))

    def scores(slot, k_tile, first_valid=0):
        s_ref, mc_ref = s_bufs[slot]
        n = k_tile.shape[0]
        for g in range(N_KV_HEADS):
            s = jnp.dot(k_tile, qt_ref[g], preferred_element_type=F32)
            if first_valid:
                key = lax.broadcasted_iota(jnp.int32, s.shape, 0)
                s = jnp.where(key >= first_valid, s, NEG_BIG)
            s_ref[g, 0:n, :] = s
            mc_ref[g] = jnp.broadcast_to(jnp.max(s, axis=0, keepdims=True), (SUBLANES, cols))

    def softmax_values(slot, vt_tiles, n):
        s_ref, mc_ref = s_bufs[slot]
        for g in range(N_KV_HEADS):
            m_old = m_ref[g]
            m_new = jnp.maximum(m_old, mc_ref[g])
            alpha = jnp.exp2(m_old[0:1, :] - m_new[0:1, :])
            acc = alpha * acc_ref[g]
            for k0 in range(0, n, KEY_SUBTILE):
                k1 = min(k0 + KEY_SUBTILE, n)
                p = jnp.exp2(s_ref[g, k0:k1, :] - m_new[0:1, :]).astype(BF16)
                acc = acc + jnp.dot(vt_tiles[g][:, k0:k1], p, preferred_element_type=F32)
            acc_ref[g] = acc
            m_ref[g] = m_new

    def k_tile(i):
        return k_ref[0, pl.ds(pl.multiple_of(i * KV_TILE, KV_TILE), KV_TILE), :]

    def vt_tiles(i):
        return vt0_ref[0, i], vt1_ref[0, i]

    scores(0, k_tile(0))

    def run_tiles(first, count, next_scores):
        for t in range(count):
            softmax_values(t % 2, vt_tiles(first + t), KV_TILE)
            if t + 1 < count:
                scores((t + 1) % 2, k_tile(first + t + 1))
            else:
                next_scores()

    def kv_group(i, carry):
        first = i * KV_UNROLL
        run_tiles(first, KV_UNROLL, lambda: scores(0, k_tile(first + KV_UNROLL)))
        return carry

    n_groups = n_kv_tiles // KV_UNROLL
    lax.fori_loop(0, n_groups - 1, kv_group, 0)
    run_tiles((n_groups - 1) * KV_UNROLL, KV_UNROLL,
              lambda: scores(0, k_ref[0, L:L + TAIL, :], PAD_ROWS))
    softmax_values(0, (vt0t_ref[0, 0], vt1t_ref[0, 0]), TAIL)

    _attn_store_out(acc_ref, o_ref)


def _attn_bounded_body(q_ref, k_ref, vt0_ref, vt1_ref, vt0t_ref, vt1t_ref, o_ref,
                       qt_ref, p0_ref, p1_ref, acc_ref, *, L, n_kv_tiles):
    _attn_load_qt(q_ref, qt_ref)
    acc_ref[...] = jnp.zeros(acc_ref.shape, F32)
    p_bufs = (p0_ref, p1_ref)

    def key_blocks(k_tile, vt_tiles, accs, first_valid=0):
        n = k_tile.shape[0]
        accs = list(accs)
        for i, k0 in enumerate(range(0, n, KEY_SUBTILE)):
            k1 = min(k0 + KEY_SUBTILE, n)
            p_ref = p_bufs[i % 2]
            for g in range(N_KV_HEADS):
                s = jnp.dot(k_tile[k0:k1], qt_ref[g], preferred_element_type=F32)
                if first_valid:
                    key = k0 + lax.broadcasted_iota(jnp.int32, s.shape, 0)
                    s = jnp.where(key >= first_valid, s, NEG_BIG)
                p_ref[g, 0:k1 - k0, :] = jnp.exp2(s).astype(BF16)
            for g in range(N_KV_HEADS):
                accs[g] = accs[g] + jnp.dot(vt_tiles[g][:, k0:k1], p_ref[g, 0:k1 - k0, :],
                                            preferred_element_type=F32)
        return accs

    def kv_group(i, carry):
        accs = [acc_ref[g] for g in range(N_KV_HEADS)]
        for t in range(KV_UNROLL):
            tile = i * KV_UNROLL + t
            start = pl.multiple_of(tile * KV_TILE, KV_TILE)
            accs = key_blocks(k_ref[0, pl.ds(start, KV_TILE), :], (vt0_ref[0, tile], vt1_ref[0, tile]),
                              accs)
        for g in range(N_KV_HEADS):
            acc_ref[g] = accs[g]
        return carry

    lax.fori_loop(0, n_kv_tiles // KV_UNROLL, kv_group, 0)
    accs = key_blocks(k_ref[0, L:L + TAIL, :], (vt0t_ref[0, 0], vt1t_ref[0, 0]),
                      [acc_ref[g] for g in range(N_KV_HEADS)], PAD_ROWS)
    for g in range(N_KV_HEADS):
        acc_ref[g] = accs[g]
    _attn_store_out(acc_ref, o_ref)


def _attn_load_qt(q_ref, qt_ref):
    tq = Q_TILE
    row = lax.broadcasted_iota(jnp.int32, (KV_WIDTH, tq), 0)
    for r in range(Q_PER_KV):
        slab_t = q_ref[0, :, r * LANES:(r + 1) * LANES].astype(F32).T
        for g in range(N_KV_HEADS):
            in_group = (row >= g * HEAD_DIM) & (row < (g + 1) * HEAD_DIM)
            qt_ref[g, :, r * tq:(r + 1) * tq] = jnp.where(in_group, slab_t, 0.0).astype(BF16)


def _attn_store_out(acc_ref, o_ref):
    tq = Q_TILE
    comb = jnp.concatenate(
        [acc_ref[g, 0:HEAD_DIM, :] / acc_ref[g, HEAD_DIM:HEAD_DIM + 1, :] for g in range(N_KV_HEADS)],
        axis=0)
    for r in range(Q_PER_KV):
        o_ref[0, :, r * LANES:(r + 1) * LANES] = comb[:, r * tq:(r + 1) * tq].T.astype(BF16)


def _attention(q, k, vts, L, Lp, score_bound):
    B = q.shape[0]
    nkt = L // KV_TILE
    assert nkt % KV_UNROLL == 0
    cols = Q_PER_KV * Q_TILE
    vt_spec = pl.BlockSpec((1, nkt, VT_ROWS, KV_TILE), lambda b, j: (b, 0, 0, 0))
    vt_tail_spec = pl.BlockSpec((1, 1, VT_ROWS, TAIL), lambda b, j: (b, 0, 0, 0))
    call = functools.partial(
        pl.pallas_call,
        grid=(B, Lp // Q_TILE),
        in_specs=[
            pl.BlockSpec((1, Q_TILE, Q_END), lambda b, j: (b, j, 0)),
            pl.BlockSpec((1, Lp, KV_WIDTH), lambda b, j: (b, 0, 0)),
            vt_spec, vt_spec, vt_tail_spec, vt_tail_spec,
        ],
        out_specs=pl.BlockSpec((1, Q_TILE, Q_END), lambda b, j: (b, j, 0)),
        out_shape=jax.ShapeDtypeStruct((B, Lp, Q_END), BF16),
        compiler_params=_cparams(("parallel", "arbitrary")),
    )
    bounded = call(
        functools.partial(_attn_bounded_body, L=L, n_kv_tiles=nkt),
        scratch_shapes=[
            pltpu.VMEM((N_KV_HEADS, KV_WIDTH, cols), BF16),
            pltpu.VMEM((N_KV_HEADS, KEY_SUBTILE, cols), BF16),
            pltpu.VMEM((N_KV_HEADS, KEY_SUBTILE, cols), BF16),
            pltpu.VMEM((N_KV_HEADS, VT_ROWS, cols), F32),
        ],
        name="attention_bounded",
    )
    general = call(
        functools.partial(_attn_body, L=L, n_kv_tiles=nkt),
        scratch_shapes=[
            pltpu.VMEM((N_KV_HEADS, KV_WIDTH, cols), BF16),
            pltpu.VMEM((N_KV_HEADS, KV_TILE, cols), F32),
            pltpu.VMEM((N_KV_HEADS, KV_TILE, cols), F32),
            pltpu.VMEM((N_KV_HEADS, SUBLANES, cols), F32),
            pltpu.VMEM((N_KV_HEADS, SUBLANES, cols), F32),
            pltpu.VMEM((N_KV_HEADS, SUBLANES, cols), F32),
            pltpu.VMEM((N_KV_HEADS, VT_ROWS, cols), F32),
        ],
        name="attention",
    )
    return lax.cond(score_bound <= SAFE_EXP2_RANGE, bounded, general, q, k, *vts)


def _ssm_weights(lam_re, lam_im, log_dt, b_re, b_im, c_re, c_im, d_skip):
    hp = lax.Precision.HIGHEST
    dt = jnp.exp(log_dt.astype(F32))[..., None]
    lr = lam_re.astype(F32)
    li = lam_im.astype(F32)

    def apow(n):
        n = jnp.asarray(n, F32)
        mag = jnp.exp(lr[..., None] * dt[..., None] * n)
        ang = li[..., None] * dt[..., None] * n
        return mag * jnp.cos(ang), mag * jnp.sin(ang)

    a1r, a1i = apow(jnp.ones((1,), F32))
    a1r, a1i = a1r[..., 0], a1i[..., 0]
    nr = a1r - 1.0
    den = lr * lr + li * li
    f_r = (nr * lr + a1i * li) / den
    f_i = (a1i * lr - nr * li) / den
    br = b_re.astype(F32)
    bi = b_im.astype(F32)
    bb_r = f_r[..., None] * br - f_i[..., None] * bi
    bb_i = f_r[..., None] * bi + f_i[..., None] * br
    cr = c_re.astype(F32)
    ci = c_im.astype(F32)

    tau = jnp.arange(CHUNK + 1, dtype=F32)
    pr, pi = apow(tau)
    ab_r = pr[..., None] * bb_r[:, :, :, None, :] - pi[..., None] * bb_i[:, :, :, None, :]
    ab_i = pr[..., None] * bb_i[:, :, :, None, :] + pi[..., None] * bb_r[:, :, :, None, :]
    kern = (jnp.einsum("dgop,dgptc->dgtco", cr, ab_r, precision=hp)
            - jnp.einsum("dgop,dgptc->dgtco", ci, ab_i, precision=hp))
    s_idx = jnp.arange(CHUNK)[:, None]
    t_idx = jnp.arange(CHUNK)[None, :]
    lag_f = jnp.clip(t_idx - s_idx, 0, CHUNK)
    lag_b = jnp.clip(s_idx - t_idx, 0, CHUNK)
    m_f = jnp.where((t_idx >= s_idx)[..., None, None], kern[0][:, lag_f], 0.0)
    m_b = jnp.where((s_idx >= t_idx)[..., None, None], kern[1][:, lag_b], 0.0)
    eye_t = jnp.eye(CHUNK, dtype=F32)[None, :, :, None, None]
    eye_c = jnp.eye(SSM_GROUP, dtype=F32)[None, None, None]
    dsk = d_skip.astype(F32).reshape(N_SSM_GROUPS, 1, 1, SSM_GROUP, 1)
    m_all = m_f + m_b + eye_t * eye_c * dsk
    m_all = m_all.transpose(0, 1, 3, 2, 4).reshape(N_SSM_GROUPS, 256, 256)

    def state_in(d, expo):
        er, ei = pr[d][..., expo], pi[d][..., expo]
        wr = er[..., None] * bb_r[d][:, :, None, :] - ei[..., None] * bb_i[d][:, :, None, :]
        wi = er[..., None] * bb_i[d][:, :, None, :] + ei[..., None] * bb_r[d][:, :, None, :]
        to_rows = lambda w: w.transpose(0, 2, 3, 1).reshape(N_SSM_GROUPS, 256, SSM_STATE)
        return to_rows(wr), to_rows(wi)

    sf_r, sf_i = state_in(0, jnp.arange(CHUNK - 1, -1, -1))
    sb_r, sb_i = state_in(1, jnp.arange(CHUNK))

    def state_out(d, expo):
        er, ei = pr[d][..., expo], pi[d][..., expo]
        wr = cr[d].transpose(0, 2, 1)[:, :, None, :] * er[..., None] \
            - ci[d].transpose(0, 2, 1)[:, :, None, :] * ei[..., None]
        wi = cr[d].transpose(0, 2, 1)[:, :, None, :] * ei[..., None] \
            + ci[d].transpose(0, 2, 1)[:, :, None, :] * er[..., None]
        flat = lambda w: w.reshape(N_SSM_GROUPS, SSM_STATE, 256)
        return flat(wr), flat(-wi)

    of_r, of_i = state_out(0, jnp.arange(1, CHUNK + 1))
    ob_r, ob_i = state_out(1, jnp.arange(CHUNK, 0, -1))

    def pair_rows(w):
        return w.reshape(N_PAIRS, 2, *w.shape[1:])

    z256 = jnp.zeros((N_PAIRS, 256, 256), F32)
    mp = pair_rows(m_all)
    w1_y = jnp.concatenate([jnp.concatenate([mp[:, 0], z256], axis=2),
                            jnp.concatenate([z256, mp[:, 1]], axis=2)], axis=1)

    def pair_cols_in(w):
        wp = pair_rows(w)
        z = jnp.zeros_like(wp[:, 0])
        return jnp.concatenate([jnp.concatenate([wp[:, 0], z], axis=2),
                                jnp.concatenate([z, wp[:, 1]], axis=2)], axis=1)

    w1 = jnp.concatenate([w1_y, pair_cols_in(sf_r), pair_cols_in(sf_i),
                          pair_cols_in(sb_r), pair_cols_in(sb_i)], axis=2)

    def pair_rows_out(w):
        wp = pair_rows(w)
        z = jnp.zeros_like(wp[:, 0])
        return jnp.concatenate([jnp.concatenate([wp[:, 0], z], axis=2),
                                jnp.concatenate([z, wp[:, 1]], axis=2)], axis=1)

    w2 = jnp.concatenate([pair_rows_out(of_r), pair_rows_out(of_i),
                          pair_rows_out(ob_r), pair_rows_out(ob_i)], axis=1)

    qr, qi = apow(CHUNK * jnp.arange(SUBLANES + 1, dtype=F32))
    zero = jnp.zeros_like(qr[..., 0])

    def dec_rows(d, q_expo):
        rows = [qr[d][..., 1], qi[d][..., 1], qr[d][..., 2], qi[d][..., 2],
                qr[d][..., 4], qi[d][..., 4], zero[d], zero[d]]
        rows += [qr[d][..., n] for n in q_expo] + [qi[d][..., n] for n in q_expo]
        return jnp.stack(rows, axis=1)

    dec = jnp.concatenate([dec_rows(0, range(1, SUBLANES + 1)),
                           dec_rows(1, range(SUBLANES, 0, -1))], axis=1)
    dec = dec.reshape(N_PAIRS, 2, DEC_ROWS, SSM_STATE).transpose(0, 2, 1, 3)
    dec = dec.reshape(N_PAIRS, DEC_ROWS, PAIR_STATE)
    return w1.astype(BF16), w2.astype(BF16), dec


def _ssm_body(u_ref, w1_ref, w2_ref, dec_ref, z_ref, sh_ref, *, B, n_chunks):
    R = B * n_chunks
    tiles = [(r0, min(SSM_ROW_TILE, R - r0)) for r0 in range(0, R, SSM_ROW_TILE)]
    PS = PAIR_STATE
    n_blocks = n_chunks // SUBLANES

    for r0, n in tiles:
        t = jnp.dot(u_ref[0, r0:r0 + n, :].astype(BF16), w1_ref[0], preferred_element_type=F32)
        z_ref[0, r0:r0 + n, :] = t[:, :PAIR_W]
        sh_ref[r0:r0 + n, :] = t[:, PAIR_W:]

    sub = lax.broadcasted_iota(jnp.int32, (SUBLANES, PS), 0)

    def cmul(ar, ai, xr, xi):
        return ar * xr - ai * xi, ar * xi + ai * xr

    def block_scan(sr, si, cr, ci, base, reverse):
        xr, xi = sr, si
        for k, d in enumerate((1, 2, 4)):
            pr_ = dec_ref[0, base + 2 * k:base + 2 * k + 1, :]
            pi_ = dec_ref[0, base + 2 * k + 1:base + 2 * k + 2, :]
            keep = (sub < SUBLANES - d) if reverse else (sub >= d)
            shift = SUBLANES - d if reverse else d
            yr = jnp.where(keep, pltpu.roll(xr, shift, 0), 0.0)
            yi = jnp.where(keep, pltpu.roll(xi, shift, 0), 0.0)
            mr, mi = cmul(pr_, pi_, yr, yi)
            xr, xi = xr + mr, xi + mi
        qr_ = dec_ref[0, base + 8:base + 16, :]
        qi_ = dec_ref[0, base + 16:base + 24, :]
        mr, mi = cmul(qr_, qi_, cr, ci)
        xr, xi = xr + mr, xi + mi
        edge = SUBLANES - 1 if reverse else 0
        shift = SUBLANES - 1 if reverse else 1
        er = jnp.where(sub == edge, cr, pltpu.roll(xr, shift, 0))
        ei = jnp.where(sub == edge, ci, pltpu.roll(xi, shift, 0))
        last = 0 if reverse else SUBLANES - 1
        nr = jnp.broadcast_to(xr[last:last + 1, :], (SUBLANES, PS))
        ni = jnp.broadcast_to(xi[last:last + 1, :], (SUBLANES, PS))
        return er, ei, nr, ni

    def scan_step(i, carry):
        new = []
        for b in range(B):
            cfr, cfi, cbr, cbi = carry[4 * b:4 * b + 4]
            tile_f = (i + n_blocks - 1) % n_blocks
            tile_b = (2 * n_blocks - 2 - i) % n_blocks
            rf = pl.multiple_of(b * n_chunks + tile_f * SUBLANES, SUBLANES)
            rb = pl.multiple_of(b * n_chunks + tile_b * SUBLANES, SUBLANES)
            f_rows, b_rows = pl.ds(rf, SUBLANES), pl.ds(rb, SUBLANES)
            er, ei, cfr, cfi = block_scan(sh_ref[f_rows, 0:PS], sh_ref[f_rows, PS:2 * PS],
                                          cfr, cfi, 0, False)
            sh_ref[f_rows, 0:PS] = er
            sh_ref[f_rows, PS:2 * PS] = ei
            er, ei, cbr, cbi = block_scan(sh_ref[b_rows, 2 * PS:3 * PS], sh_ref[b_rows, 3 * PS:4 * PS],
                                          cbr, cbi, DEC_ROWS // 2, True)
            sh_ref[b_rows, 2 * PS:3 * PS] = er
            sh_ref[b_rows, 3 * PS:4 * PS] = ei
            new += [cfr, cfi, cbr, cbi]
        return tuple(new)

    zero = jnp.zeros((SUBLANES, PS), F32)
    lax.fori_loop(0, n_blocks, scan_step, (zero,) * (4 * B))

    for r0, n in tiles:
        y = z_ref[0, r0:r0 + n, :] + jnp.dot(sh_ref[r0:r0 + n, :].astype(BF16), w2_ref[0],
                                             preferred_element_type=F32)
        z_ref[0, r0:r0 + n, :] = jax.nn.gelu(y)


def _ssm(u_pairs, w1, w2, dec, B, n_chunks):
    assert n_chunks % SUBLANES == 0
    R = B * n_chunks
    return pl.pallas_call(
        functools.partial(_ssm_body, B=B, n_chunks=n_chunks),
        grid=(N_PAIRS,),
        in_specs=[
            pl.BlockSpec((1, R, PAIR_W), lambda p: (p, 0, 0)),
            pl.BlockSpec((1, PAIR_W, 2 * PAIR_W), lambda p: (p, 0, 0)),
            pl.BlockSpec((1, PAIR_W, PAIR_W), lambda p: (p, 0, 0)),
            pl.BlockSpec((1, DEC_ROWS, PAIR_STATE), lambda p: (p, 0, 0)),
        ],
        out_specs=pl.BlockSpec((1, R, PAIR_W), lambda p: (p, 0, 0)),
        out_shape=jax.ShapeDtypeStruct((N_PAIRS, R, PAIR_W), F32),
        scratch_shapes=[pltpu.VMEM((R, 4 * PAIR_STATE), F32)],
        compiler_params=_cparams(("parallel",)),
        name="ssm",
    )(u_pairs, w1, w2, dec)


def _mix_body(o_ref, zp_ref, h0_ref, wglu_ref, bglu_ref, woa_ref, wos_ref, g_ref, b_ref, wr_ref,
              h1_ref, acc_ref, aff_ref, z_scr, *, L, Lp):
    n_rows = zp_ref.shape[1]
    lane16 = lax.broadcasted_iota(jnp.int32, (n_rows, LANES), 1) // SSM_GROUP
    for t in range(CHUNK):
        src = (t % SUBLANES) * SSM_GROUP
        for v in range(SSM_WIDTH // LANES):
            pieces = []
            for j in range(SUBLANES):
                p, gi = 4 * v + j // 2, j % 2
                lo = gi * CHUNK * SSM_GROUP + (t // SUBLANES) * LANES
                pieces.append((zp_ref[p, :, lo:lo + LANES], src))
            z_scr[v, pl.ds(t, n_rows, stride=CHUNK), :] = _lane_piece_gather(pieces, lane16)
    z = jnp.concatenate([z_scr[v] for v in range(SSM_WIDTH // LANES)], axis=1)
    gate = jax.nn.sigmoid(jnp.dot(z.astype(BF16), wglu_ref[...], preferred_element_type=F32)
                          + bglu_ref[...])
    ssm_out = (z * gate).astype(BF16)
    mix = (jnp.dot(o_ref[...], woa_ref[...], preferred_element_type=F32)
           + jnp.dot(ssm_out, wos_ref[...], preferred_element_type=F32))
    h1 = _layer_norm_rows(DEEPNORM_ALPHA * h0_ref[...] + mix, g_ref[...], b_ref[...])
    for s in range(ROW_SLABS):
        acc_ref[:, s, :] = DEEPNORM_ALPHA * h1[:, s * LANES:(s + 1) * LANES]
    h1_bits = pltpu.bitcast(h1.astype(BF16).astype(F32), jnp.uint32)
    for s in range(ROW_SLABS // 2):
        lo = h1_bits[:, s * LANES:(s + 1) * LANES]
        hi = h1_bits[:, (s + ROW_SLABS // 2) * LANES:(s + ROW_SLABS // 2 + 1) * LANES]
        h1_ref[:, s, :] = hi | lax.shift_right_logical(lo, jnp.uint32(16))
    logits = jnp.dot(h1.astype(BF16), wr_ref[...], preferred_element_type=F32)
    lane = lax.broadcasted_iota(jnp.int32, logits.shape, 1)
    logits = jnp.where(lane < N_EXPERTS, logits, NEG_BIG)
    e = jnp.exp(logits - jnp.max(logits, axis=-1, keepdims=True))
    aff = (e / jnp.sum(e, axis=-1, keepdims=True)).T[:N_EXPERTS, :]
    tiles_per_seq = Lp // ROW_TILE
    pos = (pl.program_id(0) % tiles_per_seq) * ROW_TILE + lax.broadcasted_iota(
        jnp.int32, aff.shape, 1)
    aff_ref[...] = jnp.where((pos < L) | (pos >= L + PAD_ROWS), aff, -1.0)


def _mix(o, z, h0, wglu, bglu, woa, wos, g, b, wr, L, Lp):
    N = o.shape[0]
    TM = ROW_TILE
    row = lambda w: pl.BlockSpec((TM, w), lambda i: (i, 0))
    tok_rows = pl.BlockSpec((TM, ROW_SLABS, LANES), lambda i: (i, 0, 0))
    return pl.pallas_call(
        functools.partial(_mix_body, L=L, Lp=Lp),
        grid=(N // TM,),
        in_specs=[row(ATTN_WIDTH), pl.BlockSpec((N_PAIRS, TM // CHUNK, PAIR_W), lambda i: (0, i, 0)),
                  row(D_MODEL),
                  _const_spec(SSM_WIDTH, SSM_WIDTH), _const_spec(1, SSM_WIDTH),
                  _const_spec(ATTN_WIDTH, D_MODEL), _const_spec(SSM_WIDTH, D_MODEL),
                  _const_spec(1, D_MODEL), _const_spec(1, D_MODEL), _const_spec(D_MODEL, LANES)],
        out_specs=[pl.BlockSpec((TM, ROW_SLABS // 2, LANES), lambda i: (i, 0, 0)), tok_rows,
                   pl.BlockSpec((N_EXPERTS, TM), lambda i: (0, i))],
        out_shape=[jax.ShapeDtypeStruct((N, ROW_SLABS // 2, LANES), jnp.uint32),
                   jax.ShapeDtypeStruct((N, ROW_SLABS, LANES), F32),
                   jax.ShapeDtypeStruct((N_EXPERTS, N), F32)],
        scratch_shapes=[pltpu.VMEM((SSM_WIDTH // LANES, TM, LANES), F32)],
        compiler_params=_cparams(("parallel",)),
        name="mix",
    )(o, z, h0, wglu, bglu, woa, wos, g, b, wr)


def _select_body(aff_ref, ord_ref, idx_ref, thr_ref, *, capacity, n_slot_tiles, seq_len, pad_start):
    nbp = aff_ref.shape[1]
    ri = lax.broadcasted_iota(jnp.int32, (LANES, LANES), 0)
    ci = lax.broadcasted_iota(jnp.int32, (LANES, LANES), 1)
    strict_upper = (ri < ci).astype(BF16)
    incl_upper = (ri <= ci).astype(BF16)
    ones = jnp.ones((LANES, LANES), BF16)
    order = ord_ref[...]
    blk = lax.broadcasted_iota(jnp.int32, (nbp, LANES), 0).astype(F32)
    blk_hi = jnp.floor(blk * (1.0 / 16.0))
    blk_lo = blk - 16.0 * blk_hi
    dot = functools.partial(jnp.dot, preferred_element_type=F32)
    dot_nt = functools.partial(lax.dot_general, dimension_numbers=(((1,), (1,)), ((), ())),
                               preferred_element_type=F32)

    def search_bit(i, ts):
        bit = jnp.left_shift(jnp.int32(1), 30 - i)
        new = []
        for e in range(N_EXPERTS):
            cand = ts[e] | bit
            bits = pltpu.bitcast(aff_ref[e], jnp.int32)
            cnt = jnp.sum((bits >= cand).astype(jnp.int32), keepdims=True)
            new.append(jnp.where(cnt >= capacity, cand, ts[e]))
        return tuple(new)

    thresholds = lax.fori_loop(0, 31, search_bit, (jnp.zeros((1, 1), jnp.int32),) * N_EXPERTS)
    for e in range(N_EXPERTS):
        thr_ref[e] = jnp.broadcast_to(thresholds[e], (SUBLANES, LANES))

    def per_expert(e, carry):
        bits = pltpu.bitcast(aff_ref[e], jnp.int32)
        t = thr_ref[e][0:1, 0:1]
        gt = bits > t
        eq = bits == t
        need = (capacity - jnp.sum(gt.astype(jnp.int32), keepdims=True)).astype(F32)
        eqb = eq.astype(BF16)
        tie_rank = dot(order, dot(eqb, ones).astype(BF16)) + dot(eqb, strict_upper)
        sel = (gt | (eq & (tie_rank < need))).astype(BF16)
        csum_in_blk = dot(sel, incl_upper)
        blk_tot = dot(sel, ones)
        blk_off = dot(order, blk_tot.astype(BF16))
        off_hi = jnp.floor(blk_off * (1.0 / 64.0))
        off_lo = blk_off - 64.0 * off_hi
        table = jnp.concatenate([csum_in_blk, off_hi, off_lo, blk_hi, blk_lo],
                                axis=1).astype(BF16)
        tot_l = dot_nt(ones[:SUBLANES], sel)
        off_l = dot_nt(tot_l.astype(BF16), order)[0:1, :]
        tot_l = tot_l[0:1, :]

        def per_tile(st, carry2):
            j = (st * SLOT_TILE + lax.broadcasted_iota(jnp.int32, (SLOT_TILE, 1), 0)).astype(F32)
            onehot = ((off_l <= j) & (j < off_l + tot_l)).astype(BF16)
            got = dot(onehot, table)
            local = j - (64.0 * got[:, LANES:2 * LANES] + got[:, 2 * LANES:3 * LANES])
            lane_idx = dot((got[:, :LANES] <= local).astype(BF16), ones)
            block = 16.0 * got[:, 3 * LANES:4 * LANES] + got[:, 4 * LANES:5 * LANES]
            k = j - capacity
            seq = sum((k >= PAD_ROWS * i).astype(F32) for i in range(1, SLOT_TILE // PAD_ROWS + 1))
            pad_tok = seq * (seq_len - PAD_ROWS) + pad_start + k
            tok = jnp.where(j < capacity, block * LANES + lane_idx, pad_tok)
            idx_ref[e, st] = tok.T[0:1, :].astype(jnp.int32)
            return carry2

        lax.fori_loop(0, n_slot_tiles, per_tile, 0)
        return carry

    lax.fori_loop(0, N_EXPERTS, per_expert, 0)


def _select(aff_blocks, order, capacity, n_slot_tiles, seq_len, pad_start):
    nbp = aff_blocks.shape[1]
    return pl.pallas_call(
        functools.partial(_select_body, capacity=capacity, n_slot_tiles=n_slot_tiles,
                          seq_len=seq_len, pad_start=pad_start),
        grid=(1,),
        in_specs=[_const_spec(N_EXPERTS, nbp, LANES), _const_spec(nbp, nbp)],
        out_specs=_const_spec(N_EXPERTS, n_slot_tiles, 1, SLOT_TILE),
        out_shape=jax.ShapeDtypeStruct((N_EXPERTS, n_slot_tiles, 1, SLOT_TILE), jnp.int32),
        scratch_shapes=[pltpu.VMEM((N_EXPERTS, SUBLANES, LANES), jnp.int32)],
        compiler_params=_cparams(("arbitrary",)),
        name="select",
    )(aff_blocks, order)


def _moe_body(idx_ref, idx_next_ref, h1_hbm, wg_ref, wu_ref, wd_ref, wr_ref, acc_in, acc_hbm,
              xbuf, abuf, hid_ref, sem_x, sem_a, sem_s):
    del acc_in
    T = xbuf.shape[1]
    n_c = pl.num_programs(1)
    step = pl.program_id(0) * n_c + pl.program_id(1)
    n_steps = pl.num_programs(0) * n_c
    slot = step % 2

    def start_rows(copy_of_row, idx):
        def body(j, carry):
            copy_of_row(idx[0, 0, 0, j], j).start()
            return carry
        lax.fori_loop(0, T, body, 0, unroll=8)

    def x_row(buf):
        return lambda row, j: pltpu.make_async_copy(h1_hbm.at[row], xbuf.at[buf, j], sem_x.at[buf])

    def acc_row_in(row, j):
        return pltpu.make_async_copy(acc_hbm.at[row], abuf.at[j], sem_a)

    def acc_row_out(row, j):
        return pltpu.make_async_copy(abuf.at[j], acc_hbm.at[row], sem_s)

    def wait_scatter():
        pltpu.make_async_copy(abuf, acc_hbm.at[pl.ds(0, T)], sem_s).wait()

    @pl.when(step == 0)
    def _():
        start_rows(x_row(0), idx_ref)

    @pl.when(step + 1 < n_steps)
    def _():
        start_rows(x_row(1 - slot), idx_next_ref)

    pltpu.make_async_copy(h1_hbm.at[pl.ds(0, T)], xbuf.at[slot], sem_x.at[slot]).wait()
    packed = [xbuf[slot, :, s, :] for s in range(ROW_SLABS // 2)]
    lo = [pltpu.bitcast(lax.shift_left(u, jnp.uint32(16)), F32) for u in packed]
    hi = [pltpu.bitcast(u & jnp.uint32(0xFFFF0000), F32) for u in packed]
    x = jnp.concatenate(lo + hi, axis=1).astype(BF16)
    logits = jnp.dot(x, wr_ref[...], preferred_element_type=F32)
    lane = lax.broadcasted_iota(jnp.int32, logits.shape, 1)
    logits = jnp.where(lane < N_EXPERTS, logits, NEG_BIG)
    p = jnp.exp(logits - jnp.max(logits, axis=-1, keepdims=True))
    gate = (jnp.sum(jnp.where(lane == pl.program_id(0), p, 0.0), axis=-1, keepdims=True)
            / jnp.sum(p, axis=-1, keepdims=True))
    for f in range(EXPERT_FF // FF_TILE):
        fs = slice(f * FF_TILE, (f + 1) * FF_TILE)
        hg = jnp.dot(x, wg_ref[0, :, fs], preferred_element_type=F32)
        hu = jnp.dot(x, wu_ref[0, :, fs], preferred_element_type=F32)
        hid_ref[:, fs] = (jax.nn.silu(hg) * hu).astype(BF16)
        if f == 0:
            @pl.when(step > 0)
            def _():
                wait_scatter()

            start_rows(acc_row_in, idx_ref)
    pltpu.make_async_copy(acc_hbm.at[pl.ds(0, T)], abuf, sem_a).wait()
    for n0 in range(0, D_MODEL, 2 * LANES):
        y = jnp.dot(hid_ref[...], wd_ref[0, :, n0:n0 + 2 * LANES], preferred_element_type=F32) * gate
        for h in range(2):
            s = n0 // LANES + h
            abuf[:, s, :] = abuf[:, s, :] + y[:, h * LANES:(h + 1) * LANES]
    start_rows(acc_row_out, idx_ref)

    @pl.when(step == n_steps - 1)
    def _():
        wait_scatter()


def _moe(idx, h1, acc, wg, wu, wd, wr):
    n_slots = idx.shape[1] * idx.shape[3]
    n_c = next(n for n in range(1, n_slots) if n_slots % n == 0 and n_slots // n <= MOE_TILE_MAX
               and (n_slots // n) % 16 == 0)
    T = n_slots // n_c
    idx = idx.reshape(N_EXPERTS, n_c, 1, T)

    def next_block(e, c):
        return (jnp.minimum(e + (c + 1) // n_c, N_EXPERTS - 1), (c + 1) % n_c, 0, 0)

    return pl.pallas_call(
        _moe_body,
        grid=(N_EXPERTS, n_c),
        in_specs=[
            pl.BlockSpec((1, 1, 1, T), lambda e, c: (e, c, 0, 0), memory_space=pltpu.SMEM),
            pl.BlockSpec((1, 1, 1, T), next_block, memory_space=pltpu.SMEM),
            pl.BlockSpec(memory_space=pl.ANY),
            pl.BlockSpec((1, D_MODEL, EXPERT_FF), lambda e, c: (e, 0, 0)),
            pl.BlockSpec((1, D_MODEL, EXPERT_FF), lambda e, c: (e, 0, 0)),
            pl.BlockSpec((1, EXPERT_FF, D_MODEL), lambda e, c: (e, 0, 0)),
            _const_spec(D_MODEL, LANES),
            pl.BlockSpec(memory_space=pl.ANY),
        ],
        out_specs=pl.BlockSpec(memory_space=pl.ANY),
        out_shape=jax.ShapeDtypeStruct(acc.shape, F32),
        scratch_shapes=[pltpu.VMEM((2, T, ROW_SLABS // 2, LANES), jnp.uint32),
                        pltpu.VMEM((T, ROW_SLABS, LANES), F32),
                        pltpu.VMEM((T, EXPERT_FF), BF16),
                        pltpu.SemaphoreType.DMA((2,)), pltpu.SemaphoreType.DMA(()),
                        pltpu.SemaphoreType.DMA(())],
        input_output_aliases={7: 0},
        compiler_params=pltpu.CompilerParams(dimension_semantics=("arbitrary", "arbitrary"),
                                             vmem_limit_bytes=VMEM_LIMIT,
                                             disable_bounds_checks=True),
        name="expert_ffn",
    )(idx, idx, h1, wg, wu, wd, wr, acc)


def _final_body(a_ref, g_ref, b_ref, o_ref):
    x = jnp.concatenate([a_ref[0, :, s, :] for s in range(ROW_SLABS)], axis=1)
    o_ref[0] = _layer_norm_rows(x, g_ref[...], b_ref[...])


def _final_norm(acc, g, b, L):
    B = acc.shape[0]
    TM = EMBED_TILE
    return pl.pallas_call(
        _final_body,
        grid=(B, L // TM),
        in_specs=[pl.BlockSpec((1, TM, ROW_SLABS, LANES), lambda b_, j: (b_, j, 0, 0)),
                  _const_spec(1, D_MODEL), _const_spec(1, D_MODEL)],
        out_specs=pl.BlockSpec((1, TM, D_MODEL), lambda b_, j: (b_, j, 0)),
        out_shape=jax.ShapeDtypeStruct((B, L, D_MODEL), F32),
        compiler_params=_cparams(("parallel", "parallel")),
        name="final_norm",
    )(acc, g, b)


def _rope_tables(L, Lp):
    rows = L // GRID_W
    t = jnp.arange(L, dtype=jnp.int32)
    m = jnp.arange(N_META, dtype=jnp.int32)
    pad = jnp.zeros((Lp - L - N_META,), jnp.int32)
    row = jnp.concatenate([t // GRID_W - rows // 2, pad,
                           jnp.full((N_META,), -(rows // 2) - 1, jnp.int32)])
    col = jnp.concatenate([t % GRID_W - GRID_W // 2, pad, m - GRID_W // 2])
    inv_freq = ROPE_THETA ** (-jnp.arange(0, ROPE_AXIS_DIM, 2, dtype=F32) / ROPE_AXIS_DIM)
    ang_r = row.astype(F32)[:, None] * inv_freq
    ang_c = col.astype(F32)[:, None] * inv_freq
    cos = jnp.concatenate([jnp.cos(ang_r)] * 2 + [jnp.cos(ang_c)] * 2, axis=1)
    sin = jnp.concatenate([-jnp.sin(ang_r), jnp.sin(ang_r), -jnp.sin(ang_c), jnp.sin(ang_c)], axis=1)
    return jnp.tile(cos, (1, 2)), jnp.tile(sin, (1, 2))


def _block_order(B, Lp, nbp):
    nbb = Lp // LANES
    r = jnp.arange(nbp)
    b, jb = r // nbb, r % nbb
    rank = jnp.where(r < B * nbb, b * nbb + jnp.where(jb == nbb - 1, 0, jb + 1), r)
    return (rank[None, :] < rank[:, None]).astype(BF16)


def _run_trunk(x, meta_pad, shared):
    B, L, _ = x.shape
    Lp = L + TAIL
    N = B * Lp
    cos, sin = _rope_tables(L, Lp)
    h0, q, k, vts, u = _embed(x, meta_pad, shared["ln_emb_g"], shared["ln_emb_b"], shared["w_in"],
                              shared["qg"], shared["kg"], shared["bones"], cos, sin, L, Lp)
    o = _attention(q, k, vts, L, Lp, shared["score_bound"])
    z = _ssm(u.reshape(N_PAIRS, N // CHUNK, PAIR_W), shared["ssm_w1"], shared["ssm_w2"],
             shared["ssm_dec"], B, Lp // CHUNK)
    h1, acc, aff = _mix(o.reshape(N, ATTN_WIDTH), z, h0.reshape(N, D_MODEL),
                        shared["w_glu"], shared["b_glu"], shared["w_out_attn"], shared["w_out_ssm"],
                        shared["ln1_g"], shared["ln1_b"], shared["w_router"], L, Lp)

    capacity = EC_CAPACITY_FACTOR * B * (L + N_META) // N_EXPERTS
    n_slot_tiles = -(-capacity // SLOT_TILE)
    nb = N // LANES
    nbp = -(-nb // LANES) * LANES
    aff_blocks = jnp.pad(aff.reshape(N_EXPERTS, nb, LANES), ((0, 0), (0, nbp - nb), (0, 0)),
                         constant_values=-1.0)
    assert n_slot_tiles * SLOT_TILE - capacity <= B * PAD_ROWS
    idx = _select(aff_blocks, _block_order(B, Lp, nbp), capacity, n_slot_tiles, Lp, L)
    acc = _moe(idx, h1, acc, shared["w_gate"], shared["w_up"], shared["w_down"], shared["w_router"])
    return _final_norm(acc.reshape(B, Lp, ROW_SLABS, LANES), shared["ln2_g"], shared["ln2_b"], L)


def kernel(x_prompt, x_sample, meta_tokens, ln_emb_g, ln_emb_b, w_in, q_norm_g, k_norm_g, ssm_lambda_re, ssm_lambda_im, ssm_log_dt, ssm_b_re, ssm_b_im, ssm_c_re, ssm_c_im, ssm_d, w_glu, b_glu, w_out, ln1_g, ln1_b, w_router, w_gate, w_up, w_down, ln2_g, ln2_b):
    row = lambda a: a.reshape(1, -1).astype(F32)
    w_q = w_in[0][:, :Q_END].reshape(D_MODEL, N_KV_HEADS, Q_PER_KV, HEAD_DIM)
    w_q = w_q.transpose(0, 2, 1, 3).reshape(D_MODEL, Q_END)
    w_oa = w_out[0][:ATTN_WIDTH].reshape(N_KV_HEADS, Q_PER_KV, HEAD_DIM, D_MODEL)
    w_oa = w_oa.transpose(1, 0, 2, 3).reshape(ATTN_WIDTH, D_MODEL)
    head_of = jnp.arange(Q_END) // HEAD_DIM
    ssm_w1, ssm_w2, ssm_dec = _ssm_weights(ssm_lambda_re[0], ssm_lambda_im[0], ssm_log_dt[0],
                                           ssm_b_re[0], ssm_b_im[0], ssm_c_re[0], ssm_c_im[0],
                                           ssm_d[0])
    shared = dict(
        ln_emb_g=row(ln_emb_g), ln_emb_b=row(ln_emb_b),
        w_in=jnp.concatenate([w_q, w_in[0][:, Q_END:]], axis=1).astype(BF16),
        qg=row(jnp.tile(q_norm_g[0], N_Q_HEADS)), kg=row(jnp.tile(k_norm_g[0], N_KV_HEADS)),
        bones=(head_of[:, None] == head_of[None, :]).astype(BF16),
        score_bound=(1.02 * HEAD_DIM * Q_PRESCALE * jnp.max(jnp.abs(q_norm_g[0]))
                     * jnp.max(jnp.abs(k_norm_g[0]))).astype(F32),
        ssm_w1=ssm_w1, ssm_w2=ssm_w2, ssm_dec=ssm_dec,
        w_glu=w_glu[0].astype(BF16), b_glu=row(b_glu[0]),
        w_out_attn=w_oa.astype(BF16), w_out_ssm=w_out[0][ATTN_WIDTH:].astype(BF16),
        ln1_g=row(ln1_g[0]), ln1_b=row(ln1_b[0]),
        w_router=jnp.pad(w_router[0], ((0, 0), (0, LANES - N_EXPERTS))).astype(BF16),
        w_gate=w_gate[0].astype(BF16), w_up=w_up[0].astype(BF16), w_down=w_down[0].astype(BF16),
        ln2_g=row(ln2_g[0]), ln2_b=row(ln2_b[0]),
    )
    meta_pad = jnp.pad(meta_tokens.astype(F32), ((PAD_ROWS, 0), (0, 0)))[None]
    return (_run_trunk(x_prompt, meta_pad, shared), _run_trunk(x_sample, meta_pad, shared))
```

```python
import functools
import math

import jax
import jax.numpy as jnp
from jax import lax
from jax.experimental import pallas as pl
from jax.experimental.pallas import tpu as pltpu

F32 = jnp.float32
BF16 = jnp.bfloat16

D_MODEL = 1024
N_META = 16
GRID_W = 64
ATTN_WIDTH = 512
SSM_WIDTH = 512
HEAD_DIM = 64
N_Q_HEADS = 8
N_KV_HEADS = 2
Q_PER_KV = 4
KV_WIDTH = 128
ROPE_AXIS_DIM = 32
ROPE_THETA = 10000.0
ATTN_SCALE = HEAD_DIM ** -0.5
RMS_EPS = 1e-6
SSM_GROUP = 16
N_SSM_GROUPS = 32
SSM_STATE = 64
Q_END = ATTN_WIDTH
K_END = Q_END + KV_WIDTH
V_END = K_END + KV_WIDTH
IN_WIDTH = V_END + SSM_WIDTH
N_EXPERTS = 16
EXPERT_FF = 2048
EC_CAPACITY_FACTOR = 2
LN_EPS = 1e-5
DEPTH = 1
DEEPNORM_ALPHA = (2 * DEPTH) ** 0.25

LANES = 128
SUBLANES = 8
ROW_SLABS = D_MODEL // LANES
TAIL = LANES
PAD_ROWS = TAIL - N_META
CHUNK = 16
N_PAIRS = N_SSM_GROUPS // 2
PAIR_W = 2 * CHUNK * SSM_GROUP
PAIR_STATE = 2 * SSM_STATE
EMBED_TILE = 512
ROW_TILE = 384
Q_TILE = ROW_TILE
KV_TILE = EMBED_TILE
KEY_SUBTILE = 256
KV_UNROLL = 4
SLOT_TILE = 384
MOE_TILE_MAX = 1056
FF_TILE = 512
SSM_ROW_TILE = 256
DEC_ROWS = 48
VT_ROWS = HEAD_DIM + 16
Q_PRESCALE = ATTN_SCALE * math.log2(math.e)
NEG_BIG = -1e30
SAFE_EXP2_RANGE = 60.0
VMEM_LIMIT = 56 * 1024 * 1024


def _cparams(sem):
    return pltpu.CompilerParams(dimension_semantics=sem, vmem_limit_bytes=VMEM_LIMIT)


def _const_spec(*shape):
    return pl.BlockSpec(shape, lambda *idx: (0,) * len(shape))


def _layer_norm_rows(x, g, b):
    mu = jnp.mean(x, axis=-1, keepdims=True)
    xc = x - mu
    var = jnp.mean(xc * xc, axis=-1, keepdims=True)
    return xc * lax.rsqrt(var + LN_EPS) * g + b


def _head_rms(t, gain, bones):
    sq = t * t
    hi = sq.astype(BF16)
    lo = (sq - hi.astype(F32)).astype(BF16)
    ss = (jnp.dot(hi, bones, preferred_element_type=F32)
          + jnp.dot(lo, bones, preferred_element_type=F32))
    return t * lax.rsqrt(ss * (1.0 / HEAD_DIM) + RMS_EPS) * gain


def _rope_slab(t, cos, sin_signed):
    lane = lax.broadcasted_iota(jnp.int32, t.shape, 1)
    first = (lane % ROPE_AXIS_DIM) < (ROPE_AXIS_DIM // 2)
    partner = jnp.where(first, pltpu.roll(t, LANES - 16, 1), pltpu.roll(t, 16, 1))
    return t * cos + partner * sin_signed


def _lane_piece_gather(pieces, lane16):
    out = None
    for j, (arr, src) in enumerate(pieces):
        shift = (SSM_GROUP * j - src) % LANES
        moved = pltpu.roll(arr, shift, 1) if shift else arr
        out = moved if out is None else jnp.where(lane16 == j, moved, out)
    return out


def _embed_body(x_ref, g_ref, b_ref, w_ref, qg_ref, kg_ref, bones_ref, cos_ref, sin_ref,
                h_ref, q_ref, k_ref, vt0_ref, vt1_ref, u_ref, u_scr):
    h = _layer_norm_rows(x_ref[0], g_ref[...], b_ref[...])
    h_ref[0] = h
    proj = jnp.dot(h.astype(BF16), w_ref[...], preferred_element_type=F32)
    cos = cos_ref[...]
    sin = sin_ref[...]
    qn = _head_rms(proj[:, :Q_END], qg_ref[...], bones_ref[...])
    for s in range(ATTN_WIDTH // LANES):
        sl = slice(s * LANES, (s + 1) * LANES)
        q_ref[0, :, sl] = (_rope_slab(qn[:, sl], cos, sin) * Q_PRESCALE).astype(BF16)
    kn = _head_rms(proj[:, Q_END:K_END], kg_ref[...], bones_ref[:KV_WIDTH, :KV_WIDTH])
    k_ref[0] = _rope_slab(kn, cos, sin).astype(BF16)
    vt = proj[:, K_END:V_END].T
    ones = jnp.ones((VT_ROWS - HEAD_DIM, vt.shape[1]), F32)
    vt0_ref[0, 0] = jnp.concatenate([vt[:HEAD_DIM], ones], axis=0).astype(BF16)
    vt1_ref[0, 0] = jnp.concatenate([vt[HEAD_DIM:], ones], axis=0).astype(BF16)
    n_rows = u_scr.shape[1] // CHUNK
    for v in range(SSM_WIDTH // LANES):
        u_scr[v] = proj[:, V_END + v * LANES:V_END + (v + 1) * LANES]
    by_token = [[u_scr[v, pl.ds(t, n_rows, stride=CHUNK), :] for v in range(SSM_WIDTH // LANES)]
                for t in range(CHUNK)]
    lane16 = lax.broadcasted_iota(jnp.int32, (n_rows, LANES), 1) // SSM_GROUP
    for p in range(N_PAIRS):
        for gi in range(2):
            src = (p % 4) * 2 * SSM_GROUP + gi * SSM_GROUP
            for h in range(CHUNK // SUBLANES):
                pieces = [(by_token[SUBLANES * h + j][p // 4], src) for j in range(SUBLANES)]
                lo = gi * CHUNK * SSM_GROUP + h * LANES
                u_ref[p, 0, :, lo:lo + LANES] = _lane_piece_gather(pieces, lane16)


def _embed_tail_body(x_ref, g_ref, b_ref, w_ref, qg_ref, kg_ref, bones_ref, cos_ref, sin_ref,
                     h_in, q_in, k_in, u_in, h_ref, q_ref, k_ref, vt0_ref, vt1_ref, u_ref, u_scr):
    del h_in, q_in, k_in, u_in
    _embed_body(x_ref, g_ref, b_ref, w_ref, qg_ref, kg_ref, bones_ref, cos_ref, sin_ref,
                h_ref, q_ref, k_ref, vt0_ref, vt1_ref, u_ref, u_scr)


def _embed(x, meta_pad, ln_g, ln_b, w_in, qg, kg, bones, cos, sin, L, Lp):
    B = x.shape[0]
    TM = EMBED_TILE
    nj = L // TM
    w_specs = [_const_spec(1, D_MODEL), _const_spec(1, D_MODEL), _const_spec(D_MODEL, IN_WIDTH),
               _const_spec(1, Q_END), _const_spec(1, KV_WIDTH), _const_spec(Q_END, Q_END)]

    def out_shapes(vt_tiles, vt_width):
        vt = jax.ShapeDtypeStruct((B, vt_tiles, VT_ROWS, vt_width), BF16)
        return [
            jax.ShapeDtypeStruct((B, Lp, D_MODEL), F32),
            jax.ShapeDtypeStruct((B, Lp, Q_END), BF16),
            jax.ShapeDtypeStruct((B, Lp, KV_WIDTH), BF16),
            vt, vt,
            jax.ShapeDtypeStruct((N_PAIRS, B, Lp // CHUNK, PAIR_W), F32),
        ]

    def out_specs(tm, row_block):
        return [
            pl.BlockSpec((1, tm, D_MODEL), lambda b, j: (b, row_block(j), 0)),
            pl.BlockSpec((1, tm, Q_END), lambda b, j: (b, row_block(j), 0)),
            pl.BlockSpec((1, tm, KV_WIDTH), lambda b, j: (b, row_block(j), 0)),
            pl.BlockSpec((1, 1, VT_ROWS, tm), lambda b, j: (b, j, 0, 0)),
            pl.BlockSpec((1, 1, VT_ROWS, tm), lambda b, j: (b, j, 0, 0)),
            pl.BlockSpec((N_PAIRS, 1, tm // CHUNK, PAIR_W), lambda b, j: (0, b, row_block(j), 0)),
        ]

    main = pl.pallas_call(
        _embed_body,
        grid=(B, nj),
        in_specs=[pl.BlockSpec((1, TM, D_MODEL), lambda b, j: (b, j, 0))] + w_specs
        + [pl.BlockSpec((TM, LANES), lambda b, j: (j, 0))] * 2,
        out_specs=out_specs(TM, lambda j: j),
        out_shape=out_shapes(nj, TM),
        scratch_shapes=[pltpu.VMEM((SSM_WIDTH // LANES, TM, LANES), F32)],
        compiler_params=_cparams(("parallel", "parallel")),
        name="embed_main",
    )
    h0, q, k, vt0, vt1, u = main(x, ln_g, ln_b, w_in, qg, kg, bones, cos, sin)

    jt = Lp // TAIL - 1
    tail = pl.pallas_call(
        _embed_tail_body,
        grid=(B, 1),
        in_specs=[pl.BlockSpec((1, TAIL, D_MODEL), lambda b, j: (0, 0, 0))] + w_specs
        + [pl.BlockSpec((TAIL, LANES), lambda b, j: (jt, 0))] * 2
        + [pl.BlockSpec(memory_space=pl.ANY)] * 4,
        out_specs=out_specs(TAIL, lambda j: jt),
        out_shape=out_shapes(1, TAIL),
        input_output_aliases={9: 0, 10: 1, 11: 2, 12: 5},
        scratch_shapes=[pltpu.VMEM((SSM_WIDTH // LANES, TAIL, LANES), F32)],
        compiler_params=_cparams(("parallel", "arbitrary")),
        name="embed_tail",
    )
    h0, q, k, vt0_tail, vt1_tail, u = tail(meta_pad, ln_g, ln_b, w_in, qg, kg, bones, cos, sin,
                                           h0, q, k, u)
    return h0, q, k, (vt0, vt1, vt0_tail, vt1_tail), u


def _attn_body(q_ref, k_ref, vt0_ref, vt1_ref, vt0t_ref, vt1t_ref, o_ref,
               qt_ref, s0_ref, s1_ref, mc0_ref, mc1_ref, m_ref, acc_ref, *, L, n_kv_tiles):
    cols = Q_PER_KV * Q_TILE
    _attn_load_qt(q_ref, qt_ref)
    m_ref[...] = jnp.full(m_ref.shape, NEG_BIG, F32)
    acc_ref[...] = jnp.zeros(acc_ref.shape, F32)

    s_bufs = ((s0_ref, mc0_ref), (s1_ref, mc1_ref))

    def scores(slot, k_tile, first_valid=0):
        s_ref, mc_ref = s_bufs[slot]
        n = k_tile.shape[0]
        for g in range(N_KV_HEADS):
            s = jnp.dot(k_tile, qt_ref[g], preferred_element_type=F32)
            if first_valid:
                key = lax.broadcasted_iota(jnp.int32, s.shape, 0)
                s = jnp.where(key >= first_valid, s, NEG_BIG)
            s_ref[g, 0:n, :] = s
            mc_ref[g] = jnp.broadcast_to(jnp.max(s, axis=0, keepdims=True), (SUBLANES, cols))

    def softmax_values(slot, vt_tiles, n):
        s_ref, mc_ref = s_bufs[slot]
        for g in range(N_KV_HEADS):
            m_old = m_ref[g]
            m_new = jnp.maximum(m_old, mc_ref[g])
            alpha = jnp.exp2(m_old[0:1, :] - m_new[0:1, :])
            acc = alpha * acc_ref[g]
            for k0 in range(0, n, KEY_SUBTILE):
                k1 = min(k0 + KEY_SUBTILE, n)
                p = jnp.exp2(s_ref[g, k0:k1, :] - m_new[0:1, :]).astype(BF16)
                acc = acc + jnp.dot(vt_tiles[g][:, k0:k1], p, preferred_element_type=F32)
            acc_ref[g] = acc
            m_ref[g] = m_new

    def k_tile(i):
        return k_ref[0, pl.ds(pl.multiple_of(i * KV_TILE, KV_TILE), KV_TILE), :]

    def vt_tiles(i):
        return vt0_ref[0, i], vt1_ref[0, i]

    scores(0, k_tile(0))

    def run_tiles(first, count, next_scores):
        for t in range(count):
            softmax_values(t % 2, vt_tiles(first + t), KV_TILE)
            if t + 1 < count:
                scores((t + 1) % 2, k_tile(first + t + 1))
            else:
                next_scores()

    def kv_group(i, carry):
        first = i * KV_UNROLL
        run_tiles(first, KV_UNROLL, lambda: scores(0, k_tile(first + KV_UNROLL)))
        return carry

    n_groups = n_kv_tiles // KV_UNROLL
    lax.fori_loop(0, n_groups - 1, kv_group, 0)
    run_tiles((n_groups - 1) * KV_UNROLL, KV_UNROLL,
              lambda: scores(0, k_ref[0, L:L + TAIL, :], PAD_ROWS))
    softmax_values(0, (vt0t_ref[0, 0], vt1t_ref[0, 0]), TAIL)

    _attn_store_out(acc_ref, o_ref)


def _attn_bounded_body(q_ref, k_ref, vt0_ref, vt1_ref, vt0t_ref, vt1t_ref, o_ref,
                       qt_ref, p0_ref, p1_ref, acc_ref, *, L, n_kv_tiles):
    _attn_load_qt(q_ref, qt_ref)
    acc_ref[...] = jnp.zeros(acc_ref.shape, F32)
    p_bufs = (p0_ref, p1_ref)

    def key_blocks(k_tile, vt_tiles, accs, first_valid=0):
        n = k_tile.shape[0]
        accs = list(accs)
        for i, k0 in enumerate(range(0, n, KEY_SUBTILE)):
            k1 = min(k0 + KEY_SUBTILE, n)
            p_ref = p_bufs[i % 2]
            for g in range(N_KV_HEADS):
                s = jnp.dot(k_tile[k0:k1], qt_ref[g], preferred_element_type=F32)
                if first_valid:
                    key = k0 + lax.broadcasted_iota(jnp.int32, s.shape, 0)
                    s = jnp.where(key >= first_valid, s, NEG_BIG)
                p_ref[g, 0:k1 - k0, :] = jnp.exp2(s).astype(BF16)
            for g in range(N_KV_HEADS):
                accs[g] = accs[g] + jnp.dot(vt_tiles[g][:, k0:k1], p_ref[g, 0:k1 - k0, :],
                                            preferred_element_type=F32)
        return accs

    def kv_group(i, carry):
        accs = [acc_ref[g] for g in range(N_KV_HEADS)]
        for t in range(KV_UNROLL):
            tile = i * KV_UNROLL + t
            start = pl.multiple_of(tile * KV_TILE, KV_TILE)
            accs = key_blocks(k_ref[0, pl.ds(start, KV_TILE), :], (vt0_ref[0, tile], vt1_ref[0, tile]),
                              accs)
        for g in range(N_KV_HEADS):
            acc_ref[g] = accs[g]
        return carry

    lax.fori_loop(0, n_kv_tiles // KV_UNROLL, kv_group, 0)
    accs = key_blocks(k_ref[0, L:L + TAIL, :], (vt0t_ref[0, 0], vt1t_ref[0, 0]),
                      [acc_ref[g] for g in range(N_KV_HEADS)], PAD_ROWS)
    for g in range(N_KV_HEADS):
        acc_ref[g] = accs[g]
    _attn_store_out(acc_ref, o_ref)


def _attn_load_qt(q_ref, qt_ref):
    tq = Q_TILE
    row = lax.broadcasted_iota(jnp.int32, (KV_WIDTH, tq), 0)
    for r in range(Q_PER_KV):
        slab_t = q_ref[0, :, r * LANES:(r + 1) * LANES].astype(F32).T
        for g in range(N_KV_HEADS):
            in_group = (row >= g * HEAD_DIM) & (row < (g + 1) * HEAD_DIM)
            qt_ref[g, :, r * tq:(r + 1) * tq] = jnp.where(in_group, slab_t, 0.0).astype(BF16)


def _attn_store_out(acc_ref, o_ref):
    tq = Q_TILE
    comb = jnp.concatenate(
        [acc_ref[g, 0:HEAD_DIM, :] / acc_ref[g, HEAD_DIM:HEAD_DIM + 1, :] for g in range(N_KV_HEADS)],
        axis=0)
    for r in range(Q_PER_KV):
        o_ref[0, :, r * LANES:(r + 1) * LANES] = comb[:, r * tq:(r + 1) * tq].T.astype(BF16)


def _attention(q, k, vts, L, Lp, score_bound):
    B = q.shape[0]
    nkt = L // KV_TILE
    assert nkt % KV_UNROLL == 0
    cols = Q_PER_KV * Q_TILE
    vt_spec = pl.BlockSpec((1, nkt, VT_ROWS, KV_TILE), lambda b, j: (b, 0, 0, 0))
    vt_tail_spec = pl.BlockSpec((1, 1, VT_ROWS, TAIL), lambda b, j: (b, 0, 0, 0))
    call = functools.partial(
        pl.pallas_call,
        grid=(B, Lp // Q_TILE),
        in_specs=[
            pl.BlockSpec((1, Q_TILE, Q_END), lambda b, j: (b, j, 0)),
            pl.BlockSpec((1, Lp, KV_WIDTH), lambda b, j: (b, 0, 0)),
            vt_spec, vt_spec, vt_tail_spec, vt_tail_spec,
        ],
        out_specs=pl.BlockSpec((1, Q_TILE, Q_END), lambda b, j: (b, j, 0)),
        out_shape=jax.ShapeDtypeStruct((B, Lp, Q_END), BF16),
        compiler_params=_cparams(("parallel", "arbitrary")),
    )
    bounded = call(
        functools.partial(_attn_bounded_body, L=L, n_kv_tiles=nkt),
        scratch_shapes=[
            pltpu.VMEM((N_KV_HEADS, KV_WIDTH, cols), BF16),
            pltpu.VMEM((N_KV_HEADS, KEY_SUBTILE, cols), BF16),
            pltpu.VMEM((N_KV_HEADS, KEY_SUBTILE, cols), BF16),
            pltpu.VMEM((N_KV_HEADS, VT_ROWS, cols), F32),
        ],
        name="attention_bounded",
    )
    general = call(
        functools.partial(_attn_body, L=L, n_kv_tiles=nkt),
        scratch_shapes=[
            pltpu.VMEM((N_KV_HEADS, KV_WIDTH, cols), BF16),
            pltpu.VMEM((N_KV_HEADS, KV_TILE, cols), F32),
            pltpu.VMEM((N_KV_HEADS, KV_TILE, cols), F32),
            pltpu.VMEM((N_KV_HEADS, SUBLANES, cols), F32),
            pltpu.VMEM((N_KV_HEADS, SUBLANES, cols), F32),
            pltpu.VMEM((N_KV_HEADS, SUBLANES, cols), F32),
            pltpu.VMEM((N_KV_HEADS, VT_ROWS, cols), F32),
        ],
        name="attention",
    )
    return lax.cond(score_bound <= SAFE_EXP2_RANGE, bounded, general, q, k, *vts)


def _ssm_weights(lam_re, lam_im, log_dt, b_re, b_im, c_re, c_im, d_skip):
    hp = lax.Precision.HIGHEST
    dt = jnp.exp(log_dt.astype(F32))[..., None]
    lr = lam_re.astype(F32)
    li = lam_im.astype(F32)

    def apow(n):
        n = jnp.asarray(n, F32)
        mag = jnp.exp(lr[..., None] * dt[..., None] * n)
        ang = li[..., None] * dt[..., None] * n
        return mag * jnp.cos(ang), mag * jnp.sin(ang)

    a1r, a1i = apow(jnp.ones((1,), F32))
    a1r, a1i = a1r[..., 0], a1i[..., 0]
    nr = a1r - 1.0
    den = lr * lr + li * li
    f_r = (nr * lr + a1i * li) / den
    f_i = (a1i * lr - nr * li) / den
    br = b_re.astype(F32)
    bi = b_im.astype(F32)
    bb_r = f_r[..., None] * br - f_i[..., None] * bi
    bb_i = f_r[..., None] * bi + f_i[..., None] * br
    cr = c_re.astype(F32)
    ci = c_im.astype(F32)

    tau = jnp.arange(CHUNK + 1, dtype=F32)
    pr, pi = apow(tau)
    ab_r = pr[..., None] * bb_r[:, :, :, None, :] - pi[..., None] * bb_i[:, :, :, None, :]
    ab_i = pr[..., None] * bb_i[:, :, :, None, :] + pi[..., None] * bb_r[:, :, :, None, :]
    kern = (jnp.einsum("dgop,dgptc->dgtco", cr, ab_r, precision=hp)
            - jnp.einsum("dgop,dgptc->dgtco", ci, ab_i, precision=hp))
    s_idx = jnp.arange(CHUNK)[:, None]
    t_idx = jnp.arange(CHUNK)[None, :]
    lag_f = jnp.clip(t_idx - s_idx, 0, CHUNK)
    lag_b = jnp.clip(s_idx - t_idx, 0, CHUNK)
    m_f = jnp.where((t_idx >= s_idx)[..., None, None], kern[0][:, lag_f], 0.0)
    m_b = jnp.where((s_idx >= t_idx)[..., None, None], kern[1][:, lag_b], 0.0)
    eye_t = jnp.eye(CHUNK, dtype=F32)[None, :, :, None, None]
    eye_c = jnp.eye(SSM_GROUP, dtype=F32)[None, None, None]
    dsk = d_skip.astype(F32).reshape(N_SSM_GROUPS, 1, 1, SSM_GROUP, 1)
    m_all = m_f + m_b + eye_t * eye_c * dsk
    m_all = m_all.transpose(0, 1, 3, 2, 4).reshape(N_SSM_GROUPS, 256, 256)

    def state_in(d, expo):
        er, ei = pr[d][..., expo], pi[d][..., expo]
        wr = er[..., None] * bb_r[d][:, :, None, :] - ei[..., None] * bb_i[d][:, :, None, :]
        wi = er[..., None] * bb_i[d][:, :, None, :] + ei[..., None] * bb_r[d][:, :, None, :]
        to_rows = lambda w: w.transpose(0, 2, 3, 1).reshape(N_SSM_GROUPS, 256, SSM_STATE)
        return to_rows(wr), to_rows(wi)

    sf_r, sf_i = state_in(0, jnp.arange(CHUNK - 1, -1, -1))
    sb_r, sb_i = state_in(1, jnp.arange(CHUNK))

    def state_out(d, expo):
        er, ei = pr[d][..., expo], pi[d][..., expo]
        wr = cr[d].transpose(0, 2, 1)[:, :, None, :] * er[..., None] \
            - ci[d].transpose(0, 2, 1)[:, :, None, :] * ei[..., None]
        wi = cr[d].transpose(0, 2, 1)[:, :, None, :] * ei[..., None] \
            + ci[d].transpose(0, 2, 1)[:, :, None, :] * er[..., None]
        flat = lambda w: w.reshape(N_SSM_GROUPS, SSM_STATE, 256)
        return flat(wr), flat(-wi)

    of_r, of_i = state_out(0, jnp.arange(1, CHUNK + 1))
    ob_r, ob_i = state_out(1, jnp.arange(CHUNK, 0, -1))

    def pair_rows(w):
        return w.reshape(N_PAIRS, 2, *w.shape[1:])

    z256 = jnp.zeros((N_PAIRS, 256, 256), F32)
    mp = pair_rows(m_all)
    w1_y = jnp.concatenate([jnp.concatenate([mp[:, 0], z256], axis=2),
                            jnp.concatenate([z256, mp[:, 1]], axis=2)], axis=1)

    def pair_cols_in(w):
        wp = pair_rows(w)
        z = jnp.zeros_like(wp[:, 0])
        return jnp.concatenate([jnp.concatenate([wp[:, 0], z], axis=2),
                                jnp.concatenate([z, wp[:, 1]], axis=2)], axis=1)

    w1 = jnp.concatenate([w1_y, pair_cols_in(sf_r), pair_cols_in(sf_i),
                          pair_cols_in(sb_r), pair_cols_in(sb_i)], axis=2)

    def pair_rows_out(w):
        wp = pair_rows(w)
        z = jnp.zeros_like(wp[:, 0])
        return jnp.concatenate([jnp.concatenate([wp[:, 0], z], axis=2),
                                jnp.concatenate([z, wp[:, 1]], axis=2)], axis=1)

    w2 = jnp.concatenate([pair_rows_out(of_r), pair_rows_out(of_i),
                          pair_rows_out(ob_r), pair_rows_out(ob_i)], axis=1)

    qr, qi = apow(CHUNK * jnp.arange(SUBLANES + 1, dtype=F32))
    zero = jnp.zeros_like(qr[..., 0])

    def dec_rows(d, q_expo):
        rows = [qr[d][..., 1], qi[d][..., 1], qr[d][..., 2], qi[d][..., 2],
                qr[d][..., 4], qi[d][..., 4], zero[d], zero[d]]
        rows += [qr[d][..., n] for n in q_expo] + [qi[d][..., n] for n in q_expo]
        return jnp.stack(rows, axis=1)

    dec = jnp.concatenate([dec_rows(0, range(1, SUBLANES + 1)),
                           dec_rows(1, range(SUBLANES, 0, -1))], axis=1)
    dec = dec.reshape(N_PAIRS, 2, DEC_ROWS, SSM_STATE).transpose(0, 2, 1, 3)
    dec = dec.reshape(N_PAIRS, DEC_ROWS, PAIR_STATE)
    return w1.astype(BF16), w2.astype(BF16), dec


def _ssm_body(u_ref, w1_ref, w2_ref, dec_ref, z_ref, sh_ref, *, B, n_chunks):
    R = B * n_chunks
    tiles = [(r0, min(SSM_ROW_TILE, R - r0)) for r0 in range(0, R, SSM_ROW_TILE)]
    PS = PAIR_STATE
    n_blocks = n_chunks // SUBLANES

    for r0, n in tiles:
        t = jnp.dot(u_ref[0, r0:r0 + n, :].astype(BF16), w1_ref[0], preferred_element_type=F32)
        z_ref[0, r0:r0 + n, :] = t[:, :PAIR_W]
        sh_ref[r0:r0 + n, :] = t[:, PAIR_W:]

    sub = lax.broadcasted_iota(jnp.int32, (SUBLANES, PS), 0)

    def cmul(ar, ai, xr, xi):
        return ar * xr - ai * xi, ar * xi + ai * xr

    def tile_scan(xr, xi, base, reverse):
        for k, d in enumerate((1, 2, 4)):
            pr_ = dec_ref[0, base + 2 * k:base + 2 * k + 1, :]
            pi_ = dec_ref[0, base + 2 * k + 1:base + 2 * k + 2, :]
            keep = (sub < SUBLANES - d) if reverse else (sub >= d)
            shift = SUBLANES - d if reverse else d
            yr = jnp.where(keep, pltpu.roll(xr, shift, 0), 0.0)
            yi = jnp.where(keep, pltpu.roll(xi, shift, 0), 0.0)
            mr, mi = cmul(pr_, pi_, yr, yi)
            xr, xi = xr + mr, xi + mi
        return xr, xi

    def scan_tile(i, carry):
        rows = pl.ds(pl.multiple_of(i * SUBLANES, SUBLANES), SUBLANES)
        xr, xi = tile_scan(sh_ref[rows, 0:PS], sh_ref[rows, PS:2 * PS], 0, False)
        sh_ref[rows, 0:PS] = xr
        sh_ref[rows, PS:2 * PS] = xi
        xr, xi = tile_scan(sh_ref[rows, 2 * PS:3 * PS], sh_ref[rows, 3 * PS:4 * PS], DEC_ROWS // 2, True)
        sh_ref[rows, 2 * PS:3 * PS] = xr
        sh_ref[rows, 3 * PS:4 * PS] = xi
        return carry

    lax.fori_loop(0, R // SUBLANES, scan_tile, 0, unroll=2)

    def block_scan(xr, xi, cr, ci, base, reverse):
        qr_ = dec_ref[0, base + 8:base + 16, :]
        qi_ = dec_ref[0, base + 16:base + 24, :]
        mr, mi = cmul(qr_, qi_, cr, ci)
        xr, xi = xr + mr, xi + mi
        edge = SUBLANES - 1 if reverse else 0
        shift = SUBLANES - 1 if reverse else 1
        er = jnp.where(sub == edge, cr, pltpu.roll(xr, shift, 0))
        ei = jnp.where(sub == edge, ci, pltpu.roll(xi, shift, 0))
        last = 0 if reverse else SUBLANES - 1
        nr = jnp.broadcast_to(xr[last:last + 1, :], (SUBLANES, PS))
        ni = jnp.broadcast_to(xi[last:last + 1, :], (SUBLANES, PS))
        return er, ei, nr, ni

    def scan_step(i, carry):
        new = []
        for b in range(B):
            cfr, cfi, cbr, cbi = carry[4 * b:4 * b + 4]
            tile_f = (i + n_blocks - 1) % n_blocks
            tile_b = (2 * n_blocks - 2 - i) % n_blocks
            rf = pl.multiple_of(b * n_chunks + tile_f * SUBLANES, SUBLANES)
            rb = pl.multiple_of(b * n_chunks + tile_b * SUBLANES, SUBLANES)
            f_rows, b_rows = pl.ds(rf, SUBLANES), pl.ds(rb, SUBLANES)
            er, ei, cfr, cfi = block_scan(sh_ref[f_rows, 0:PS], sh_ref[f_rows, PS:2 * PS],
                                          cfr, cfi, 0, False)
            sh_ref[f_rows, 0:PS] = er
            sh_ref[f_rows, PS:2 * PS] = ei
            er, ei, cbr, cbi = block_scan(sh_ref[b_rows, 2 * PS:3 * PS], sh_ref[b_rows, 3 * PS:4 * PS],
                                          cbr, cbi, DEC_ROWS // 2, True)
            sh_ref[b_rows, 2 * PS:3 * PS] = er
            sh_ref[b_rows, 3 * PS:4 * PS] = ei
            new += [cfr, cfi, cbr, cbi]
        return tuple(new)

    zero = jnp.zeros((SUBLANES, PS), F32)
    lax.fori_loop(0, n_blocks, scan_step, (zero,) * (4 * B))

    for r0, n in tiles:
        y = z_ref[0, r0:r0 + n, :] + jnp.dot(sh_ref[r0:r0 + n, :].astype(BF16), w2_ref[0],
                                             preferred_element_type=F32)
        z_ref[0, r0:r0 + n, :] = jax.nn.gelu(y)


def _ssm(u_pairs, w1, w2, dec, B, n_chunks):
    assert n_chunks % SUBLANES == 0
    R = B * n_chunks
    return pl.pallas_call(
        functools.partial(_ssm_body, B=B, n_chunks=n_chunks),
        grid=(N_PAIRS,),
        in_specs=[
            pl.BlockSpec((1, R, PAIR_W), lambda p: (p, 0, 0)),
            pl.BlockSpec((1, PAIR_W, 2 * PAIR_W), lambda p: (p, 0, 0)),
            pl.BlockSpec((1, PAIR_W, PAIR_W), lambda p: (p, 0, 0)),
            pl.BlockSpec((1, DEC_ROWS, PAIR_STATE), lambda p: (p, 0, 0)),
        ],
        out_specs=pl.BlockSpec((1, R, PAIR_W), lambda p: (p, 0, 0)),
        out_shape=jax.ShapeDtypeStruct((N_PAIRS, R, PAIR_W), F32),
        scratch_shapes=[pltpu.VMEM((R, 4 * PAIR_STATE), F32)],
        compiler_params=_cparams(("parallel",)),
        name="ssm",
    )(u_pairs, w1, w2, dec)


def _mix_body(o_ref, zp_ref, h0_ref, wglu_ref, bglu_ref, woa_ref, wos_ref, g_ref, b_ref, wr_ref,
              h1_ref, acc_ref, aff_ref, z_scr, *, L, Lp):
    n_rows = zp_ref.shape[1]
    lane16 = lax.broadcasted_iota(jnp.int32, (n_rows, LANES), 1) // SSM_GROUP
    for t in range(CHUNK):
        src = (t % SUBLANES) * SSM_GROUP
        for v in range(SSM_WIDTH // LANES):
            pieces = []
            for j in range(SUBLANES):
                p, gi = 4 * v + j // 2, j % 2
                lo = gi * CHUNK * SSM_GROUP + (t // SUBLANES) * LANES
                pieces.append((zp_ref[p, :, lo:lo + LANES], src))
            z_scr[v, pl.ds(t, n_rows, stride=CHUNK), :] = _lane_piece_gather(pieces, lane16)
    z = jnp.concatenate([z_scr[v] for v in range(SSM_WIDTH // LANES)], axis=1)
    gate = jax.nn.sigmoid(jnp.dot(z.astype(BF16), wglu_ref[...], preferred_element_type=F32)
                          + bglu_ref[...])
    ssm_out = (z * gate).astype(BF16)
    mix = (jnp.dot(o_ref[...], woa_ref[...], preferred_element_type=F32)
           + jnp.dot(ssm_out, wos_ref[...], preferred_element_type=F32))
    h1 = _layer_norm_rows(DEEPNORM_ALPHA * h0_ref[...] + mix, g_ref[...], b_ref[...])
    for s in range(ROW_SLABS):
        acc_ref[:, s, :] = DEEPNORM_ALPHA * h1[:, s * LANES:(s + 1) * LANES]
    h1_bits = pltpu.bitcast(h1.astype(BF16).astype(F32), jnp.uint32)
    for s in range(ROW_SLABS // 2):
        lo = h1_bits[:, s * LANES:(s + 1) * LANES]
        hi = h1_bits[:, (s + ROW_SLABS // 2) * LANES:(s + ROW_SLABS // 2 + 1) * LANES]
        h1_ref[:, s, :] = hi | lax.shift_right_logical(lo, jnp.uint32(16))
    logits = jnp.dot(h1.astype(BF16), wr_ref[...], preferred_element_type=F32)
    lane = lax.broadcasted_iota(jnp.int32, logits.shape, 1)
    logits = jnp.where(lane < N_EXPERTS, logits, NEG_BIG)
    e = jnp.exp(logits - jnp.max(logits, axis=-1, keepdims=True))
    aff = (e / jnp.sum(e, axis=-1, keepdims=True)).T[:N_EXPERTS, :]
    tiles_per_seq = Lp // ROW_TILE
    pos = (pl.program_id(0) % tiles_per_seq) * ROW_TILE + lax.broadcasted_iota(
        jnp.int32, aff.shape, 1)
    aff_ref[...] = jnp.where((pos < L) | (pos >= L + PAD_ROWS), aff, -1.0)


def _mix(o, z, h0, wglu, bglu, woa, wos, g, b, wr, L, Lp):
    N = o.shape[0]
    TM = ROW_TILE
    row = lambda w: pl.BlockSpec((TM, w), lambda i: (i, 0))
    tok_rows = pl.BlockSpec((TM, ROW_SLABS, LANES), lambda i: (i, 0, 0))
    return pl.pallas_call(
        functools.partial(_mix_body, L=L, Lp=Lp),
        grid=(N // TM,),
        in_specs=[row(ATTN_WIDTH), pl.BlockSpec((N_PAIRS, TM // CHUNK, PAIR_W), lambda i: (0, i, 0)),
                  row(D_MODEL),
                  _const_spec(SSM_WIDTH, SSM_WIDTH), _const_spec(1, SSM_WIDTH),
                  _const_spec(ATTN_WIDTH, D_MODEL), _const_spec(SSM_WIDTH, D_MODEL),
                  _const_spec(1, D_MODEL), _const_spec(1, D_MODEL), _const_spec(D_MODEL, LANES)],
        out_specs=[pl.BlockSpec((TM, ROW_SLABS // 2, LANES), lambda i: (i, 0, 0)), tok_rows,
                   pl.BlockSpec((N_EXPERTS, TM), lambda i: (0, i))],
        out_shape=[jax.ShapeDtypeStruct((N, ROW_SLABS // 2, LANES), jnp.uint32),
                   jax.ShapeDtypeStruct((N, ROW_SLABS, LANES), F32),
                   jax.ShapeDtypeStruct((N_EXPERTS, N), F32)],
        scratch_shapes=[pltpu.VMEM((SSM_WIDTH // LANES, TM, LANES), F32)],
        compiler_params=_cparams(("parallel",)),
        name="mix",
    )(o, z, h0, wglu, bglu, woa, wos, g, b, wr)


def _select_body(aff_ref, ord_ref, idx_ref, thr_ref, *, capacity, n_slot_tiles, seq_len, pad_start):
    nbp = aff_ref.shape[1]
    ri = lax.broadcasted_iota(jnp.int32, (LANES, LANES), 0)
    ci = lax.broadcasted_iota(jnp.int32, (LANES, LANES), 1)
    strict_upper = (ri < ci).astype(BF16)
    incl_upper = (ri <= ci).astype(BF16)
    ones = jnp.ones((LANES, LANES), BF16)
    order = ord_ref[...]
    blk = lax.broadcasted_iota(jnp.int32, (nbp, LANES), 0).astype(F32)
    blk_hi = jnp.floor(blk * (1.0 / 16.0))
    blk_lo = blk - 16.0 * blk_hi
    dot = functools.partial(jnp.dot, preferred_element_type=F32)
    dot_nt = functools.partial(lax.dot_general, dimension_numbers=(((1,), (1,)), ((), ())),
                               preferred_element_type=F32)

    def search_bit(i, ts):
        bit = jnp.left_shift(jnp.int32(1), 30 - i)
        new = []
        for e in range(N_EXPERTS):
            cand = ts[e] | bit
            bits = pltpu.bitcast(aff_ref[e], jnp.int32)
            cnt = jnp.sum((bits >= cand).astype(jnp.int32), keepdims=True)
            new.append(jnp.where(cnt >= capacity, cand, ts[e]))
        return tuple(new)

    thresholds = lax.fori_loop(0, 31, search_bit, (jnp.zeros((1, 1), jnp.int32),) * N_EXPERTS)
    for e in range(N_EXPERTS):
        thr_ref[e] = jnp.broadcast_to(thresholds[e], (SUBLANES, LANES))

    def per_expert(e, carry):
        bits = pltpu.bitcast(aff_ref[e], jnp.int32)
        t = thr_ref[e][0:1, 0:1]
        gt = bits > t
        eq = bits == t
        need = (capacity - jnp.sum(gt.astype(jnp.int32), keepdims=True)).astype(F32)
        eqb = eq.astype(BF16)
        tie_rank = dot(order, dot(eqb, ones).astype(BF16)) + dot(eqb, strict_upper)
        sel = (gt | (eq & (tie_rank < need))).astype(BF16)
        csum_in_blk = dot(sel, incl_upper)
        blk_tot = dot(sel, ones)
        blk_off = dot(order, blk_tot.astype(BF16))
        off_hi = jnp.floor(blk_off * (1.0 / 64.0))
        off_lo = blk_off - 64.0 * off_hi
        table = jnp.concatenate([csum_in_blk, off_hi, off_lo, blk_hi, blk_lo],
                                axis=1).astype(BF16)
        tot_l = dot_nt(ones[:SUBLANES], sel)
        off_l = dot_nt(tot_l.astype(BF16), order)[0:1, :]
        tot_l = tot_l[0:1, :]

        def per_tile(st, carry2):
            j = (st * SLOT_TILE + lax.broadcasted_iota(jnp.int32, (SLOT_TILE, 1), 0)).astype(F32)
            onehot = ((off_l <= j) & (j < off_l + tot_l)).astype(BF16)
            got = dot(onehot, table)
            local = j - (64.0 * got[:, LANES:2 * LANES] + got[:, 2 * LANES:3 * LANES])
            lane_idx = dot((got[:, :LANES] <= local).astype(BF16), ones)
            block = 16.0 * got[:, 3 * LANES:4 * LANES] + got[:, 4 * LANES:5 * LANES]
            k = j - capacity
            seq = sum((k >= PAD_ROWS * i).astype(F32) for i in range(1, SLOT_TILE // PAD_ROWS + 1))
            pad_tok = seq * (seq_len - PAD_ROWS) + pad_start + k
            tok = jnp.where(j < capacity, block * LANES + lane_idx, pad_tok)
            idx_ref[e, st] = tok.T[0:1, :].astype(jnp.int32)
            return carry2

        lax.fori_loop(0, n_slot_tiles, per_tile, 0)
        return carry

    lax.fori_loop(0, N_EXPERTS, per_expert, 0)


def _select(aff_blocks, order, capacity, n_slot_tiles, seq_len, pad_start):
    nbp = aff_blocks.shape[1]
    return pl.pallas_call(
        functools.partial(_select_body, capacity=capacity, n_slot_tiles=n_slot_tiles,
                          seq_len=seq_len, pad_start=pad_start),
        grid=(1,),
        in_specs=[_const_spec(N_EXPERTS, nbp, LANES), _const_spec(nbp, nbp)],
        out_specs=_const_spec(N_EXPERTS, n_slot_tiles, 1, SLOT_TILE),
        out_shape=jax.ShapeDtypeStruct((N_EXPERTS, n_slot_tiles, 1, SLOT_TILE), jnp.int32),
        scratch_shapes=[pltpu.VMEM((N_EXPERTS, SUBLANES, LANES), jnp.int32)],
        compiler_params=_cparams(("arbitrary",)),
        name="select",
    )(aff_blocks, order)


def _moe_body(idx_ref, idx_next_ref, h1_hbm, wg_ref, wu_ref, wd_ref, wr_ref, acc_in, acc_hbm,
              xbuf, abuf, hid_ref, sem_x, sem_a, sem_s):
    del acc_in
    T = xbuf.shape[1]
    n_c = pl.num_programs(1)
    step = pl.program_id(0) * n_c + pl.program_id(1)
    n_steps = pl.num_programs(0) * n_c
    slot = step % 2

    def start_rows(copy_of_row, idx):
        def body(j, carry):
            copy_of_row(idx[0, 0, 0, j], j).start()
            return carry
        lax.fori_loop(0, T, body, 0, unroll=8)

    def x_row(buf):
        return lambda row, j: pltpu.make_async_copy(h1_hbm.at[row], xbuf.at[buf, j], sem_x.at[buf])

    def acc_row_in(row, j):
        return pltpu.make_async_copy(acc_hbm.at[row], abuf.at[j], sem_a)

    def acc_row_out(row, j):
        return pltpu.make_async_copy(abuf.at[j], acc_hbm.at[row], sem_s)

    def wait_scatter():
        pltpu.make_async_copy(abuf, acc_hbm.at[pl.ds(0, T)], sem_s).wait()

    @pl.when(step == 0)
    def _():
        start_rows(x_row(0), idx_ref)

    @pl.when(step + 1 < n_steps)
    def _():
        start_rows(x_row(1 - slot), idx_next_ref)

    pltpu.make_async_copy(h1_hbm.at[pl.ds(0, T)], xbuf.at[slot], sem_x.at[slot]).wait()
    packed = [xbuf[slot, :, s, :] for s in range(ROW_SLABS // 2)]
    lo = [pltpu.bitcast(lax.shift_left(u, jnp.uint32(16)), F32) for u in packed]
    hi = [pltpu.bitcast(u & jnp.uint32(0xFFFF0000), F32) for u in packed]
    x = jnp.concatenate(lo + hi, axis=1).astype(BF16)
    logits = jnp.dot(x, wr_ref[...], preferred_element_type=F32)
    lane = lax.broadcasted_iota(jnp.int32, logits.shape, 1)
    logits = jnp.where(lane < N_EXPERTS, logits, NEG_BIG)
    p = jnp.exp(logits - jnp.max(logits, axis=-1, keepdims=True))
    gate = (jnp.sum(jnp.where(lane == pl.program_id(0), p, 0.0), axis=-1, keepdims=True)
            / jnp.sum(p, axis=-1, keepdims=True))
    for f in range(EXPERT_FF // FF_TILE):
        fs = slice(f * FF_TILE, (f + 1) * FF_TILE)
        hg = jnp.dot(x, wg_ref[0, :, fs], preferred_element_type=F32)
        hu = jnp.dot(x, wu_ref[0, :, fs], preferred_element_type=F32)
        hid_ref[:, fs] = (jax.nn.silu(hg) * hu).astype(BF16)
        if f == 0:
            @pl.when(step > 0)
            def _():
                wait_scatter()

            start_rows(acc_row_in, idx_ref)
    pltpu.make_async_copy(acc_hbm.at[pl.ds(0, T)], abuf, sem_a).wait()
    for n0 in range(0, D_MODEL, 2 * LANES):
        y = jnp.dot(hid_ref[...], wd_ref[0, :, n0:n0 + 2 * LANES], preferred_element_type=F32) * gate
        for h in range(2):
            s = n0 // LANES + h
            abuf[:, s, :] = abuf[:, s, :] + y[:, h * LANES:(h + 1) * LANES]
    start_rows(acc_row_out, idx_ref)

    @pl.when(step == n_steps - 1)
    def _():
        wait_scatter()


def _moe(idx, h1, acc, wg, wu, wd, wr):
    n_slots = idx.shape[1] * idx.shape[3]
    n_c = next(n for n in range(1, n_slots) if n_slots % n == 0 and n_slots // n <= MOE_TILE_MAX
               and (n_slots // n) % 16 == 0)
    T = n_slots // n_c
    idx = idx.reshape(N_EXPERTS, n_c, 1, T)

    def next_block(e, c):
        return (jnp.minimum(e + (c + 1) // n_c, N_EXPERTS - 1), (c + 1) % n_c, 0, 0)

    return pl.pallas_call(
        _moe_body,
        grid=(N_EXPERTS, n_c),
        in_specs=[
            pl.BlockSpec((1, 1, 1, T), lambda e, c: (e, c, 0, 0), memory_space=pltpu.SMEM),
            pl.BlockSpec((1, 1, 1, T), next_block, memory_space=pltpu.SMEM),
            pl.BlockSpec(memory_space=pl.ANY),
            pl.BlockSpec((1, D_MODEL, EXPERT_FF), lambda e, c: (e, 0, 0)),
            pl.BlockSpec((1, D_MODEL, EXPERT_FF), lambda e, c: (e, 0, 0)),
            pl.BlockSpec((1, EXPERT_FF, D_MODEL), lambda e, c: (e, 0, 0)),
            _const_spec(D_MODEL, LANES),
            pl.BlockSpec(memory_space=pl.ANY),
        ],
        out_specs=pl.BlockSpec(memory_space=pl.ANY),
        out_shape=jax.ShapeDtypeStruct(acc.shape, F32),
        scratch_shapes=[pltpu.VMEM((2, T, ROW_SLABS // 2, LANES), jnp.uint32),
                        pltpu.VMEM((T, ROW_SLABS, LANES), F32),
                        pltpu.VMEM((T, EXPERT_FF), BF16),
                        pltpu.SemaphoreType.DMA((2,)), pltpu.SemaphoreType.DMA(()),
                        pltpu.SemaphoreType.DMA(())],
        input_output_aliases={7: 0},
        compiler_params=pltpu.CompilerParams(dimension_semantics=("arbitrary", "arbitrary"),
                                             vmem_limit_bytes=VMEM_LIMIT,
                                             disable_bounds_checks=True),
        name="expert_ffn",
    )(idx, idx, h1, wg, wu, wd, wr, acc)


def _final_body(a_ref, g_ref, b_ref, o_ref):
    x = jnp.concatenate([a_ref[0, :, s, :] for s in range(ROW_SLABS)], axis=1)
    o_ref[0] = _layer_norm_rows(x, g_ref[...], b_ref[...])


def _final_norm(acc, g, b, L):
    B = acc.shape[0]
    TM = EMBED_TILE
    return pl.pallas_call(
        _final_body,
        grid=(B, L // TM),
        in_specs=[pl.BlockSpec((1, TM, ROW_SLABS, LANES), lambda b_, j: (b_, j, 0, 0)),
                  _const_spec(1, D_MODEL), _const_spec(1, D_MODEL)],
        out_specs=pl.BlockSpec((1, TM, D_MODEL), lambda b_, j: (b_, j, 0)),
        out_shape=jax.ShapeDtypeStruct((B, L, D_MODEL), F32),
        compiler_params=_cparams(("parallel", "parallel")),
        name="final_norm",
    )(acc, g, b)


def _rope_tables(L, Lp):
    rows = L // GRID_W
    t = jnp.arange(L, dtype=jnp.int32)
    m = jnp.arange(N_META, dtype=jnp.int32)
    pad = jnp.zeros((Lp - L - N_META,), jnp.int32)
    row = jnp.concatenate([t // GRID_W - rows // 2, pad,
                           jnp.full((N_META,), -(rows // 2) - 1, jnp.int32)])
    col = jnp.concatenate([t % GRID_W - GRID_W // 2, pad, m - GRID_W // 2])
    inv_freq = ROPE_THETA ** (-jnp.arange(0, ROPE_AXIS_DIM, 2, dtype=F32) / ROPE_AXIS_DIM)
    ang_r = row.astype(F32)[:, None] * inv_freq
    ang_c = col.astype(F32)[:, None] * inv_freq
    cos = jnp.concatenate([jnp.cos(ang_r)] * 2 + [jnp.cos(ang_c)] * 2, axis=1)
    sin = jnp.concatenate([-jnp.sin(ang_r), jnp.sin(ang_r), -jnp.sin(ang_c), jnp.sin(ang_c)], axis=1)
    return jnp.tile(cos, (1, 2)), jnp.tile(sin, (1, 2))


def _block_order(B, Lp, nbp):
    nbb = Lp // LANES
    r = jnp.arange(nbp)
    b, jb = r // nbb, r % nbb
    rank = jnp.where(r < B * nbb, b * nbb + jnp.where(jb == nbb - 1, 0, jb + 1), r)
    return (rank[None, :] < rank[:, None]).astype(BF16)


def _run_trunk(x, meta_pad, shared):
    B, L, _ = x.shape
    Lp = L + TAIL
    N = B * Lp
    cos, sin = _rope_tables(L, Lp)
    h0, q, k, vts, u = _embed(x, meta_pad, shared["ln_emb_g"], shared["ln_emb_b"], shared["w_in"],
                              shared["qg"], shared["kg"], shared["bones"], cos, sin, L, Lp)
    o = _attention(q, k, vts, L, Lp, shared["score_bound"])
    z = _ssm(u.reshape(N_PAIRS, N // CHUNK, PAIR_W), shared["ssm_w1"], shared["ssm_w2"],
             shared["ssm_dec"], B, Lp // CHUNK)
    h1, acc, aff = _mix(o.reshape(N, ATTN_WIDTH), z, h0.reshape(N, D_MODEL),
                        shared["w_glu"], shared["b_glu"], shared["w_out_attn"], shared["w_out_ssm"],
                        shared["ln1_g"], shared["ln1_b"], shared["w_router"], L, Lp)

    capacity = EC_CAPACITY_FACTOR * B * (L + N_META) // N_EXPERTS
    n_slot_tiles = -(-capacity // SLOT_TILE)
    nb = N // LANES
    nbp = -(-nb // LANES) * LANES
    aff_blocks = jnp.pad(aff.reshape(N_EXPERTS, nb, LANES), ((0, 0), (0, nbp - nb), (0, 0)),
                         constant_values=-1.0)
    assert n_slot_tiles * SLOT_TILE - capacity <= B * PAD_ROWS
    idx = _select(aff_blocks, _block_order(B, Lp, nbp), capacity, n_slot_tiles, Lp, L)
    acc = _moe(idx, h1, acc, shared["w_gate"], shared["w_up"], shared["w_down"], shared["w_router"])
    return _final_norm(acc.reshape(B, Lp, ROW_SLABS, LANES), shared["ln2_g"], shared["ln2_b"], L)


def kernel(x_prompt, x_sample, meta_tokens, ln_emb_g, ln_emb_b, w_in, q_norm_g, k_norm_g, ssm_lambda_re, ssm_lambda_im, ssm_log_dt, ssm_b_re, ssm_b_im, ssm_c_re, ssm_c_im, ssm_d, w_glu, b_glu, w_out, ln1_g, ln1_b, w_router, w_gate, w_up, w_down, ln2_g, ln2_b):
    row = lambda a: a.reshape(1, -1).astype(F32)
    w_q = w_in[0][:, :Q_END].reshape(D_MODEL, N_KV_HEADS, Q_PER_KV, HEAD_DIM)
    w_q = w_q.transpose(0, 2, 1, 3).reshape(D_MODEL, Q_END)
    w_oa = w_out[0][:ATTN_WIDTH].reshape(N_KV_HEADS, Q_PER_KV, HEAD_DIM, D_MODEL)
    w_oa = w_oa.transpose(1, 0, 2, 3).reshape(ATTN_WIDTH, D_MODEL)
    head_of = jnp.arange(Q_END) // HEAD_DIM
    ssm_w1, ssm_w2, ssm_dec = _ssm_weights(ssm_lambda_re[0], ssm_lambda_im[0], ssm_log_dt[0],
                                           ssm_b_re[0], ssm_b_im[0], ssm_c_re[0], ssm_c_im[0],
                                           ssm_d[0])
    shared = dict(
        ln_emb_g=row(ln_emb_g), ln_emb_b=row(ln_emb_b),
        w_in=jnp.concatenate([w_q, w_in[0][:, Q_END:]], axis=1).astype(BF16),
        qg=row(jnp.tile(q_norm_g[0], N_Q_HEADS)), kg=row(jnp.tile(k_norm_g[0], N_KV_HEADS)),
        bones=(head_of[:, None] == head_of[None, :]).astype(BF16),
        score_bound=(1.02 * HEAD_DIM * Q_PRESCALE * jnp.max(jnp.abs(q_norm_g[0]))
                     * jnp.max(jnp.abs(k_norm_g[0]))).astype(F32),
        ssm_w1=ssm_w1, ssm_w2=ssm_w2, ssm_dec=ssm_dec,
        w_glu=w_glu[0].astype(BF16), b_glu=row(b_glu[0]),
        w_out_attn=w_oa.astype(BF16), w_out_ssm=w_out[0][ATTN_WIDTH:].astype(BF16),
        ln1_g=row(ln1_g[0]), ln1_b=row(ln1_b[0]),
        w_router=jnp.pad(w_router[0], ((0, 0), (0, LANES - N_EXPERTS))).astype(BF16),
        w_gate=w_gate[0].astype(BF16), w_up=w_up[0].astype(BF16), w_down=w_down[0].astype(BF16),
        ln2_g=row(ln2_g[0]), ln2_b=row(ln2_b[0]),
    )
    meta_pad = jnp.pad(meta_tokens.astype(F32), ((PAD_ROWS, 0), (0, 0)))[None]
    return (_run_trunk(x_prompt, meta_pad, shared), _run_trunk(x_sample, meta_pad, shared))
```

```python
import functools
import math

import jax
import jax.numpy as jnp
from jax import lax
from jax.experimental import pallas as pl
from jax.experimental.pallas import tpu as pltpu

F32 = jnp.float32
BF16 = jnp.bfloat16

D_MODEL = 1024
N_META = 16
GRID_W = 64
ATTN_WIDTH = 512
SSM_WIDTH = 512
HEAD_DIM = 64
N_Q_HEADS = 8
N_KV_HEADS = 2
Q_PER_KV = 4
KV_WIDTH = 128
ROPE_AXIS_DIM = 32
ROPE_THETA = 10000.0
ATTN_SCALE = HEAD_DIM ** -0.5
RMS_EPS = 1e-6
SSM_GROUP = 16
N_SSM_GROUPS = 32
SSM_STATE = 64
Q_END = ATTN_WIDTH
K_END = Q_END + KV_WIDTH
V_END = K_END + KV_WIDTH
IN_WIDTH = V_END + SSM_WIDTH
N_EXPERTS = 16
EXPERT_FF = 2048
EC_CAPACITY_FACTOR = 2
LN_EPS = 1e-5
DEPTH = 1
DEEPNORM_ALPHA = (2 * DEPTH) ** 0.25

LANES = 128
SUBLANES = 8
ROW_SLABS = D_MODEL // LANES
TAIL = LANES
PAD_ROWS = TAIL - N_META
CHUNK = 16
N_PAIRS = N_SSM_GROUPS // 2
PAIR_W = 2 * CHUNK * SSM_GROUP
PAIR_STATE = 2 * SSM_STATE
EMBED_TILE = 512
ROW_TILE = 384
Q_TILE = ROW_TILE
KV_TILE = EMBED_TILE
KEY_SUBTILE = 256
KV_UNROLL = 4
KV_UNROLL_BOUNDED = 8
SLOT_TILE = 384
MOE_TILE_MAX = 1056
FF_TILE = 512
SSM_ROW_TILE = 256
DEC_ROWS = 48
VT_ROWS = HEAD_DIM + 16
Q_PRESCALE = ATTN_SCALE * math.log2(math.e)
NEG_BIG = -1e30
SAFE_EXP2_RANGE = 60.0
VMEM_LIMIT = 56 * 1024 * 1024


def _cparams(sem):
    return pltpu.CompilerParams(dimension_semantics=sem, vmem_limit_bytes=VMEM_LIMIT)


def _const_spec(*shape):
    return pl.BlockSpec(shape, lambda *idx: (0,) * len(shape))


def _layer_norm_rows(x, g, b):
    mu = jnp.mean(x, axis=-1, keepdims=True)
    xc = x - mu
    var = jnp.mean(xc * xc, axis=-1, keepdims=True)
    return xc * lax.rsqrt(var + LN_EPS) * g + b


def _head_rms(t, gain, bones):
    sq = t * t
    hi = sq.astype(BF16)
    lo = (sq - hi.astype(F32)).astype(BF16)
    ss = (jnp.dot(hi, bones, preferred_element_type=F32)
          + jnp.dot(lo, bones, preferred_element_type=F32))
    return t * lax.rsqrt(ss * (1.0 / HEAD_DIM) + RMS_EPS) * gain


def _rope_slab(t, cos, sin_signed):
    lane = lax.broadcasted_iota(jnp.int32, t.shape, 1)
    first = (lane % ROPE_AXIS_DIM) < (ROPE_AXIS_DIM // 2)
    partner = jnp.where(first, pltpu.roll(t, LANES - 16, 1), pltpu.roll(t, 16, 1))
    return t * cos + partner * sin_signed


def _lane_piece_gather(pieces, lane16):
    out = None
    for j, (arr, src) in enumerate(pieces):
        shift = (SSM_GROUP * j - src) % LANES
        moved = pltpu.roll(arr, shift, 1) if shift else arr
        out = moved if out is None else jnp.where(lane16 == j, moved, out)
    return out


def _embed_body(x_ref, g_ref, b_ref, w_ref, qg_ref, kg_ref, bones_ref, cos_ref, sin_ref,
                h_ref, q_ref, k_ref, vt0_ref, vt1_ref, u_ref, u_scr):
    h = _layer_norm_rows(x_ref[0], g_ref[...], b_ref[...])
    h_ref[0] = h
    proj = jnp.dot(h.astype(BF16), w_ref[...], preferred_element_type=F32)
    cos = cos_ref[...]
    sin = sin_ref[...]
    qn = _head_rms(proj[:, :Q_END], qg_ref[...], bones_ref[...])
    for s in range(ATTN_WIDTH // LANES):
        sl = slice(s * LANES, (s + 1) * LANES)
        q_ref[0, :, sl] = (_rope_slab(qn[:, sl], cos, sin) * Q_PRESCALE).astype(BF16)
    kn = _head_rms(proj[:, Q_END:K_END], kg_ref[...], bones_ref[:KV_WIDTH, :KV_WIDTH])
    k_ref[0] = _rope_slab(kn, cos, sin).astype(BF16)
    vt = proj[:, K_END:V_END].T
    ones = jnp.ones((VT_ROWS - HEAD_DIM, vt.shape[1]), F32)
    vt0_ref[0, 0] = jnp.concatenate([vt[:HEAD_DIM], ones], axis=0).astype(BF16)
    vt1_ref[0, 0] = jnp.concatenate([vt[HEAD_DIM:], ones], axis=0).astype(BF16)
    n_rows = u_scr.shape[1] // CHUNK
    for v in range(SSM_WIDTH // LANES):
        u_scr[v] = proj[:, V_END + v * LANES:V_END + (v + 1) * LANES]
    by_token = [[u_scr[v, pl.ds(t, n_rows, stride=CHUNK), :] for v in range(SSM_WIDTH // LANES)]
                for t in range(CHUNK)]
    lane16 = lax.broadcasted_iota(jnp.int32, (n_rows, LANES), 1) // SSM_GROUP
    for p in range(N_PAIRS):
        for gi in range(2):
            src = (p % 4) * 2 * SSM_GROUP + gi * SSM_GROUP
            for h in range(CHUNK // SUBLANES):
                pieces = [(by_token[SUBLANES * h + j][p // 4], src) for j in range(SUBLANES)]
                lo = gi * CHUNK * SSM_GROUP + h * LANES
                u_ref[p, 0, :, lo:lo + LANES] = _lane_piece_gather(pieces, lane16)


def _embed_tail_body(x_ref, g_ref, b_ref, w_ref, qg_ref, kg_ref, bones_ref, cos_ref, sin_ref,
                     h_in, q_in, k_in, u_in, h_ref, q_ref, k_ref, vt0_ref, vt1_ref, u_ref, u_scr):
    del h_in, q_in, k_in, u_in
    _embed_body(x_ref, g_ref, b_ref, w_ref, qg_ref, kg_ref, bones_ref, cos_ref, sin_ref,
                h_ref, q_ref, k_ref, vt0_ref, vt1_ref, u_ref, u_scr)


def _embed(x, meta_pad, ln_g, ln_b, w_in, qg, kg, bones, cos, sin, L, Lp):
    B = x.shape[0]
    TM = EMBED_TILE
    nj = L // TM
    w_specs = [_const_spec(1, D_MODEL), _const_spec(1, D_MODEL), _const_spec(D_MODEL, IN_WIDTH),
               _const_spec(1, Q_END), _const_spec(1, KV_WIDTH), _const_spec(Q_END, Q_END)]

    def out_shapes(vt_tiles, vt_width):
        vt = jax.ShapeDtypeStruct((B, vt_tiles, VT_ROWS, vt_width), BF16)
        return [
            jax.ShapeDtypeStruct((B, Lp, D_MODEL), F32),
            jax.ShapeDtypeStruct((B, Lp, Q_END), BF16),
            jax.ShapeDtypeStruct((B, Lp, KV_WIDTH), BF16),
            vt, vt,
            jax.ShapeDtypeStruct((N_PAIRS, B, Lp // CHUNK, PAIR_W), F32),
        ]

    def out_specs(tm, row_block):
        return [
            pl.BlockSpec((1, tm, D_MODEL), lambda b, j: (b, row_block(j), 0)),
            pl.BlockSpec((1, tm, Q_END), lambda b, j: (b, row_block(j), 0)),
            pl.BlockSpec((1, tm, KV_WIDTH), lambda b, j: (b, row_block(j), 0)),
            pl.BlockSpec((1, 1, VT_ROWS, tm), lambda b, j: (b, j, 0, 0)),
            pl.BlockSpec((1, 1, VT_ROWS, tm), lambda b, j: (b, j, 0, 0)),
            pl.BlockSpec((N_PAIRS, 1, tm // CHUNK, PAIR_W), lambda b, j: (0, b, row_block(j), 0)),
        ]

    main = pl.pallas_call(
        _embed_body,
        grid=(B, nj),
        in_specs=[pl.BlockSpec((1, TM, D_MODEL), lambda b, j: (b, j, 0))] + w_specs
        + [pl.BlockSpec((TM, LANES), lambda b, j: (j, 0))] * 2,
        out_specs=out_specs(TM, lambda j: j),
        out_shape=out_shapes(nj, TM),
        scratch_shapes=[pltpu.VMEM((SSM_WIDTH // LANES, TM, LANES), F32)],
        compiler_params=_cparams(("parallel", "parallel")),
        name="embed_main",
    )
    h0, q, k, vt0, vt1, u = main(x, ln_g, ln_b, w_in, qg, kg, bones, cos, sin)

    jt = Lp // TAIL - 1
    tail = pl.pallas_call(
        _embed_tail_body,
        grid=(B, 1),
        in_specs=[pl.BlockSpec((1, TAIL, D_MODEL), lambda b, j: (0, 0, 0))] + w_specs
        + [pl.BlockSpec((TAIL, LANES), lambda b, j: (jt, 0))] * 2
        + [pl.BlockSpec(memory_space=pl.ANY)] * 4,
        out_specs=out_specs(TAIL, lambda j: jt),
        out_shape=out_shapes(1, TAIL),
        input_output_aliases={9: 0, 10: 1, 11: 2, 12: 5},
        scratch_shapes=[pltpu.VMEM((SSM_WIDTH // LANES, TAIL, LANES), F32)],
        compiler_params=_cparams(("parallel", "arbitrary")),
        name="embed_tail",
    )
    h0, q, k, vt0_tail, vt1_tail, u = tail(meta_pad, ln_g, ln_b, w_in, qg, kg, bones, cos, sin,
                                           h0, q, k, u)
    return h0, q, k, (vt0, vt1, vt0_tail, vt1_tail), u


def _attn_body(q_ref, k_ref, vt0_ref, vt1_ref, vt0t_ref, vt1t_ref, o_ref,
               qt_ref, s0_ref, s1_ref, mc0_ref, mc1_ref, m_ref, acc_ref, *, L, n_kv_tiles):
    cols = Q_PER_KV * Q_TILE
    _attn_load_qt(q_ref, qt_ref)
    m_ref[...] = jnp.full(m_ref.shape, NEG_BIG, F32)
    acc_ref[...] = jnp.zeros(acc_ref.shape, F32)

    s_bufs = ((s0_ref, mc0_ref), (s1_ref, mc1_ref))

    def scores(slot, k_tile, first_valid=0):
        s_ref, mc_ref = s_bufs[slot]
        n = k_tile.shape[0]
        for g in range(N_KV_HEADS):
            s = jnp.dot(k_tile, qt_ref[g], preferred_element_type=F32)
            if first_valid:
                key = lax.broadcasted_iota(jnp.int32, s.shape, 0)
                s = jnp.where(key >= first_valid, s, NEG_BIG)
            s_ref[g, 0:n, :] = s
            mc_ref[g] = jnp.broadcast_to(jnp.max(s, axis=0, keepdims=True), (SUBLANES, cols))

    def softmax_values(slot, vt_tiles, n):
        s_ref, mc_ref = s_bufs[slot]
        for g in range(N_KV_HEADS):
            m_old = m_ref[g]
            m_new = jnp.maximum(m_old, mc_ref[g])
            alpha = jnp.exp2(m_old[0:1, :] - m_new[0:1, :])
            acc = alpha * acc_ref[g]
            for k0 in range(0, n, KEY_SUBTILE):
                k1 = min(k0 + KEY_SUBTILE, n)
                p = jnp.exp2(s_ref[g, k0:k1, :] - m_new[0:1, :]).astype(BF16)
                acc = acc + jnp.dot(vt_tiles[g][:, k0:k1], p, preferred_element_type=F32)
            acc_ref[g] = acc
            m_ref[g] = m_new

    def k_tile(i):
        return k_ref[0, pl.ds(pl.multiple_of(i * KV_TILE, KV_TILE), KV_TILE), :]

    def vt_tiles(i):
        return vt0_ref[0, i], vt1_ref[0, i]

    scores(0, k_tile(0))

    def run_tiles(first, count, next_scores):
        for t in range(count):
            softmax_values(t % 2, vt_tiles(first + t), KV_TILE)
            if t + 1 < count:
                scores((t + 1) % 2, k_tile(first + t + 1))
            else:
                next_scores()

    def kv_group(i, carry):
        first = i * KV_UNROLL
        run_tiles(first, KV_UNROLL, lambda: scores(0, k_tile(first + KV_UNROLL)))
        return carry

    n_groups = n_kv_tiles // KV_UNROLL
    lax.fori_loop(0, n_groups - 1, kv_group, 0)
    run_tiles((n_groups - 1) * KV_UNROLL, KV_UNROLL,
              lambda: scores(0, k_ref[0, L:L + TAIL, :], PAD_ROWS))
    softmax_values(0, (vt0t_ref[0, 0], vt1t_ref[0, 0]), TAIL)

    _attn_store_out(acc_ref, o_ref)


def _attn_bounded_body(q_ref, k_ref, vt0_ref, vt1_ref, vt0t_ref, vt1t_ref, o_ref,
                       qt_ref, p0_ref, p1_ref, acc_ref, *, L, n_kv_tiles):
    _attn_load_qt(q_ref, qt_ref)
    acc_ref[...] = jnp.zeros(acc_ref.shape, F32)
    p_bufs = (p0_ref, p1_ref)

    def key_blocks(k_tile, vt_tiles, accs, first_valid=0):
        n = k_tile.shape[0]
        accs = list(accs)
        for i, k0 in enumerate(range(0, n, KEY_SUBTILE)):
            k1 = min(k0 + KEY_SUBTILE, n)
            p_ref = p_bufs[i % 2]
            for g in range(N_KV_HEADS):
                s = jnp.dot(k_tile[k0:k1], qt_ref[g], preferred_element_type=F32)
                if first_valid:
                    key = k0 + lax.broadcasted_iota(jnp.int32, s.shape, 0)
                    s = jnp.where(key >= first_valid, s, NEG_BIG)
                p_ref[g, 0:k1 - k0, :] = jnp.exp2(s).astype(BF16)
            for g in range(N_KV_HEADS):
                accs[g] = accs[g] + jnp.dot(vt_tiles[g][:, k0:k1], p_ref[g, 0:k1 - k0, :],
                                            preferred_element_type=F32)
        return accs

    def kv_group(i, carry):
        accs = [acc_ref[g] for g in range(N_KV_HEADS)]
        for t in range(KV_UNROLL_BOUNDED):
            tile = i * KV_UNROLL_BOUNDED + t
            start = pl.multiple_of(tile * KV_TILE, KV_TILE)
            accs = key_blocks(k_ref[0, pl.ds(start, KV_TILE), :], (vt0_ref[0, tile], vt1_ref[0, tile]),
                              accs)
        for g in range(N_KV_HEADS):
            acc_ref[g] = accs[g]
        return carry

    lax.fori_loop(0, n_kv_tiles // KV_UNROLL_BOUNDED, kv_group, 0)
    accs = key_blocks(k_ref[0, L:L + TAIL, :], (vt0t_ref[0, 0], vt1t_ref[0, 0]),
                      [acc_ref[g] for g in range(N_KV_HEADS)], PAD_ROWS)
    for g in range(N_KV_HEADS):
        acc_ref[g] = accs[g]
    _attn_store_out(acc_ref, o_ref)


def _attn_load_qt(q_ref, qt_ref):
    tq = Q_TILE
    row = lax.broadcasted_iota(jnp.int32, (KV_WIDTH, tq), 0)
    for r in range(Q_PER_KV):
        slab_t = q_ref[0, :, r * LANES:(r + 1) * LANES].astype(F32).T
        for g in range(N_KV_HEADS):
            in_group = (row >= g * HEAD_DIM) & (row < (g + 1) * HEAD_DIM)
            qt_ref[g, :, r * tq:(r + 1) * tq] = jnp.where(in_group, slab_t, 0.0).astype(BF16)


def _attn_store_out(acc_ref, o_ref):
    tq = Q_TILE
    comb = jnp.concatenate(
        [acc_ref[g, 0:HEAD_DIM, :] / acc_ref[g, HEAD_DIM:HEAD_DIM + 1, :] for g in range(N_KV_HEADS)],
        axis=0)
    for r in range(Q_PER_KV):
        o_ref[0, :, r * LANES:(r + 1) * LANES] = comb[:, r * tq:(r + 1) * tq].T.astype(BF16)


def _attention(q, k, vts, L, Lp, score_bound):
    B = q.shape[0]
    nkt = L // KV_TILE
    assert nkt % KV_UNROLL == 0 and nkt % KV_UNROLL_BOUNDED == 0
    cols = Q_PER_KV * Q_TILE
    vt_spec = pl.BlockSpec((1, nkt, VT_ROWS, KV_TILE), lambda b, j: (b, 0, 0, 0))
    vt_tail_spec = pl.BlockSpec((1, 1, VT_ROWS, TAIL), lambda b, j: (b, 0, 0, 0))
    call = functools.partial(
        pl.pallas_call,
        grid=(B, Lp // Q_TILE),
        in_specs=[
            pl.BlockSpec((1, Q_TILE, Q_END), lambda b, j: (b, j, 0)),
            pl.BlockSpec((1, Lp, KV_WIDTH), lambda b, j: (b, 0, 0)),
            vt_spec, vt_spec, vt_tail_spec, vt_tail_spec,
        ],
        out_specs=pl.BlockSpec((1, Q_TILE, Q_END), lambda b, j: (b, j, 0)),
        out_shape=jax.ShapeDtypeStruct((B, Lp, Q_END), BF16),
        compiler_params=_cparams(("parallel", "arbitrary")),
    )
    bounded = call(
        functools.partial(_attn_bounded_body, L=L, n_kv_tiles=nkt),
        scratch_shapes=[
            pltpu.VMEM((N_KV_HEADS, KV_WIDTH, cols), BF16),
            pltpu.VMEM((N_KV_HEADS, KEY_SUBTILE, cols), BF16),
            pltpu.VMEM((N_KV_HEADS, KEY_SUBTILE, cols), BF16),
            pltpu.VMEM((N_KV_HEADS, VT_ROWS, cols), F32),
        ],
        name="attention_bounded",
    )
    general = call(
        functools.partial(_attn_body, L=L, n_kv_tiles=nkt),
        scratch_shapes=[
            pltpu.VMEM((N_KV_HEADS, KV_WIDTH, cols), BF16),
            pltpu.VMEM((N_KV_HEADS, KV_TILE, cols), F32),
            pltpu.VMEM((N_KV_HEADS, KV_TILE, cols), F32),
            pltpu.VMEM((N_KV_HEADS, SUBLANES, cols), F32),
            pltpu.VMEM((N_KV_HEADS, SUBLANES, cols), F32),
            pltpu.VMEM((N_KV_HEADS, SUBLANES, cols), F32),
            pltpu.VMEM((N_KV_HEADS, VT_ROWS, cols), F32),
        ],
        name="attention",
    )
    return lax.cond(score_bound <= SAFE_EXP2_RANGE, bounded, general, q, k, *vts)


def _ssm_weights(lam_re, lam_im, log_dt, b_re, b_im, c_re, c_im, d_skip):
    hp = lax.Precision.HIGHEST
    dt = jnp.exp(log_dt.astype(F32))[..., None]
    lr = lam_re.astype(F32)
    li = lam_im.astype(F32)

    def apow(n):
        n = jnp.asarray(n, F32)
        mag = jnp.exp(lr[..., None] * dt[..., None] * n)
        ang = li[..., None] * dt[..., None] * n
        return mag * jnp.cos(ang), mag * jnp.sin(ang)

    a1r, a1i = apow(jnp.ones((1,), F32))
    a1r, a1i = a1r[..., 0], a1i[..., 0]
    nr = a1r - 1.0
    den = lr * lr + li * li
    f_r = (nr * lr + a1i * li) / den
    f_i = (a1i * lr - nr * li) / den
    br = b_re.astype(F32)
    bi = b_im.astype(F32)
    bb_r = f_r[..., None] * br - f_i[..., None] * bi
    bb_i = f_r[..., None] * bi + f_i[..., None] * br
    cr = c_re.astype(F32)
    ci = c_im.astype(F32)

    tau = jnp.arange(CHUNK + 1, dtype=F32)
    pr, pi = apow(tau)
    ab_r = pr[..., None] * bb_r[:, :, :, None, :] - pi[..., None] * bb_i[:, :, :, None, :]
    ab_i = pr[..., None] * bb_i[:, :, :, None, :] + pi[..., None] * bb_r[:, :, :, None, :]
    kern = (jnp.einsum("dgop,dgptc->dgtco", cr, ab_r, precision=hp)
            - jnp.einsum("dgop,dgptc->dgtco", ci, ab_i, precision=hp))
    s_idx = jnp.arange(CHUNK)[:, None]
    t_idx = jnp.arange(CHUNK)[None, :]
    lag_f = jnp.clip(t_idx - s_idx, 0, CHUNK)
    lag_b = jnp.clip(s_idx - t_idx, 0, CHUNK)
    m_f = jnp.where((t_idx >= s_idx)[..., None, None], kern[0][:, lag_f], 0.0)
    m_b = jnp.where((s_idx >= t_idx)[..., None, None], kern[1][:, lag_b], 0.0)
    eye_t = jnp.eye(CHUNK, dtype=F32)[None, :, :, None, None]
    eye_c = jnp.eye(SSM_GROUP, dtype=F32)[None, None, None]
    dsk = d_skip.astype(F32).reshape(N_SSM_GROUPS, 1, 1, SSM_GROUP, 1)
    m_all = m_f + m_b + eye_t * eye_c * dsk
    m_all = m_all.transpose(0, 1, 3, 2, 4).reshape(N_SSM_GROUPS, 256, 256)

    def state_in(d, expo):
        er, ei = pr[d][..., expo], pi[d][..., expo]
        wr = er[..., None] * bb_r[d][:, :, None, :] - ei[..., None] * bb_i[d][:, :, None, :]
        wi = er[..., None] * bb_i[d][:, :, None, :] + ei[..., None] * bb_r[d][:, :, None, :]
        to_rows = lambda w: w.transpose(0, 2, 3, 1).reshape(N_SSM_GROUPS, 256, SSM_STATE)
        return to_rows(wr), to_rows(wi)

    sf_r, sf_i = state_in(0, jnp.arange(CHUNK - 1, -1, -1))
    sb_r, sb_i = state_in(1, jnp.arange(CHUNK))

    def state_out(d, expo):
        er, ei = pr[d][..., expo], pi[d][..., expo]
        wr = cr[d].transpose(0, 2, 1)[:, :, None, :] * er[..., None] \
            - ci[d].transpose(0, 2, 1)[:, :, None, :] * ei[..., None]
        wi = cr[d].transpose(0, 2, 1)[:, :, None, :] * ei[..., None] \
            + ci[d].transpose(0, 2, 1)[:, :, None, :] * er[..., None]
        flat = lambda w: w.reshape(N_SSM_GROUPS, SSM_STATE, 256)
        return flat(wr), flat(-wi)

    of_r, of_i = state_out(0, jnp.arange(1, CHUNK + 1))
    ob_r, ob_i = state_out(1, jnp.arange(CHUNK, 0, -1))

    def pair_rows(w):
        return w.reshape(N_PAIRS, 2, *w.shape[1:])

    z256 = jnp.zeros((N_PAIRS, 256, 256), F32)
    mp = pair_rows(m_all)
    w1_y = jnp.concatenate([jnp.concatenate([mp[:, 0], z256], axis=2),
                            jnp.concatenate([z256, mp[:, 1]], axis=2)], axis=1)

    def pair_cols_in(w):
        wp = pair_rows(w)
        z = jnp.zeros_like(wp[:, 0])
        return jnp.concatenate([jnp.concatenate([wp[:, 0], z], axis=2),
                                jnp.concatenate([z, wp[:, 1]], axis=2)], axis=1)

    w1 = jnp.concatenate([w1_y, pair_cols_in(sf_r), pair_cols_in(sf_i),
                          pair_cols_in(sb_r), pair_cols_in(sb_i)], axis=2)

    def pair_rows_out(w):
        wp = pair_rows(w)
        z = jnp.zeros_like(wp[:, 0])
        return jnp.concatenate([jnp.concatenate([wp[:, 0], z], axis=2),
                                jnp.concatenate([z, wp[:, 1]], axis=2)], axis=1)

    w2 = jnp.concatenate([pair_rows_out(of_r), pair_rows_out(of_i),
                          pair_rows_out(ob_r), pair_rows_out(ob_i)], axis=1)

    qr, qi = apow(CHUNK * jnp.arange(SUBLANES + 1, dtype=F32))
    zero = jnp.zeros_like(qr[..., 0])

    def dec_rows(d, q_expo):
        rows = [qr[d][..., 1], qi[d][..., 1], qr[d][..., 2], qi[d][..., 2],
                qr[d][..., 4], qi[d][..., 4], zero[d], zero[d]]
        rows += [qr[d][..., n] for n in q_expo] + [qi[d][..., n] for n in q_expo]
        return jnp.stack(rows, axis=1)

    dec = jnp.concatenate([dec_rows(0, range(1, SUBLANES + 1)),
                           dec_rows(1, range(SUBLANES, 0, -1))], axis=1)
    dec = dec.reshape(N_PAIRS, 2, DEC_ROWS, SSM_STATE).transpose(0, 2, 1, 3)
    dec = dec.reshape(N_PAIRS, DEC_ROWS, PAIR_STATE)
    return w1.astype(BF16), w2.astype(BF16), dec


def _ssm_body(u_ref, w1_ref, w2_ref, dec_ref, z_ref, sh_ref, *, B, n_chunks):
    R = B * n_chunks
    tiles = [(r0, min(SSM_ROW_TILE, R - r0)) for r0 in range(0, R, SSM_ROW_TILE)]
    PS = PAIR_STATE
    n_blocks = n_chunks // SUBLANES

    for r0, n in tiles:
        t = jnp.dot(u_ref[0, r0:r0 + n, :].astype(BF16), w1_ref[0], preferred_element_type=F32)
        z_ref[0, r0:r0 + n, :] = t[:, :PAIR_W]
        sh_ref[r0:r0 + n, :] = t[:, PAIR_W:]

    sub = lax.broadcasted_iota(jnp.int32, (SUBLANES, PS), 0)

    def cmul(ar, ai, xr, xi):
        return ar * xr - ai * xi, ar * xi + ai * xr

    def tile_scan(xr, xi, base, reverse):
        for k, d in enumerate((1, 2, 4)):
            pr_ = dec_ref[0, base + 2 * k:base + 2 * k + 1, :]
            pi_ = dec_ref[0, base + 2 * k + 1:base + 2 * k + 2, :]
            keep = (sub < SUBLANES - d) if reverse else (sub >= d)
            shift = SUBLANES - d if reverse else d
            yr = jnp.where(keep, pltpu.roll(xr, shift, 0), 0.0)
            yi = jnp.where(keep, pltpu.roll(xi, shift, 0), 0.0)
            mr, mi = cmul(pr_, pi_, yr, yi)
            xr, xi = xr + mr, xi + mi
        return xr, xi

    def scan_tile(i, carry):
        rows = pl.ds(pl.multiple_of(i * SUBLANES, SUBLANES), SUBLANES)
        xr, xi = tile_scan(sh_ref[rows, 0:PS], sh_ref[rows, PS:2 * PS], 0, False)
        sh_ref[rows, 0:PS] = xr
        sh_ref[rows, PS:2 * PS] = xi
        xr, xi = tile_scan(sh_ref[rows, 2 * PS:3 * PS], sh_ref[rows, 3 * PS:4 * PS], DEC_ROWS // 2, True)
        sh_ref[rows, 2 * PS:3 * PS] = xr
        sh_ref[rows, 3 * PS:4 * PS] = xi
        return carry

    lax.fori_loop(0, R // SUBLANES, scan_tile, 0, unroll=2)

    def block_scan(xr, xi, cr, ci, base, reverse):
        qr_ = dec_ref[0, base + 8:base + 16, :]
        qi_ = dec_ref[0, base + 16:base + 24, :]
        mr, mi = cmul(qr_, qi_, cr, ci)
        xr, xi = xr + mr, xi + mi
        edge = SUBLANES - 1 if reverse else 0
        shift = SUBLANES - 1 if reverse else 1
        er = jnp.where(sub == edge, cr, pltpu.roll(xr, shift, 0))
        ei = jnp.where(sub == edge, ci, pltpu.roll(xi, shift, 0))
        last = 0 if reverse else SUBLANES - 1
        nr = jnp.broadcast_to(xr[last:last + 1, :], (SUBLANES, PS))
        ni = jnp.broadcast_to(xi[last:last + 1, :], (SUBLANES, PS))
        return er, ei, nr, ni

    def scan_step(i, carry):
        new = []
        for b in range(B):
            cfr, cfi, cbr, cbi = carry[4 * b:4 * b + 4]
            tile_f = (i + n_blocks - 1) % n_blocks
            tile_b = (2 * n_blocks - 2 - i) % n_blocks
            rf = pl.multiple_of(b * n_chunks + tile_f * SUBLANES, SUBLANES)
            rb = pl.multiple_of(b * n_chunks + tile_b * SUBLANES, SUBLANES)
            f_rows, b_rows = pl.ds(rf, SUBLANES), pl.ds(rb, SUBLANES)
            er, ei, cfr, cfi = block_scan(sh_ref[f_rows, 0:PS], sh_ref[f_rows, PS:2 * PS],
                                          cfr, cfi, 0, False)
            sh_ref[f_rows, 0:PS] = er
            sh_ref[f_rows, PS:2 * PS] = ei
            er, ei, cbr, cbi = block_scan(sh_ref[b_rows, 2 * PS:3 * PS], sh_ref[b_rows, 3 * PS:4 * PS],
                                          cbr, cbi, DEC_ROWS // 2, True)
            sh_ref[b_rows, 2 * PS:3 * PS] = er
            sh_ref[b_rows, 3 * PS:4 * PS] = ei
            new += [cfr, cfi, cbr, cbi]
        return tuple(new)

    zero = jnp.zeros((SUBLANES, PS), F32)
    lax.fori_loop(0, n_blocks, scan_step, (zero,) * (4 * B))

    for r0, n in tiles:
        y = z_ref[0, r0:r0 + n, :] + jnp.dot(sh_ref[r0:r0 + n, :].astype(BF16), w2_ref[0],
                                             preferred_element_type=F32)
        z_ref[0, r0:r0 + n, :] = jax.nn.gelu(y)


def _ssm(u_pairs, w1, w2, dec, B, n_chunks):
    assert n_chunks % SUBLANES == 0
    R = B * n_chunks
    return pl.pallas_call(
        functools.partial(_ssm_body, B=B, n_chunks=n_chunks),
        grid=(N_PAIRS,),
        in_specs=[
            pl.BlockSpec((1, R, PAIR_W), lambda p: (p, 0, 0)),
            pl.BlockSpec((1, PAIR_W, 2 * PAIR_W), lambda p: (p, 0, 0)),
            pl.BlockSpec((1, PAIR_W, PAIR_W), lambda p: (p, 0, 0)),
            pl.BlockSpec((1, DEC_ROWS, PAIR_STATE), lambda p: (p, 0, 0)),
        ],
        out_specs=pl.BlockSpec((1, R, PAIR_W), lambda p: (p, 0, 0)),
        out_shape=jax.ShapeDtypeStruct((N_PAIRS, R, PAIR_W), F32),
        scratch_shapes=[pltpu.VMEM((R, 4 * PAIR_STATE), F32)],
        compiler_params=_cparams(("parallel",)),
        name="ssm",
    )(u_pairs, w1, w2, dec)


def _mix_body(o_ref, zp_ref, h0_ref, wglu_ref, bglu_ref, woa_ref, wos_ref, g_ref, b_ref, wr_ref,
              h1_ref, acc_ref, aff_ref, z_scr, *, L, Lp):
    n_rows = zp_ref.shape[1]
    lane16 = lax.broadcasted_iota(jnp.int32, (n_rows, LANES), 1) // SSM_GROUP
    for t in range(CHUNK):
        src = (t % SUBLANES) * SSM_GROUP
        for v in range(SSM_WIDTH // LANES):
            pieces = []
            for j in range(SUBLANES):
                p, gi = 4 * v + j // 2, j % 2
                lo = gi * CHUNK * SSM_GROUP + (t // SUBLANES) * LANES
                pieces.append((zp_ref[p, :, lo:lo + LANES], src))
            z_scr[v, pl.ds(t, n_rows, stride=CHUNK), :] = _lane_piece_gather(pieces, lane16)
    z = jnp.concatenate([z_scr[v] for v in range(SSM_WIDTH // LANES)], axis=1)
    gate = jax.nn.sigmoid(jnp.dot(z.astype(BF16), wglu_ref[...], preferred_element_type=F32)
                          + bglu_ref[...])
    ssm_out = (z * gate).astype(BF16)
    mix = (jnp.dot(o_ref[...], woa_ref[...], preferred_element_type=F32)
           + jnp.dot(ssm_out, wos_ref[...], preferred_element_type=F32))
    h1 = _layer_norm_rows(DEEPNORM_ALPHA * h0_ref[...] + mix, g_ref[...], b_ref[...])
    for s in range(ROW_SLABS):
        acc_ref[:, s, :] = DEEPNORM_ALPHA * h1[:, s * LANES:(s + 1) * LANES]
    h1_bits = pltpu.bitcast(h1.astype(BF16).astype(F32), jnp.uint32)
    for s in range(ROW_SLABS // 2):
        lo = h1_bits[:, s * LANES:(s + 1) * LANES]
        hi = h1_bits[:, (s + ROW_SLABS // 2) * LANES:(s + ROW_SLABS // 2 + 1) * LANES]
        h1_ref[:, s, :] = hi | lax.shift_right_logical(lo, jnp.uint32(16))
    logits = jnp.dot(h1.astype(BF16), wr_ref[...], preferred_element_type=F32)
    lane = lax.broadcasted_iota(jnp.int32, logits.shape, 1)
    logits = jnp.where(lane < N_EXPERTS, logits, NEG_BIG)
    e = jnp.exp(logits - jnp.max(logits, axis=-1, keepdims=True))
    aff = (e / jnp.sum(e, axis=-1, keepdims=True)).T[:N_EXPERTS, :]
    tiles_per_seq = Lp // ROW_TILE
    pos = (pl.program_id(0) % tiles_per_seq) * ROW_TILE + lax.broadcasted_iota(
        jnp.int32, aff.shape, 1)
    aff_ref[...] = jnp.where((pos < L) | (pos >= L + PAD_ROWS), aff, -1.0)


def _mix(o, z, h0, wglu, bglu, woa, wos, g, b, wr, L, Lp):
    N = o.shape[0]
    TM = ROW_TILE
    row = lambda w: pl.BlockSpec((TM, w), lambda i: (i, 0))
    tok_rows = pl.BlockSpec((TM, ROW_SLABS, LANES), lambda i: (i, 0, 0))
    return pl.pallas_call(
        functools.partial(_mix_body, L=L, Lp=Lp),
        grid=(N // TM,),
        in_specs=[row(ATTN_WIDTH), pl.BlockSpec((N_PAIRS, TM // CHUNK, PAIR_W), lambda i: (0, i, 0)),
                  row(D_MODEL),
                  _const_spec(SSM_WIDTH, SSM_WIDTH), _const_spec(1, SSM_WIDTH),
                  _const_spec(ATTN_WIDTH, D_MODEL), _const_spec(SSM_WIDTH, D_MODEL),
                  _const_spec(1, D_MODEL), _const_spec(1, D_MODEL), _const_spec(D_MODEL, LANES)],
        out_specs=[pl.BlockSpec((TM, ROW_SLABS // 2, LANES), lambda i: (i, 0, 0)), tok_rows,
                   pl.BlockSpec((N_EXPERTS, TM), lambda i: (0, i))],
        out_shape=[jax.ShapeDtypeStruct((N, ROW_SLABS // 2, LANES), jnp.uint32),
                   jax.ShapeDtypeStruct((N, ROW_SLABS, LANES), F32),
                   jax.ShapeDtypeStruct((N_EXPERTS, N), F32)],
        scratch_shapes=[pltpu.VMEM((SSM_WIDTH // LANES, TM, LANES), F32)],
        compiler_params=_cparams(("parallel",)),
        name="mix",
    )(o, z, h0, wglu, bglu, woa, wos, g, b, wr)


def _select_body(aff_ref, ord_ref, idx_ref, thr_ref, *, capacity, n_slot_tiles, seq_len, pad_start):
    nbp = aff_ref.shape[1]
    ri = lax.broadcasted_iota(jnp.int32, (LANES, LANES), 0)
    ci = lax.broadcasted_iota(jnp.int32, (LANES, LANES), 1)
    strict_upper = (ri < ci).astype(BF16)
    incl_upper = (ri <= ci).astype(BF16)
    ones = jnp.ones((LANES, LANES), BF16)
    order = ord_ref[...]
    blk = lax.broadcasted_iota(jnp.int32, (nbp, LANES), 0).astype(F32)
    blk_hi = jnp.floor(blk * (1.0 / 16.0))
    blk_lo = blk - 16.0 * blk_hi
    dot = functools.partial(jnp.dot, preferred_element_type=F32)
    dot_nt = functools.partial(lax.dot_general, dimension_numbers=(((1,), (1,)), ((), ())),
                               preferred_element_type=F32)

    def search_bit(i, ts):
        bit = jnp.left_shift(jnp.int32(1), 30 - i)
        new = []
        for e in range(N_EXPERTS):
            cand = ts[e] | bit
            bits = pltpu.bitcast(aff_ref[e], jnp.int32)
            cnt = jnp.sum((bits >= cand).astype(jnp.int32), keepdims=True)
            new.append(jnp.where(cnt >= capacity, cand, ts[e]))
        return tuple(new)

    thresholds = lax.fori_loop(0, 31, search_bit, (jnp.zeros((1, 1), jnp.int32),) * N_EXPERTS)
    for e in range(N_EXPERTS):
        thr_ref[e] = jnp.broadcast_to(thresholds[e], (SUBLANES, LANES))

    def per_expert(e, carry):
        bits = pltpu.bitcast(aff_ref[e], jnp.int32)
        t = thr_ref[e][0:1, 0:1]
        gt = bits > t
        eq = bits == t
        need = (capacity - jnp.sum(gt.astype(jnp.int32), keepdims=True)).astype(F32)
        eqb = eq.astype(BF16)
        tie_rank = dot(order, dot(eqb, ones).astype(BF16)) + dot(eqb, strict_upper)
        sel = (gt | (eq & (tie_rank < need))).astype(BF16)
        csum_in_blk = dot(sel, incl_upper)
        blk_tot = dot(sel, ones)
        blk_off = dot(order, blk_tot.astype(BF16))
        off_hi = jnp.floor(blk_off * (1.0 / 64.0))
        off_lo = blk_off - 64.0 * off_hi
        table = jnp.concatenate([csum_in_blk, off_hi, off_lo, blk_hi, blk_lo],
                                axis=1).astype(BF16)
        tot_l = dot_nt(ones[:SUBLANES], sel)
        off_l = dot_nt(tot_l.astype(BF16), order)[0:1, :]
        tot_l = tot_l[0:1, :]

        def per_tile(st, carry2):
            j = (st * SLOT_TILE + lax.broadcasted_iota(jnp.int32, (SLOT_TILE, 1), 0)).astype(F32)
            onehot = ((off_l <= j) & (j < off_l + tot_l)).astype(BF16)
            got = dot(onehot, table)
            local = j - (64.0 * got[:, LANES:2 * LANES] + got[:, 2 * LANES:3 * LANES])
            lane_idx = dot((got[:, :LANES] <= local).astype(BF16), ones)
            block = 16.0 * got[:, 3 * LANES:4 * LANES] + got[:, 4 * LANES:5 * LANES]
            k = j - capacity
            seq = sum((k >= PAD_ROWS * i).astype(F32) for i in range(1, SLOT_TILE // PAD_ROWS + 1))
            pad_tok = seq * (seq_len - PAD_ROWS) + pad_start + k
            tok = jnp.where(j < capacity, block * LANES + lane_idx, pad_tok)
            idx_ref[e, st] = tok.T[0:1, :].astype(jnp.int32)
            return carry2

        lax.fori_loop(0, n_slot_tiles, per_tile, 0)
        return carry

    lax.fori_loop(0, N_EXPERTS, per_expert, 0)


def _select(aff_blocks, order, capacity, n_slot_tiles, seq_len, pad_start):
    nbp = aff_blocks.shape[1]
    return pl.pallas_call(
        functools.partial(_select_body, capacity=capacity, n_slot_tiles=n_slot_tiles,
                          seq_len=seq_len, pad_start=pad_start),
        grid=(1,),
        in_specs=[_const_spec(N_EXPERTS, nbp, LANES), _const_spec(nbp, nbp)],
        out_specs=_const_spec(N_EXPERTS, n_slot_tiles, 1, SLOT_TILE),
        out_shape=jax.ShapeDtypeStruct((N_EXPERTS, n_slot_tiles, 1, SLOT_TILE), jnp.int32),
        scratch_shapes=[pltpu.VMEM((N_EXPERTS, SUBLANES, LANES), jnp.int32)],
        compiler_params=_cparams(("arbitrary",)),
        name="select",
    )(aff_blocks, order)


def _moe_body(idx_ref, idx_next_ref, h1_hbm, wg_ref, wu_ref, wd_ref, wr_ref, acc_in, acc_hbm,
              xbuf, abuf, hid_ref, sem_x, sem_a, sem_s):
    del acc_in
    T = xbuf.shape[1]
    n_c = pl.num_programs(1)
    step = pl.program_id(0) * n_c + pl.program_id(1)
    n_steps = pl.num_programs(0) * n_c
    slot = step % 2

    def start_rows(copy_of_row, idx):
        def body(j, carry):
            copy_of_row(idx[0, 0, 0, j], j).start()
            return carry
        lax.fori_loop(0, T, body, 0, unroll=8)

    def x_row(buf):
        return lambda row, j: pltpu.make_async_copy(h1_hbm.at[row], xbuf.at[buf, j], sem_x.at[buf])

    def acc_row_in(row, j):
        return pltpu.make_async_copy(acc_hbm.at[row], abuf.at[j], sem_a)

    def acc_row_out(row, j):
        return pltpu.make_async_copy(abuf.at[j], acc_hbm.at[row], sem_s)

    def wait_scatter():
        pltpu.make_async_copy(abuf, acc_hbm.at[pl.ds(0, T)], sem_s).wait()

    @pl.when(step == 0)
    def _():
        start_rows(x_row(0), idx_ref)

    @pl.when(step + 1 < n_steps)
    def _():
        start_rows(x_row(1 - slot), idx_next_ref)

    pltpu.make_async_copy(h1_hbm.at[pl.ds(0, T)], xbuf.at[slot], sem_x.at[slot]).wait()
    packed = [xbuf[slot, :, s, :] for s in range(ROW_SLABS // 2)]
    lo = [pltpu.bitcast(lax.shift_left(u, jnp.uint32(16)), F32) for u in packed]
    hi = [pltpu.bitcast(u & jnp.uint32(0xFFFF0000), F32) for u in packed]
    x = jnp.concatenate(lo + hi, axis=1).astype(BF16)
    logits = jnp.dot(x, wr_ref[...], preferred_element_type=F32)
    lane = lax.broadcasted_iota(jnp.int32, logits.shape, 1)
    logits = jnp.where(lane < N_EXPERTS, logits, NEG_BIG)
    p = jnp.exp(logits - jnp.max(logits, axis=-1, keepdims=True))
    gate = (jnp.sum(jnp.where(lane == pl.program_id(0), p, 0.0), axis=-1, keepdims=True)
            / jnp.sum(p, axis=-1, keepdims=True))
    for f in range(EXPERT_FF // FF_TILE):
        fs = slice(f * FF_TILE, (f + 1) * FF_TILE)
        hg = jnp.dot(x, wg_ref[0, :, fs], preferred_element_type=F32)
        hu = jnp.dot(x, wu_ref[0, :, fs], preferred_element_type=F32)
        hid_ref[:, fs] = (jax.nn.silu(hg) * hu).astype(BF16)
        if f == 0:
            @pl.when(step > 0)
            def _():
                wait_scatter()

            start_rows(acc_row_in, idx_ref)
    pltpu.make_async_copy(acc_hbm.at[pl.ds(0, T)], abuf, sem_a).wait()
    for n0 in range(0, D_MODEL, 2 * LANES):
        y = jnp.dot(hid_ref[...], wd_ref[0, :, n0:n0 + 2 * LANES], preferred_element_type=F32) * gate
        for h in range(2):
            s = n0 // LANES + h
            abuf[:, s, :] = abuf[:, s, :] + y[:, h * LANES:(h + 1) * LANES]
    start_rows(acc_row_out, idx_ref)

    @pl.when(step == n_steps - 1)
    def _():
        wait_scatter()


def _moe(idx, h1, acc, wg, wu, wd, wr):
    n_slots = idx.shape[1] * idx.shape[3]
    n_c = next(n for n in range(1, n_slots) if n_slots % n == 0 and n_slots // n <= MOE_TILE_MAX
               and (n_slots // n) % 16 == 0)
    T = n_slots // n_c
    idx = idx.reshape(N_EXPERTS, n_c, 1, T)

    def next_block(e, c):
        return (jnp.minimum(e + (c + 1) // n_c, N_EXPERTS - 1), (c + 1) % n_c, 0, 0)

    return pl.pallas_call(
        _moe_body,
        grid=(N_EXPERTS, n_c),
        in_specs=[
            pl.BlockSpec((1, 1, 1, T), lambda e, c: (e, c, 0, 0), memory_space=pltpu.SMEM),
            pl.BlockSpec((1, 1, 1, T), next_block, memory_space=pltpu.SMEM),
            pl.BlockSpec(memory_space=pl.ANY),
            pl.BlockSpec((1, D_MODEL, EXPERT_FF), lambda e, c: (e, 0, 0)),
            pl.BlockSpec((1, D_MODEL, EXPERT_FF), lambda e, c: (e, 0, 0)),
            pl.BlockSpec((1, EXPERT_FF, D_MODEL), lambda e, c: (e, 0, 0)),
            _const_spec(D_MODEL, LANES),
            pl.BlockSpec(memory_space=pl.ANY),
        ],
        out_specs=pl.BlockSpec(memory_space=pl.ANY),
        out_shape=jax.ShapeDtypeStruct(acc.shape, F32),
        scratch_shapes=[pltpu.VMEM((2, T, ROW_SLABS // 2, LANES), jnp.uint32),
                        pltpu.VMEM((T, ROW_SLABS, LANES), F32),
                        pltpu.VMEM((T, EXPERT_FF), BF16),
                        pltpu.SemaphoreType.DMA((2,)), pltpu.SemaphoreType.DMA(()),
                        pltpu.SemaphoreType.DMA(())],
        input_output_aliases={7: 0},
        compiler_params=pltpu.CompilerParams(dimension_semantics=("arbitrary", "arbitrary"),
                                             vmem_limit_bytes=VMEM_LIMIT,
                                             disable_bounds_checks=True),
        name="expert_ffn",
    )(idx, idx, h1, wg, wu, wd, wr, acc)


def _final_body(a_ref, g_ref, b_ref, o_ref):
    x = jnp.concatenate([a_ref[0, :, s, :] for s in range(ROW_SLABS)], axis=1)
    o_ref[0] = _layer_norm_rows(x, g_ref[...], b_ref[...])


def _final_norm(acc, g, b, L):
    B = acc.shape[0]
    TM = EMBED_TILE
    return pl.pallas_call(
        _final_body,
        grid=(B, L // TM),
        in_specs=[pl.BlockSpec((1, TM, ROW_SLABS, LANES), lambda b_, j: (b_, j, 0, 0)),
                  _const_spec(1, D_MODEL), _const_spec(1, D_MODEL)],
        out_specs=pl.BlockSpec((1, TM, D_MODEL), lambda b_, j: (b_, j, 0)),
        out_shape=jax.ShapeDtypeStruct((B, L, D_MODEL), F32),
        compiler_params=_cparams(("parallel", "parallel")),
        name="final_norm",
    )(acc, g, b)


def _rope_tables(L, Lp):
    rows = L // GRID_W
    t = jnp.arange(L, dtype=jnp.int32)
    m = jnp.arange(N_META, dtype=jnp.int32)
    pad = jnp.zeros((Lp - L - N_META,), jnp.int32)
    row = jnp.concatenate([t // GRID_W - rows // 2, pad,
                           jnp.full((N_META,), -(rows // 2) - 1, jnp.int32)])
    col = jnp.concatenate([t % GRID_W - GRID_W // 2, pad, m - GRID_W // 2])
    inv_freq = ROPE_THETA ** (-jnp.arange(0, ROPE_AXIS_DIM, 2, dtype=F32) / ROPE_AXIS_DIM)
    ang_r = row.astype(F32)[:, None] * inv_freq
    ang_c = col.astype(F32)[:, None] * inv_freq
    cos = jnp.concatenate([jnp.cos(ang_r)] * 2 + [jnp.cos(ang_c)] * 2, axis=1)
    sin = jnp.concatenate([-jnp.sin(ang_r), jnp.sin(ang_r), -jnp.sin(ang_c), jnp.sin(ang_c)], axis=1)
    return jnp.tile(cos, (1, 2)), jnp.tile(sin, (1, 2))


def _block_order(B, Lp, nbp):
    nbb = Lp // LANES
    r = jnp.arange(nbp)
    b, jb = r // nbb, r % nbb
    rank = jnp.where(r < B * nbb, b * nbb + jnp.where(jb == nbb - 1, 0, jb + 1), r)
    return (rank[None, :] < rank[:, None]).astype(BF16)


def _run_trunk(x, meta_pad, shared):
    B, L, _ = x.shape
    Lp = L + TAIL
    N = B * Lp
    cos, sin = _rope_tables(L, Lp)
    h0, q, k, vts, u = _embed(x, meta_pad, shared["ln_emb_g"], shared["ln_emb_b"], shared["w_in"],
                              shared["qg"], shared["kg"], shared["bones"], cos, sin, L, Lp)
    o = _attention(q, k, vts, L, Lp, shared["score_bound"])
    z = _ssm(u.reshape(N_PAIRS, N // CHUNK, PAIR_W), shared["ssm_w1"], shared["ssm_w2"],
             shared["ssm_dec"], B, Lp // CHUNK)
    h1, acc, aff = _mix(o.reshape(N, ATTN_WIDTH), z, h0.reshape(N, D_MODEL),
                        shared["w_glu"], shared["b_glu"], shared["w_out_attn"], shared["w_out_ssm"],
                        shared["ln1_g"], shared["ln1_b"], shared["w_router"], L, Lp)

    capacity = EC_CAPACITY_FACTOR * B * (L + N_META) // N_EXPERTS
    n_slot_tiles = -(-capacity // SLOT_TILE)
    nb = N // LANES
    nbp = -(-nb // LANES) * LANES
    aff_blocks = jnp.pad(aff.reshape(N_EXPERTS, nb, LANES), ((0, 0), (0, nbp - nb), (0, 0)),
                         constant_values=-1.0)
    assert n_slot_tiles * SLOT_TILE - capacity <= B * PAD_ROWS
    idx = _select(aff_blocks, _block_order(B, Lp, nbp), capacity, n_slot_tiles, Lp, L)
    acc = _moe(idx, h1, acc, shared["w_gate"], shared["w_up"], shared["w_down"], shared["w_router"])
    return _final_norm(acc.reshape(B, Lp, ROW_SLABS, LANES), shared["ln2_g"], shared["ln2_b"], L)


def kernel(x_prompt, x_sample, meta_tokens, ln_emb_g, ln_emb_b, w_in, q_norm_g, k_norm_g, ssm_lambda_re, ssm_lambda_im, ssm_log_dt, ssm_b_re, ssm_b_im, ssm_c_re, ssm_c_im, ssm_d, w_glu, b_glu, w_out, ln1_g, ln1_b, w_router, w_gate, w_up, w_down, ln2_g, ln2_b):
    row = lambda a: a.reshape(1, -1).astype(F32)
    w_q = w_in[0][:, :Q_END].reshape(D_MODEL, N_KV_HEADS, Q_PER_KV, HEAD_DIM)
    w_q = w_q.transpose(0, 2, 1, 3).reshape(D_MODEL, Q_END)
    w_oa = w_out[0][:ATTN_WIDTH].reshape(N_KV_HEADS, Q_PER_KV, HEAD_DIM, D_MODEL)
    w_oa = w_oa.transpose(1, 0, 2, 3).reshape(ATTN_WIDTH, D_MODEL)
    head_of = jnp.arange(Q_END) // HEAD_DIM
    ssm_w1, ssm_w2, ssm_dec = _ssm_weights(ssm_lambda_re[0], ssm_lambda_im[0], ssm_log_dt[0],
                                           ssm_b_re[0], ssm_b_im[0], ssm_c_re[0], ssm_c_im[0],
                                           ssm_d[0])
    shared = dict(
        ln_emb_g=row(ln_emb_g), ln_emb_b=row(ln_emb_b),
        w_in=jnp.concatenate([w_q, w_in[0][:, Q_END:]], axis=1).astype(BF16),
        qg=row(jnp.tile(q_norm_g[0], N_Q_HEADS)), kg=row(jnp.tile(k_norm_g[0], N_KV_HEADS)),
        bones=(head_of[:, None] == head_of[None, :]).astype(BF16),
        score_bound=(1.02 * HEAD_DIM * Q_PRESCALE * jnp.max(jnp.abs(q_norm_g[0]))
                     * jnp.max(jnp.abs(k_norm_g[0]))).astype(F32),
        ssm_w1=ssm_w1, ssm_w2=ssm_w2, ssm_dec=ssm_dec,
        w_glu=w_glu[0].astype(BF16), b_glu=row(b_glu[0]),
        w_out_attn=w_oa.astype(BF16), w_out_ssm=w_out[0][ATTN_WIDTH:].astype(BF16),
        ln1_g=row(ln1_g[0]), ln1_b=row(ln1_b[0]),
        w_router=jnp.pad(w_router[0], ((0, 0), (0, LANES - N_EXPERTS))).astype(BF16),
        w_gate=w_gate[0].astype(BF16), w_up=w_up[0].astype(BF16), w_down=w_down[0].astype(BF16),
        ln2_g=row(ln2_g[0]), ln2_b=row(ln2_b[0]),
    )
    meta_pad = jnp.pad(meta_tokens.astype(F32), ((PAD_ROWS, 0), (0, 0)))[None]
    return (_run_trunk(x_prompt, meta_pad, shared), _run_trunk(x_sample, meta_pad, shared))
```

```python
import functools
import math

import jax
import jax.numpy as jnp
from jax import lax
from jax.experimental import pallas as pl
from jax.experimental.pallas import tpu as pltpu

F32 = jnp.float32
BF16 = jnp.bfloat16

D_MODEL = 1024
N_META = 16
GRID_W = 64
ATTN_WIDTH = 512
SSM_WIDTH = 512
HEAD_DIM = 64
N_Q_HEADS = 8
N_KV_HEADS = 2
Q_PER_KV = 4
KV_WIDTH = 128
ROPE_AXIS_DIM = 32
ROPE_THETA = 10000.0
ATTN_SCALE = HEAD_DIM ** -0.5
RMS_EPS = 1e-6
SSM_GROUP = 16
N_SSM_GROUPS = 32
SSM_STATE = 64
Q_END = ATTN_WIDTH
K_END = Q_END + KV_WIDTH
V_END = K_END + KV_WIDTH
IN_WIDTH = V_END + SSM_WIDTH
N_EXPERTS = 16
EXPERT_FF = 2048
EC_CAPACITY_FACTOR = 2
LN_EPS = 1e-5
DEPTH = 1
DEEPNORM_ALPHA = (2 * DEPTH) ** 0.25

LANES = 128
SUBLANES = 8
ROW_SLABS = D_MODEL // LANES
TAIL = LANES
PAD_ROWS = TAIL - N_META
CHUNK = 16
N_PAIRS = N_SSM_GROUPS // 2
PAIR_W = 2 * CHUNK * SSM_GROUP
PAIR_STATE = 2 * SSM_STATE
EMBED_TILE = 512
ROW_TILE = 384
Q_TILE = ROW_TILE
KV_TILE = EMBED_TILE
KEY_SUBTILE = 256
KV_UNROLL = 4
KV_UNROLL_BOUNDED = 16
SLOT_TILE = 384
MOE_TILE_MAX = 1056
FF_TILE = 512
SSM_ROW_TILE = 256
DEC_ROWS = 48
VT_ROWS = HEAD_DIM + 16
Q_PRESCALE = ATTN_SCALE * math.log2(math.e)
NEG_BIG = -1e30
SAFE_EXP2_RANGE = 60.0
VMEM_LIMIT = 56 * 1024 * 1024


def _cparams(sem):
    return pltpu.CompilerParams(dimension_semantics=sem, vmem_limit_bytes=VMEM_LIMIT)


def _const_spec(*shape):
    return pl.BlockSpec(shape, lambda *idx: (0,) * len(shape))


def _layer_norm_rows(x, g, b):
    mu = jnp.mean(x, axis=-1, keepdims=True)
    xc = x - mu
    var = jnp.mean(xc * xc, axis=-1, keepdims=True)
    return xc * lax.rsqrt(var + LN_EPS) * g + b


def _head_rms(t, gain, bones):
    sq = t * t
    hi = sq.astype(BF16)
    lo = (sq - hi.astype(F32)).astype(BF16)
    ss = (jnp.dot(hi, bones, preferred_element_type=F32)
          + jnp.dot(lo, bones, preferred_element_type=F32))
    return t * lax.rsqrt(ss * (1.0 / HEAD_DIM) + RMS_EPS) * gain


def _rope_slab(t, cos, sin_signed):
    lane = lax.broadcasted_iota(jnp.int32, t.shape, 1)
    first = (lane % ROPE_AXIS_DIM) < (ROPE_AXIS_DIM // 2)
    partner = jnp.where(first, pltpu.roll(t, LANES - 16, 1), pltpu.roll(t, 16, 1))
    return t * cos + partner * sin_signed


def _lane_piece_gather(pieces, lane16):
    out = None
    for j, (arr, src) in enumerate(pieces):
        shift = (SSM_GROUP * j - src) % LANES
        moved = pltpu.roll(arr, shift, 1) if shift else arr
        out = moved if out is None else jnp.where(lane16 == j, moved, out)
    return out


def _embed_body(x_ref, g_ref, b_ref, w_ref, qg_ref, kg_ref, bones_ref, cos_ref, sin_ref,
                h_ref, q_ref, k_ref, vt0_ref, vt1_ref, u_ref, u_scr):
    h = _layer_norm_rows(x_ref[0], g_ref[...], b_ref[...])
    h_ref[0] = h
    proj = jnp.dot(h.astype(BF16), w_ref[...], preferred_element_type=F32)
    cos = cos_ref[...]
    sin = sin_ref[...]
    qn = _head_rms(proj[:, :Q_END], qg_ref[...], bones_ref[...])
    for s in range(ATTN_WIDTH // LANES):
        sl = slice(s * LANES, (s + 1) * LANES)
        q_ref[0, :, sl] = (_rope_slab(qn[:, sl], cos, sin) * Q_PRESCALE).astype(BF16)
    kn = _head_rms(proj[:, Q_END:K_END], kg_ref[...], bones_ref[:KV_WIDTH, :KV_WIDTH])
    k_ref[0] = _rope_slab(kn, cos, sin).astype(BF16)
    vt = proj[:, K_END:V_END].T
    ones = jnp.ones((VT_ROWS - HEAD_DIM, vt.shape[1]), F32)
    vt0_ref[0, 0] = jnp.concatenate([vt[:HEAD_DIM], ones], axis=0).astype(BF16)
    vt1_ref[0, 0] = jnp.concatenate([vt[HEAD_DIM:], ones], axis=0).astype(BF16)
    n_rows = u_scr.shape[1] // CHUNK
    for v in range(SSM_WIDTH // LANES):
        u_scr[v] = proj[:, V_END + v * LANES:V_END + (v + 1) * LANES]
    by_token = [[u_scr[v, pl.ds(t, n_rows, stride=CHUNK), :] for v in range(SSM_WIDTH // LANES)]
                for t in range(CHUNK)]
    lane16 = lax.broadcasted_iota(jnp.int32, (n_rows, LANES), 1) // SSM_GROUP
    for p in range(N_PAIRS):
        for gi in range(2):
            src = (p % 4) * 2 * SSM_GROUP + gi * SSM_GROUP
            for h in range(CHUNK // SUBLANES):
                pieces = [(by_token[SUBLANES * h + j][p // 4], src) for j in range(SUBLANES)]
                lo = gi * CHUNK * SSM_GROUP + h * LANES
                u_ref[p, 0, :, lo:lo + LANES] = _lane_piece_gather(pieces, lane16)


def _embed_tail_body(x_ref, g_ref, b_ref, w_ref, qg_ref, kg_ref, bones_ref, cos_ref, sin_ref,
                     h_in, q_in, k_in, u_in, h_ref, q_ref, k_ref, vt0_ref, vt1_ref, u_ref, u_scr):
    del h_in, q_in, k_in, u_in
    _embed_body(x_ref, g_ref, b_ref, w_ref, qg_ref, kg_ref, bones_ref, cos_ref, sin_ref,
                h_ref, q_ref, k_ref, vt0_ref, vt1_ref, u_ref, u_scr)


def _embed(x, meta_pad, ln_g, ln_b, w_in, qg, kg, bones, cos, sin, L, Lp):
    B = x.shape[0]
    TM = EMBED_TILE
    nj = L // TM
    w_specs = [_const_spec(1, D_MODEL), _const_spec(1, D_MODEL), _const_spec(D_MODEL, IN_WIDTH),
               _const_spec(1, Q_END), _const_spec(1, KV_WIDTH), _const_spec(Q_END, Q_END)]

    def out_shapes(vt_tiles, vt_width):
        vt = jax.ShapeDtypeStruct((B, vt_tiles, VT_ROWS, vt_width), BF16)
        return [
            jax.ShapeDtypeStruct((B, Lp, D_MODEL), F32),
            jax.ShapeDtypeStruct((B, Lp, Q_END), BF16),
            jax.ShapeDtypeStruct((B, Lp, KV_WIDTH), BF16),
            vt, vt,
            jax.ShapeDtypeStruct((N_PAIRS, B, Lp // CHUNK, PAIR_W), F32),
        ]

    def out_specs(tm, row_block):
        return [
            pl.BlockSpec((1, tm, D_MODEL), lambda b, j: (b, row_block(j), 0)),
            pl.BlockSpec((1, tm, Q_END), lambda b, j: (b, row_block(j), 0)),
            pl.BlockSpec((1, tm, KV_WIDTH), lambda b, j: (b, row_block(j), 0)),
            pl.BlockSpec((1, 1, VT_ROWS, tm), lambda b, j: (b, j, 0, 0)),
            pl.BlockSpec((1, 1, VT_ROWS, tm), lambda b, j: (b, j, 0, 0)),
            pl.BlockSpec((N_PAIRS, 1, tm // CHUNK, PAIR_W), lambda b, j: (0, b, row_block(j), 0)),
        ]

    main = pl.pallas_call(
        _embed_body,
        grid=(B, nj),
        in_specs=[pl.BlockSpec((1, TM, D_MODEL), lambda b, j: (b, j, 0))] + w_specs
        + [pl.BlockSpec((TM, LANES), lambda b, j: (j, 0))] * 2,
        out_specs=out_specs(TM, lambda j: j),
        out_shape=out_shapes(nj, TM),
        scratch_shapes=[pltpu.VMEM((SSM_WIDTH // LANES, TM, LANES), F32)],
        compiler_params=_cparams(("parallel", "parallel")),
        name="embed_main",
    )
    h0, q, k, vt0, vt1, u = main(x, ln_g, ln_b, w_in, qg, kg, bones, cos, sin)

    jt = Lp // TAIL - 1
    tail = pl.pallas_call(
        _embed_tail_body,
        grid=(B, 1),
        in_specs=[pl.BlockSpec((1, TAIL, D_MODEL), lambda b, j: (0, 0, 0))] + w_specs
        + [pl.BlockSpec((TAIL, LANES), lambda b, j: (jt, 0))] * 2
        + [pl.BlockSpec(memory_space=pl.ANY)] * 4,
        out_specs=out_specs(TAIL, lambda j: jt),
        out_shape=out_shapes(1, TAIL),
        input_output_aliases={9: 0, 10: 1, 11: 2, 12: 5},
        scratch_shapes=[pltpu.VMEM((SSM_WIDTH // LANES, TAIL, LANES), F32)],
        compiler_params=_cparams(("parallel", "arbitrary")),
        name="embed_tail",
    )
    h0, q, k, vt0_tail, vt1_tail, u = tail(meta_pad, ln_g, ln_b, w_in, qg, kg, bones, cos, sin,
                                           h0, q, k, u)
    return h0, q, k, (vt0, vt1, vt0_tail, vt1_tail), u


def _attn_body(q_ref, k_ref, vt0_ref, vt1_ref, vt0t_ref, vt1t_ref, o_ref,
               qt_ref, s0_ref, s1_ref, mc0_ref, mc1_ref, m_ref, acc_ref, *, L, n_kv_tiles):
    cols = Q_PER_KV * Q_TILE
    _attn_load_qt(q_ref, qt_ref)
    m_ref[...] = jnp.full(m_ref.shape, NEG_BIG, F32)
    acc_ref[...] = jnp.zeros(acc_ref.shape, F32)

    s_bufs = ((s0_ref, mc0_ref), (s1_ref, mc1_ref))

    def scores(slot, k_tile, first_valid=0):
        s_ref, mc_ref = s_bufs[slot]
        n = k_tile.shape[0]
        for g in range(N_KV_HEADS):
            s = jnp.dot(k_tile, qt_ref[g], preferred_element_type=F32)
            if first_valid:
                key = lax.broadcasted_iota(jnp.int32, s.shape, 0)
                s = jnp.where(key >= first_valid, s, NEG_BIG)
            s_ref[g, 0:n, :] = s
            mc_ref[g] = jnp.broadcast_to(jnp.max(s, axis=0, keepdims=True), (SUBLANES, cols))

    def softmax_values(slot, vt_tiles, n):
        s_ref, mc_ref = s_bufs[slot]
        for g in range(N_KV_HEADS):
            m_old = m_ref[g]
            m_new = jnp.maximum(m_old, mc_ref[g])
            alpha = jnp.exp2(m_old[0:1, :] - m_new[0:1, :])
            acc = alpha * acc_ref[g]
            for k0 in range(0, n, KEY_SUBTILE):
                k1 = min(k0 + KEY_SUBTILE, n)
                p = jnp.exp2(s_ref[g, k0:k1, :] - m_new[0:1, :]).astype(BF16)
                acc = acc + jnp.dot(vt_tiles[g][:, k0:k1], p, preferred_element_type=F32)
            acc_ref[g] = acc
            m_ref[g] = m_new

    def k_tile(i):
        return k_ref[0, pl.ds(pl.multiple_of(i * KV_TILE, KV_TILE), KV_TILE), :]

    def vt_tiles(i):
        return vt0_ref[0, i], vt1_ref[0, i]

    scores(0, k_tile(0))

    def run_tiles(first, count, next_scores):
        for t in range(count):
            softmax_values(t % 2, vt_tiles(first + t), KV_TILE)
            if t + 1 < count:
                scores((t + 1) % 2, k_tile(first + t + 1))
            else:
                next_scores()

    def kv_group(i, carry):
        first = i * KV_UNROLL
        run_tiles(first, KV_UNROLL, lambda: scores(0, k_tile(first + KV_UNROLL)))
        return carry

    n_groups = n_kv_tiles // KV_UNROLL
    lax.fori_loop(0, n_groups - 1, kv_group, 0)
    run_tiles((n_groups - 1) * KV_UNROLL, KV_UNROLL,
              lambda: scores(0, k_ref[0, L:L + TAIL, :], PAD_ROWS))
    softmax_values(0, (vt0t_ref[0, 0], vt1t_ref[0, 0]), TAIL)

    _attn_store_out(acc_ref, o_ref)


def _attn_bounded_body(q_ref, k_ref, vt0_ref, vt1_ref, vt0t_ref, vt1t_ref, o_ref,
                       qt_ref, p0_ref, p1_ref, acc_ref, *, L, n_kv_tiles):
    _attn_load_qt(q_ref, qt_ref)
    acc_ref[...] = jnp.zeros(acc_ref.shape, F32)
    p_bufs = (p0_ref, p1_ref)
    unroll = math.gcd(n_kv_tiles, KV_UNROLL_BOUNDED)

    def key_blocks(k_tile, vt_tiles, accs, first_valid=0):
        n = k_tile.shape[0]
        accs = list(accs)
        for i, k0 in enumerate(range(0, n, KEY_SUBTILE)):
            k1 = min(k0 + KEY_SUBTILE, n)
            p_ref = p_bufs[i % 2]
            for g in range(N_KV_HEADS):
                s = jnp.dot(k_tile[k0:k1], qt_ref[g], preferred_element_type=F32)
                if first_valid:
                    key = k0 + lax.broadcasted_iota(jnp.int32, s.shape, 0)
                    s = jnp.where(key >= first_valid, s, NEG_BIG)
                p_ref[g, 0:k1 - k0, :] = jnp.exp2(s).astype(BF16)
            for g in range(N_KV_HEADS):
                accs[g] = accs[g] + jnp.dot(vt_tiles[g][:, k0:k1], p_ref[g, 0:k1 - k0, :],
                                            preferred_element_type=F32)
        return accs

    def kv_group(i, carry):
        accs = [acc_ref[g] for g in range(N_KV_HEADS)]
        for t in range(unroll):
            tile = i * unroll + t
            start = pl.multiple_of(tile * KV_TILE, KV_TILE)
            accs = key_blocks(k_ref[0, pl.ds(start, KV_TILE), :], (vt0_ref[0, tile], vt1_ref[0, tile]),
                              accs)
        for g in range(N_KV_HEADS):
            acc_ref[g] = accs[g]
        return carry

    lax.fori_loop(0, n_kv_tiles // unroll, kv_group, 0)
    accs = key_blocks(k_ref[0, L:L + TAIL, :], (vt0t_ref[0, 0], vt1t_ref[0, 0]),
                      [acc_ref[g] for g in range(N_KV_HEADS)], PAD_ROWS)
    for g in range(N_KV_HEADS):
        acc_ref[g] = accs[g]
    _attn_store_out(acc_ref, o_ref)


def _attn_load_qt(q_ref, qt_ref):
    tq = Q_TILE
    row = lax.broadcasted_iota(jnp.int32, (KV_WIDTH, tq), 0)
    for r in range(Q_PER_KV):
        slab_t = q_ref[0, :, r * LANES:(r + 1) * LANES].astype(F32).T
        for g in range(N_KV_HEADS):
            in_group = (row >= g * HEAD_DIM) & (row < (g + 1) * HEAD_DIM)
            qt_ref[g, :, r * tq:(r + 1) * tq] = jnp.where(in_group, slab_t, 0.0).astype(BF16)


def _attn_store_out(acc_ref, o_ref):
    tq = Q_TILE
    comb = jnp.concatenate(
        [acc_ref[g, 0:HEAD_DIM, :] / acc_ref[g, HEAD_DIM:HEAD_DIM + 1, :] for g in range(N_KV_HEADS)],
        axis=0)
    for r in range(Q_PER_KV):
        o_ref[0, :, r * LANES:(r + 1) * LANES] = comb[:, r * tq:(r + 1) * tq].T.astype(BF16)


def _attention(q, k, vts, L, Lp, score_bound):
    B = q.shape[0]
    nkt = L // KV_TILE
    assert nkt % KV_UNROLL == 0
    cols = Q_PER_KV * Q_TILE
    vt_spec = pl.BlockSpec((1, nkt, VT_ROWS, KV_TILE), lambda b, j: (b, 0, 0, 0))
    vt_tail_spec = pl.BlockSpec((1, 1, VT_ROWS, TAIL), lambda b, j: (b, 0, 0, 0))
    call = functools.partial(
        pl.pallas_call,
        grid=(B, Lp // Q_TILE),
        in_specs=[
            pl.BlockSpec((1, Q_TILE, Q_END), lambda b, j: (b, j, 0)),
            pl.BlockSpec((1, Lp, KV_WIDTH), lambda b, j: (b, 0, 0)),
            vt_spec, vt_spec, vt_tail_spec, vt_tail_spec,
        ],
        out_specs=pl.BlockSpec((1, Q_TILE, Q_END), lambda b, j: (b, j, 0)),
        out_shape=jax.ShapeDtypeStruct((B, Lp, Q_END), BF16),
        compiler_params=_cparams(("parallel", "arbitrary")),
    )
    bounded = call(
        functools.partial(_attn_bounded_body, L=L, n_kv_tiles=nkt),
        scratch_shapes=[
            pltpu.VMEM((N_KV_HEADS, KV_WIDTH, cols), BF16),
            pltpu.VMEM((N_KV_HEADS, KEY_SUBTILE, cols), BF16),
            pltpu.VMEM((N_KV_HEADS, KEY_SUBTILE, cols), BF16),
            pltpu.VMEM((N_KV_HEADS, VT_ROWS, cols), F32),
        ],
        name="attention_bounded",
    )
    general = call(
        functools.partial(_attn_body, L=L, n_kv_tiles=nkt),
        scratch_shapes=[
            pltpu.VMEM((N_KV_HEADS, KV_WIDTH, cols), BF16),
            pltpu.VMEM((N_KV_HEADS, KV_TILE, cols), F32),
            pltpu.VMEM((N_KV_HEADS, KV_TILE, cols), F32),
            pltpu.VMEM((N_KV_HEADS, SUBLANES, cols), F32),
            pltpu.VMEM((N_KV_HEADS, SUBLANES, cols), F32),
            pltpu.VMEM((N_KV_HEADS, SUBLANES, cols), F32),
            pltpu.VMEM((N_KV_HEADS, VT_ROWS, cols), F32),
        ],
        name="attention",
    )
    return lax.cond(score_bound <= SAFE_EXP2_RANGE, bounded, general, q, k, *vts)


def _ssm_weights(lam_re, lam_im, log_dt, b_re, b_im, c_re, c_im, d_skip):
    hp = lax.Precision.HIGHEST
    dt = jnp.exp(log_dt.astype(F32))[..., None]
    lr = lam_re.astype(F32)
    li = lam_im.astype(F32)

    def apow(n):
        n = jnp.asarray(n, F32)
        mag = jnp.exp(lr[..., None] * dt[..., None] * n)
        ang = li[..., None] * dt[..., None] * n
        return mag * jnp.cos(ang), mag * jnp.sin(ang)

    a1r, a1i = apow(jnp.ones((1,), F32))
    a1r, a1i = a1r[..., 0], a1i[..., 0]
    nr = a1r - 1.0
    den = lr * lr + li * li
    f_r = (nr * lr + a1i * li) / den
    f_i = (a1i * lr - nr * li) / den
    br = b_re.astype(F32)
    bi = b_im.astype(F32)
    bb_r = f_r[..., None] * br - f_i[..., None] * bi
    bb_i = f_r[..., None] * bi + f_i[..., None] * br
    cr = c_re.astype(F32)
    ci = c_im.astype(F32)

    tau = jnp.arange(CHUNK + 1, dtype=F32)
    pr, pi = apow(tau)
    ab_r = pr[..., None] * bb_r[:, :, :, None, :] - pi[..., None] * bb_i[:, :, :, None, :]
    ab_i = pr[..., None] * bb_i[:, :, :, None, :] + pi[..., None] * bb_r[:, :, :, None, :]
    kern = (jnp.einsum("dgop,dgptc->dgtco", cr, ab_r, precision=hp)
            - jnp.einsum("dgop,dgptc->dgtco", ci, ab_i, precision=hp))
    s_idx = jnp.arange(CHUNK)[:, None]
    t_idx = jnp.arange(CHUNK)[None, :]
    lag_f = jnp.clip(t_idx - s_idx, 0, CHUNK)
    lag_b = jnp.clip(s_idx - t_idx, 0, CHUNK)
    m_f = jnp.where((t_idx >= s_idx)[..., None, None], kern[0][:, lag_f], 0.0)
    m_b = jnp.where((s_idx >= t_idx)[..., None, None], kern[1][:, lag_b], 0.0)
    eye_t = jnp.eye(CHUNK, dtype=F32)[None, :, :, None, None]
    eye_c = jnp.eye(SSM_GROUP, dtype=F32)[None, None, None]
    dsk = d_skip.astype(F32).reshape(N_SSM_GROUPS, 1, 1, SSM_GROUP, 1)
    m_all = m_f + m_b + eye_t * eye_c * dsk
    m_all = m_all.transpose(0, 1, 3, 2, 4).reshape(N_SSM_GROUPS, 256, 256)

    def state_in(d, expo):
        er, ei = pr[d][..., expo], pi[d][..., expo]
        wr = er[..., None] * bb_r[d][:, :, None, :] - ei[..., None] * bb_i[d][:, :, None, :]
        wi = er[..., None] * bb_i[d][:, :, None, :] + ei[..., None] * bb_r[d][:, :, None, :]
        to_rows = lambda w: w.transpose(0, 2, 3, 1).reshape(N_SSM_GROUPS, 256, SSM_STATE)
        return to_rows(wr), to_rows(wi)

    sf_r, sf_i = state_in(0, jnp.arange(CHUNK - 1, -1, -1))
    sb_r, sb_i = state_in(1, jnp.arange(CHUNK))

    def state_out(d, expo):
        er, ei = pr[d][..., expo], pi[d][..., expo]
        wr = cr[d].transpose(0, 2, 1)[:, :, None, :] * er[..., None] \
            - ci[d].transpose(0, 2, 1)[:, :, None, :] * ei[..., None]
        wi = cr[d].transpose(0, 2, 1)[:, :, None, :] * ei[..., None] \
            + ci[d].transpose(0, 2, 1)[:, :, None, :] * er[..., None]
        flat = lambda w: w.reshape(N_SSM_GROUPS, SSM_STATE, 256)
        return flat(wr), flat(-wi)

    of_r, of_i = state_out(0, jnp.arange(1, CHUNK + 1))
    ob_r, ob_i = state_out(1, jnp.arange(CHUNK, 0, -1))

    def pair_rows(w):
        return w.reshape(N_PAIRS, 2, *w.shape[1:])

    z256 = jnp.zeros((N_PAIRS, 256, 256), F32)
    mp = pair_rows(m_all)
    w1_y = jnp.concatenate([jnp.concatenate([mp[:, 0], z256], axis=2),
                            jnp.concatenate([z256, mp[:, 1]], axis=2)], axis=1)

    def pair_cols_in(w):
        wp = pair_rows(w)
        z = jnp.zeros_like(wp[:, 0])
        return jnp.concatenate([jnp.concatenate([wp[:, 0], z], axis=2),
                                jnp.concatenate([z, wp[:, 1]], axis=2)], axis=1)

    w1 = jnp.concatenate([w1_y, pair_cols_in(sf_r), pair_cols_in(sf_i),
                          pair_cols_in(sb_r), pair_cols_in(sb_i)], axis=2)

    def pair_rows_out(w):
        wp = pair_rows(w)
        z = jnp.zeros_like(wp[:, 0])
        return jnp.concatenate([jnp.concatenate([wp[:, 0], z], axis=2),
                                jnp.concatenate([z, wp[:, 1]], axis=2)], axis=1)

    w2 = jnp.concatenate([pair_rows_out(of_r), pair_rows_out(of_i),
                          pair_rows_out(ob_r), pair_rows_out(ob_i)], axis=1)

    qr, qi = apow(CHUNK * jnp.arange(SUBLANES + 1, dtype=F32))
    zero = jnp.zeros_like(qr[..., 0])

    def dec_rows(d, q_expo):
        rows = [qr[d][..., 1], qi[d][..., 1], qr[d][..., 2], qi[d][..., 2],
                qr[d][..., 4], qi[d][..., 4], zero[d], zero[d]]
        rows += [qr[d][..., n] for n in q_expo] + [qi[d][..., n] for n in q_expo]
        return jnp.stack(rows, axis=1)

    dec = jnp.concatenate([dec_rows(0, range(1, SUBLANES + 1)),
                           dec_rows(1, range(SUBLANES, 0, -1))], axis=1)
    dec = dec.reshape(N_PAIRS, 2, DEC_ROWS, SSM_STATE).transpose(0, 2, 1, 3)
    dec = dec.reshape(N_PAIRS, DEC_ROWS, PAIR_STATE)
    return w1.astype(BF16), w2.astype(BF16), dec


def _ssm_body(u_ref, w1_ref, w2_ref, dec_ref, z_ref, sh_ref, *, B, n_chunks):
    R = B * n_chunks
    tiles = [(r0, min(SSM_ROW_TILE, R - r0)) for r0 in range(0, R, SSM_ROW_TILE)]
    PS = PAIR_STATE
    n_blocks = n_chunks // SUBLANES

    for r0, n in tiles:
        t = jnp.dot(u_ref[0, r0:r0 + n, :].astype(BF16), w1_ref[0], preferred_element_type=F32)
        z_ref[0, r0:r0 + n, :] = t[:, :PAIR_W]
        sh_ref[r0:r0 + n, :] = t[:, PAIR_W:]

    sub = lax.broadcasted_iota(jnp.int32, (SUBLANES, PS), 0)

    def cmul(ar, ai, xr, xi):
        return ar * xr - ai * xi, ar * xi + ai * xr

    def tile_scan(xr, xi, base, reverse):
        for k, d in enumerate((1, 2, 4)):
            pr_ = dec_ref[0, base + 2 * k:base + 2 * k + 1, :]
            pi_ = dec_ref[0, base + 2 * k + 1:base + 2 * k + 2, :]
            keep = (sub < SUBLANES - d) if reverse else (sub >= d)
            shift = SUBLANES - d if reverse else d
            yr = jnp.where(keep, pltpu.roll(xr, shift, 0), 0.0)
            yi = jnp.where(keep, pltpu.roll(xi, shift, 0), 0.0)
            mr, mi = cmul(pr_, pi_, yr, yi)
            xr, xi = xr + mr, xi + mi
        return xr, xi

    def scan_tile(i, carry):
        rows = pl.ds(pl.multiple_of(i * SUBLANES, SUBLANES), SUBLANES)
        xr, xi = tile_scan(sh_ref[rows, 0:PS], sh_ref[rows, PS:2 * PS], 0, False)
        sh_ref[rows, 0:PS] = xr
        sh_ref[rows, PS:2 * PS] = xi
        xr, xi = tile_scan(sh_ref[rows, 2 * PS:3 * PS], sh_ref[rows, 3 * PS:4 * PS], DEC_ROWS // 2, True)
        sh_ref[rows, 2 * PS:3 * PS] = xr
        sh_ref[rows, 3 * PS:4 * PS] = xi
        return carry

    lax.fori_loop(0, R // SUBLANES, scan_tile, 0, unroll=2)

    def block_scan(xr, xi, cr, ci, base, reverse):
        qr_ = dec_ref[0, base + 8:base + 16, :]
        qi_ = dec_ref[0, base + 16:base + 24, :]
        mr, mi = cmul(qr_, qi_, cr, ci)
        xr, xi = xr + mr, xi + mi
        edge = SUBLANES - 1 if reverse else 0
        shift = SUBLANES - 1 if reverse else 1
        er = jnp.where(sub == edge, cr, pltpu.roll(xr, shift, 0))
        ei = jnp.where(sub == edge, ci, pltpu.roll(xi, shift, 0))
        last = 0 if reverse else SUBLANES - 1
        nr = jnp.broadcast_to(xr[last:last + 1, :], (SUBLANES, PS))
        ni = jnp.broadcast_to(xi[last:last + 1, :], (SUBLANES, PS))
        return er, ei, nr, ni

    def scan_step(i, carry):
        new = []
        for b in range(B):
            cfr, cfi, cbr, cbi = carry[4 * b:4 * b + 4]
            tile_f = (i + n_blocks - 1) % n_blocks
            tile_b = (2 * n_blocks - 2 - i) % n_blocks
            rf = pl.multiple_of(b * n_chunks + tile_f * SUBLANES, SUBLANES)
            rb = pl.multiple_of(b * n_chunks + tile_b * SUBLANES, SUBLANES)
            f_rows, b_rows = pl.ds(rf, SUBLANES), pl.ds(rb, SUBLANES)
            er, ei, cfr, cfi = block_scan(sh_ref[f_rows, 0:PS], sh_ref[f_rows, PS:2 * PS],
                                          cfr, cfi, 0, False)
            sh_ref[f_rows, 0:PS] = er
            sh_ref[f_rows, PS:2 * PS] = ei
            er, ei, cbr, cbi = block_scan(sh_ref[b_rows, 2 * PS:3 * PS], sh_ref[b_rows, 3 * PS:4 * PS],
                                          cbr, cbi, DEC_ROWS // 2, True)
            sh_ref[b_rows, 2 * PS:3 * PS] = er
            sh_ref[b_rows, 3 * PS:4 * PS] = ei
            new += [cfr, cfi, cbr, cbi]
        return tuple(new)

    zero = jnp.zeros((SUBLANES, PS), F32)
    lax.fori_loop(0, n_blocks, scan_step, (zero,) * (4 * B))

    for r0, n in tiles:
        y = z_ref[0, r0:r0 + n, :] + jnp.dot(sh_ref[r0:r0 + n, :].astype(BF16), w2_ref[0],
                                             preferred_element_type=F32)
        z_ref[0, r0:r0 + n, :] = jax.nn.gelu(y)


def _ssm(u_pairs, w1, w2, dec, B, n_chunks):
    assert n_chunks % SUBLANES == 0
    R = B * n_chunks
    return pl.pallas_call(
        functools.partial(_ssm_body, B=B, n_chunks=n_chunks),
        grid=(N_PAIRS,),
        in_specs=[
            pl.BlockSpec((1, R, PAIR_W), lambda p: (p, 0, 0)),
            pl.BlockSpec((1, PAIR_W, 2 * PAIR_W), lambda p: (p, 0, 0)),
            pl.BlockSpec((1, PAIR_W, PAIR_W), lambda p: (p, 0, 0)),
            pl.BlockSpec((1, DEC_ROWS, PAIR_STATE), lambda p: (p, 0, 0)),
        ],
        out_specs=pl.BlockSpec((1, R, PAIR_W), lambda p: (p, 0, 0)),
        out_shape=jax.ShapeDtypeStruct((N_PAIRS, R, PAIR_W), F32),
        scratch_shapes=[pltpu.VMEM((R, 4 * PAIR_STATE), F32)],
        compiler_params=_cparams(("parallel",)),
        name="ssm",
    )(u_pairs, w1, w2, dec)


def _mix_body(o_ref, zp_ref, h0_ref, wglu_ref, bglu_ref, woa_ref, wos_ref, g_ref, b_ref, wr_ref,
              h1_ref, acc_ref, aff_ref, z_scr, *, L, Lp):
    n_rows = zp_ref.shape[1]
    lane16 = lax.broadcasted_iota(jnp.int32, (n_rows, LANES), 1) // SSM_GROUP
    for t in range(CHUNK):
        src = (t % SUBLANES) * SSM_GROUP
        for v in range(SSM_WIDTH // LANES):
            pieces = []
            for j in range(SUBLANES):
                p, gi = 4 * v + j // 2, j % 2
                lo = gi * CHUNK * SSM_GROUP + (t // SUBLANES) * LANES
                pieces.append((zp_ref[p, :, lo:lo + LANES], src))
            z_scr[v, pl.ds(t, n_rows, stride=CHUNK), :] = _lane_piece_gather(pieces, lane16)
    z = jnp.concatenate([z_scr[v] for v in range(SSM_WIDTH // LANES)], axis=1)
    gate = jax.nn.sigmoid(jnp.dot(z.astype(BF16), wglu_ref[...], preferred_element_type=F32)
                          + bglu_ref[...])
    ssm_out = (z * gate).astype(BF16)
    mix = (jnp.dot(o_ref[...], woa_ref[...], preferred_element_type=F32)
           + jnp.dot(ssm_out, wos_ref[...], preferred_element_type=F32))
    h1 = _layer_norm_rows(DEEPNORM_ALPHA * h0_ref[...] + mix, g_ref[...], b_ref[...])
    for s in range(ROW_SLABS):
        acc_ref[:, s, :] = DEEPNORM_ALPHA * h1[:, s * LANES:(s + 1) * LANES]
    h1_bits = pltpu.bitcast(h1.astype(BF16).astype(F32), jnp.uint32)
    for s in range(ROW_SLABS // 2):
        lo = h1_bits[:, s * LANES:(s + 1) * LANES]
        hi = h1_bits[:, (s + ROW_SLABS // 2) * LANES:(s + ROW_SLABS // 2 + 1) * LANES]
        h1_ref[:, s, :] = hi | lax.shift_right_logical(lo, jnp.uint32(16))
    logits = jnp.dot(h1.astype(BF16), wr_ref[...], preferred_element_type=F32)
    lane = lax.broadcasted_iota(jnp.int32, logits.shape, 1)
    logits = jnp.where(lane < N_EXPERTS, logits, NEG_BIG)
    e = jnp.exp(logits - jnp.max(logits, axis=-1, keepdims=True))
    aff = (e / jnp.sum(e, axis=-1, keepdims=True)).T[:N_EXPERTS, :]
    tiles_per_seq = Lp // ROW_TILE
    pos = (pl.program_id(0) % tiles_per_seq) * ROW_TILE + lax.broadcasted_iota(
        jnp.int32, aff.shape, 1)
    aff_ref[...] = jnp.where((pos < L) | (pos >= L + PAD_ROWS), aff, -1.0)


def _mix(o, z, h0, wglu, bglu, woa, wos, g, b, wr, L, Lp):
    N = o.shape[0]
    TM = ROW_TILE
    row = lambda w: pl.BlockSpec((TM, w), lambda i: (i, 0))
    tok_rows = pl.BlockSpec((TM, ROW_SLABS, LANES), lambda i: (i, 0, 0))
    return pl.pallas_call(
        functools.partial(_mix_body, L=L, Lp=Lp),
        grid=(N // TM,),
        in_specs=[row(ATTN_WIDTH), pl.BlockSpec((N_PAIRS, TM // CHUNK, PAIR_W), lambda i: (0, i, 0)),
                  row(D_MODEL),
                  _const_spec(SSM_WIDTH, SSM_WIDTH), _const_spec(1, SSM_WIDTH),
                  _const_spec(ATTN_WIDTH, D_MODEL), _const_spec(SSM_WIDTH, D_MODEL),
                  _const_spec(1, D_MODEL), _const_spec(1, D_MODEL), _const_spec(D_MODEL, LANES)],
        out_specs=[pl.BlockSpec((TM, ROW_SLABS // 2, LANES), lambda i: (i, 0, 0)), tok_rows,
                   pl.BlockSpec((N_EXPERTS, TM), lambda i: (0, i))],
        out_shape=[jax.ShapeDtypeStruct((N, ROW_SLABS // 2, LANES), jnp.uint32),
                   jax.ShapeDtypeStruct((N, ROW_SLABS, LANES), F32),
                   jax.ShapeDtypeStruct((N_EXPERTS, N), F32)],
        scratch_shapes=[pltpu.VMEM((SSM_WIDTH // LANES, TM, LANES), F32)],
        compiler_params=_cparams(("parallel",)),
        name="mix",
    )(o, z, h0, wglu, bglu, woa, wos, g, b, wr)


def _select_body(aff_ref, ord_ref, idx_ref, thr_ref, *, capacity, n_slot_tiles, seq_len, pad_start):
    nbp = aff_ref.shape[1]
    ri = lax.broadcasted_iota(jnp.int32, (LANES, LANES), 0)
    ci = lax.broadcasted_iota(jnp.int32, (LANES, LANES), 1)
    strict_upper = (ri < ci).astype(BF16)
    incl_upper = (ri <= ci).astype(BF16)
    ones = jnp.ones((LANES, LANES), BF16)
    order = ord_ref[...]
    blk = lax.broadcasted_iota(jnp.int32, (nbp, LANES), 0).astype(F32)
    blk_hi = jnp.floor(blk * (1.0 / 16.0))
    blk_lo = blk - 16.0 * blk_hi
    dot = functools.partial(jnp.dot, preferred_element_type=F32)
    dot_nt = functools.partial(lax.dot_general, dimension_numbers=(((1,), (1,)), ((), ())),
                               preferred_element_type=F32)

    def search_bit(i, ts):
        bit = jnp.left_shift(jnp.int32(1), 30 - i)
        new = []
        for e in range(N_EXPERTS):
            cand = ts[e] | bit
            bits = pltpu.bitcast(aff_ref[e], jnp.int32)
            cnt = jnp.sum((bits >= cand).astype(jnp.int32), keepdims=True)
            new.append(jnp.where(cnt >= capacity, cand, ts[e]))
        return tuple(new)

    thresholds = lax.fori_loop(0, 31, search_bit, (jnp.zeros((1, 1), jnp.int32),) * N_EXPERTS)
    for e in range(N_EXPERTS):
        thr_ref[e] = jnp.broadcast_to(thresholds[e], (SUBLANES, LANES))

    def per_expert(e, carry):
        bits = pltpu.bitcast(aff_ref[e], jnp.int32)
        t = thr_ref[e][0:1, 0:1]
        gt = bits > t
        eq = bits == t
        need = (capacity - jnp.sum(gt.astype(jnp.int32), keepdims=True)).astype(F32)
        eqb = eq.astype(BF16)
        tie_rank = dot(order, dot(eqb, ones).astype(BF16)) + dot(eqb, strict_upper)
        sel = (gt | (eq & (tie_rank < need))).astype(BF16)
        csum_in_blk = dot(sel, incl_upper)
        blk_tot = dot(sel, ones)
        blk_off = dot(order, blk_tot.astype(BF16))
        off_hi = jnp.floor(blk_off * (1.0 / 64.0))
        off_lo = blk_off - 64.0 * off_hi
        table = jnp.concatenate([csum_in_blk, off_hi, off_lo, blk_hi, blk_lo],
                                axis=1).astype(BF16)
        tot_l = dot_nt(ones[:SUBLANES], sel)
        off_l = dot_nt(tot_l.astype(BF16), order)[0:1, :]
        tot_l = tot_l[0:1, :]

        def per_tile(st, carry2):
            j = (st * SLOT_TILE + lax.broadcasted_iota(jnp.int32, (SLOT_TILE, 1), 0)).astype(F32)
            onehot = ((off_l <= j) & (j < off_l + tot_l)).astype(BF16)
            got = dot(onehot, table)
            local = j - (64.0 * got[:, LANES:2 * LANES] + got[:, 2 * LANES:3 * LANES])
            lane_idx = dot((got[:, :LANES] <= local).astype(BF16), ones)
            block = 16.0 * got[:, 3 * LANES:4 * LANES] + got[:, 4 * LANES:5 * LANES]
            k = j - capacity
            seq = sum((k >= PAD_ROWS * i).astype(F32) for i in range(1, SLOT_TILE // PAD_ROWS + 1))
            pad_tok = seq * (seq_len - PAD_ROWS) + pad_start + k
            tok = jnp.where(j < capacity, block * LANES + lane_idx, pad_tok)
            idx_ref[e, st] = tok.T[0:1, :].astype(jnp.int32)
            return carry2

        lax.fori_loop(0, n_slot_tiles, per_tile, 0)
        return carry

    lax.fori_loop(0, N_EXPERTS, per_expert, 0)


def _select(aff_blocks, order, capacity, n_slot_tiles, seq_len, pad_start):
    nbp = aff_blocks.shape[1]
    return pl.pallas_call(
        functools.partial(_select_body, capacity=capacity, n_slot_tiles=n_slot_tiles,
                          seq_len=seq_len, pad_start=pad_start),
        grid=(1,),
        in_specs=[_const_spec(N_EXPERTS, nbp, LANES), _const_spec(nbp, nbp)],
        out_specs=_const_spec(N_EXPERTS, n_slot_tiles, 1, SLOT_TILE),
        out_shape=jax.ShapeDtypeStruct((N_EXPERTS, n_slot_tiles, 1, SLOT_TILE), jnp.int32),
        scratch_shapes=[pltpu.VMEM((N_EXPERTS, SUBLANES, LANES), jnp.int32)],
        compiler_params=_cparams(("arbitrary",)),
        name="select",
    )(aff_blocks, order)


def _moe_body(idx_ref, idx_next_ref, h1_hbm, wg_ref, wu_ref, wd_ref, wr_ref, acc_in, acc_hbm,
              xbuf, abuf, hid_ref, sem_x, sem_a, sem_s):
    del acc_in
    T = xbuf.shape[1]
    n_c = pl.num_programs(1)
    step = pl.program_id(0) * n_c + pl.program_id(1)
    n_steps = pl.num_programs(0) * n_c
    slot = step % 2

    def start_rows(copy_of_row, idx):
        def body(j, carry):
            copy_of_row(idx[0, 0, 0, j], j).start()
            return carry
        lax.fori_loop(0, T, body, 0, unroll=8)

    def x_row(buf):
        return lambda row, j: pltpu.make_async_copy(h1_hbm.at[row], xbuf.at[buf, j], sem_x.at[buf])

    def acc_row_in(row, j):
        return pltpu.make_async_copy(acc_hbm.at[row], abuf.at[j], sem_a)

    def acc_row_out(row, j):
        return pltpu.make_async_copy(abuf.at[j], acc_hbm.at[row], sem_s)

    def wait_scatter():
        pltpu.make_async_copy(abuf, acc_hbm.at[pl.ds(0, T)], sem_s).wait()

    @pl.when(step == 0)
    def _():
        start_rows(x_row(0), idx_ref)

    @pl.when(step + 1 < n_steps)
    def _():
        start_rows(x_row(1 - slot), idx_next_ref)

    pltpu.make_async_copy(h1_hbm.at[pl.ds(0, T)], xbuf.at[slot], sem_x.at[slot]).wait()
    packed = [xbuf[slot, :, s, :] for s in range(ROW_SLABS // 2)]
    lo = [pltpu.bitcast(lax.shift_left(u, jnp.uint32(16)), F32) for u in packed]
    hi = [pltpu.bitcast(u & jnp.uint32(0xFFFF0000), F32) for u in packed]
    x = jnp.concatenate(lo + hi, axis=1).astype(BF16)
    logits = jnp.dot(x, wr_ref[...], preferred_element_type=F32)
    lane = lax.broadcasted_iota(jnp.int32, logits.shape, 1)
    logits = jnp.where(lane < N_EXPERTS, logits, NEG_BIG)
    p = jnp.exp(logits - jnp.max(logits, axis=-1, keepdims=True))
    gate = (jnp.sum(jnp.where(lane == pl.program_id(0), p, 0.0), axis=-1, keepdims=True)
            / jnp.sum(p, axis=-1, keepdims=True))
    for f in range(EXPERT_FF // FF_TILE):
        fs = slice(f * FF_TILE, (f + 1) * FF_TILE)
        hg = jnp.dot(x, wg_ref[0, :, fs], preferred_element_type=F32)
        hu = jnp.dot(x, wu_ref[0, :, fs], preferred_element_type=F32)
        hid_ref[:, fs] = (jax.nn.silu(hg) * hu).astype(BF16)
        if f == 0:
            @pl.when(step > 0)
            def _():
                wait_scatter()

            start_rows(acc_row_in, idx_ref)
    pltpu.make_async_copy(acc_hbm.at[pl.ds(0, T)], abuf, sem_a).wait()
    for n0 in range(0, D_MODEL, 2 * LANES):
        y = jnp.dot(hid_ref[...], wd_ref[0, :, n0:n0 + 2 * LANES], preferred_element_type=F32) * gate
        for h in range(2):
            s = n0 // LANES + h
            abuf[:, s, :] = abuf[:, s, :] + y[:, h * LANES:(h + 1) * LANES]
    start_rows(acc_row_out, idx_ref)

    @pl.when(step == n_steps - 1)
    def _():
        wait_scatter()


def _moe(idx, h1, acc, wg, wu, wd, wr):
    n_slots = idx.shape[1] * idx.shape[3]
    n_c = next(n for n in range(1, n_slots) if n_slots % n == 0 and n_slots // n <= MOE_TILE_MAX
               and (n_slots // n) % 16 == 0)
    T = n_slots // n_c
    idx = idx.reshape(N_EXPERTS, n_c, 1, T)

    def next_block(e, c):
        return (jnp.minimum(e + (c + 1) // n_c, N_EXPERTS - 1), (c + 1) % n_c, 0, 0)

    return pl.pallas_call(
        _moe_body,
        grid=(N_EXPERTS, n_c),
        in_specs=[
            pl.BlockSpec((1, 1, 1, T), lambda e, c: (e, c, 0, 0), memory_space=pltpu.SMEM),
            pl.BlockSpec((1, 1, 1, T), next_block, memory_space=pltpu.SMEM),
            pl.BlockSpec(memory_space=pl.ANY),
            pl.BlockSpec((1, D_MODEL, EXPERT_FF), lambda e, c: (e, 0, 0)),
            pl.BlockSpec((1, D_MODEL, EXPERT_FF), lambda e, c: (e, 0, 0)),
            pl.BlockSpec((1, EXPERT_FF, D_MODEL), lambda e, c: (e, 0, 0)),
            _const_spec(D_MODEL, LANES),
            pl.BlockSpec(memory_space=pl.ANY),
        ],
        out_specs=pl.BlockSpec(memory_space=pl.ANY),
        out_shape=jax.ShapeDtypeStruct(acc.shape, F32),
        scratch_shapes=[pltpu.VMEM((2, T, ROW_SLABS // 2, LANES), jnp.uint32),
                        pltpu.VMEM((T, ROW_SLABS, LANES), F32),
                        pltpu.VMEM((T, EXPERT_FF), BF16),
                        pltpu.SemaphoreType.DMA((2,)), pltpu.SemaphoreType.DMA(()),
                        pltpu.SemaphoreType.DMA(())],
        input_output_aliases={7: 0},
        compiler_params=pltpu.CompilerParams(dimension_semantics=("arbitrary", "arbitrary"),
                                             vmem_limit_bytes=VMEM_LIMIT,
                                             disable_bounds_checks=True),
        name="expert_ffn",
    )(idx, idx, h1, wg, wu, wd, wr, acc)


def _final_body(a_ref, g_ref, b_ref, o_ref):
    x = jnp.concatenate([a_ref[0, :, s, :] for s in range(ROW_SLABS)], axis=1)
    o_ref[0] = _layer_norm_rows(x, g_ref[...], b_ref[...])


def _final_norm(acc, g, b, L):
    B = acc.shape[0]
    TM = EMBED_TILE
    return pl.pallas_call(
        _final_body,
        grid=(B, L // TM),
        in_specs=[pl.BlockSpec((1, TM, ROW_SLABS, LANES), lambda b_, j: (b_, j, 0, 0)),
                  _const_spec(1, D_MODEL), _const_spec(1, D_MODEL)],
        out_specs=pl.BlockSpec((1, TM, D_MODEL), lambda b_, j: (b_, j, 0)),
        out_shape=jax.ShapeDtypeStruct((B, L, D_MODEL), F32),
        compiler_params=_cparams(("parallel", "parallel")),
        name="final_norm",
    )(acc, g, b)


def _rope_tables(L, Lp):
    rows = L // GRID_W
    t = jnp.arange(L, dtype=jnp.int32)
    m = jnp.arange(N_META, dtype=jnp.int32)
    pad = jnp.zeros((Lp - L - N_META,), jnp.int32)
    row = jnp.concatenate([t // GRID_W - rows // 2, pad,
                           jnp.full((N_META,), -(rows // 2) - 1, jnp.int32)])
    col = jnp.concatenate([t % GRID_W - GRID_W // 2, pad, m - GRID_W // 2])
    inv_freq = ROPE_THETA ** (-jnp.arange(0, ROPE_AXIS_DIM, 2, dtype=F32) / ROPE_AXIS_DIM)
    ang_r = row.astype(F32)[:, None] * inv_freq
    ang_c = col.astype(F32)[:, None] * inv_freq
    cos = jnp.concatenate([jnp.cos(ang_r)] * 2 + [jnp.cos(ang_c)] * 2, axis=1)
    sin = jnp.concatenate([-jnp.sin(ang_r), jnp.sin(ang_r), -jnp.sin(ang_c), jnp.sin(ang_c)], axis=1)
    return jnp.tile(cos, (1, 2)), jnp.tile(sin, (1, 2))


def _block_order(B, Lp, nbp):
    nbb = Lp // LANES
    r = jnp.arange(nbp)
    b, jb = r // nbb, r % nbb
    rank = jnp.where(r < B * nbb, b * nbb + jnp.where(jb == nbb - 1, 0, jb + 1), r)
    return (rank[None, :] < rank[:, None]).astype(BF16)


def _run_trunk(x, meta_pad, shared):
    B, L, _ = x.shape
    Lp = L + TAIL
    N = B * Lp
    cos, sin = _rope_tables(L, Lp)
    h0, q, k, vts, u = _embed(x, meta_pad, shared["ln_emb_g"], shared["ln_emb_b"], shared["w_in"],
                              shared["qg"], shared["kg"], shared["bones"], cos, sin, L, Lp)
    o = _attention(q, k, vts, L, Lp, shared["score_bound"])
    z = _ssm(u.reshape(N_PAIRS, N // CHUNK, PAIR_W), shared["ssm_w1"], shared["ssm_w2"],
             shared["ssm_dec"], B, Lp // CHUNK)
    h1, acc, aff = _mix(o.reshape(N, ATTN_WIDTH), z, h0.reshape(N, D_MODEL),
                        shared["w_glu"], shared["b_glu"], shared["w_out_attn"], shared["w_out_ssm"],
                        shared["ln1_g"], shared["ln1_b"], shared["w_router"], L, Lp)

    capacity = EC_CAPACITY_FACTOR * B * (L + N_META) // N_EXPERTS
    n_slot_tiles = -(-capacity // SLOT_TILE)
    nb = N // LANES
    nbp = -(-nb // LANES) * LANES
    aff_blocks = jnp.pad(aff.reshape(N_EXPERTS, nb, LANES), ((0, 0), (0, nbp - nb), (0, 0)),
                         constant_values=-1.0)
    assert n_slot_tiles * SLOT_TILE - capacity <= B * PAD_ROWS
    idx = _select(aff_blocks, _block_order(B, Lp, nbp), capacity, n_slot_tiles, Lp, L)
    acc = _moe(idx, h1, acc, shared["w_gate"], shared["w_up"], shared["w_down"], shared["w_router"])
    return _final_norm(acc.reshape(B, Lp, ROW_SLABS, LANES), shared["ln2_g"], shared["ln2_b"], L)


def kernel(x_prompt, x_sample, meta_tokens, ln_emb_g, ln_emb_b, w_in, q_norm_g, k_norm_g, ssm_lambda_re, ssm_lambda_im, ssm_log_dt, ssm_b_re, ssm_b_im, ssm_c_re, ssm_c_im, ssm_d, w_glu, b_glu, w_out, ln1_g, ln1_b, w_router, w_gate, w_up, w_down, ln2_g, ln2_b):
    row = lambda a: a.reshape(1, -1).astype(F32)
    w_q = w_in[0][:, :Q_END].reshape(D_MODEL, N_KV_HEADS, Q_PER_KV, HEAD_DIM)
    w_q = w_q.transpose(0, 2, 1, 3).reshape(D_MODEL, Q_END)
    w_oa = w_out[0][:ATTN_WIDTH].reshape(N_KV_HEADS, Q_PER_KV, HEAD_DIM, D_MODEL)
    w_oa = w_oa.transpose(1, 0, 2, 3).reshape(ATTN_WIDTH, D_MODEL)
    head_of = jnp.arange(Q_END) // HEAD_DIM
    ssm_w1, ssm_w2, ssm_dec = _ssm_weights(ssm_lambda_re[0], ssm_lambda_im[0], ssm_log_dt[0],
                                           ssm_b_re[0], ssm_b_im[0], ssm_c_re[0], ssm_c_im[0],
                                           ssm_d[0])
    shared = dict(
        ln_emb_g=row(ln_emb_g), ln_emb_b=row(ln_emb_b),
        w_in=jnp.concatenate([w_q, w_in[0][:, Q_END:]], axis=1).astype(BF16),
        qg=row(jnp.tile(q_norm_g[0], N_Q_HEADS)), kg=row(jnp.tile(k_norm_g[0], N_KV_HEADS)),
        bones=(head_of[:, None] == head_of[None, :]).astype(BF16),
        score_bound=(1.02 * HEAD_DIM * Q_PRESCALE * jnp.max(jnp.abs(q_norm_g[0]))
                     * jnp.max(jnp.abs(k_norm_g[0]))).astype(F32),
        ssm_w1=ssm_w1, ssm_w2=ssm_w2, ssm_dec=ssm_dec,
        w_glu=w_glu[0].astype(BF16), b_glu=row(b_glu[0]),
        w_out_attn=w_oa.astype(BF16), w_out_ssm=w_out[0][ATTN_WIDTH:].astype(BF16),
        ln1_g=row(ln1_g[0]), ln1_b=row(ln1_b[0]),
        w_router=jnp.pad(w_router[0], ((0, 0), (0, LANES - N_EXPERTS))).astype(BF16),
        w_gate=w_gate[0].astype(BF16), w_up=w_up[0].astype(BF16), w_down=w_down[0].astype(BF16),
        ln2_g=row(ln2_g[0]), ln2_b=row(ln2_b[0]),
    )
    meta_pad = jnp.pad(meta_tokens.astype(F32), ((PAD_ROWS, 0), (0, 0)))[None]
    return (_run_trunk(x_prompt, meta_pad, shared), _run_trunk(x_sample, meta_pad, shared))
```
